```python
import math
import jax, jax.numpy as jnp
from jax import lax
import numpy as np

D_MODEL = 1024
BATCH = 8
SEQ = 2048
DEPTH = 2
DEC_BATCH = 128
DEC_SEQ = 8
PAST_LEN = 16384
PAGE_SIZE = 128

N_MIXERS = 4
BRANCH = D_MODEL // N_MIXERS
N_HEADS = 4
HEAD_DIM = BRANCH // N_HEADS
MIX_W = N_MIXERS * BRANCH
GLA_KEY_DIM = HEAD_DIM // 2
GLA_KEY_WIDTH = N_HEADS * GLA_KEY_DIM
GLA_GATE_RANK = 16
GLA_TAU = 16.0
RWKV_DECAY_RANK = 32
RWKV_A_RANK = 32
RWKV_V_RANK = 32
N_MEM = 256
X_HEADS = 4
X_HEAD_DIM = 64
X_INNER = X_HEADS * X_HEAD_DIM
CHUNK = 16
ROPE_BASE = 10000.0
RMS_EPS = 1e-6
RET_GN_EPS = 1e-5
RWKV_GN_EPS = 64e-5
LB_FLOOR = 1e-20
RET_W = 4 * BRANCH
HGRN_W = 4 * BRANCH
GLA_W = 2 * GLA_KEY_WIDTH + BRANCH + GLA_GATE_RANK + BRANCH
RWKV_W = 3 * BRANCH + RWKV_DECAY_RANK + RWKV_A_RANK + BRANCH
D_IN = RET_W + HGRN_W + GLA_W + RWKV_W
STATE_NAMES = ('ret', 'hgrn', 'gla', 'rwkv', 'rwkv_shift')

kernel_name = 'hybrid_quad_recurrent_decoder_step'


def rmsnorm(x, g):
    xf = x.astype(jnp.float32)
    y = xf * lax.rsqrt(jnp.mean(xf * xf, axis=-1, keepdims=True) + RMS_EPS)
    return (y * g).astype(x.dtype)


def split_heads(t, d):
    return t.reshape(t.shape[:-1] + (t.shape[-1] // d, d))


def merge_heads(t):
    return t.reshape(t.shape[:-2] + (t.shape[-2] * t.shape[-1],))


def head_group_norm(o, eps):
    of = o.astype(jnp.float32)
    mu = jnp.mean(of, axis=-1, keepdims=True)
    c = of - mu
    return (c * lax.rsqrt(jnp.mean(c * c, axis=-1, keepdims=True) + eps)).astype(o.dtype)


def head_rms(o, g):
    of = o.astype(jnp.float32)
    return (of * lax.rsqrt(jnp.mean(of * of, axis=-1, keepdims=True) + RMS_EPS) * g).astype(o.dtype)


def rotary(x, pos):
    half = x.shape[-1] // 2
    inv = ROPE_BASE ** (-jnp.arange(half, dtype=jnp.float32) / half)
    ang = pos.astype(jnp.float32)[:, None] * inv[None, :]
    cos = jnp.cos(ang)[None, :, None, :]
    sin = jnp.sin(ang)[None, :, None, :]
    x1 = x[..., :half].astype(jnp.float32)
    x2 = x[..., half:].astype(jnp.float32)
    return jnp.concatenate([x1 * cos - x2 * sin, x1 * sin + x2 * cos], axis=-1).astype(x.dtype)


def chunked_gated_linear(q, k, v, log_g, s0):
    B, L, H, _ = q.shape
    dv = v.shape[-1]
    c = math.gcd(L, CHUNK)
    n = L // c

    def to_chunks(t):
        return t.reshape(B, n, c, H, t.shape[-1]).transpose(1, 0, 3, 2, 4)

    causal = jnp.tril(jnp.ones((c, c), dtype=bool))[:, :, None]

    def step(S, inp):
        qi, ki, vi, gi = inp
        qf = qi.astype(jnp.float32)
        kf = ki.astype(jnp.float32)
        vf = vi.astype(jnp.float32)
        b = jnp.cumsum(gi.astype(jnp.float32), axis=2)
        diff = b[:, :, :, None, :] - b[:, :, None, :, :]
        decay = jnp.where(causal, jnp.exp(jnp.minimum(diff, 0.0)), 0.0)
        scores = jnp.sum(qf[:, :, :, None, :] * kf[:, :, None, :, :] * decay, axis=-1)
        o = (jnp.einsum('bhjm,bhmv->bhjv', scores, vf)
             + jnp.einsum('bhjk,bhkv->bhjv', qf * jnp.exp(b), S))
        b_last = b[:, :, -1:, :]
        S_new = (jnp.exp(b_last[:, :, 0, :])[..., None] * S
                 + jnp.einsum('bhmk,bhmv->bhkv', kf * jnp.exp(b_last - b), vf))
        return S_new, o

    S, o = lax.scan(step, s0.astype(jnp.float32),
                    (to_chunks(q), to_chunks(k), to_chunks(v), to_chunks(log_g)))
    o = o.transpose(1, 0, 3, 2, 4).reshape(B, L, H, dv)
    return o.astype(v.dtype), S.astype(s0.dtype)


def rwkv7_scan(r, w, k, v, kk, a, s0):
    def step(S, inp):
        rt, wt, kt, vt, kkt, at = inp
        sa = jnp.einsum('bhvk,bhk->bhv', S, kkt)
        S = (S * wt[:, :, None, :] - sa[..., None] * (kkt * at)[:, :, None, :]
             + vt[..., None] * kt[:, :, None, :])
        return S, jnp.einsum('bhvk,bhk->bhv', S, rt)

    xs = tuple(t.transpose(1, 0, 2, 3) for t in (r, w, k, v, kk, a))
    S, o = lax.scan(step, s0, xs)
    return o.transpose(1, 0, 2, 3), S


def rwkv_branch(p, shift, l, v_first, s0, P):
    B, L, _ = p.shape
    prev = jnp.concatenate([shift[:, None, :].astype(p.dtype), p[:, :-1]], axis=1)
    ps = p + (prev - p) * P['rwkv_mu'][l]
    offs = [BRANCH, BRANCH + RWKV_DECAY_RANK, 2 * BRANCH + RWKV_DECAY_RANK,
            3 * BRANCH + RWKV_DECAY_RANK, 3 * BRANCH + RWKV_DECAY_RANK + RWKV_A_RANK]
    r, wl, k, v, al, g = jnp.split(ps, offs, axis=-1)
    w_log = -jax.nn.softplus(-(P['rwkv_w0'][l] + jnp.tanh(wl) @ P['rwkv_w2'][l]).astype(jnp.float32)) - 0.5
    decay = jnp.exp(-jnp.exp(w_log))
    a = jax.nn.sigmoid((P['rwkv_a0'][l] + al @ P['rwkv_a2'][l]).astype(jnp.float32))
    if l == 0:
        v_first = v
    else:
        vmix = jax.nn.sigmoid(P['rwkv_v0'][l - 1] + (v @ P['rwkv_v1'][l - 1]) @ P['rwkv_v2'][l - 1])
        v = v + (v_first - v) * vmix
    kk = split_heads((k * P['rwkv_k_k'][l]).astype(jnp.float32), HEAD_DIM)
    kk = kk / jnp.maximum(jnp.sqrt(jnp.sum(kk * kk, axis=-1, keepdims=True)), 1e-12)
    k = k.astype(jnp.float32) * (1.0 + (a - 1.0) * P['rwkv_k_a'][l])
    rh = split_heads(r.astype(jnp.float32), HEAD_DIM)
    kh = split_heads(k, HEAD_DIM)
    vh = split_heads(v.astype(jnp.float32), HEAD_DIM)
    o, S = rwkv7_scan(rh, split_heads(decay, HEAD_DIM), kh, vh, kk, split_heads(a, HEAD_DIM),
                      s0.astype(jnp.float32))
    o = (head_group_norm(o, RWKV_GN_EPS) * split_heads(P['rwkv_gn_g'][l], HEAD_DIM)
         + split_heads(P['rwkv_gn_b'][l], HEAD_DIM))
    o = o + jnp.sum(rh * kh * P['rwkv_r_k'][l], axis=-1, keepdims=True) * vh
    o = merge_heads(o).astype(p.dtype) * jax.nn.silu(g)
    return o, S.astype(s0.dtype), p[:, -1], v_first


def mixer(h, l, pos, st, v_first, P):
    B, L, _ = h.shape
    proj = h @ P['w_in'][l]
    p_ret, p_hgrn, p_gla, p_rwkv = jnp.split(
        proj, [RET_W, RET_W + HGRN_W, RET_W + HGRN_W + GLA_W], axis=-1)

    q, k, v, g = jnp.split(p_ret, 4, axis=-1)
    q = rotary(split_heads(q, HEAD_DIM), pos)
    k = rotary(split_heads(k, HEAD_DIM), pos) * (HEAD_DIM ** -0.5)
    log_gamma = jnp.log1p(-jnp.exp2(-5.0 - jnp.arange(N_HEADS, dtype=jnp.float32)))
    lg = jnp.broadcast_to(log_gamma[:, None], (B, L, N_HEADS, 1))
    o, s_ret = chunked_gated_linear(q, k, split_heads(v, HEAD_DIM), lg, st['ret'])
    o_ret = merge_heads(head_group_norm(o, RET_GN_EPS)) * jax.nn.silu(g)

    q, f, i, g = jnp.split(p_hgrn, 4, axis=-1)
    sm = jax.nn.softmax(P['hgrn_lb_logits'].astype(jnp.float32), axis=0)
    lb = (jnp.cumsum(sm, axis=0) - sm[0])[l]
    ff = f.astype(jnp.float32)
    log_f = jnp.logaddexp(jnp.log(jnp.maximum(lb, LB_FLOOR)), jnp.log1p(-lb) + jax.nn.log_sigmoid(ff))
    k_h = (1.0 - lb) * jax.nn.sigmoid(-ff)
    o, s_hgrn = chunked_gated_linear(split_heads(jax.nn.silu(q), HEAD_DIM), split_heads(k_h, HEAD_DIM),
                                     split_heads(i, HEAD_DIM), split_heads(log_f, HEAD_DIM), st['hgrn'])
    o_hgrn = merge_heads(head_rms(o, P['hgrn_norm'][l])) * jax.nn.silu(g)

    q, k, v, gl, g = jnp.split(p_gla, [GLA_KEY_WIDTH, 2 * GLA_KEY_WIDTH, 2 * GLA_KEY_WIDTH + BRANCH,
                                       2 * GLA_KEY_WIDTH + BRANCH + GLA_GATE_RANK], axis=-1)
    log_a = jax.nn.log_sigmoid((gl @ P['gla_w_gate2'][l] + P['gla_b_gate'][l]).astype(jnp.float32)) / GLA_TAU
    o, s_gla = chunked_gated_linear(split_heads(q, GLA_KEY_DIM) * (GLA_KEY_DIM ** -0.5),
                                    split_heads(k, GLA_KEY_DIM), split_heads(v, HEAD_DIM),
                                    split_heads(log_a, GLA_KEY_DIM), st['gla'])
    o_gla = merge_heads(head_rms(o, P['gla_norm'][l])) * jax.nn.silu(g)

    o_rwkv, s_rwkv, new_shift, v_first = rwkv_branch(p_rwkv, st['rwkv_shift'], l, v_first, st['rwkv'], P)

    out = jnp.concatenate([o_ret, o_hgrn, o_gla, o_rwkv], axis=-1) @ P['w_out'][l]
    states = {'ret': s_ret, 'hgrn': s_hgrn, 'gla': s_gla, 'rwkv': s_rwkv,
              'rwkv_shift': new_shift.astype(st['rwkv_shift'].dtype)}
    return out, states, v_first


def cross_attend(hn, wq, wo, mk, mv):
    q = split_heads(hn @ wq, X_HEAD_DIM)
    s = jnp.einsum('blhd,bmhd->bhlm', q, mk).astype(jnp.float32) * (X_HEAD_DIM ** -0.5)
    p = jax.nn.softmax(s, axis=-1).astype(mv.dtype)
    o = jnp.einsum('bhlm,bmhd->blhd', p, mv)
    return merge_heads(o) @ wo


def run_trunk(x, pos0, mem_k, mem_v, init, P):
    L = x.shape[1]
    pos = pos0 + jnp.arange(L, dtype=jnp.int32)
    new = {n: [] for n in STATE_NAMES}
    v_first = None
    for l in range(DEPTH):
        out, st_l, v_first = mixer(rmsnorm(x, P['norm_mix'][l]), l, pos,
                                   {n: init[n][l] for n in STATE_NAMES}, v_first, P)
        x = x + out
        x = x + cross_attend(rmsnorm(x, P['norm_x'][l]), P['wq_x'][l], P['wo_x'][l], mem_k[l], mem_v[l])
        for n in STATE_NAMES:
            new[n].append(st_l[n])
    return rmsnorm(x, P['norm_f']), {n: jnp.stack(new[n], axis=0) for n in STATE_NAMES}


def setup_inputs(seed: int = 0) -> dict:
    key = jax.random.key(seed)
    ks = iter(jax.random.split(key, 48))

    def nrm(shape, scale=1.0):
        return scale * jax.random.normal(next(ks), shape, jnp.float32)

    def unif(shape, lo, hi):
        return jax.random.uniform(next(ks), shape, jnp.float32, lo, hi)

    L1 = DEPTH - 1
    return {
        'x_prompt': nrm((BATCH, SEQ, D_MODEL)),
        'x_sample': nrm((DEC_BATCH, DEC_SEQ, D_MODEL)),
        'mem_prompt': nrm((BATCH, N_MEM, D_MODEL)),
        'state_ret': nrm((DEPTH, DEC_BATCH, N_HEADS, HEAD_DIM, HEAD_DIM), 0.5),
        'state_hgrn': nrm((DEPTH, DEC_BATCH, N_HEADS, HEAD_DIM, HEAD_DIM), 0.5),
        'state_gla': nrm((DEPTH, DEC_BATCH, N_HEADS, GLA_KEY_DIM, HEAD_DIM), 0.5),
        'state_rwkv': nrm((DEPTH, DEC_BATCH, N_HEADS, HEAD_DIM, HEAD_DIM), 0.3),
        'state_rwkv_shift': nrm((DEPTH, DEC_BATCH, RWKV_W)),
        'cache_mem_k': nrm((DEPTH, DEC_BATCH, N_MEM, X_HEADS, X_HEAD_DIM)),
        'cache_mem_v': nrm((DEPTH, DEC_BATCH, N_MEM, X_HEADS, X_HEAD_DIM)),
        'norm_mix': 1.0 + nrm((DEPTH, D_MODEL), 0.02),
        'w_in': nrm((DEPTH, D_MODEL, D_IN), D_MODEL ** -0.5),
        'hgrn_lb_logits': nrm((DEPTH, BRANCH), 0.1),
        'hgrn_norm': 1.0 + nrm((DEPTH, HEAD_DIM), 0.02),
        'gla_w_gate2': nrm((DEPTH, GLA_GATE_RANK, GLA_KEY_WIDTH), GLA_GATE_RANK ** -0.5),
        'gla_b_gate': nrm((DEPTH, GLA_KEY_WIDTH), 0.01),
        'gla_norm': 1.0 + nrm((DEPTH, HEAD_DIM), 0.02),
        'rwkv_mu': unif((DEPTH, RWKV_W), 0.0, 1.0),
        'rwkv_w0': unif((DEPTH, BRANCH), -6.5, -1.5),
        'rwkv_w2': nrm((DEPTH, RWKV_DECAY_RANK, BRANCH), 0.1 * RWKV_DECAY_RANK ** -0.5),
        'rwkv_a0': nrm((DEPTH, BRANCH), 0.01),
        'rwkv_a2': nrm((DEPTH, RWKV_A_RANK, BRANCH), RWKV_A_RANK ** -0.5),
        'rwkv_v0': 1.0 + nrm((L1, BRANCH), 0.01),
        'rwkv_v1': nrm((L1, BRANCH, RWKV_V_RANK), BRANCH ** -0.5),
        'rwkv_v2': nrm((L1, RWKV_V_RANK, BRANCH), RWKV_V_RANK ** -0.5),
        'rwkv_k_k': 0.85 + nrm((DEPTH, BRANCH), 0.02),
        'rwkv_k_a': 1.0 + nrm((DEPTH, BRANCH), 0.02),
        'rwkv_r_k': nrm((DEPTH, N_HEADS, HEAD_DIM), 0.1),
        'rwkv_gn_g': 1.0 + nrm((DEPTH, BRANCH), 0.02),
        'rwkv_gn_b': nrm((DEPTH, BRANCH), 0.01),
        'w_out': nrm((DEPTH, MIX_W, D_MODEL), MIX_W ** -0.5),
        'norm_x': 1.0 + nrm((DEPTH, D_MODEL), 0.02),
        'wq_x': nrm((DEPTH, D_MODEL, X_INNER), D_MODEL ** -0.5),
        'wo_x': nrm((DEPTH, X_INNER, D_MODEL), X_INNER ** -0.5),
        'norm_mem': 1.0 + nrm((DEPTH, D_MODEL), 0.02),
        'wk_x': nrm((DEPTH, D_MODEL, X_INNER), D_MODEL ** -0.5),
        'wv_x': nrm((DEPTH, D_MODEL, X_INNER), D_MODEL ** -0.5),
        'norm_f': 1.0 + nrm((D_MODEL,), 0.02),
    }


def reference(x_prompt, x_sample, mem_prompt, state_ret, state_hgrn, state_gla, state_rwkv,
              state_rwkv_shift, cache_mem_k, cache_mem_v, norm_mix, w_in, hgrn_lb_logits, hgrn_norm,
              gla_w_gate2, gla_b_gate, gla_norm, rwkv_mu, rwkv_w0, rwkv_w2, rwkv_a0, rwkv_a2,
              rwkv_v0, rwkv_v1, rwkv_v2, rwkv_k_k, rwkv_k_a, rwkv_r_k, rwkv_gn_g, rwkv_gn_b,
              w_out, norm_x, wq_x, wo_x, norm_mem, wk_x, wv_x, norm_f):
    P = {'norm_mix': norm_mix, 'w_in': w_in, 'hgrn_lb_logits': hgrn_lb_logits, 'hgrn_norm': hgrn_norm,
         'gla_w_gate2': gla_w_gate2, 'gla_b_gate': gla_b_gate, 'gla_norm': gla_norm,
         'rwkv_mu': rwkv_mu, 'rwkv_w0': rwkv_w0, 'rwkv_w2': rwkv_w2, 'rwkv_a0': rwkv_a0,
         'rwkv_a2': rwkv_a2, 'rwkv_v0': rwkv_v0, 'rwkv_v1': rwkv_v1, 'rwkv_v2': rwkv_v2,
         'rwkv_k_k': rwkv_k_k, 'rwkv_k_a': rwkv_k_a, 'rwkv_r_k': rwkv_r_k,
         'rwkv_gn_g': rwkv_gn_g, 'rwkv_gn_b': rwkv_gn_b, 'w_out': w_out,
         'norm_x': norm_x, 'wq_x': wq_x, 'wo_x': wo_x, 'norm_f': norm_f}

    B = x_prompt.shape[0]
    dt = x_prompt.dtype
    mk_list, mv_list = [], []
    for l in range(DEPTH):
        m = rmsnorm(mem_prompt, norm_mem[l])
        mk_list.append(split_heads(m @ wk_x[l], X_HEAD_DIM))
        mv_list.append(split_heads(m @ wv_x[l], X_HEAD_DIM))
    mem_k_p = jnp.stack(mk_list, axis=0)
    mem_v_p = jnp.stack(mv_list, axis=0)
    zero_init = {
        'ret': jnp.zeros((DEPTH, B, N_HEADS, HEAD_DIM, HEAD_DIM), dt),
        'hgrn': jnp.zeros((DEPTH, B, N_HEADS, HEAD_DIM, HEAD_DIM), dt),
        'gla': jnp.zeros((DEPTH, B, N_HEADS, GLA_KEY_DIM, HEAD_DIM), dt),
        'rwkv': jnp.zeros((DEPTH, B, N_HEADS, HEAD_DIM, HEAD_DIM), dt),
        'rwkv_shift': jnp.zeros((DEPTH, B, RWKV_W), dt),
    }
    y_prompt, sp = run_trunk(x_prompt, 0, mem_k_p, mem_v_p, zero_init, P)

    cached = {'ret': state_ret, 'hgrn': state_hgrn, 'gla': state_gla, 'rwkv': state_rwkv,
              'rwkv_shift': state_rwkv_shift}
    y_sample, ss = run_trunk(x_sample, PAST_LEN, cache_mem_k, cache_mem_v, cached, P)

    return (y_prompt, y_sample, sp['ret'], ss['ret'], sp['hgrn'], ss['hgrn'], sp['gla'], ss['gla'],
            sp['rwkv'], ss['rwkv'], sp['rwkv_shift'], ss['rwkv_shift'], mem_k_p, mem_v_p)
```

```python
import functools

import jax
import jax.numpy as jnp
from jax import lax
from jax.experimental import pallas as pl
from jax.experimental.pallas import tpu as pltpu

F32 = jnp.float32
BF16 = jnp.bfloat16

D_MODEL = 1024
DEPTH = 2
PAST_LEN = 16384
N_HEADS = 4
HEAD_DIM = 64
BRANCH = N_HEADS * HEAD_DIM
GLA_KEY_DIM = 32
GLA_KEY_WIDTH = N_HEADS * GLA_KEY_DIM
GLA_TAU = 16.0
N_MEM = 256
ROPE_BASE = 10000.0
RMS_EPS = 1e-6
RET_GN_EPS = 1e-5
RWKV_GN_EPS = 64e-5
LB_FLOOR = 1e-20
RWKV_W = 1088
CHUNK = 16

RET_OFF = 0
HGRN_OFF = 1024
GLA_OFF = 2048
RWKV_OFF = 2944
RWKV_PW = 1152
D_IN_PAD = 4096

V7X_VMEM_LIMIT = 56 * 1024 * 1024
NEG_BIG = -1e30

(V_LG_RET, V_LB_A, V_LB_B, V_LB_OM, V_HGRN_G, V_GLA_G, V_W0, V_A0, V_V0, V_KK, V_KA, V_RK,
 V_GN_G, V_GN_B) = range(14)
N_VEC = 16


def _rms(x, g):
    return x * lax.rsqrt(jnp.mean(x * x, axis=-1, keepdims=True) + RMS_EPS) * g


def _sigmoid(x):
    return 1.0 / (1.0 + jnp.exp(-x))


def _silu(x):
    return x * _sigmoid(x)


def _log_sigmoid(x):
    return jnp.minimum(x, 0.0) - jnp.log1p(jnp.exp(-jnp.abs(x)))


def _heads(x):
    return [x[:, h * HEAD_DIM:(h + 1) * HEAD_DIM] for h in range(N_HEADS)]


def _dot(a, b):
    return jnp.dot(a.astype(BF16), b.astype(BF16), preferred_element_type=F32)


def _in_proj_kernel(x_ref, g_ref, w_ref, o_ref):
    o_ref[...] = _dot(_rms(x_ref[...], g_ref[...]), w_ref[...])


def _in_proj(x2d, gain, w_bf16, tm):
    m, d = x2d.shape
    n = w_bf16.shape[1]
    return pl.pallas_call(
        _in_proj_kernel,
        grid=(m // tm,),
        in_specs=[pl.BlockSpec((tm, d), lambda i: (i, 0)),
                  pl.BlockSpec((1, d), lambda i: (0, 0)),
                  pl.BlockSpec((d, n), lambda i: (0, 0))],
        out_specs=pl.BlockSpec((tm, n), lambda i: (i, 0)),
        out_shape=jax.ShapeDtypeStruct((m, n), F32),
        compiler_params=pltpu.CompilerParams(dimension_semantics=("arbitrary",),
                                             vmem_limit_bytes=V7X_VMEM_LIMIT),
        name="in_proj",
    )(x2d, gain, w_bf16)


def _chunk_cumsum(x, c):
    t_local = jnp.bitwise_and(lax.broadcasted_iota(jnp.int32, x.shape, 0), c - 1)
    s = 1
    while s < c:
        x = x + jnp.where(t_local >= s, pltpu.roll(x, s, 0), 0.0)
        s *= 2
    return x


def _gl_chunk(q, k, v, b, st, bo, mbd, c):
    rows = lax.broadcasted_iota(jnp.int32, q.shape, 0)
    tiles = []
    for m in range(c):
        arg = jnp.where(rows >= m, b - b[m:m + 1, :], NEG_BIG)
        tiles.append(q * k[m:m + 1, :] * jnp.exp(arg))
    scores = _dot(jnp.concatenate(tiles, axis=0), bo)
    o = lax.dot_general((q * jnp.exp(b)).astype(BF16), st.astype(BF16),
                        (((1,), (1,)), ((), ())), preferred_element_type=F32)
    for m in range(c):
        o = o + scores[m * c:(m + 1) * c, :] * v[m:m + 1, :]
    b_last = b[c - 1:c, :]
    upd = lax.dot_general(v.astype(BF16), (k * jnp.exp(b_last - b)).astype(BF16),
                          (((0,), (0,)), ((), ())), preferred_element_type=F32)
    return o, st * jnp.exp(b_last) + upd * mbd


def _mixers_kernel(*refs, bb, tc, c, layer0, n_t):
    it = iter(refs)
    proj_ref = next(it)
    vfirst_in_ref = None if layer0 else next(it)
    cos_ref, sin_ref = next(it), next(it)
    s_ret0, s_hgrn0, s_gla0, s_rw0, shift0 = next(it), next(it), next(it), next(it), next(it)
    vec_ref, mu_ref, bgate_ref = next(it), next(it), next(it)
    wg_ref, w2_ref, a2_ref = next(it), next(it), next(it)
    v1_ref, v2_ref = (None, None) if layer0 else (next(it), next(it))
    bo_ref, bog_ref, mbd_ref, mbdg_ref = next(it), next(it), next(it), next(it)
    ocat_ref = next(it)
    vfirst_out_ref = next(it) if layer0 else None
    o_ret_st, o_hgrn_st, o_gla_st, o_rw_st = next(it), next(it), next(it), next(it)
    st_ret, st_hgrn, st_gla, s_rw, carry = next(it), next(it), next(it), next(it), next(it)
    qkvb = [[next(it) for _ in range(4)] for _ in range(3)]
    o_gl = [next(it) for _ in range(3)]
    ps_ref, rw_ref, orw_ref = next(it), next(it), next(it)

    r = bb * tc
    ti = pl.program_id(1)

    def vec(i):
        return vec_ref[i:i + 1, :]

    @pl.when(ti == 0)
    def _init():
        for st_ref, s0_ref, dk in ((st_ret, s_ret0, HEAD_DIM), (st_hgrn, s_hgrn0, HEAD_DIM),
                                   (st_gla, s_gla0, GLA_KEY_DIM)):
            st_ref[...] = jnp.zeros(st_ref.shape, F32)
            for b in range(bb):
                for h in range(N_HEADS):
                    st_ref[b, h * HEAD_DIM:(h + 1) * HEAD_DIM, h * dk:(h + 1) * dk] = s0_ref[b, h]
        s_rw[...] = s_rw0[...]
        carry[...] = shift0[...]

    def blk(lo, width):
        return proj_ref[:, :, lo:lo + width].reshape(r, width)

    cos = jnp.concatenate([cos_ref[...]] * bb, axis=0)
    sin = jnp.concatenate([sin_ref[...]] * bb, axis=0)
    lane = lax.broadcasted_iota(jnp.int32, (r, BRANCH), 1)
    first_half = jnp.bitwise_and(lane, HEAD_DIM - 1) < (HEAD_DIM // 2)

    def rope(x):
        swapped = jnp.where(first_half, pltpu.roll(x, BRANCH - HEAD_DIM // 2, 1),
                            pltpu.roll(x, HEAD_DIM // 2, 1))
        return x * cos + swapped * sin

    q_r, k_r, v_r, b_r = qkvb[0]
    q_r[...] = rope(blk(RET_OFF, BRANCH))
    k_r[...] = rope(blk(RET_OFF + 256, BRANCH)) * (HEAD_DIM ** -0.5)
    v_r[...] = blk(RET_OFF + 512, BRANCH)
    b_r[...] = _chunk_cumsum(jnp.broadcast_to(vec(V_LG_RET), (r, BRANCH)), c)

    q_h, k_h, v_h, b_h = qkvb[1]
    q_h[...] = _silu(blk(HGRN_OFF, BRANCH))
    ff = blk(HGRN_OFF + 256, BRANCH)
    y = vec(V_LB_B) + _log_sigmoid(ff)
    a_lb = jnp.broadcast_to(vec(V_LB_A), y.shape)
    mx = jnp.maximum(a_lb, y)
    log_f = mx + jnp.log1p(jnp.exp(jnp.minimum(a_lb, y) - mx))
    k_h[...] = vec(V_LB_OM) * _sigmoid(-ff)
    v_h[...] = blk(HGRN_OFF + 512, BRANCH)
    b_h[...] = _chunk_cumsum(log_f, c)

    q_g, k_g, v_g, b_g = qkvb[2]
    q_g[...] = blk(GLA_OFF, GLA_KEY_WIDTH) * (GLA_KEY_DIM ** -0.5)
    k_g[...] = blk(GLA_OFF + 128, GLA_KEY_WIDTH)
    v_g[...] = blk(GLA_OFF + 256, BRANCH)
    z = _dot(blk(GLA_OFF + 768, 128), wg_ref[...]) + bgate_ref[...]
    b_g[...] = _chunk_cumsum(_log_sigmoid(z) / GLA_TAU, c)

    n_c = tc // c
    mix = ((qkvb[0], o_gl[0], st_ret, bo_ref, mbd_ref), (qkvb[1], o_gl[1], st_hgrn, bo_ref, mbd_ref),
           (qkvb[2], o_gl[2], st_gla, bog_ref, mbdg_ref))

    def seq_body(b, _):
        def chunk_body(ci, _):
            r0 = pl.multiple_of(b * tc + ci * c, c)
            for (q_s, k_s, v_s, b_s), o_s, st_ref, bo, mbd in mix:
                o, st_new = _gl_chunk(q_s[pl.ds(r0, c), :], k_s[pl.ds(r0, c), :], v_s[pl.ds(r0, c), :],
                                      b_s[pl.ds(r0, c), :], st_ref[b], bo[...], mbd[...], c)
                o_s[pl.ds(r0, c), :] = o
                st_ref[b] = st_new
            return 0
        return lax.fori_loop(0, n_c, chunk_body, 0)

    lax.fori_loop(0, bb, seq_body, 0)

    def group_norm(o, eps):
        outs = []
        for oh in _heads(o):
            cen = oh - jnp.mean(oh, axis=-1, keepdims=True)
            outs.append(cen * lax.rsqrt(jnp.mean(cen * cen, axis=-1, keepdims=True) + eps))
        return outs

    def head_rms(o, g):
        return jnp.concatenate(
            [oh * lax.rsqrt(jnp.mean(oh * oh, axis=-1, keepdims=True) + RMS_EPS) * gh
             for oh, gh in zip(_heads(o), _heads(g))], axis=-1)

    def put(lo, val):
        ocat_ref[:, :, lo:lo + BRANCH] = val.reshape(bb, tc, BRANCH)

    put(0, jnp.concatenate(group_norm(o_gl[0][...], RET_GN_EPS), axis=-1) * _silu(blk(RET_OFF + 768, BRANCH)))
    put(256, head_rms(o_gl[1][...], vec(V_HGRN_G)) * _silu(blk(HGRN_OFF + 768, BRANCH)))
    put(512, head_rms(o_gl[2][...], vec(V_GLA_G)) * _silu(blk(GLA_OFF + 512, BRANCH)))

    p_rw = blk(RWKV_OFF, RWKV_PW)
    mu = mu_ref[...]
    ps_ref[...] = p_rw + (pltpu.roll(p_rw, 1, 0) - p_rw) * mu
    for b in range(bb):
        p0 = p_rw[b * tc:b * tc + 1, :]
        ps_ref[b * tc:b * tc + 1, :] = p0 + (carry[b:b + 1, :] - p0) * mu
        carry[b:b + 1, :] = p_rw[(b + 1) * tc - 1:(b + 1) * tc, :]

    rr = ps_ref[:, 0:256]
    k_raw = ps_ref[:, 256:512]
    vv = ps_ref[:, 512:768]
    low_rank = ps_ref[:, 1024:1152]
    w_pre = vec(V_W0) + _dot(jnp.tanh(low_rank), w2_ref[...])
    w_log = -(jnp.maximum(-w_pre, 0.0) + jnp.log1p(jnp.exp(-jnp.abs(w_pre)))) - 0.5
    decay = jnp.exp(-jnp.exp(w_log))
    a = _sigmoid(vec(V_A0) + _dot(low_rank, a2_ref[...]))
    if layer0:
        vfirst_out_ref[...] = vv.reshape(bb, tc, BRANCH)
    else:
        vmix = _sigmoid(vec(V_V0) + _dot(_dot(vv, v1_ref[...]), v2_ref[...]))
        vv = vv + (vfirst_in_ref[...].reshape(r, BRANCH) - vv) * vmix
    kk_raw = k_raw * vec(V_KK)
    kk = jnp.concatenate(
        [kh / jnp.maximum(jnp.sqrt(jnp.sum(kh * kh, axis=-1, keepdims=True)), 1e-12) for kh in _heads(kk_raw)],
        axis=-1)
    kmod = k_raw * (1.0 + (a - 1.0) * vec(V_KA))
    for i, arr in enumerate((kk, decay, kk * a, kmod, rr, vv)):
        for h, ah in enumerate(_heads(arr)):
            rw_ref[i, h] = ah

    eye = (lax.broadcasted_iota(jnp.int32, (HEAD_DIM, HEAD_DIM), 0)
           == lax.broadcasted_iota(jnp.int32, (HEAD_DIM, HEAD_DIM), 1))

    def step(t, _):
        for b in range(bb):
            row = b * tc + t
            for h in range(N_HEADS):
                kk_t, w_t, ka_t, k_t, r_t, v_t = (rw_ref[i, h, pl.ds(row, 1), :] for i in range(6))
                s = s_rw[b, h]
                sa = jnp.sum(s * kk_t, axis=-1, keepdims=True)
                v_col = jnp.sum(jnp.where(eye, v_t, 0.0), axis=-1, keepdims=True)
                s = s * w_t - sa * ka_t + v_col * k_t
                s_rw[b, h] = s
                o_col = jnp.sum(s * r_t, axis=-1, keepdims=True)
                orw_ref[h, pl.ds(row, 1), :] = jnp.sum(jnp.where(eye, o_col, 0.0), axis=0, keepdims=True)
        return 0

    lax.fori_loop(0, tc, step, 0)

    o_heads = []
    for h in range(N_HEADS):
        oh = orw_ref[h]
        cen = oh - jnp.mean(oh, axis=-1, keepdims=True)
        o_heads.append(cen * lax.rsqrt(jnp.mean(cen * cen, axis=-1, keepdims=True) + RWKV_GN_EPS))
    o_rw = jnp.concatenate(o_heads, axis=-1) * vec(V_GN_G) + vec(V_GN_B)
    rk = rr * kmod * vec(V_RK)
    bonus = jnp.concatenate(
        [jnp.broadcast_to(jnp.sum(x, axis=-1, keepdims=True), x.shape) for x in _heads(rk)], axis=-1)
    put(768, (o_rw + bonus * vv) * _silu(ps_ref[:, 768:1024]))

    @pl.when(ti == n_t - 1)
    def _final():
        for st_ref, out_ref, dk in ((st_ret, o_ret_st, HEAD_DIM), (st_hgrn, o_hgrn_st, HEAD_DIM),
                                    (st_gla, o_gla_st, GLA_KEY_DIM)):
            for b in range(bb):
                for h in range(N_HEADS):
                    out_ref[b, h] = st_ref[b, h * HEAD_DIM:(h + 1) * HEAD_DIM, :][:, h * dk:(h + 1) * dk]
        o_rw_st[...] = s_rw[...]


def _mixers(proj, vfirst, cos, sin, states, lp, consts, *, bb, tc, c):
    bsz, seq, _ = proj.shape
    layer0 = vfirst is None
    n_t = seq // tc
    r = bb * tc

    def full(arr):
        nd = arr.ndim
        return pl.BlockSpec(arr.shape, lambda bi, ti, _nd=nd: (0,) * _nd)

    def per_b(arr):
        nd = arr.ndim
        return pl.BlockSpec((bb,) + arr.shape[1:], lambda bi, ti, _nd=nd: (bi,) + (0,) * (_nd - 1))

    def tok(width):
        return pl.BlockSpec((bb, tc, width), lambda bi, ti: (bi, ti, 0))

    ins, specs = [proj], [tok(D_IN_PAD)]
    if not layer0:
        ins.append(vfirst)
        specs.append(tok(BRANCH))
    ins += [cos, sin]
    specs += [pl.BlockSpec((tc, BRANCH), lambda bi, ti: (ti, 0))] * 2
    for s in states:
        ins.append(s)
        specs.append(per_b(s))
    small = [lp['vec'], lp['mu'], lp['bgate'], lp['wg'], lp['w2'], lp['a2']]
    if not layer0:
        small += [lp['v1'], lp['v2']]
    small += list(consts)
    for s in small:
        ins.append(s)
        specs.append(full(s))

    st_shapes = [(bsz, N_HEADS, HEAD_DIM, HEAD_DIM), (bsz, N_HEADS, HEAD_DIM, HEAD_DIM),
                 (bsz, N_HEADS, HEAD_DIM, GLA_KEY_DIM), (bsz, N_HEADS, HEAD_DIM, HEAD_DIM)]
    out_shape = [jax.ShapeDtypeStruct((bsz, seq, D_MODEL), F32)]
    out_specs = [tok(D_MODEL)]
    if layer0:
        out_shape.append(jax.ShapeDtypeStruct((bsz, seq, BRANCH), F32))
        out_specs.append(tok(BRANCH))
    for shp in st_shapes:
        out_shape.append(jax.ShapeDtypeStruct(shp, F32))
        out_specs.append(pl.BlockSpec((bb,) + shp[1:], lambda bi, ti: (bi, 0, 0, 0)))

    scratch = [pltpu.VMEM((bb, BRANCH, BRANCH), F32), pltpu.VMEM((bb, BRANCH, BRANCH), F32),
               pltpu.VMEM((bb, BRANCH, GLA_KEY_WIDTH), F32),
               pltpu.VMEM((bb, N_HEADS, HEAD_DIM, HEAD_DIM), F32), pltpu.VMEM((bb, RWKV_PW), F32)]
    for dkt in (BRANCH, BRANCH, GLA_KEY_WIDTH):
        scratch += [pltpu.VMEM((r, dkt), F32), pltpu.VMEM((r, dkt), F32), pltpu.VMEM((r, BRANCH), F32),
                    pltpu.VMEM((r, dkt), F32)]
    scratch += [pltpu.VMEM((r, BRANCH), F32)] * 3
    scratch += [pltpu.VMEM((r, RWKV_PW), F32), pltpu.VMEM((6, N_HEADS, r, HEAD_DIM), F32),
                pltpu.VMEM((N_HEADS, r, HEAD_DIM), F32)]

    outs = pl.pallas_call(
        functools.partial(_mixers_kernel, bb=bb, tc=tc, c=c, layer0=layer0, n_t=n_t),
        grid=(bsz // bb, n_t),
        in_specs=specs,
        out_specs=out_specs,
        out_shape=out_shape,
        scratch_shapes=scratch,
        compiler_params=pltpu.CompilerParams(dimension_semantics=("arbitrary", "arbitrary"),
                                             vmem_limit_bytes=V7X_VMEM_LIMIT),
        name="mixers",
    )(*ins)
    if layer0:
        return outs[0], outs[1], outs[2:]
    return outs[0], None, outs[1:]


def _out_attn_kernel(x_ref, oc_ref, mk_ref, mv_ref, wout_ref, wq_ref, wo_ref, gx_ref, gf_ref, out_ref,
                     *, bbc, lc, final):
    r = bbc * lc
    x1 = x_ref[...].reshape(r, D_MODEL) + _dot(oc_ref[...].reshape(r, D_MODEL), wout_ref[...])
    q = _dot(_rms(x1, gx_ref[...]), wq_ref[...])
    lane_head = jnp.right_shift(lax.broadcasted_iota(jnp.int32, (lc, BRANCH), 1), 6)
    outs = []
    for b in range(bbc):
        qb = q[b * lc:(b + 1) * lc, :]
        qs = jnp.concatenate([jnp.where(lane_head == h, qb, 0.0) for h in range(N_HEADS)], axis=0)
        s = lax.dot_general(qs.astype(BF16), mk_ref[b].astype(BF16), (((1,), (1,)), ((), ())),
                            preferred_element_type=F32) * (HEAD_DIM ** -0.5)
        e = jnp.exp(s - jnp.max(s, axis=-1, keepdims=True))
        p = e / jnp.sum(e, axis=-1, keepdims=True)
        o4 = _dot(p, mv_ref[b])
        ob = jnp.where(lane_head == 0, o4[0:lc, :], 0.0)
        for h in range(1, N_HEADS):
            ob = ob + jnp.where(lane_head == h, o4[h * lc:(h + 1) * lc, :], 0.0)
        outs.append(ob)
    o = outs[0] if bbc == 1 else jnp.concatenate(outs, axis=0)
    x2 = x1 + _dot(o, wo_ref[...])
    if final:
        x2 = _rms(x2, gf_ref[...])
    out_ref[...] = x2.reshape(bbc, lc, D_MODEL)


def _out_attn(x, ocat, mk, mv, lp, norm_f, *, bbc, lc, final):
    bsz, seq, _ = x.shape

    def tok(width):
        return pl.BlockSpec((bbc, lc, width), lambda bi, li: (bi, li, 0))

    def full(arr):
        nd = arr.ndim
        return pl.BlockSpec(arr.shape, lambda bi, li, _nd=nd: (0,) * _nd)

    mem_spec = pl.BlockSpec((bbc, N_MEM, BRANCH), lambda bi, li: (bi, 0, 0))
    small = [lp['w_out'], lp['wq'], lp['wo'], lp['norm_x'], norm_f]
    return pl.pallas_call(
        functools.partial(_out_attn_kernel, bbc=bbc, lc=lc, final=final),
        grid=(bsz // bbc, seq // lc),
        in_specs=[tok(D_MODEL), tok(D_MODEL), mem_spec, mem_spec] + [full(s) for s in small],
        out_specs=tok(D_MODEL),
        out_shape=jax.ShapeDtypeStruct((bsz, seq, D_MODEL), F32),
        compiler_params=pltpu.CompilerParams(dimension_semantics=("arbitrary", "arbitrary"),
                                             vmem_limit_bytes=V7X_VMEM_LIMIT),
        name="out_attn",
    )(x, ocat, mk, mv, *small)


def _rope_tables(pos0, seq):
    half = HEAD_DIM // 2
    inv = ROPE_BASE ** (-jnp.arange(half, dtype=F32) / half)
    pos = (pos0 + jnp.arange(seq, dtype=jnp.int32)).astype(F32)
    ang = pos[:, None] * inv[None, :]
    cos, sin = jnp.cos(ang), jnp.sin(ang)
    return (jnp.tile(jnp.concatenate([cos, cos], axis=-1), (1, N_HEADS)),
            jnp.tile(jnp.concatenate([-sin, sin], axis=-1), (1, N_HEADS)))


def _rwkv_to_padded(t):
    pad = jnp.zeros(t.shape[:-1] + (RWKV_PW - RWKV_W,), t.dtype)
    return jnp.concatenate([t[..., 0:256], t[..., 288:544], t[..., 544:800], t[..., 832:1088],
                            t[..., 256:288], t[..., 800:832], pad], axis=-1)


def _rwkv_from_padded(t):
    return jnp.concatenate([t[..., 0:256], t[..., 1024:1056], t[..., 256:512], t[..., 512:768],
                            t[..., 1056:1088], t[..., 768:1024]], axis=-1)


def _pad_rows(m, rows, at=0):
    out = jnp.zeros((rows, m.shape[1]), m.dtype)
    return out.at[at:at + m.shape[0]].set(m)


def _layer_params(l, P):
    w = P['w_in'][l]
    gla = w[:, 2048:2832]
    w_pad = jnp.concatenate([
        w[:, 0:2048],
        gla[:, 0:512], gla[:, 528:784], gla[:, 512:528], jnp.zeros((D_MODEL, 112), w.dtype),
        _rwkv_to_padded(w[:, 2832:3920])], axis=1).astype(BF16)

    sm = jax.nn.softmax(P['hgrn_lb_logits'].astype(F32), axis=0)
    lb = (jnp.cumsum(sm, axis=0) - sm[0])[l]
    log_gamma = jnp.log1p(-jnp.exp2(-5.0 - jnp.arange(N_HEADS, dtype=F32)))
    rows = [None] * N_VEC
    rows[V_LG_RET] = jnp.repeat(log_gamma, HEAD_DIM)
    rows[V_LB_A] = jnp.log(jnp.maximum(lb, LB_FLOOR))
    rows[V_LB_B] = jnp.log1p(-lb)
    rows[V_LB_OM] = 1.0 - lb
    rows[V_HGRN_G] = jnp.tile(P['hgrn_norm'][l], N_HEADS)
    rows[V_GLA_G] = jnp.tile(P['gla_norm'][l], N_HEADS)
    rows[V_W0] = P['rwkv_w0'][l]
    rows[V_A0] = P['rwkv_a0'][l]
    rows[V_V0] = P['rwkv_v0'][l - 1] if l > 0 else jnp.zeros((BRANCH,), F32)
    rows[V_KK] = P['rwkv_k_k'][l]
    rows[V_KA] = P['rwkv_k_a'][l]
    rows[V_RK] = P['rwkv_r_k'][l].reshape(BRANCH)
    rows[V_GN_G] = P['rwkv_gn_g'][l]
    rows[V_GN_B] = P['rwkv_gn_b'][l]
    zero = jnp.zeros((BRANCH,), F32)
    vec = jnp.stack([zero if x is None else x.astype(F32) for x in rows], axis=0)

    lp = {
        'w_in': w_pad,
        'norm_mix': P['norm_mix'][l][None, :],
        'vec': vec,
        'mu': _rwkv_to_padded(P['rwkv_mu'][l])[None, :],
        'bgate': P['gla_b_gate'][l][None, :],
        'wg': _pad_rows(P['gla_w_gate2'][l], 128).astype(BF16),
        'w2': _pad_rows(P['rwkv_w2'][l], 128, 0).astype(BF16),
        'a2': _pad_rows(P['rwkv_a2'][l], 128, 32).astype(BF16),
        'w_out': P['w_out'][l].astype(BF16),
        'wq': P['wq_x'][l].astype(BF16),
        'wo': P['wo_x'][l].astype(BF16),
        'norm_x': P['norm_x'][l][None, :],
    }
    if l > 0:
        lp['v1'] = jnp.pad(P['rwkv_v1'][l - 1], ((0, 0), (0, 96))).astype(BF16)
        lp['v2'] = _pad_rows(P['rwkv_v2'][l - 1], 128).astype(BF16)
    return lp


def _consts():
    i256 = jnp.arange(BRANCH) // HEAD_DIM
    i128 = jnp.arange(GLA_KEY_WIDTH) // GLA_KEY_DIM
    bo = (i256[:, None] == i256[None, :])
    bog = (i128[:, None] == i256[None, :])
    return bo.astype(BF16), bog.astype(BF16), bo.astype(F32), bog.T.astype(F32)


def _run_trunk(x, pos0, mem_k, mem_v, init, lps, norm_f, consts, *, bb, tc, c, bbc, lc, tm):
    bsz, seq, _ = x.shape
    cos, sin = _rope_tables(pos0, seq)
    new = {n: [] for n in ('ret', 'hgrn', 'gla', 'rwkv', 'shift')}
    vfirst = None
    for l in range(DEPTH):
        lp = lps[l]
        proj = _in_proj(x.reshape(bsz * seq, D_MODEL), lp['norm_mix'], lp['w_in'], tm)
        proj = proj.reshape(bsz, seq, D_IN_PAD)
        states = [jnp.swapaxes(init['ret'][l], -1, -2), jnp.swapaxes(init['hgrn'][l], -1, -2),
                  jnp.swapaxes(init['gla'][l], -1, -2), init['rwkv'][l], _rwkv_to_padded(init['shift'][l])]
        ocat, vf, st = _mixers(proj, vfirst, cos, sin, states, lp, consts, bb=bb, tc=tc, c=c)
        if l == 0:
            vfirst = vf
        new['ret'].append(jnp.swapaxes(st[0], -1, -2))
        new['hgrn'].append(jnp.swapaxes(st[1], -1, -2))
        new['gla'].append(jnp.swapaxes(st[2], -1, -2))
        new['rwkv'].append(st[3])
        new['shift'].append(_rwkv_from_padded(proj[:, seq - 1, RWKV_OFF:RWKV_OFF + RWKV_PW]))
        x = _out_attn(x, ocat, mem_k[l], mem_v[l], lp, norm_f, bbc=bbc, lc=lc, final=(l == DEPTH - 1))
    return x, {n: jnp.stack(v, axis=0) for n, v in new.items()}


def kernel(x_prompt, x_sample, mem_prompt, state_ret, state_hgrn, state_gla, state_rwkv, state_rwkv_shift,
           cache_mem_k, cache_mem_v, norm_mix, w_in, hgrn_lb_logits, hgrn_norm, gla_w_gate2, gla_b_gate,
           gla_norm, rwkv_mu, rwkv_w0, rwkv_w2, rwkv_a0, rwkv_a2, rwkv_v0, rwkv_v1, rwkv_v2, rwkv_k_k,
           rwkv_k_a, rwkv_r_k, rwkv_gn_g, rwkv_gn_b, w_out, norm_x, wq_x, wo_x, norm_mem, wk_x, wv_x, norm_f):
    P = {'norm_mix': norm_mix, 'w_in': w_in, 'hgrn_lb_logits': hgrn_lb_logits, 'hgrn_norm': hgrn_norm,
         'gla_w_gate2': gla_w_gate2, 'gla_b_gate': gla_b_gate, 'gla_norm': gla_norm,
         'rwkv_mu': rwkv_mu, 'rwkv_w0': rwkv_w0, 'rwkv_w2': rwkv_w2, 'rwkv_a0': rwkv_a0,
         'rwkv_a2': rwkv_a2, 'rwkv_v0': rwkv_v0, 'rwkv_v1': rwkv_v1, 'rwkv_v2': rwkv_v2,
         'rwkv_k_k': rwkv_k_k, 'rwkv_k_a': rwkv_k_a, 'rwkv_r_k': rwkv_r_k,
         'rwkv_gn_g': rwkv_gn_g, 'rwkv_gn_b': rwkv_gn_b, 'w_out': w_out,
         'norm_x': norm_x, 'wq_x': wq_x, 'wo_x': wo_x}
    lps = [_layer_params(l, P) for l in range(DEPTH)]
    consts = _consts()
    norm_f2 = norm_f[None, :]
    bsz, seq, _ = x_prompt.shape
    dbsz, dseq, _ = x_sample.shape

    mem2d = mem_prompt.reshape(bsz * N_MEM, D_MODEL)
    mk_l, mv_l = [], []
    for l in range(DEPTH):
        wkv = jnp.concatenate([wk_x[l], wv_x[l]], axis=1).astype(BF16)
        kv = _in_proj(mem2d, norm_mem[l][None, :], wkv, 256).reshape(bsz, N_MEM, 2 * BRANCH)
        mk_l.append(kv[..., :BRANCH])
        mv_l.append(kv[..., BRANCH:])
    zero_init = {
        'ret': jnp.zeros((DEPTH, bsz, N_HEADS, HEAD_DIM, HEAD_DIM), F32),
        'hgrn': jnp.zeros((DEPTH, bsz, N_HEADS, HEAD_DIM, HEAD_DIM), F32),
        'gla': jnp.zeros((DEPTH, bsz, N_HEADS, GLA_KEY_DIM, HEAD_DIM), F32),
        'rwkv': jnp.zeros((DEPTH, bsz, N_HEADS, HEAD_DIM, HEAD_DIM), F32),
        'shift': jnp.zeros((DEPTH, bsz, RWKV_W), F32),
    }
    y_prompt, sp = _run_trunk(x_prompt, 0, mk_l, mv_l, zero_init, lps, norm_f2, consts,
                              bb=8, tc=64, c=CHUNK, bbc=1, lc=256, tm=256)

    cached = {'ret': state_ret, 'hgrn': state_hgrn, 'gla': state_gla, 'rwkv': state_rwkv,
              'shift': state_rwkv_shift}
    cmk = cache_mem_k.reshape(DEPTH, dbsz, N_MEM, BRANCH)
    cmv = cache_mem_v.reshape(DEPTH, dbsz, N_MEM, BRANCH)
    y_sample, ss = _run_trunk(x_sample, PAST_LEN, cmk, cmv, cached, lps, norm_f2, consts,
                              bb=8, tc=dseq, c=dseq, bbc=16, lc=dseq, tm=256)

    mem_k_p = jnp.stack(mk_l, axis=0).reshape(DEPTH, bsz, N_MEM, N_HEADS, HEAD_DIM)
    mem_v_p = jnp.stack(mv_l, axis=0).reshape(DEPTH, bsz, N_MEM, N_HEADS, HEAD_DIM)
    return (y_prompt, y_sample, sp['ret'], ss['ret'], sp['hgrn'], ss['hgrn'], sp['gla'], ss['gla'],
            sp['rwkv'], ss['rwkv'], sp['shift'], ss['shift'], mem_k_p, mem_v_p)
```

```python
import functools

import jax
import jax.numpy as jnp
from jax import lax
from jax.experimental import pallas as pl
from jax.experimental.pallas import tpu as pltpu

F32 = jnp.float32
BF16 = jnp.bfloat16

D_MODEL = 1024
DEPTH = 2
PAST_LEN = 16384
N_HEADS = 4
HEAD_DIM = 64
BRANCH = N_HEADS * HEAD_DIM
GLA_KEY_DIM = 32
GLA_KEY_WIDTH = N_HEADS * GLA_KEY_DIM
GLA_TAU = 16.0
N_MEM = 256
ROPE_BASE = 10000.0
RMS_EPS = 1e-6
RET_GN_EPS = 1e-5
RWKV_GN_EPS = 64e-5
LB_FLOOR = 1e-20
RWKV_W = 1088
CHUNK = 16

RET_OFF = 0
HGRN_OFF = 1024
GLA_OFF = 2048
RWKV_OFF = 2944
RWKV_PW = 1152
D_IN_PAD = 4096

V7X_VMEM_LIMIT = 56 * 1024 * 1024
NEG_BIG = -1e30

(V_LG_RET, V_LB_A, V_LB_B, V_LB_OM, V_HGRN_G, V_GLA_G, V_W0, V_A0, V_V0, V_KK, V_KA, V_RK,
 V_GN_G, V_GN_B) = range(14)
N_VEC = 16


def _rms(x, g):
    return x * lax.rsqrt(jnp.mean(x * x, axis=-1, keepdims=True) + RMS_EPS) * g


def _sigmoid(x):
    return 1.0 / (1.0 + jnp.exp(-x))


def _silu(x):
    return x * _sigmoid(x)


def _log_sigmoid(x):
    return jnp.minimum(x, 0.0) - jnp.log1p(jnp.exp(-jnp.abs(x)))


def _heads(x):
    return [x[:, h * HEAD_DIM:(h + 1) * HEAD_DIM] for h in range(N_HEADS)]


def _dot(a, b):
    return jnp.dot(a.astype(BF16), b.astype(BF16), preferred_element_type=F32)


def _in_proj_kernel(x_ref, g_ref, w_ref, o_ref):
    o_ref[...] = _dot(_rms(x_ref[...], g_ref[...]), w_ref[...])


def _in_proj(x2d, gain, w_bf16, tm):
    m, d = x2d.shape
    n = w_bf16.shape[1]
    return pl.pallas_call(
        _in_proj_kernel,
        grid=(m // tm,),
        in_specs=[pl.BlockSpec((tm, d), lambda i: (i, 0)),
                  pl.BlockSpec((1, d), lambda i: (0, 0)),
                  pl.BlockSpec((d, n), lambda i: (0, 0))],
        out_specs=pl.BlockSpec((tm, n), lambda i: (i, 0)),
        out_shape=jax.ShapeDtypeStruct((m, n), F32),
        compiler_params=pltpu.CompilerParams(dimension_semantics=("arbitrary",),
                                             vmem_limit_bytes=V7X_VMEM_LIMIT),
        name="in_proj",
    )(x2d, gain, w_bf16)


def _chunk_cumsum(x, c):
    t_local = jnp.bitwise_and(lax.broadcasted_iota(jnp.int32, x.shape, 0), c - 1)
    s = 1
    while s < c:
        x = x + jnp.where(t_local >= s, pltpu.roll(x, s, 0), 0.0)
        s *= 2
    return x


def _gl_chunk(q, k, v, b, st, bo, mbd, c):
    rows = lax.broadcasted_iota(jnp.int32, q.shape, 0)
    tiles = []
    for m in range(c):
        arg = jnp.where(rows >= m, b - b[m:m + 1, :], NEG_BIG)
        tiles.append(q * k[m:m + 1, :] * jnp.exp(arg))
    scores = _dot(jnp.concatenate(tiles, axis=0), bo)
    o = lax.dot_general((q * jnp.exp(b)).astype(BF16), st.astype(BF16),
                        (((1,), (1,)), ((), ())), preferred_element_type=F32)
    for m in range(c):
        o = o + scores[m * c:(m + 1) * c, :] * v[m:m + 1, :]
    b_last = b[c - 1:c, :]
    upd = lax.dot_general(v.astype(BF16), (k * jnp.exp(b_last - b)).astype(BF16),
                          (((0,), (0,)), ((), ())), preferred_element_type=F32)
    return o, st * jnp.exp(b_last) + upd * mbd


def _mixers_kernel(*refs, bb, tc, c, layer0, n_t):
    it = iter(refs)
    proj_ref = next(it)
    vfirst_in_ref = None if layer0 else next(it)
    cos_ref, sin_ref = next(it), next(it)
    s_ret0, s_hgrn0, s_gla0, s_rw0, shift0 = next(it), next(it), next(it), next(it), next(it)
    vec_ref, mu_ref, bgate_ref = next(it), next(it), next(it)
    wg_ref, w2_ref, a2_ref = next(it), next(it), next(it)
    v1_ref, v2_ref = (None, None) if layer0 else (next(it), next(it))
    bo_ref, bog_ref, mbd_ref, mbdg_ref = next(it), next(it), next(it), next(it)
    ocat_ref = next(it)
    vfirst_out_ref = next(it) if layer0 else None
    o_ret_st, o_hgrn_st, o_gla_st, o_rw_st = next(it), next(it), next(it), next(it)
    st_ret, st_hgrn, st_gla, s_rw, carry = next(it), next(it), next(it), next(it), next(it)
    qkvb = [[next(it) for _ in range(4)] for _ in range(3)]
    o_gl = [next(it) for _ in range(3)]
    ps_ref, rw_ref, orw_ref = next(it), next(it), next(it)

    r = bb * tc
    ti = pl.program_id(1)

    def vec(i):
        return vec_ref[i:i + 1, :]

    @pl.when(ti == 0)
    def _init():
        for st_ref, s0_ref, dk in ((st_ret, s_ret0, HEAD_DIM), (st_hgrn, s_hgrn0, HEAD_DIM),
                                   (st_gla, s_gla0, GLA_KEY_DIM)):
            st_ref[...] = jnp.zeros(st_ref.shape, F32)
            for b in range(bb):
                for h in range(N_HEADS):
                    st_ref[b, h * HEAD_DIM:(h + 1) * HEAD_DIM, h * dk:(h + 1) * dk] = s0_ref[b, h]
        for b in range(bb):
            for h in range(N_HEADS):
                s_rw[b * HEAD_DIM:(b + 1) * HEAD_DIM, h * HEAD_DIM:(h + 1) * HEAD_DIM] = s_rw0[b, h]
        carry[...] = shift0[...]

    def blk(lo, width):
        return proj_ref[:, :, lo:lo + width].reshape(r, width)

    cos = jnp.concatenate([cos_ref[...]] * bb, axis=0)
    sin = jnp.concatenate([sin_ref[...]] * bb, axis=0)
    lane = lax.broadcasted_iota(jnp.int32, (r, BRANCH), 1)
    first_half = jnp.bitwise_and(lane, HEAD_DIM - 1) < (HEAD_DIM // 2)

    def rope(x):
        swapped = jnp.where(first_half, pltpu.roll(x, BRANCH - HEAD_DIM // 2, 1),
                            pltpu.roll(x, HEAD_DIM // 2, 1))
        return x * cos + swapped * sin

    q_r, k_r, v_r, b_r = qkvb[0]
    q_r[...] = rope(blk(RET_OFF, BRANCH))
    k_r[...] = rope(blk(RET_OFF + 256, BRANCH)) * (HEAD_DIM ** -0.5)
    v_r[...] = blk(RET_OFF + 512, BRANCH)
    b_r[...] = _chunk_cumsum(jnp.broadcast_to(vec(V_LG_RET), (r, BRANCH)), c)

    q_h, k_h, v_h, b_h = qkvb[1]
    q_h[...] = _silu(blk(HGRN_OFF, BRANCH))
    ff = blk(HGRN_OFF + 256, BRANCH)
    y = vec(V_LB_B) + _log_sigmoid(ff)
    a_lb = jnp.broadcast_to(vec(V_LB_A), y.shape)
    mx = jnp.maximum(a_lb, y)
    log_f = mx + jnp.log1p(jnp.exp(jnp.minimum(a_lb, y) - mx))
    k_h[...] = vec(V_LB_OM) * _sigmoid(-ff)
    v_h[...] = blk(HGRN_OFF + 512, BRANCH)
    b_h[...] = _chunk_cumsum(log_f, c)

    q_g, k_g, v_g, b_g = qkvb[2]
    q_g[...] = blk(GLA_OFF, GLA_KEY_WIDTH) * (GLA_KEY_DIM ** -0.5)
    k_g[...] = blk(GLA_OFF + 128, GLA_KEY_WIDTH)
    v_g[...] = blk(GLA_OFF + 256, BRANCH)
    z = _dot(blk(GLA_OFF + 768, 128), wg_ref[...]) + bgate_ref[...]
    b_g[...] = _chunk_cumsum(_log_sigmoid(z) / GLA_TAU, c)

    n_c = tc // c
    mix = ((qkvb[0], o_gl[0], st_ret, bo_ref, mbd_ref), (qkvb[1], o_gl[1], st_hgrn, bo_ref, mbd_ref),
           (qkvb[2], o_gl[2], st_gla, bog_ref, mbdg_ref))

    def seq_body(b, _):
        def chunk_body(ci, _):
            r0 = pl.multiple_of(b * tc + ci * c, c)
            for (q_s, k_s, v_s, b_s), o_s, st_ref, bo, mbd in mix:
                o, st_new = _gl_chunk(q_s[pl.ds(r0, c), :], k_s[pl.ds(r0, c), :], v_s[pl.ds(r0, c), :],
                                      b_s[pl.ds(r0, c), :], st_ref[b], bo[...], mbd[...], c)
                o_s[pl.ds(r0, c), :] = o
                st_ref[b] = st_new
            return 0
        return lax.fori_loop(0, n_c, chunk_body, 0)

    lax.fori_loop(0, bb, seq_body, 0)

    def group_norm(o, eps):
        outs = []
        for oh in _heads(o):
            cen = oh - jnp.mean(oh, axis=-1, keepdims=True)
            outs.append(cen * lax.rsqrt(jnp.mean(cen * cen, axis=-1, keepdims=True) + eps))
        return outs

    def head_rms(o, g):
        return jnp.concatenate(
            [oh * lax.rsqrt(jnp.mean(oh * oh, axis=-1, keepdims=True) + RMS_EPS) * gh
             for oh, gh in zip(_heads(o), _heads(g))], axis=-1)

    def put(lo, val):
        ocat_ref[:, :, lo:lo + BRANCH] = val.reshape(bb, tc, BRANCH)

    put(0, jnp.concatenate(group_norm(o_gl[0][...], RET_GN_EPS), axis=-1) * _silu(blk(RET_OFF + 768, BRANCH)))
    put(256, head_rms(o_gl[1][...], vec(V_HGRN_G)) * _silu(blk(HGRN_OFF + 768, BRANCH)))
    put(512, head_rms(o_gl[2][...], vec(V_GLA_G)) * _silu(blk(GLA_OFF + 512, BRANCH)))

    p_rw = blk(RWKV_OFF, RWKV_PW)
    mu = mu_ref[...]
    ps_ref[...] = p_rw + (pltpu.roll(p_rw, 1, 0) - p_rw) * mu
    for b in range(bb):
        p0 = p_rw[b * tc:b * tc + 1, :]
        ps_ref[b * tc:b * tc + 1, :] = p0 + (carry[b:b + 1, :] - p0) * mu
        carry[b:b + 1, :] = p_rw[(b + 1) * tc - 1:(b + 1) * tc, :]

    rr = ps_ref[:, 0:256]
    k_raw = ps_ref[:, 256:512]
    vv = ps_ref[:, 512:768]
    low_rank = ps_ref[:, 1024:1152]
    w_pre = vec(V_W0) + _dot(jnp.tanh(low_rank), w2_ref[...])
    w_log = -(jnp.maximum(-w_pre, 0.0) + jnp.log1p(jnp.exp(-jnp.abs(w_pre)))) - 0.5
    decay = jnp.exp(-jnp.exp(w_log))
    a = _sigmoid(vec(V_A0) + _dot(low_rank, a2_ref[...]))
    if layer0:
        vfirst_out_ref[...] = vv.reshape(bb, tc, BRANCH)
    else:
        vmix = _sigmoid(vec(V_V0) + _dot(_dot(vv, v1_ref[...]), v2_ref[...]))
        vv = vv + (vfirst_in_ref[...].reshape(r, BRANCH) - vv) * vmix
    kk_raw = k_raw * vec(V_KK)
    kk = jnp.concatenate(
        [kh / jnp.maximum(jnp.sqrt(jnp.sum(kh * kh, axis=-1, keepdims=True)), 1e-12) for kh in _heads(kk_raw)],
        axis=-1)
    kmod = k_raw * (1.0 + (a - 1.0) * vec(V_KA))
    def head_sum(x):
        return jnp.concatenate(
            [jnp.broadcast_to(jnp.sum(xh, axis=-1, keepdims=True), xh.shape) for xh in _heads(x)], axis=-1)

    ka = kk * a
    for i, arr in enumerate((kk, decay, ka, kmod, decay * rr, vv, head_sum(ka * rr), head_sum(kmod * rr))):
        rw_ref[i] = arr

    diag = (lax.broadcasted_iota(jnp.int32, (HEAD_DIM, BRANCH), 0)
            == jnp.bitwise_and(lax.broadcasted_iota(jnp.int32, (HEAD_DIM, BRANCH), 1), HEAD_DIM - 1))
    bo = bo_ref[...]
    bo2 = jnp.concatenate([bo, bo], axis=0)

    def diag_row(x):
        return jnp.sum(jnp.where(diag, x, 0.0), axis=0, keepdims=True)

    def step(t, _):
        vecs = [[rw_ref[i, pl.ds(b * tc + t, 1), :] for i in range(8)] for b in range(bb)]
        states = [s_rw[b * HEAD_DIM:(b + 1) * HEAD_DIM, :] for b in range(bb)]
        x_sa = jnp.concatenate([states[b] * vecs[b][0] for b in range(bb)], axis=0)
        hi = x_sa.astype(BF16)
        lo = (x_sa - hi.astype(F32)).astype(BF16)
        sa_all = jnp.dot(jnp.concatenate([hi, lo], axis=1), bo2, preferred_element_type=F32)
        rest = _dot(jnp.concatenate([states[b] * vecs[b][4] for b in range(bb)]
                                    + [jnp.where(diag, vecs[b][5], 0.0) for b in range(bb)], axis=0), bo)
        for b in range(bb):
            _, w_t, ka_t, k_t, _, v_t, ar_t, kr_t = vecs[b]
            sa = sa_all[b * HEAD_DIM:(b + 1) * HEAD_DIM, :]
            swr = rest[b * HEAD_DIM:(b + 1) * HEAD_DIM, :]
            v_col = rest[(bb + b) * HEAD_DIM:(bb + b + 1) * HEAD_DIM, :]
            s_rw[b * HEAD_DIM:(b + 1) * HEAD_DIM, :] = states[b] * w_t - sa * ka_t + v_col * k_t
            orw_ref[pl.ds(b * tc + t, 1), :] = diag_row(swr) - diag_row(sa) * ar_t + v_t * kr_t
        return 0

    lax.fori_loop(0, tc, step, 0)

    o_heads = []
    for oh in _heads(orw_ref[...]):
        cen = oh - jnp.mean(oh, axis=-1, keepdims=True)
        o_heads.append(cen * lax.rsqrt(jnp.mean(cen * cen, axis=-1, keepdims=True) + RWKV_GN_EPS))
    o_rw = jnp.concatenate(o_heads, axis=-1) * vec(V_GN_G) + vec(V_GN_B)
    rk = rr * kmod * vec(V_RK)
    bonus = jnp.concatenate(
        [jnp.broadcast_to(jnp.sum(x, axis=-1, keepdims=True), x.shape) for x in _heads(rk)], axis=-1)
    put(768, (o_rw + bonus * vv) * _silu(ps_ref[:, 768:1024]))

    @pl.when(ti == n_t - 1)
    def _final():
        for st_ref, out_ref, dk in ((st_ret, o_ret_st, HEAD_DIM), (st_hgrn, o_hgrn_st, HEAD_DIM),
                                    (st_gla, o_gla_st, GLA_KEY_DIM)):
            for b in range(bb):
                for h in range(N_HEADS):
                    out_ref[b, h] = st_ref[b, h * HEAD_DIM:(h + 1) * HEAD_DIM, :][:, h * dk:(h + 1) * dk]
        for b in range(bb):
            s_b = s_rw[b * HEAD_DIM:(b + 1) * HEAD_DIM, :]
            for h in range(N_HEADS):
                o_rw_st[b, h] = s_b[:, h * HEAD_DIM:(h + 1) * HEAD_DIM]


def _mixers(proj, vfirst, cos, sin, states, lp, consts, *, bb, tc, c):
    bsz, seq, _ = proj.shape
    layer0 = vfirst is None
    n_t = seq // tc
    r = bb * tc

    def full(arr):
        nd = arr.ndim
        return pl.BlockSpec(arr.shape, lambda bi, ti, _nd=nd: (0,) * _nd)

    def per_b(arr):
        nd = arr.ndim
        return pl.BlockSpec((bb,) + arr.shape[1:], lambda bi, ti, _nd=nd: (bi,) + (0,) * (_nd - 1))

    def tok(width):
        return pl.BlockSpec((bb, tc, width), lambda bi, ti: (bi, ti, 0))

    ins, specs = [proj], [tok(D_IN_PAD)]
    if not layer0:
        ins.append(vfirst)
        specs.append(tok(BRANCH))
    ins += [cos, sin]
    specs += [pl.BlockSpec((tc, BRANCH), lambda bi, ti: (ti, 0))] * 2
    for s in states:
        ins.append(s)
        specs.append(per_b(s))
    small = [lp['vec'], lp['mu'], lp['bgate'], lp['wg'], lp['w2'], lp['a2']]
    if not layer0:
        small += [lp['v1'], lp['v2']]
    small += list(consts)
    for s in small:
        ins.append(s)
        specs.append(full(s))

    st_shapes = [(bsz, N_HEADS, HEAD_DIM, HEAD_DIM), (bsz, N_HEADS, HEAD_DIM, HEAD_DIM),
                 (bsz, N_HEADS, HEAD_DIM, GLA_KEY_DIM), (bsz, N_HEADS, HEAD_DIM, HEAD_DIM)]
    out_shape = [jax.ShapeDtypeStruct((bsz, seq, D_MODEL), F32)]
    out_specs = [tok(D_MODEL)]
    if layer0:
        out_shape.append(jax.ShapeDtypeStruct((bsz, seq, BRANCH), F32))
        out_specs.append(tok(BRANCH))
    for shp in st_shapes:
        out_shape.append(jax.ShapeDtypeStruct(shp, F32))
        out_specs.append(pl.BlockSpec((bb,) + shp[1:], lambda bi, ti: (bi, 0, 0, 0)))

    scratch = [pltpu.VMEM((bb, BRANCH, BRANCH), F32), pltpu.VMEM((bb, BRANCH, BRANCH), F32),
               pltpu.VMEM((bb, BRANCH, GLA_KEY_WIDTH), F32),
               pltpu.VMEM((bb * HEAD_DIM, BRANCH), F32), pltpu.VMEM((bb, RWKV_PW), F32)]
    for dkt in (BRANCH, BRANCH, GLA_KEY_WIDTH):
        scratch += [pltpu.VMEM((r, dkt), F32), pltpu.VMEM((r, dkt), F32), pltpu.VMEM((r, BRANCH), F32),
                    pltpu.VMEM((r, dkt), F32)]
    scratch += [pltpu.VMEM((r, BRANCH), F32)] * 3
    scratch += [pltpu.VMEM((r, RWKV_PW), F32), pltpu.VMEM((8, r, BRANCH), F32), pltpu.VMEM((r, BRANCH), F32)]

    outs = pl.pallas_call(
        functools.partial(_mixers_kernel, bb=bb, tc=tc, c=c, layer0=layer0, n_t=n_t),
        grid=(bsz // bb, n_t),
        in_specs=specs,
        out_specs=out_specs,
        out_shape=out_shape,
        scratch_shapes=scratch,
        compiler_params=pltpu.CompilerParams(dimension_semantics=("arbitrary", "arbitrary"),
                                             vmem_limit_bytes=V7X_VMEM_LIMIT),
        name="mixers",
    )(*ins)
    if layer0:
        return outs[0], outs[1], outs[2:]
    return outs[0], None, outs[1:]


def _out_attn_kernel(x_ref, oc_ref, mk_ref, mv_ref, wout_ref, wq_ref, wo_ref, gx_ref, gf_ref, out_ref,
                     *, bbc, lc, final):
    r = bbc * lc
    x1 = x_ref[...].reshape(r, D_MODEL) + _dot(oc_ref[...].reshape(r, D_MODEL), wout_ref[...])
    q = _dot(_rms(x1, gx_ref[...]), wq_ref[...])
    lane_head = jnp.right_shift(lax.broadcasted_iota(jnp.int32, (lc, BRANCH), 1), 6)
    outs = []
    for b in range(bbc):
        qb = q[b * lc:(b + 1) * lc, :]
        qs = jnp.concatenate([jnp.where(lane_head == h, qb, 0.0) for h in range(N_HEADS)], axis=0)
        s = lax.dot_general(qs.astype(BF16), mk_ref[b].astype(BF16), (((1,), (1,)), ((), ())),
                            preferred_element_type=F32) * (HEAD_DIM ** -0.5)
        e = jnp.exp(s - jnp.max(s, axis=-1, keepdims=True))
        p = e / jnp.sum(e, axis=-1, keepdims=True)
        o4 = _dot(p, mv_ref[b])
        ob = jnp.where(lane_head == 0, o4[0:lc, :], 0.0)
        for h in range(1, N_HEADS):
            ob = ob + jnp.where(lane_head == h, o4[h * lc:(h + 1) * lc, :], 0.0)
        outs.append(ob)
    o = outs[0] if bbc == 1 else jnp.concatenate(outs, axis=0)
    x2 = x1 + _dot(o, wo_ref[...])
    if final:
        x2 = _rms(x2, gf_ref[...])
    out_ref[...] = x2.reshape(bbc, lc, D_MODEL)


def _out_attn(x, ocat, mk, mv, lp, norm_f, *, bbc, lc, final):
    bsz, seq, _ = x.shape

    def tok(width):
        return pl.BlockSpec((bbc, lc, width), lambda bi, li: (bi, li, 0))

    def full(arr):
        nd = arr.ndim
        return pl.BlockSpec(arr.shape, lambda bi, li, _nd=nd: (0,) * _nd)

    mem_spec = pl.BlockSpec((bbc, N_MEM, BRANCH), lambda bi, li: (bi, 0, 0))
    small = [lp['w_out'], lp['wq'], lp['wo'], lp['norm_x'], norm_f]
    return pl.pallas_call(
        functools.partial(_out_attn_kernel, bbc=bbc, lc=lc, final=final),
        grid=(bsz // bbc, seq // lc),
        in_specs=[tok(D_MODEL), tok(D_MODEL), mem_spec, mem_spec] + [full(s) for s in small],
        out_specs=tok(D_MODEL),
        out_shape=jax.ShapeDtypeStruct((bsz, seq, D_MODEL), F32),
        compiler_params=pltpu.CompilerParams(dimension_semantics=("arbitrary", "arbitrary"),
                                             vmem_limit_bytes=V7X_VMEM_LIMIT),
        name="out_attn",
    )(x, ocat, mk, mv, *small)


def _rope_tables(pos0, seq):
    half = HEAD_DIM // 2
    inv = ROPE_BASE ** (-jnp.arange(half, dtype=F32) / half)
    pos = (pos0 + jnp.arange(seq, dtype=jnp.int32)).astype(F32)
    ang = pos[:, None] * inv[None, :]
    cos, sin = jnp.cos(ang), jnp.sin(ang)
    return (jnp.tile(jnp.concatenate([cos, cos], axis=-1), (1, N_HEADS)),
            jnp.tile(jnp.concatenate([-sin, sin], axis=-1), (1, N_HEADS)))


def _rwkv_to_padded(t):
    pad = jnp.zeros(t.shape[:-1] + (RWKV_PW - RWKV_W,), t.dtype)
    return jnp.concatenate([t[..., 0:256], t[..., 288:544], t[..., 544:800], t[..., 832:1088],
                            t[..., 256:288], t[..., 800:832], pad], axis=-1)


def _rwkv_from_padded(t):
    return jnp.concatenate([t[..., 0:256], t[..., 1024:1056], t[..., 256:512], t[..., 512:768],
                            t[..., 1056:1088], t[..., 768:1024]], axis=-1)


def _pad_rows(m, rows, at=0):
    out = jnp.zeros((rows, m.shape[1]), m.dtype)
    return out.at[at:at + m.shape[0]].set(m)


def _layer_params(l, P):
    w = P['w_in'][l]
    gla = w[:, 2048:2832]
    w_pad = jnp.concatenate([
        w[:, 0:2048],
        gla[:, 0:512], gla[:, 528:784], gla[:, 512:528], jnp.zeros((D_MODEL, 112), w.dtype),
        _rwkv_to_padded(w[:, 2832:3920])], axis=1).astype(BF16)

    sm = jax.nn.softmax(P['hgrn_lb_logits'].astype(F32), axis=0)
    lb = (jnp.cumsum(sm, axis=0) - sm[0])[l]
    log_gamma = jnp.log1p(-jnp.exp2(-5.0 - jnp.arange(N_HEADS, dtype=F32)))
    rows = [None] * N_VEC
    rows[V_LG_RET] = jnp.repeat(log_gamma, HEAD_DIM)
    rows[V_LB_A] = jnp.log(jnp.maximum(lb, LB_FLOOR))
    rows[V_LB_B] = jnp.log1p(-lb)
    rows[V_LB_OM] = 1.0 - lb
    rows[V_HGRN_G] = jnp.tile(P['hgrn_norm'][l], N_HEADS)
    rows[V_GLA_G] = jnp.tile(P['gla_norm'][l], N_HEADS)
    rows[V_W0] = P['rwkv_w0'][l]
    rows[V_A0] = P['rwkv_a0'][l]
    rows[V_V0] = P['rwkv_v0'][l - 1] if l > 0 else jnp.zeros((BRANCH,), F32)
    rows[V_KK] = P['rwkv_k_k'][l]
    rows[V_KA] = P['rwkv_k_a'][l]
    rows[V_RK] = P['rwkv_r_k'][l].reshape(BRANCH)
    rows[V_GN_G] = P['rwkv_gn_g'][l]
    rows[V_GN_B] = P['rwkv_gn_b'][l]
    zero = jnp.zeros((BRANCH,), F32)
    vec = jnp.stack([zero if x is None else x.astype(F32) for x in rows], axis=0)

    lp = {
        'w_in': w_pad,
        'norm_mix': P['norm_mix'][l][None, :],
        'vec': vec,
        'mu': _rwkv_to_padded(P['rwkv_mu'][l])[None, :],
        'bgate': P['gla_b_gate'][l][None, :],
        'wg': _pad_rows(P['gla_w_gate2'][l], 128).astype(BF16),
        'w2': _pad_rows(P['rwkv_w2'][l], 128, 0).astype(BF16),
        'a2': _pad_rows(P['rwkv_a2'][l], 128, 32).astype(BF16),
        'w_out': P['w_out'][l].astype(BF16),
        'wq': P['wq_x'][l].astype(BF16),
        'wo': P['wo_x'][l].astype(BF16),
        'norm_x': P['norm_x'][l][None, :],
    }
    if l > 0:
        lp['v1'] = jnp.pad(P['rwkv_v1'][l - 1], ((0, 0), (0, 96))).astype(BF16)
        lp['v2'] = _pad_rows(P['rwkv_v2'][l - 1], 128).astype(BF16)
    return lp


def _consts():
    i256 = jnp.arange(BRANCH) // HEAD_DIM
    i128 = jnp.arange(GLA_KEY_WIDTH) // GLA_KEY_DIM
    bo = (i256[:, None] == i256[None, :])
    bog = (i128[:, None] == i256[None, :])
    return bo.astype(BF16), bog.astype(BF16), bo.astype(F32), bog.T.astype(F32)


def _run_trunk(x, pos0, mem_k, mem_v, init, lps, norm_f, consts, *, bb, tc, c, bbc, lc, tm):
    bsz, seq, _ = x.shape
    cos, sin = _rope_tables(pos0, seq)
    new = {n: [] for n in ('ret', 'hgrn', 'gla', 'rwkv', 'shift')}
    vfirst = None
    for l in range(DEPTH):
        lp = lps[l]
        proj = _in_proj(x.reshape(bsz * seq, D_MODEL), lp['norm_mix'], lp['w_in'], tm)
        proj = proj.reshape(bsz, seq, D_IN_PAD)
        states = [jnp.swapaxes(init['ret'][l], -1, -2), jnp.swapaxes(init['hgrn'][l], -1, -2),
                  jnp.swapaxes(init['gla'][l], -1, -2), init['rwkv'][l], _rwkv_to_padded(init['shift'][l])]
        ocat, vf, st = _mixers(proj, vfirst, cos, sin, states, lp, consts, bb=bb, tc=tc, c=c)
        if l == 0:
            vfirst = vf
        new['ret'].append(jnp.swapaxes(st[0], -1, -2))
        new['hgrn'].append(jnp.swapaxes(st[1], -1, -2))
        new['gla'].append(jnp.swapaxes(st[2], -1, -2))
        new['rwkv'].append(st[3])
        new['shift'].append(_rwkv_from_padded(proj[:, seq - 1, RWKV_OFF:RWKV_OFF + RWKV_PW]))
        x = _out_attn(x, ocat, mem_k[l], mem_v[l], lp, norm_f, bbc=bbc, lc=lc, final=(l == DEPTH - 1))
    return x, {n: jnp.stack(v, axis=0) for n, v in new.items()}


def kernel(x_prompt, x_sample, mem_prompt, state_ret, state_hgrn, state_gla, state_rwkv, state_rwkv_shift,
           cache_mem_k, cache_mem_v, norm_mix, w_in, hgrn_lb_logits, hgrn_norm, gla_w_gate2, gla_b_gate,
           gla_norm, rwkv_mu, rwkv_w0, rwkv_w2, rwkv_a0, rwkv_a2, rwkv_v0, rwkv_v1, rwkv_v2, rwkv_k_k,
           rwkv_k_a, rwkv_r_k, rwkv_gn_g, rwkv_gn_b, w_out, norm_x, wq_x, wo_x, norm_mem, wk_x, wv_x, norm_f):
    P = {'norm_mix': norm_mix, 'w_in': w_in, 'hgrn_lb_logits': hgrn_lb_logits, 'hgrn_norm': hgrn_norm,
         'gla_w_gate2': gla_w_gate2, 'gla_b_gate': gla_b_gate, 'gla_norm': gla_norm,
         'rwkv_mu': rwkv_mu, 'rwkv_w0': rwkv_w0, 'rwkv_w2': rwkv_w2, 'rwkv_a0': rwkv_a0,
         'rwkv_a2': rwkv_a2, 'rwkv_v0': rwkv_v0, 'rwkv_v1': rwkv_v1, 'rwkv_v2': rwkv_v2,
         'rwkv_k_k': rwkv_k_k, 'rwkv_k_a': rwkv_k_a, 'rwkv_r_k': rwkv_r_k,
         'rwkv_gn_g': rwkv_gn_g, 'rwkv_gn_b': rwkv_gn_b, 'w_out': w_out,
         'norm_x': norm_x, 'wq_x': wq_x, 'wo_x': wo_x}
    lps = [_layer_params(l, P) for l in range(DEPTH)]
    consts = _consts()
    norm_f2 = norm_f[None, :]
    bsz, seq, _ = x_prompt.shape
    dbsz, dseq, _ = x_sample.shape

    mem2d = mem_prompt.reshape(bsz * N_MEM, D_MODEL)
    mk_l, mv_l = [], []
    for l in range(DEPTH):
        wkv = jnp.concatenate([wk_x[l], wv_x[l]], axis=1).astype(BF16)
        kv = _in_proj(mem2d, norm_mem[l][None, :], wkv, 256).reshape(bsz, N_MEM, 2 * BRANCH)
        mk_l.append(kv[..., :BRANCH])
        mv_l.append(kv[..., BRANCH:])
    zero_init = {
        'ret': jnp.zeros((DEPTH, bsz, N_HEADS, HEAD_DIM, HEAD_DIM), F32),
        'hgrn': jnp.zeros((DEPTH, bsz, N_HEADS, HEAD_DIM, HEAD_DIM), F32),
        'gla': jnp.zeros((DEPTH, bsz, N_HEADS, GLA_KEY_DIM, HEAD_DIM), F32),
        'rwkv': jnp.zeros((DEPTH, bsz, N_HEADS, HEAD_DIM, HEAD_DIM), F32),
        'shift': jnp.zeros((DEPTH, bsz, RWKV_W), F32),
    }
    y_prompt, sp = _run_trunk(x_prompt, 0, mk_l, mv_l, zero_init, lps, norm_f2, consts,
                              bb=8, tc=64, c=CHUNK, bbc=1, lc=256, tm=256)

    cached = {'ret': state_ret, 'hgrn': state_hgrn, 'gla': state_gla, 'rwkv': state_rwkv,
              'shift': state_rwkv_shift}
    cmk = cache_mem_k.reshape(DEPTH, dbsz, N_MEM, BRANCH)
    cmv = cache_mem_v.reshape(DEPTH, dbsz, N_MEM, BRANCH)
    y_sample, ss = _run_trunk(x_sample, PAST_LEN, cmk, cmv, cached, lps, norm_f2, consts,
                              bb=8, tc=dseq, c=dseq, bbc=16, lc=dseq, tm=256)

    mem_k_p = jnp.stack(mk_l, axis=0).reshape(DEPTH, bsz, N_MEM, N_HEADS, HEAD_DIM)
    mem_v_p = jnp.stack(mv_l, axis=0).reshape(DEPTH, bsz, N_MEM, N_HEADS, HEAD_DIM)
    return (y_prompt, y_sample, sp['ret'], ss['ret'], sp['hgrn'], ss['hgrn'], sp['gla'], ss['gla'],
            sp['rwkv'], ss['rwkv'], sp['shift'], ss['shift'], mem_k_p, mem_v_p)
```

```python
import functools

import jax
import jax.numpy as jnp
from jax import lax
from jax.experimental import pallas as pl
from jax.experimental.pallas import tpu as pltpu

F32 = jnp.float32
BF16 = jnp.bfloat16

D_MODEL = 1024
DEPTH = 2
PAST_LEN = 16384
N_HEADS = 4
HEAD_DIM = 64
BRANCH = N_HEADS * HEAD_DIM
GLA_KEY_DIM = 32
GLA_KEY_WIDTH = N_HEADS * GLA_KEY_DIM
GLA_TAU = 16.0
N_MEM = 256
ROPE_BASE = 10000.0
RMS_EPS = 1e-6
RET_GN_EPS = 1e-5
RWKV_GN_EPS = 64e-5
LB_FLOOR = 1e-20
RWKV_W = 1088
CHUNK = 16

RET_OFF = 0
HGRN_OFF = 1024
GLA_OFF = 2048
RWKV_OFF = 2944
RWKV_PW = 1152
D_IN_PAD = 4096

V7X_VMEM_LIMIT = 56 * 1024 * 1024
NEG_BIG = -1e30

(V_LG_RET, V_LB_A, V_LB_B, V_LB_OM, V_HGRN_G, V_GLA_G, V_W0, V_A0, V_V0, V_KK, V_KA, V_RK,
 V_GN_G, V_GN_B) = range(14)
N_VEC = 16


def _rms(x, g):
    return x * lax.rsqrt(jnp.mean(x * x, axis=-1, keepdims=True) + RMS_EPS) * g


def _sigmoid(x):
    return 1.0 / (1.0 + jnp.exp(-x))


def _silu(x):
    return x * _sigmoid(x)


def _log_sigmoid(x):
    return jnp.minimum(x, 0.0) - jnp.log1p(jnp.exp(-jnp.abs(x)))


def _dot(a, b):
    return jnp.dot(a.astype(BF16), b.astype(BF16), preferred_element_type=F32)


def _dot_nt(a, b):
    return lax.dot_general(a.astype(BF16), b.astype(BF16), (((1,), (1,)), ((), ())),
                           preferred_element_type=F32)


def _dot_tn(a, b):
    return lax.dot_general(a.astype(BF16), b.astype(BF16), (((0,), (0,)), ((), ())),
                           preferred_element_type=F32)


def _in_proj_kernel(x_ref, g_ref, w_ref, o_ref):
    o_ref[...] = _dot(_rms(x_ref[...], g_ref[...]), w_ref[...])


def _in_proj(x2d, gain, w_bf16, tm):
    m, d = x2d.shape
    n = w_bf16.shape[1]
    return pl.pallas_call(
        _in_proj_kernel,
        grid=(m // tm,),
        in_specs=[pl.BlockSpec((tm, d), lambda i: (i, 0)),
                  pl.BlockSpec((1, d), lambda i: (0, 0)),
                  pl.BlockSpec((d, n), lambda i: (0, 0))],
        out_specs=pl.BlockSpec((tm, n), lambda i: (i, 0)),
        out_shape=jax.ShapeDtypeStruct((m, n), F32),
        compiler_params=pltpu.CompilerParams(dimension_semantics=("arbitrary",),
                                             vmem_limit_bytes=V7X_VMEM_LIMIT),
        name="in_proj",
    )(x2d, gain, w_bf16)


def _chunk_cumsum(x, c):
    t_local = jnp.bitwise_and(lax.broadcasted_iota(jnp.int32, x.shape, 0), c - 1)
    s = 1
    while s < c:
        x = x + jnp.where(t_local >= s, pltpu.roll(x, s, 0), 0.0)
        s *= 2
    return x


def _gl_chunk(q, k, v, b, st, bo, mbd, c):
    rows = lax.broadcasted_iota(jnp.int32, q.shape, 0)
    tiles = []
    for m in range(c):
        arg = jnp.where(rows >= m, b - b[m:m + 1, :], NEG_BIG)
        tiles.append(q * k[m:m + 1, :] * jnp.exp(arg))
    scores = _dot(jnp.concatenate(tiles, axis=0), bo)
    o = _dot_nt(q * jnp.exp(b), st)
    for m in range(c):
        o = o + scores[m * c:(m + 1) * c, :] * v[m:m + 1, :]
    b_last = b[c - 1:c, :]
    upd = _dot_tn(v, k * jnp.exp(b_last - b))
    return o, st * jnp.exp(b_last) + upd * mbd


def _rwkv_chunk(kkt, rt, khat, ahat, v, p_last, st, bo, mbd, c):
    rows = lax.broadcasted_iota(jnp.int32, kkt.shape, 0)
    t_a, t_k, t_ba, t_bk = [], [], [], []
    for m in range(c):
        kk_m = jnp.where(rows > m, kkt, 0.0)
        r_m = jnp.where(rows >= m, rt, 0.0)
        a_row, k_row = ahat[m:m + 1, :], khat[m:m + 1, :]
        t_a.append(kk_m * a_row)
        t_k.append(kk_m * k_row)
        t_ba.append(r_m * a_row)
        t_bk.append(r_m * k_row)
    n = c * c
    sc = _dot(jnp.concatenate(t_a + t_k + t_ba + t_bk, axis=0), bo)

    def col(kind, m):
        return sc[kind * n + m * c:kind * n + (m + 1) * c, :]

    from_state = _dot_nt(jnp.concatenate([kkt, rt], axis=0), st)
    u, o = from_state[0:c, :], from_state[c:2 * c, :]
    for m in range(c):
        v_m = v[m:m + 1, :]
        u = u + col(1, m) * v_m
        o = o + col(3, m) * v_m
    for s in range(c - 1):
        u = u - col(0, s) * u[s:s + 1, :]
    for m in range(c):
        o = o - col(2, m) * u[m:m + 1, :]
    upd = _dot_tn(jnp.concatenate([v, -u], axis=0),
                  jnp.concatenate([khat * p_last, ahat * p_last], axis=0))
    return o, st * p_last + upd * mbd


def _mixers_kernel(*refs, bb, tc, c, layer0, n_t):
    it = iter(refs)
    proj_ref = next(it)
    vfirst_in_ref = None if layer0 else next(it)
    cos_ref, sin_ref = next(it), next(it)
    s_ret0, s_hgrn0, s_gla0, s_rw0, shift0 = next(it), next(it), next(it), next(it), next(it)
    vec_ref, mu_ref, bgate_ref = next(it), next(it), next(it)
    wg_ref, w2_ref, a2_ref = next(it), next(it), next(it)
    v1_ref, v2_ref = (None, None) if layer0 else (next(it), next(it))
    bo_ref, bog_ref, mbd_ref, mbdg_ref = next(it), next(it), next(it), next(it)
    ocat_ref = next(it)
    vfirst_out_ref = next(it) if layer0 else None
    o_ret_st, o_hgrn_st, o_gla_st, o_rw_st = next(it), next(it), next(it), next(it)
    st_ret, st_hgrn, st_gla, st_rw, carry = next(it), next(it), next(it), next(it), next(it)
    qkvb = [[next(it) for _ in range(4)] for _ in range(3)]
    o_gl = [next(it) for _ in range(3)]
    ps_ref, orw_ref = next(it), next(it)
    rw = [next(it) for _ in range(6)]

    r = bb * tc
    ti = pl.program_id(1)
    state_io = ((st_ret, s_ret0, o_ret_st, HEAD_DIM), (st_hgrn, s_hgrn0, o_hgrn_st, HEAD_DIM),
                (st_gla, s_gla0, o_gla_st, GLA_KEY_DIM), (st_rw, s_rw0, o_rw_st, HEAD_DIM))

    def vec(i):
        return vec_ref[i:i + 1, :]

    @pl.when(ti == 0)
    def _init():
        for st_ref, s0_ref, _, dk in state_io:
            st_ref[...] = jnp.zeros(st_ref.shape, F32)
            for b in range(bb):
                for h in range(N_HEADS):
                    st_ref[b, h * HEAD_DIM:(h + 1) * HEAD_DIM, h * dk:(h + 1) * dk] = s0_ref[b, h]
        carry[...] = shift0[...]

    def blk(lo, width):
        return proj_ref[:, :, lo:lo + width].reshape(r, width)

    bo = bo_ref[...]
    bo2 = jnp.concatenate([bo, bo], axis=0)

    def head_sum(x):
        hi = x.astype(BF16)
        lo = (x - hi.astype(F32)).astype(BF16)
        return jnp.dot(jnp.concatenate([hi, lo], axis=1), bo2, preferred_element_type=F32)

    def head_mean(x):
        return head_sum(x) * (1.0 / HEAD_DIM)

    cos = jnp.concatenate([cos_ref[...]] * bb, axis=0)
    sin = jnp.concatenate([sin_ref[...]] * bb, axis=0)
    lane = lax.broadcasted_iota(jnp.int32, (r, BRANCH), 1)
    first_half = jnp.bitwise_and(lane, HEAD_DIM - 1) < (HEAD_DIM // 2)

    def rope(x):
        swapped = jnp.where(first_half, pltpu.roll(x, BRANCH - HEAD_DIM // 2, 1),
                            pltpu.roll(x, HEAD_DIM // 2, 1))
        return x * cos + swapped * sin

    q_r, k_r, v_r, b_r = qkvb[0]
    q_r[...] = rope(blk(RET_OFF, BRANCH))
    k_r[...] = rope(blk(RET_OFF + 256, BRANCH)) * (HEAD_DIM ** -0.5)
    v_r[...] = blk(RET_OFF + 512, BRANCH)
    b_r[...] = _chunk_cumsum(jnp.broadcast_to(vec(V_LG_RET), (r, BRANCH)), c)

    q_h, k_h, v_h, b_h = qkvb[1]
    q_h[...] = _silu(blk(HGRN_OFF, BRANCH))
    ff = blk(HGRN_OFF + 256, BRANCH)
    y = vec(V_LB_B) + _log_sigmoid(ff)
    a_lb = jnp.broadcast_to(vec(V_LB_A), y.shape)
    mx = jnp.maximum(a_lb, y)
    log_f = mx + jnp.log1p(jnp.exp(jnp.minimum(a_lb, y) - mx))
    k_h[...] = vec(V_LB_OM) * _sigmoid(-ff)
    v_h[...] = blk(HGRN_OFF + 512, BRANCH)
    b_h[...] = _chunk_cumsum(log_f, c)

    q_g, k_g, v_g, b_g = qkvb[2]
    q_g[...] = blk(GLA_OFF, GLA_KEY_WIDTH) * (GLA_KEY_DIM ** -0.5)
    k_g[...] = blk(GLA_OFF + 128, GLA_KEY_WIDTH)
    v_g[...] = blk(GLA_OFF + 256, BRANCH)
    z = _dot(blk(GLA_OFF + 768, 128), wg_ref[...]) + bgate_ref[...]
    b_g[...] = _chunk_cumsum(_log_sigmoid(z) / GLA_TAU, c)

    p_rw = blk(RWKV_OFF, RWKV_PW)
    mu = mu_ref[...]
    ps_ref[...] = p_rw + (pltpu.roll(p_rw, 1, 0) - p_rw) * mu
    for b in range(bb):
        p0 = p_rw[b * tc:b * tc + 1, :]
        ps_ref[b * tc:b * tc + 1, :] = p0 + (carry[b:b + 1, :] - p0) * mu
        carry[b:b + 1, :] = p_rw[(b + 1) * tc - 1:(b + 1) * tc, :]

    rr = ps_ref[:, 0:256]
    k_raw = ps_ref[:, 256:512]
    vv = ps_ref[:, 512:768]
    low_rank = ps_ref[:, 1024:1152]
    w_pre = vec(V_W0) + _dot(jnp.tanh(low_rank), w2_ref[...])
    w_log = -(jnp.maximum(-w_pre, 0.0) + jnp.log1p(jnp.exp(-jnp.abs(w_pre)))) - 0.5
    log_w = -jnp.exp(w_log)
    a = _sigmoid(vec(V_A0) + _dot(low_rank, a2_ref[...]))
    if layer0:
        vfirst_out_ref[...] = vv.reshape(bb, tc, BRANCH)
    else:
        vmix = _sigmoid(vec(V_V0) + _dot(_dot(vv, v1_ref[...]), v2_ref[...]))
        vv = vv + (vfirst_in_ref[...].reshape(r, BRANCH) - vv) * vmix
    kk_raw = k_raw * vec(V_KK)
    kk = kk_raw / jnp.maximum(jnp.sqrt(head_sum(kk_raw * kk_raw)), 1e-12)
    kmod = k_raw * (1.0 + (a - 1.0) * vec(V_KA))
    b_w = _chunk_cumsum(log_w, c)
    inv_p = jnp.exp(-b_w)
    p_w = jnp.exp(b_w)
    for ref, arr in zip(rw, (kk * jnp.exp(b_w - log_w), rr * p_w, kmod * inv_p, kk * a * inv_p, vv, p_w)):
        ref[...] = arr

    n_c = tc // c
    mix = ((qkvb[0], o_gl[0], st_ret, bo_ref, mbd_ref), (qkvb[1], o_gl[1], st_hgrn, bo_ref, mbd_ref),
           (qkvb[2], o_gl[2], st_gla, bog_ref, mbdg_ref))

    def chunk_body(ci, _):
        for b in range(bb):
            r0 = pl.multiple_of(b * tc + ci * c, c)
            rows = pl.ds(r0, c)
            for (q_s, k_s, v_s, b_s), o_s, st_ref, bo_m, mbd in mix:
                o, st_new = _gl_chunk(q_s[rows, :], k_s[rows, :], v_s[rows, :], b_s[rows, :],
                                      st_ref[b], bo_m[...], mbd[...], c)
                o_s[rows, :] = o
                st_ref[b] = st_new
            kkt, rt, khat, ahat, v_w, p_c = (x[rows, :] for x in rw)
            o, st_new = _rwkv_chunk(kkt, rt, khat, ahat, v_w, p_c[c - 1:c, :], st_rw[b], bo_ref[...],
                                    mbd_ref[...], c)
            orw_ref[rows, :] = o
            st_rw[b] = st_new
        return 0

    lax.fori_loop(0, n_c, chunk_body, 0)

    def group_norm(o, eps):
        cen = o - head_mean(o)
        return cen * lax.rsqrt(head_mean(cen * cen) + eps)

    def head_rms(o, g):
        return o * lax.rsqrt(head_mean(o * o) + RMS_EPS) * g

    def put(lo, val):
        ocat_ref[:, :, lo:lo + BRANCH] = val.reshape(bb, tc, BRANCH)

    put(0, group_norm(o_gl[0][...], RET_GN_EPS) * _silu(blk(RET_OFF + 768, BRANCH)))
    put(256, head_rms(o_gl[1][...], vec(V_HGRN_G)) * _silu(blk(HGRN_OFF + 768, BRANCH)))
    put(512, head_rms(o_gl[2][...], vec(V_GLA_G)) * _silu(blk(GLA_OFF + 512, BRANCH)))
    o_rw = group_norm(orw_ref[...], RWKV_GN_EPS) * vec(V_GN_G) + vec(V_GN_B)
    bonus = head_sum(rr * kmod * vec(V_RK))
    put(768, (o_rw + bonus * vv) * _silu(ps_ref[:, 768:1024]))

    @pl.when(ti == n_t - 1)
    def _final():
        for st_ref, _, out_ref, dk in state_io:
            for b in range(bb):
                for h in range(N_HEADS):
                    out_ref[b, h] = st_ref[b, h * HEAD_DIM:(h + 1) * HEAD_DIM, :][:, h * dk:(h + 1) * dk]


def _mixers(proj, vfirst, cos, sin, states, lp, consts, *, bb, tc, c):
    bsz, seq, _ = proj.shape
    layer0 = vfirst is None
    n_t = seq // tc
    r = bb * tc

    def full(arr):
        nd = arr.ndim
        return pl.BlockSpec(arr.shape, lambda bi, ti, _nd=nd: (0,) * _nd)

    def per_b(arr):
        nd = arr.ndim
        return pl.BlockSpec((bb,) + arr.shape[1:], lambda bi, ti, _nd=nd: (bi,) + (0,) * (_nd - 1))

    def tok(width):
        return pl.BlockSpec((bb, tc, width), lambda bi, ti: (bi, ti, 0))

    ins, specs = [proj], [tok(D_IN_PAD)]
    if not layer0:
        ins.append(vfirst)
        specs.append(tok(BRANCH))
    ins += [cos, sin]
    specs += [pl.BlockSpec((tc, BRANCH), lambda bi, ti: (ti, 0))] * 2
    for s in states:
        ins.append(s)
        specs.append(per_b(s))
    small = [lp['vec'], lp['mu'], lp['bgate'], lp['wg'], lp['w2'], lp['a2']]
    if not layer0:
        small += [lp['v1'], lp['v2']]
    small += list(consts)
    for s in small:
        ins.append(s)
        specs.append(full(s))

    st_shapes = [(bsz, N_HEADS, HEAD_DIM, HEAD_DIM), (bsz, N_HEADS, HEAD_DIM, HEAD_DIM),
                 (bsz, N_HEADS, HEAD_DIM, GLA_KEY_DIM), (bsz, N_HEADS, HEAD_DIM, HEAD_DIM)]
    out_shape = [jax.ShapeDtypeStruct((bsz, seq, D_MODEL), F32)]
    out_specs = [tok(D_MODEL)]
    if layer0:
        out_shape.append(jax.ShapeDtypeStruct((bsz, seq, BRANCH), F32))
        out_specs.append(tok(BRANCH))
    for shp in st_shapes:
        out_shape.append(jax.ShapeDtypeStruct(shp, F32))
        out_specs.append(pl.BlockSpec((bb,) + shp[1:], lambda bi, ti: (bi, 0, 0, 0)))

    scratch = [pltpu.VMEM((bb, BRANCH, BRANCH), F32), pltpu.VMEM((bb, BRANCH, BRANCH), F32),
               pltpu.VMEM((bb, BRANCH, GLA_KEY_WIDTH), F32), pltpu.VMEM((bb, BRANCH, BRANCH), F32),
               pltpu.VMEM((bb, RWKV_PW), F32)]
    for dkt in (BRANCH, BRANCH, GLA_KEY_WIDTH):
        scratch += [pltpu.VMEM((r, dkt), F32), pltpu.VMEM((r, dkt), F32), pltpu.VMEM((r, BRANCH), F32),
                    pltpu.VMEM((r, dkt), F32)]
    scratch += [pltpu.VMEM((r, BRANCH), F32)] * 3
    scratch += [pltpu.VMEM((r, RWKV_PW), F32), pltpu.VMEM((r, BRANCH), F32)]
    scratch += [pltpu.VMEM((r, BRANCH), F32)] * 6

    outs = pl.pallas_call(
        functools.partial(_mixers_kernel, bb=bb, tc=tc, c=c, layer0=layer0, n_t=n_t),
        grid=(bsz // bb, n_t),
        in_specs=specs,
        out_specs=out_specs,
        out_shape=out_shape,
        scratch_shapes=scratch,
        compiler_params=pltpu.CompilerParams(dimension_semantics=("arbitrary", "arbitrary"),
                                             vmem_limit_bytes=V7X_VMEM_LIMIT),
        name="mixers",
    )(*ins)
    if layer0:
        return outs[0], outs[1], outs[2:]
    return outs[0], None, outs[1:]


def _out_attn_kernel(x_ref, oc_ref, mk_ref, mv_ref, wout_ref, wq_ref, wo_ref, gx_ref, gf_ref, out_ref,
                     *, bbc, lc, final):
    r = bbc * lc
    x1 = x_ref[...].reshape(r, D_MODEL) + _dot(oc_ref[...].reshape(r, D_MODEL), wout_ref[...])
    q = _dot(_rms(x1, gx_ref[...]), wq_ref[...])
    lane_head = jnp.right_shift(lax.broadcasted_iota(jnp.int32, (lc, BRANCH), 1), 6)
    outs = []
    for b in range(bbc):
        qb = q[b * lc:(b + 1) * lc, :]
        qs = jnp.concatenate([jnp.where(lane_head == h, qb, 0.0) for h in range(N_HEADS)], axis=0)
        s = _dot_nt(qs, mk_ref[b]) * (HEAD_DIM ** -0.5)
        e = jnp.exp(s - jnp.max(s, axis=-1, keepdims=True))
        p = e / jnp.sum(e, axis=-1, keepdims=True)
        o4 = _dot(p, mv_ref[b])
        ob = jnp.where(lane_head == 0, o4[0:lc, :], 0.0)
        for h in range(1, N_HEADS):
            ob = ob + jnp.where(lane_head == h, o4[h * lc:(h + 1) * lc, :], 0.0)
        outs.append(ob)
    o = outs[0] if bbc == 1 else jnp.concatenate(outs, axis=0)
    x2 = x1 + _dot(o, wo_ref[...])
    if final:
        x2 = _rms(x2, gf_ref[...])
    out_ref[...] = x2.reshape(bbc, lc, D_MODEL)


def _out_attn(x, ocat, mem, lp, norm_f, *, bbc, lc, final):
    bsz, seq, _ = x.shape
    mk, mv, mk_spec, mv_spec = mem

    def tok(width):
        return pl.BlockSpec((bbc, lc, width), lambda bi, li: (bi, li, 0))

    def full(arr):
        nd = arr.ndim
        return pl.BlockSpec(arr.shape, lambda bi, li, _nd=nd: (0,) * _nd)

    small = [lp['w_out'], lp['wq'], lp['wo'], lp['norm_x'], norm_f]
    return pl.pallas_call(
        functools.partial(_out_attn_kernel, bbc=bbc, lc=lc, final=final),
        grid=(bsz // bbc, seq // lc),
        in_specs=[tok(D_MODEL), tok(D_MODEL), mk_spec, mv_spec] + [full(s) for s in small],
        out_specs=tok(D_MODEL),
        out_shape=jax.ShapeDtypeStruct((bsz, seq, D_MODEL), F32),
        compiler_params=pltpu.CompilerParams(dimension_semantics=("arbitrary", "arbitrary"),
                                             vmem_limit_bytes=V7X_VMEM_LIMIT),
        name="out_attn",
    )(x, ocat, mk, mv, *small)


def _rope_tables(pos0, seq):
    half = HEAD_DIM // 2
    inv = ROPE_BASE ** (-jnp.arange(half, dtype=F32) / half)
    pos = (pos0 + jnp.arange(seq, dtype=jnp.int32)).astype(F32)
    ang = pos[:, None] * inv[None, :]
    cos, sin = jnp.cos(ang), jnp.sin(ang)
    return (jnp.tile(jnp.concatenate([cos, cos], axis=-1), (1, N_HEADS)),
            jnp.tile(jnp.concatenate([-sin, sin], axis=-1), (1, N_HEADS)))


def _rwkv_to_padded(t):
    pad = jnp.zeros(t.shape[:-1] + (RWKV_PW - RWKV_W,), t.dtype)
    return jnp.concatenate([t[..., 0:256], t[..., 288:544], t[..., 544:800], t[..., 832:1088],
                            t[..., 256:288], t[..., 800:832], pad], axis=-1)


def _rwkv_from_padded(t):
    return jnp.concatenate([t[..., 0:256], t[..., 1024:1056], t[..., 256:512], t[..., 512:768],
                            t[..., 1056:1088], t[..., 768:1024]], axis=-1)


def _pad_rows(m, rows, at=0):
    out = jnp.zeros((rows, m.shape[1]), m.dtype)
    return out.at[at:at + m.shape[0]].set(m)


def _layer_params(l, P):
    w = P['w_in'][l]
    gla = w[:, 2048:2832]
    w_pad = jnp.concatenate([
        w[:, 0:2048],
        gla[:, 0:512], gla[:, 528:784], gla[:, 512:528], jnp.zeros((D_MODEL, 112), w.dtype),
        _rwkv_to_padded(w[:, 2832:3920])], axis=1).astype(BF16)

    sm = jax.nn.softmax(P['hgrn_lb_logits'].astype(F32), axis=0)
    lb = (jnp.cumsum(sm, axis=0) - sm[0])[l]
    log_gamma = jnp.log1p(-jnp.exp2(-5.0 - jnp.arange(N_HEADS, dtype=F32)))
    rows = [None] * N_VEC
    rows[V_LG_RET] = jnp.repeat(log_gamma, HEAD_DIM)
    rows[V_LB_A] = jnp.log(jnp.maximum(lb, LB_FLOOR))
    rows[V_LB_B] = jnp.log1p(-lb)
    rows[V_LB_OM] = 1.0 - lb
    rows[V_HGRN_G] = jnp.tile(P['hgrn_norm'][l], N_HEADS)
    rows[V_GLA_G] = jnp.tile(P['gla_norm'][l], N_HEADS)
    rows[V_W0] = P['rwkv_w0'][l]
    rows[V_A0] = P['rwkv_a0'][l]
    rows[V_V0] = P['rwkv_v0'][l - 1] if l > 0 else jnp.zeros((BRANCH,), F32)
    rows[V_KK] = P['rwkv_k_k'][l]
    rows[V_KA] = P['rwkv_k_a'][l]
    rows[V_RK] = P['rwkv_r_k'][l].reshape(BRANCH)
    rows[V_GN_G] = P['rwkv_gn_g'][l]
    rows[V_GN_B] = P['rwkv_gn_b'][l]
    zero = jnp.zeros((BRANCH,), F32)
    vec = jnp.stack([zero if x is None else x.astype(F32) for x in rows], axis=0)

    lp = {
        'w_in': w_pad,
        'norm_mix': P['norm_mix'][l][None, :],
        'vec': vec,
        'mu': _rwkv_to_padded(P['rwkv_mu'][l])[None, :],
        'bgate': P['gla_b_gate'][l][None, :],
        'wg': _pad_rows(P['gla_w_gate2'][l], 128).astype(BF16),
        'w2': _pad_rows(P['rwkv_w2'][l], 128, 0).astype(BF16),
        'a2': _pad_rows(P['rwkv_a2'][l], 128, 32).astype(BF16),
        'w_out': P['w_out'][l].astype(BF16),
        'wq': P['wq_x'][l].astype(BF16),
        'wo': P['wo_x'][l].astype(BF16),
        'norm_x': P['norm_x'][l][None, :],
    }
    if l > 0:
        lp['v1'] = jnp.pad(P['rwkv_v1'][l - 1], ((0, 0), (0, 96))).astype(BF16)
        lp['v2'] = _pad_rows(P['rwkv_v2'][l - 1], 128).astype(BF16)
    return lp


def _consts():
    i256 = jnp.arange(BRANCH) // HEAD_DIM
    i128 = jnp.arange(GLA_KEY_WIDTH) // GLA_KEY_DIM
    bo = (i256[:, None] == i256[None, :])
    bog = (i128[:, None] == i256[None, :])
    return bo.astype(BF16), bog.astype(BF16), bo.astype(F32), bog.T.astype(F32)


def _run_trunk(x, pos0, mems, init, lps, norm_f, consts, *, bb, tc, c, bbc, lc, tm):
    bsz, seq, _ = x.shape
    cos, sin = _rope_tables(pos0, seq)
    new = {n: [] for n in ('ret', 'hgrn', 'gla', 'rwkv', 'shift')}
    vfirst = None
    for l in range(DEPTH):
        lp = lps[l]
        proj = _in_proj(x.reshape(bsz * seq, D_MODEL), lp['norm_mix'], lp['w_in'], tm)
        proj = proj.reshape(bsz, seq, D_IN_PAD)
        states = [jnp.swapaxes(init['ret'][l], -1, -2), jnp.swapaxes(init['hgrn'][l], -1, -2),
                  jnp.swapaxes(init['gla'][l], -1, -2), init['rwkv'][l], _rwkv_to_padded(init['shift'][l])]
        ocat, vf, st = _mixers(proj, vfirst, cos, sin, states, lp, consts, bb=bb, tc=tc, c=c)
        if l == 0:
            vfirst = vf
        new['ret'].append(jnp.swapaxes(st[0], -1, -2))
        new['hgrn'].append(jnp.swapaxes(st[1], -1, -2))
        new['gla'].append(jnp.swapaxes(st[2], -1, -2))
        new['rwkv'].append(st[3])
        new['shift'].append(_rwkv_from_padded(proj[:, seq - 1, RWKV_OFF:RWKV_OFF + RWKV_PW]))
        x = _out_attn(x, ocat, mems[l], lp, norm_f, bbc=bbc, lc=lc, final=(l == DEPTH - 1))
    return x, {n: jnp.stack(v, axis=0) for n, v in new.items()}


def kernel(x_prompt, x_sample, mem_prompt, state_ret, state_hgrn, state_gla, state_rwkv, state_rwkv_shift,
           cache_mem_k, cache_mem_v, norm_mix, w_in, hgrn_lb_logits, hgrn_norm, gla_w_gate2, gla_b_gate,
           gla_norm, rwkv_mu, rwkv_w0, rwkv_w2, rwkv_a0, rwkv_a2, rwkv_v0, rwkv_v1, rwkv_v2, rwkv_k_k,
           rwkv_k_a, rwkv_r_k, rwkv_gn_g, rwkv_gn_b, w_out, norm_x, wq_x, wo_x, norm_mem, wk_x, wv_x, norm_f):
    P = {'norm_mix': norm_mix, 'w_in': w_in, 'hgrn_lb_logits': hgrn_lb_logits, 'hgrn_norm': hgrn_norm,
         'gla_w_gate2': gla_w_gate2, 'gla_b_gate': gla_b_gate, 'gla_norm': gla_norm,
         'rwkv_mu': rwkv_mu, 'rwkv_w0': rwkv_w0, 'rwkv_w2': rwkv_w2, 'rwkv_a0': rwkv_a0,
         'rwkv_a2': rwkv_a2, 'rwkv_v0': rwkv_v0, 'rwkv_v1': rwkv_v1, 'rwkv_v2': rwkv_v2,
         'rwkv_k_k': rwkv_k_k, 'rwkv_k_a': rwkv_k_a, 'rwkv_r_k': rwkv_r_k,
         'rwkv_gn_g': rwkv_gn_g, 'rwkv_gn_b': rwkv_gn_b, 'w_out': w_out,
         'norm_x': norm_x, 'wq_x': wq_x, 'wo_x': wo_x}
    lps = [_layer_params(l, P) for l in range(DEPTH)]
    consts = _consts()
    norm_f2 = norm_f[None, :]
    bsz, seq, _ = x_prompt.shape
    dbsz, dseq, _ = x_sample.shape
    p_bbc, s_bbc = 1, 16

    mem2d = mem_prompt.reshape(bsz * N_MEM, D_MODEL)
    kv_l, p_mems = [], []
    for l in range(DEPTH):
        wkv = jnp.concatenate([wk_x[l], wv_x[l]], axis=1).astype(BF16)
        kv = _in_proj(mem2d, norm_mem[l][None, :], wkv, 256).reshape(bsz, N_MEM, 2 * BRANCH)
        kv_l.append(kv)
        p_mems.append((kv, kv, pl.BlockSpec((p_bbc, N_MEM, BRANCH), lambda bi, li: (bi, 0, 0)),
                       pl.BlockSpec((p_bbc, N_MEM, BRANCH), lambda bi, li: (bi, 0, 1))))
    zero_init = {
        'ret': jnp.zeros((DEPTH, bsz, N_HEADS, HEAD_DIM, HEAD_DIM), F32),
        'hgrn': jnp.zeros((DEPTH, bsz, N_HEADS, HEAD_DIM, HEAD_DIM), F32),
        'gla': jnp.zeros((DEPTH, bsz, N_HEADS, GLA_KEY_DIM, HEAD_DIM), F32),
        'rwkv': jnp.zeros((DEPTH, bsz, N_HEADS, HEAD_DIM, HEAD_DIM), F32),
        'shift': jnp.zeros((DEPTH, bsz, RWKV_W), F32),
    }
    y_prompt, sp = _run_trunk(x_prompt, 0, p_mems, zero_init, lps, norm_f2, consts,
                              bb=8, tc=64, c=CHUNK, bbc=p_bbc, lc=256, tm=256)

    cached = {'ret': state_ret, 'hgrn': state_hgrn, 'gla': state_gla, 'rwkv': state_rwkv,
              'shift': state_rwkv_shift}
    cmk = cache_mem_k.reshape(DEPTH, dbsz, N_MEM, BRANCH)
    cmv = cache_mem_v.reshape(DEPTH, dbsz, N_MEM, BRANCH)
    s_mems = []
    for l in range(DEPTH):
        spec = pl.BlockSpec((None, s_bbc, N_MEM, BRANCH), lambda bi, li, _l=l: (_l, bi, 0, 0))
        s_mems.append((cmk, cmv, spec, spec))
    y_sample, ss = _run_trunk(x_sample, PAST_LEN, s_mems, cached, lps, norm_f2, consts,
                              bb=8, tc=dseq, c=dseq, bbc=s_bbc, lc=dseq, tm=256)

    kv_all = jnp.stack(kv_l, axis=0)
    mem_k_p = kv_all[..., :BRANCH].reshape(DEPTH, bsz, N_MEM, N_HEADS, HEAD_DIM)
    mem_v_p = kv_all[..., BRANCH:].reshape(DEPTH, bsz, N_MEM, N_HEADS, HEAD_DIM)
    return (y_prompt, y_sample, sp['ret'], ss['ret'], sp['hgrn'], ss['hgrn'], sp['gla'], ss['gla'],
            sp['rwkv'], ss['rwkv'], sp['shift'], ss['shift'], mem_k_p, mem_v_p)
```

```python
import functools

import jax
import jax.numpy as jnp
from jax import lax
from jax.experimental import pallas as pl
from jax.experimental.pallas import tpu as pltpu

F32 = jnp.float32
BF16 = jnp.bfloat16

D_MODEL = 1024
DEPTH = 2
PAST_LEN = 16384
N_HEADS = 4
HEAD_DIM = 64
BRANCH = N_HEADS * HEAD_DIM
GLA_KEY_DIM = 32
GLA_KEY_WIDTH = N_HEADS * GLA_KEY_DIM
GLA_TAU = 16.0
N_MEM = 256
ROPE_BASE = 10000.0
RMS_EPS = 1e-6
RET_GN_EPS = 1e-5
RWKV_GN_EPS = 64e-5
LB_FLOOR = 1e-20
RWKV_W = 1088
CHUNK = 16

RET_OFF = 0
HGRN_OFF = 1024
GLA_OFF = 2048
RWKV_OFF = 2944
RWKV_PW = 1152
D_IN_PAD = 4096

V7X_VMEM_LIMIT = 56 * 1024 * 1024
SUBLANES = 8
NEG_BIG = -1e30

(V_LB_A, V_LB_B, V_LB_OM, V_HGRN_G, V_GLA_G, V_W0, V_A0, V_V0, V_KK, V_KA, V_RK,
 V_GN_G, V_GN_B) = range(13)
N_VEC = 16


def _rms(x, g):
    return x * lax.rsqrt(jnp.mean(x * x, axis=-1, keepdims=True) + RMS_EPS) * g


def _sigmoid(x):
    return 1.0 / (1.0 + jnp.exp(-x))


def _silu(x):
    return x * _sigmoid(x)


def _log_sigmoid(x):
    return jnp.minimum(x, 0.0) - jnp.log1p(jnp.exp(-jnp.abs(x)))


def _dot(a, b):
    return jnp.dot(a.astype(BF16), b.astype(BF16), preferred_element_type=F32)


def _dot_nt(a, b):
    return lax.dot_general(a.astype(BF16), b.astype(BF16), (((1,), (1,)), ((), ())),
                           preferred_element_type=F32)


def _dot_tn(a, b):
    return lax.dot_general(a.astype(BF16), b.astype(BF16), (((0,), (0,)), ((), ())),
                           preferred_element_type=F32)


def _row_tiles(x, n):
    return [x[i * SUBLANES:(i + 1) * SUBLANES, :] for i in range(n)]


def _in_proj_kernel(x_ref, g_ref, w_ref, o_ref):
    o_ref[...] = _dot(_rms(x_ref[...], g_ref[...]), w_ref[...])


def _in_proj(x2d, gain, w_bf16, tm):
    m, d = x2d.shape
    n = w_bf16.shape[1]
    return pl.pallas_call(
        _in_proj_kernel,
        grid=(m // tm,),
        in_specs=[pl.BlockSpec((tm, d), lambda i: (i, 0)),
                  pl.BlockSpec((1, d), lambda i: (0, 0)),
                  pl.BlockSpec((d, n), lambda i: (0, 0))],
        out_specs=pl.BlockSpec((tm, n), lambda i: (i, 0)),
        out_shape=jax.ShapeDtypeStruct((m, n), F32),
        compiler_params=pltpu.CompilerParams(dimension_semantics=("arbitrary",),
                                             vmem_limit_bytes=V7X_VMEM_LIMIT),
        name="in_proj",
    )(x2d, gain, w_bf16)


def _chunk_cumsum(x, c):
    t_local = jnp.bitwise_and(lax.broadcasted_iota(jnp.int32, x.shape, 0), c - 1)
    s = 1
    while s < c:
        x = x + jnp.where(t_local >= s, pltpu.roll(x, s, 0), 0.0)
        s *= 2
    return x


def _gl_tiles(q, k, b, c):
    n_rt = c // SUBLANES
    rows = lax.broadcasted_iota(jnp.int32, (SUBLANES, q.shape[1]), 0)
    q_t, b_t = _row_tiles(q, n_rt), _row_tiles(b, n_rt)
    tiles, where_ = [], {}
    for m in range(c):
        for rt in range(m // SUBLANES, n_rt):
            arg = b_t[rt] - b[m:m + 1, :]
            if rt == m // SUBLANES:
                arg = jnp.where(rows >= m % SUBLANES, arg, NEG_BIG)
            where_[m, rt] = len(tiles) * SUBLANES
            tiles.append(q_t[rt] * k[m:m + 1, :] * jnp.exp(arg))
    return tiles, where_


def _gl_output(scores, o_state, v, where_, c):
    o = _row_tiles(o_state, c // SUBLANES)
    for (m, rt), off in where_.items():
        o[rt] = o[rt] + scores[off:off + SUBLANES, :] * v[m:m + 1, :]
    return jnp.concatenate(o, axis=0)


def _gl_state(k, v, b, st, mbd, c):
    b_last = b[c - 1:c, :]
    return st * jnp.exp(b_last) + _dot_tn(v, k * jnp.exp(b_last - b)) * mbd


def _ret_scores(q, k, dmat):
    lane_head = jnp.right_shift(lax.broadcasted_iota(jnp.int32, q.shape, 1), 6)
    qs = jnp.concatenate([jnp.where(lane_head == h, q, 0.0) for h in range(N_HEADS)], axis=0)
    return _dot_nt(qs, k) * dmat


def _ret_output(o4, o_state, n):
    lane_head = jnp.right_shift(lax.broadcasted_iota(jnp.int32, o_state.shape, 1), 6)
    o = o_state
    for h in range(N_HEADS):
        o = o + jnp.where(lane_head == h, o4[h * n:(h + 1) * n, :], 0.0)
    return o


def _rwkv_tiles(kkt, rt, khat, ahat, c):
    n_rt = c // SUBLANES
    rows = lax.broadcasted_iota(jnp.int32, (SUBLANES, BRANCH), 0)
    kk_t, r_t = _row_tiles(kkt, n_rt), _row_tiles(rt, n_rt)
    tiles, where_ = [], {}
    for m in range(c):
        a_row, k_row = ahat[m:m + 1, :], khat[m:m + 1, :]
        for rt_i in range(m // SUBLANES, n_rt):
            on_diag = rt_i == m // SUBLANES
            r_m = jnp.where(rows >= m % SUBLANES, r_t[rt_i], 0.0) if on_diag else r_t[rt_i]
            pairs = [('ba', r_m, a_row), ('bk', r_m, k_row)]
            if not (on_diag and m % SUBLANES == SUBLANES - 1):
                kk_m = jnp.where(rows > m % SUBLANES, kk_t[rt_i], 0.0) if on_diag else kk_t[rt_i]
                pairs += [('a', kk_m, a_row), ('k', kk_m, k_row)]
            for kind, lhs, row in pairs:
                where_[kind, m, rt_i] = len(tiles) * SUBLANES
                tiles.append(lhs * row)
    return tiles, where_


def _rwkv_solve(sc, from_state, v, where_, c):
    n_rt = c // SUBLANES

    def cols(kind, m):
        return [(rt_i, sc[where_[kind, m, rt_i]:where_[kind, m, rt_i] + SUBLANES, :])
                for rt_i in range(m // SUBLANES, n_rt) if (kind, m, rt_i) in where_]

    u, o = _row_tiles(from_state[0:c, :], n_rt), _row_tiles(from_state[c:2 * c, :], n_rt)
    for m in range(c):
        v_m = v[m:m + 1, :]
        for rt_i, col in cols('k', m):
            u[rt_i] = u[rt_i] + col * v_m
        for rt_i, col in cols('bk', m):
            o[rt_i] = o[rt_i] + col * v_m
    for s in range(c - 1):
        u_s = u[s // SUBLANES][s % SUBLANES:s % SUBLANES + 1, :]
        for rt_i, col in cols('a', s):
            u[rt_i] = u[rt_i] - col * u_s
    u = jnp.concatenate(u, axis=0)
    for m in range(c):
        for rt_i, col in cols('ba', m):
            o[rt_i] = o[rt_i] - col * u[m:m + 1, :]
    return jnp.concatenate(o, axis=0), u


def _rwkv_state(u, v, khat, ahat, p_last, st, mbd):
    upd = _dot_tn(jnp.concatenate([v, -u], axis=0),
                  jnp.concatenate([khat * p_last, ahat * p_last], axis=0))
    return st * p_last + upd * mbd


def _mixers_kernel(*refs, bb, tc, c, layer0, n_t):
    it = iter(refs)
    proj_ref = next(it)
    vfirst_in_ref = None if layer0 else next(it)
    cos_ref, sin_ref = next(it), next(it)
    s_ret0, s_hgrn0, s_gla0, s_rw0, shift0 = next(it), next(it), next(it), next(it), next(it)
    vec_ref, mu_ref, bgate_ref = next(it), next(it), next(it)
    wg_ref, w2_ref, a2_ref = next(it), next(it), next(it)
    v1_ref, v2_ref = (None, None) if layer0 else (next(it), next(it))
    bo_ref, bog_ref, mbd_ref, mbdg_ref = next(it), next(it), next(it), next(it)
    dmat_ref, eq_ref, ek_ref, gblk_ref = next(it), next(it), next(it), next(it)
    ocat_ref = next(it)
    vfirst_out_ref = next(it) if layer0 else None
    o_ret_st, o_hgrn_st, o_gla_st, o_rw_st = next(it), next(it), next(it), next(it)
    st_ret, st_hgrn, st_gla, st_rw, carry = next(it), next(it), next(it), next(it), next(it)
    qkvb = [[next(it) for _ in range(4)] for _ in range(2)]
    o_gl = [next(it) for _ in range(2)]
    ps_ref, orw_ref = next(it), next(it)
    rw = [next(it) for _ in range(6)]

    r = bb * tc
    ti = pl.program_id(1)
    state_io = ((st_ret, s_ret0, o_ret_st, HEAD_DIM), (st_hgrn, s_hgrn0, o_hgrn_st, HEAD_DIM),
                (st_gla, s_gla0, o_gla_st, GLA_KEY_DIM), (st_rw, s_rw0, o_rw_st, HEAD_DIM))

    def vec(i):
        return vec_ref[i:i + 1, :]

    @pl.when(ti == 0)
    def _init():
        for st_ref, s0_ref, _, dk in state_io:
            st_ref[...] = jnp.zeros(st_ref.shape, F32)
            for b in range(bb):
                for h in range(N_HEADS):
                    st_ref[b, h * HEAD_DIM:(h + 1) * HEAD_DIM, h * dk:(h + 1) * dk] = s0_ref[b, h]
        carry[...] = shift0[...]

    def blk(lo, width):
        return proj_ref[:, :, lo:lo + width].reshape(r, width)

    bo = bo_ref[...]
    bo2 = jnp.concatenate([bo, bo], axis=0)

    def head_sum(x):
        hi = x.astype(BF16)
        lo = (x - hi.astype(F32)).astype(BF16)
        return jnp.dot(jnp.concatenate([hi, lo], axis=1), bo2, preferred_element_type=F32)

    def head_mean(x):
        return head_sum(x) * (1.0 / HEAD_DIM)

    cos = jnp.concatenate([cos_ref[...]] * bb, axis=0)
    sin = jnp.concatenate([sin_ref[...]] * bb, axis=0)
    lane = lax.broadcasted_iota(jnp.int32, (r, BRANCH), 1)
    first_half = jnp.bitwise_and(lane, HEAD_DIM - 1) < (HEAD_DIM // 2)

    def rope(x):
        swapped = jnp.where(first_half, pltpu.roll(x, BRANCH - HEAD_DIM // 2, 1),
                            pltpu.roll(x, HEAD_DIM // 2, 1))
        return x * cos + swapped * sin

    q_r = rope(blk(RET_OFF, BRANCH))
    k_r = rope(blk(RET_OFF + 256, BRANCH)) * (HEAD_DIM ** -0.5)
    v_r = blk(RET_OFF + 512, BRANCH)
    seqs = range(bb)
    blocks = [slice(b * tc, (b + 1) * tc) for b in seqs]
    ret_sc = [_ret_scores(q_r[rows, :], k_r[rows, :], dmat_ref[...]) for rows in blocks]
    ret_o4 = [_dot(ret_sc[b], v_r[blocks[b], :]) for b in seqs]
    ret_os = [_dot_nt(q_r[blocks[b], :] * eq_ref[...], st_ret[b]) for b in seqs]
    for b in seqs:
        upd = _dot_tn(v_r[blocks[b], :], k_r[blocks[b], :] * ek_ref[...])
        st_ret[b] = st_ret[b] * gblk_ref[...] + upd * mbd_ref[...]
    o_ret = jnp.concatenate([_ret_output(ret_o4[b], ret_os[b], tc) for b in seqs], axis=0)

    q_h, k_h, v_h, b_h = qkvb[0]
    q_h[...] = _silu(blk(HGRN_OFF, BRANCH))
    ff = blk(HGRN_OFF + 256, BRANCH)
    y = vec(V_LB_B) + _log_sigmoid(ff)
    a_lb = jnp.broadcast_to(vec(V_LB_A), y.shape)
    mx = jnp.maximum(a_lb, y)
    log_f = mx + jnp.log1p(jnp.exp(jnp.minimum(a_lb, y) - mx))
    k_h[...] = vec(V_LB_OM) * _sigmoid(-ff)
    v_h[...] = blk(HGRN_OFF + 512, BRANCH)
    b_h[...] = _chunk_cumsum(log_f, c)

    q_g, k_g, v_g, b_g = qkvb[1]
    q_g[...] = blk(GLA_OFF, GLA_KEY_WIDTH) * (GLA_KEY_DIM ** -0.5)
    k_g[...] = blk(GLA_OFF + 128, GLA_KEY_WIDTH)
    v_g[...] = blk(GLA_OFF + 256, BRANCH)
    z = _dot(blk(GLA_OFF + 768, 128), wg_ref[...]) + bgate_ref[...]
    b_g[...] = _chunk_cumsum(_log_sigmoid(z) / GLA_TAU, c)

    p_rw = blk(RWKV_OFF, RWKV_PW)
    mu = mu_ref[...]
    ps_ref[...] = p_rw + (pltpu.roll(p_rw, 1, 0) - p_rw) * mu
    for b in range(bb):
        p0 = p_rw[b * tc:b * tc + 1, :]
        ps_ref[b * tc:b * tc + 1, :] = p0 + (carry[b:b + 1, :] - p0) * mu
        carry[b:b + 1, :] = p_rw[(b + 1) * tc - 1:(b + 1) * tc, :]

    rr = ps_ref[:, 0:256]
    k_raw = ps_ref[:, 256:512]
    vv = ps_ref[:, 512:768]
    low_rank = ps_ref[:, 1024:1152]
    w_pre = vec(V_W0) + _dot(jnp.tanh(low_rank), w2_ref[...])
    w_log = -(jnp.maximum(-w_pre, 0.0) + jnp.log1p(jnp.exp(-jnp.abs(w_pre)))) - 0.5
    log_w = -jnp.exp(w_log)
    a = _sigmoid(vec(V_A0) + _dot(low_rank, a2_ref[...]))
    if layer0:
        vfirst_out_ref[...] = vv.reshape(bb, tc, BRANCH)
    else:
        vmix = _sigmoid(vec(V_V0) + _dot(_dot(vv, v1_ref[...]), v2_ref[...]))
        vv = vv + (vfirst_in_ref[...].reshape(r, BRANCH) - vv) * vmix
    kk_raw = k_raw * vec(V_KK)
    kk = kk_raw / jnp.maximum(jnp.sqrt(head_sum(kk_raw * kk_raw)), 1e-12)
    kmod = k_raw * (1.0 + (a - 1.0) * vec(V_KA))
    b_w = _chunk_cumsum(log_w, c)
    inv_p = jnp.exp(-b_w)
    p_w = jnp.exp(b_w)
    for ref, arr in zip(rw, (kk * jnp.exp(b_w - log_w), rr * p_w, kmod * inv_p, kk * a * inv_p, vv, p_w)):
        ref[...] = arr

    n_c = tc // c
    mix = ((qkvb[0], o_gl[0], st_hgrn, bo_ref, mbd_ref), (qkvb[1], o_gl[1], st_gla, bog_ref, mbdg_ref))

    def chunk_body(ci, _):
        rows = [pl.ds(pl.multiple_of(b * tc + ci * c, c), c) for b in seqs]

        def all_scores(tiles, bo_m):
            sc = _dot(jnp.concatenate(tiles, axis=0), bo_m)
            n = len(tiles) // bb * SUBLANES
            return [sc[b * n:(b + 1) * n, :] for b in seqs]

        gl = []
        for (q_s, k_s, v_s, b_s), o_s, st_ref, bo_m, mbd in mix:
            ins = [(q_s[rows[b], :], k_s[rows[b], :], v_s[rows[b], :], b_s[rows[b], :]) for b in seqs]
            tiles = []
            for q, k, _, b_c in ins:
                t, where_ = _gl_tiles(q, k, b_c, c)
                tiles += t
            gl.append((ins, all_scores(tiles, bo_m[...]), where_, o_s, st_ref, mbd))
        rw_in = [tuple(x[rows[b], :] for x in rw) for b in seqs]
        tiles = []
        for kkt, rt, khat, ahat, _, _ in rw_in:
            t, rw_where = _rwkv_tiles(kkt, rt, khat, ahat, c)
            tiles += t
        rw_sc = all_scores(tiles, bo)
        gl_os = [[_dot_nt(q * jnp.exp(b_c), st_ref[b]) for b, (q, _, _, b_c) in enumerate(ins)]
                 for ins, _, _, _, st_ref, _ in gl]
        rw_fs = [_dot_nt(jnp.concatenate([rw_in[b][0], rw_in[b][1]], axis=0), st_rw[b]) for b in seqs]
        for ins, _, _, _, st_ref, mbd in gl:
            for b, (_, k, v, b_c) in enumerate(ins):
                st_ref[b] = _gl_state(k, v, b_c, st_ref[b], mbd[...], c)
        for (ins, sc, where_, o_s, _, _), o_state in zip(gl, gl_os):
            for b in seqs:
                o_s[rows[b], :] = _gl_output(sc[b], o_state[b], ins[b][2], where_, c)
        us = []
        for b in seqs:
            o, u = _rwkv_solve(rw_sc[b], rw_fs[b], rw_in[b][4], rw_where, c)
            orw_ref[rows[b], :] = o
            us.append(u)
        for b in seqs:
            _, _, khat, ahat, v_w, p_c = rw_in[b]
            st_rw[b] = _rwkv_state(us[b], v_w, khat, ahat, p_c[c - 1:c, :], st_rw[b], mbd_ref[...])
        return 0

    lax.fori_loop(0, n_c, chunk_body, 0)

    def group_norm(o, eps):
        cen = o - head_mean(o)
        return cen * lax.rsqrt(head_mean(cen * cen) + eps)

    def head_rms(o, g):
        return o * lax.rsqrt(head_mean(o * o) + RMS_EPS) * g

    def put(lo, val):
        ocat_ref[:, :, lo:lo + BRANCH] = val.reshape(bb, tc, BRANCH)

    put(0, group_norm(o_ret, RET_GN_EPS) * _silu(blk(RET_OFF + 768, BRANCH)))
    put(256, head_rms(o_gl[0][...], vec(V_HGRN_G)) * _silu(blk(HGRN_OFF + 768, BRANCH)))
    put(512, head_rms(o_gl[1][...], vec(V_GLA_G)) * _silu(blk(GLA_OFF + 512, BRANCH)))
    o_rw = group_norm(orw_ref[...], RWKV_GN_EPS) * vec(V_GN_G) + vec(V_GN_B)
    bonus = head_sum(rr * kmod * vec(V_RK))
    put(768, (o_rw + bonus * vv) * _silu(ps_ref[:, 768:1024]))

    @pl.when(ti == n_t - 1)
    def _final():
        for st_ref, _, out_ref, dk in state_io:
            for b in range(bb):
                for h in range(N_HEADS):
                    out_ref[b, h] = st_ref[b, h * HEAD_DIM:(h + 1) * HEAD_DIM, :][:, h * dk:(h + 1) * dk]


def _mixers(proj, vfirst, cos, sin, states, lp, consts, *, bb, tc, c):
    bsz, seq, _ = proj.shape
    layer0 = vfirst is None
    n_t = seq // tc
    r = bb * tc

    def full(arr):
        nd = arr.ndim
        return pl.BlockSpec(arr.shape, lambda bi, ti, _nd=nd: (0,) * _nd)

    def per_b(arr):
        nd = arr.ndim
        return pl.BlockSpec((bb,) + arr.shape[1:], lambda bi, ti, _nd=nd: (bi,) + (0,) * (_nd - 1))

    def tok(width):
        return pl.BlockSpec((bb, tc, width), lambda bi, ti: (bi, ti, 0))

    ins, specs = [proj], [tok(D_IN_PAD)]
    if not layer0:
        ins.append(vfirst)
        specs.append(tok(BRANCH))
    ins += [cos, sin]
    specs += [pl.BlockSpec((tc, BRANCH), lambda bi, ti: (ti, 0))] * 2
    for s in states:
        ins.append(s)
        specs.append(per_b(s))
    small = [lp['vec'], lp['mu'], lp['bgate'], lp['wg'], lp['w2'], lp['a2']]
    if not layer0:
        small += [lp['v1'], lp['v2']]
    small += list(consts) + list(_retention_tables(tc))
    for s in small:
        ins.append(s)
        specs.append(full(s))

    st_shapes = [(bsz, N_HEADS, HEAD_DIM, HEAD_DIM), (bsz, N_HEADS, HEAD_DIM, HEAD_DIM),
                 (bsz, N_HEADS, HEAD_DIM, GLA_KEY_DIM), (bsz, N_HEADS, HEAD_DIM, HEAD_DIM)]
    out_shape = [jax.ShapeDtypeStruct((bsz, seq, D_MODEL), F32)]
    out_specs = [tok(D_MODEL)]
    if layer0:
        out_shape.append(jax.ShapeDtypeStruct((bsz, seq, BRANCH), F32))
        out_specs.append(tok(BRANCH))
    for shp in st_shapes:
        out_shape.append(jax.ShapeDtypeStruct(shp, F32))
        out_specs.append(pl.BlockSpec((bb,) + shp[1:], lambda bi, ti: (bi, 0, 0, 0)))

    scratch = [pltpu.VMEM((bb, BRANCH, BRANCH), F32), pltpu.VMEM((bb, BRANCH, BRANCH), F32),
               pltpu.VMEM((bb, BRANCH, GLA_KEY_WIDTH), F32), pltpu.VMEM((bb, BRANCH, BRANCH), F32),
               pltpu.VMEM((bb, RWKV_PW), F32)]
    for dkt in (BRANCH, GLA_KEY_WIDTH):
        scratch += [pltpu.VMEM((r, dkt), F32), pltpu.VMEM((r, dkt), F32), pltpu.VMEM((r, BRANCH), F32),
                    pltpu.VMEM((r, dkt), F32)]
    scratch += [pltpu.VMEM((r, BRANCH), F32)] * 2
    scratch += [pltpu.VMEM((r, RWKV_PW), F32), pltpu.VMEM((r, BRANCH), F32)]
    scratch += [pltpu.VMEM((r, BRANCH), F32)] * 6

    outs = pl.pallas_call(
        functools.partial(_mixers_kernel, bb=bb, tc=tc, c=c, layer0=layer0, n_t=n_t),
        grid=(bsz // bb, n_t),
        in_specs=specs,
        out_specs=out_specs,
        out_shape=out_shape,
        scratch_shapes=scratch,
        compiler_params=pltpu.CompilerParams(dimension_semantics=("arbitrary", "arbitrary"),
                                             vmem_limit_bytes=V7X_VMEM_LIMIT),
        name="mixers",
    )(*ins)
    if layer0:
        return outs[0], outs[1], outs[2:]
    return outs[0], None, outs[1:]


def _out_attn_kernel(x_ref, oc_ref, mk_ref, mv_ref, wout_ref, wq_ref, wo_ref, gx_ref, gf_ref, out_ref,
                     *, bbc, lc, final):
    r = bbc * lc
    x1 = x_ref[...].reshape(r, D_MODEL) + _dot(oc_ref[...].reshape(r, D_MODEL), wout_ref[...])
    q = _dot(_rms(x1, gx_ref[...]), wq_ref[...])
    lane_head = jnp.right_shift(lax.broadcasted_iota(jnp.int32, (lc, BRANCH), 1), 6)
    outs = []
    for b in range(bbc):
        qb = q[b * lc:(b + 1) * lc, :]
        qs = jnp.concatenate([jnp.where(lane_head == h, qb, 0.0) for h in range(N_HEADS)], axis=0)
        s = _dot_nt(qs, mk_ref[b]) * (HEAD_DIM ** -0.5)
        e = jnp.exp(s - jnp.max(s, axis=-1, keepdims=True))
        p = e / jnp.sum(e, axis=-1, keepdims=True)
        o4 = _dot(p, mv_ref[b])
        ob = jnp.where(lane_head == 0, o4[0:lc, :], 0.0)
        for h in range(1, N_HEADS):
            ob = ob + jnp.where(lane_head == h, o4[h * lc:(h + 1) * lc, :], 0.0)
        outs.append(ob)
    o = outs[0] if bbc == 1 else jnp.concatenate(outs, axis=0)
    x2 = x1 + _dot(o, wo_ref[...])
    if final:
        x2 = _rms(x2, gf_ref[...])
    out_ref[...] = x2.reshape(bbc, lc, D_MODEL)


def _out_attn(x, ocat, mem, lp, norm_f, *, bbc, lc, final):
    bsz, seq, _ = x.shape
    mk, mv, mk_spec, mv_spec = mem

    def tok(width):
        return pl.BlockSpec((bbc, lc, width), lambda bi, li: (bi, li, 0))

    def full(arr):
        nd = arr.ndim
        return pl.BlockSpec(arr.shape, lambda bi, li, _nd=nd: (0,) * _nd)

    small = [lp['w_out'], lp['wq'], lp['wo'], lp['norm_x'], norm_f]
    return pl.pallas_call(
        functools.partial(_out_attn_kernel, bbc=bbc, lc=lc, final=final),
        grid=(bsz // bbc, seq // lc),
        in_specs=[tok(D_MODEL), tok(D_MODEL), mk_spec, mv_spec] + [full(s) for s in small],
        out_specs=tok(D_MODEL),
        out_shape=jax.ShapeDtypeStruct((bsz, seq, D_MODEL), F32),
        compiler_params=pltpu.CompilerParams(dimension_semantics=("arbitrary", "arbitrary"),
                                             vmem_limit_bytes=V7X_VMEM_LIMIT),
        name="out_attn",
    )(x, ocat, mk, mv, *small)


def _rope_tables(pos0, seq):
    half = HEAD_DIM // 2
    inv = ROPE_BASE ** (-jnp.arange(half, dtype=F32) / half)
    pos = (pos0 + jnp.arange(seq, dtype=jnp.int32)).astype(F32)
    ang = pos[:, None] * inv[None, :]
    cos, sin = jnp.cos(ang), jnp.sin(ang)
    return (jnp.tile(jnp.concatenate([cos, cos], axis=-1), (1, N_HEADS)),
            jnp.tile(jnp.concatenate([-sin, sin], axis=-1), (1, N_HEADS)))


def _retention_tables(n):
    log_gamma = jnp.log1p(-jnp.exp2(-5.0 - jnp.arange(N_HEADS, dtype=F32)))
    j = jnp.arange(n, dtype=F32)
    diff = j[:, None] - j[None, :]
    dmat = jnp.where(diff[None] >= 0, jnp.exp(diff[None] * log_gamma[:, None, None]), 0.0)
    lg_lanes = jnp.repeat(log_gamma, HEAD_DIM)[None, :]
    e_q = jnp.exp((j[:, None] + 1.0) * lg_lanes)
    e_k = jnp.exp((n - 1.0 - j[:, None]) * lg_lanes)
    g_blk = jnp.exp(float(n) * lg_lanes)
    return dmat.reshape(N_HEADS * n, n), e_q, e_k, g_blk


def _rwkv_to_padded(t):
    pad = jnp.zeros(t.shape[:-1] + (RWKV_PW - RWKV_W,), t.dtype)
    return jnp.concatenate([t[..., 0:256], t[..., 288:544], t[..., 544:800], t[..., 832:1088],
                            t[..., 256:288], t[..., 800:832], pad], axis=-1)


def _rwkv_from_padded(t):
    return jnp.concatenate([t[..., 0:256], t[..., 1024:1056], t[..., 256:512], t[..., 512:768],
                            t[..., 1056:1088], t[..., 768:1024]], axis=-1)


def _pad_rows(m, rows, at=0):
    out = jnp.zeros((rows, m.shape[1]), m.dtype)
    return out.at[at:at + m.shape[0]].set(m)


def _layer_params(l, P):
    w = P['w_in'][l]
    gla = w[:, 2048:2832]
    w_pad = jnp.concatenate([
        w[:, 0:2048],
        gla[:, 0:512], gla[:, 528:784], gla[:, 512:528], jnp.zeros((D_MODEL, 112), w.dtype),
        _rwkv_to_padded(w[:, 2832:3920])], axis=1).astype(BF16)

    sm = jax.nn.softmax(P['hgrn_lb_logits'].astype(F32), axis=0)
    lb = (jnp.cumsum(sm, axis=0) - sm[0])[l]
    rows = [None] * N_VEC
    rows[V_LB_A] = jnp.log(jnp.maximum(lb, LB_FLOOR))
    rows[V_LB_B] = jnp.log1p(-lb)
    rows[V_LB_OM] = 1.0 - lb
    rows[V_HGRN_G] = jnp.tile(P['hgrn_norm'][l], N_HEADS)
    rows[V_GLA_G] = jnp.tile(P['gla_norm'][l], N_HEADS)
    rows[V_W0] = P['rwkv_w0'][l]
    rows[V_A0] = P['rwkv_a0'][l]
    rows[V_V0] = P['rwkv_v0'][l - 1] if l > 0 else jnp.zeros((BRANCH,), F32)
    rows[V_KK] = P['rwkv_k_k'][l]
    rows[V_KA] = P['rwkv_k_a'][l]
    rows[V_RK] = P['rwkv_r_k'][l].reshape(BRANCH)
    rows[V_GN_G] = P['rwkv_gn_g'][l]
    rows[V_GN_B] = P['rwkv_gn_b'][l]
    zero = jnp.zeros((BRANCH,), F32)
    vec = jnp.stack([zero if x is None else x.astype(F32) for x in rows], axis=0)

    lp = {
        'w_in': w_pad,
        'norm_mix': P['norm_mix'][l][None, :],
        'vec': vec,
        'mu': _rwkv_to_padded(P['rwkv_mu'][l])[None, :],
        'bgate': P['gla_b_gate'][l][None, :],
        'wg': _pad_rows(P['gla_w_gate2'][l], 128).astype(BF16),
        'w2': _pad_rows(P['rwkv_w2'][l], 128, 0).astype(BF16),
        'a2': _pad_rows(P['rwkv_a2'][l], 128, 32).astype(BF16),
        'w_out': P['w_out'][l].astype(BF16),
        'wq': P['wq_x'][l].astype(BF16),
        'wo': P['wo_x'][l].astype(BF16),
        'norm_x': P['norm_x'][l][None, :],
    }
    if l > 0:
        lp['v1'] = jnp.pad(P['rwkv_v1'][l - 1], ((0, 0), (0, 96))).astype(BF16)
        lp['v2'] = _pad_rows(P['rwkv_v2'][l - 1], 128).astype(BF16)
    return lp


def _consts():
    i256 = jnp.arange(BRANCH) // HEAD_DIM
    i128 = jnp.arange(GLA_KEY_WIDTH) // GLA_KEY_DIM
    bo = (i256[:, None] == i256[None, :])
    bog = (i128[:, None] == i256[None, :])
    return bo.astype(BF16), bog.astype(BF16), bo.astype(F32), bog.T.astype(F32)


def _run_trunk(x, pos0, mems, init, lps, norm_f, consts, *, bb, tc, c, bbc, lc, tm):
    bsz, seq, _ = x.shape
    cos, sin = _rope_tables(pos0, seq)
    new = {n: [] for n in ('ret', 'hgrn', 'gla', 'rwkv', 'shift')}
    vfirst = None
    for l in range(DEPTH):
        lp = lps[l]
        proj = _in_proj(x.reshape(bsz * seq, D_MODEL), lp['norm_mix'], lp['w_in'], tm)
        proj = proj.reshape(bsz, seq, D_IN_PAD)
        states = [jnp.swapaxes(init['ret'][l], -1, -2), jnp.swapaxes(init['hgrn'][l], -1, -2),
                  jnp.swapaxes(init['gla'][l], -1, -2), init['rwkv'][l], _rwkv_to_padded(init['shift'][l])]
        ocat, vf, st = _mixers(proj, vfirst, cos, sin, states, lp, consts, bb=bb, tc=tc, c=c)
        if l == 0:
            vfirst = vf
        new['ret'].append(jnp.swapaxes(st[0], -1, -2))
        new['hgrn'].append(jnp.swapaxes(st[1], -1, -2))
        new['gla'].append(jnp.swapaxes(st[2], -1, -2))
        new['rwkv'].append(st[3])
        new['shift'].append(_rwkv_from_padded(proj[:, seq - 1, RWKV_OFF:RWKV_OFF + RWKV_PW]))
        x = _out_attn(x, ocat, mems[l], lp, norm_f, bbc=bbc, lc=lc, final=(l == DEPTH - 1))
    return x, {n: jnp.stack(v, axis=0) for n, v in new.items()}


def kernel(x_prompt, x_sample, mem_prompt, state_ret, state_hgrn, state_gla, state_rwkv, state_rwkv_shift,
           cache_mem_k, cache_mem_v, norm_mix, w_in, hgrn_lb_logits, hgrn_norm, gla_w_gate2, gla_b_gate,
           gla_norm, rwkv_mu, rwkv_w0, rwkv_w2, rwkv_a0, rwkv_a2, rwkv_v0, rwkv_v1, rwkv_v2, rwkv_k_k,
           rwkv_k_a, rwkv_r_k, rwkv_gn_g, rwkv_gn_b, w_out, norm_x, wq_x, wo_x, norm_mem, wk_x, wv_x, norm_f):
    P = {'norm_mix': norm_mix, 'w_in': w_in, 'hgrn_lb_logits': hgrn_lb_logits, 'hgrn_norm': hgrn_norm,
         'gla_w_gate2': gla_w_gate2, 'gla_b_gate': gla_b_gate, 'gla_norm': gla_norm,
         'rwkv_mu': rwkv_mu, 'rwkv_w0': rwkv_w0, 'rwkv_w2': rwkv_w2, 'rwkv_a0': rwkv_a0,
         'rwkv_a2': rwkv_a2, 'rwkv_v0': rwkv_v0, 'rwkv_v1': rwkv_v1, 'rwkv_v2': rwkv_v2,
         'rwkv_k_k': rwkv_k_k, 'rwkv_k_a': rwkv_k_a, 'rwkv_r_k': rwkv_r_k,
         'rwkv_gn_g': rwkv_gn_g, 'rwkv_gn_b': rwkv_gn_b, 'w_out': w_out,
         'norm_x': norm_x, 'wq_x': wq_x, 'wo_x': wo_x}
    lps = [_layer_params(l, P) for l in range(DEPTH)]
    consts = _consts()
    norm_f2 = norm_f[None, :]
    bsz, seq, _ = x_prompt.shape
    dbsz, dseq, _ = x_sample.shape
    p_bbc, s_bbc = 1, 16

    mem2d = mem_prompt.reshape(bsz * N_MEM, D_MODEL)
    kv_l, p_mems = [], []
    for l in range(DEPTH):
        wkv = jnp.concatenate([wk_x[l], wv_x[l]], axis=1).astype(BF16)
        kv = _in_proj(mem2d, norm_mem[l][None, :], wkv, 256).reshape(bsz, N_MEM, 2 * BRANCH)
        kv_l.append(kv)
        p_mems.append((kv, kv, pl.BlockSpec((p_bbc, N_MEM, BRANCH), lambda bi, li: (bi, 0, 0)),
                       pl.BlockSpec((p_bbc, N_MEM, BRANCH), lambda bi, li: (bi, 0, 1))))
    zero_init = {
        'ret': jnp.zeros((DEPTH, bsz, N_HEADS, HEAD_DIM, HEAD_DIM), F32),
        'hgrn': jnp.zeros((DEPTH, bsz, N_HEADS, HEAD_DIM, HEAD_DIM), F32),
        'gla': jnp.zeros((DEPTH, bsz, N_HEADS, GLA_KEY_DIM, HEAD_DIM), F32),
        'rwkv': jnp.zeros((DEPTH, bsz, N_HEADS, HEAD_DIM, HEAD_DIM), F32),
        'shift': jnp.zeros((DEPTH, bsz, RWKV_W), F32),
    }
    y_prompt, sp = _run_trunk(x_prompt, 0, p_mems, zero_init, lps, norm_f2, consts,
                              bb=8, tc=64, c=CHUNK, bbc=p_bbc, lc=256, tm=256)

    cached = {'ret': state_ret, 'hgrn': state_hgrn, 'gla': state_gla, 'rwkv': state_rwkv,
              'shift': state_rwkv_shift}
    cmk = cache_mem_k.reshape(DEPTH, dbsz, N_MEM, BRANCH)
    cmv = cache_mem_v.reshape(DEPTH, dbsz, N_MEM, BRANCH)
    s_mems = []
    for l in range(DEPTH):
        spec = pl.BlockSpec((None, s_bbc, N_MEM, BRANCH), lambda bi, li, _l=l: (_l, bi, 0, 0))
        s_mems.append((cmk, cmv, spec, spec))
    y_sample, ss = _run_trunk(x_sample, PAST_LEN, s_mems, cached, lps, norm_f2, consts,
                              bb=8, tc=dseq, c=dseq, bbc=s_bbc, lc=dseq, tm=256)

    kv_all = jnp.stack(kv_l, axis=0)
    mem_k_p = kv_all[..., :BRANCH].reshape(DEPTH, bsz, N_MEM, N_HEADS, HEAD_DIM)
    mem_v_p = kv_all[..., BRANCH:].reshape(DEPTH, bsz, N_MEM, N_HEADS, HEAD_DIM)
    return (y_prompt, y_sample, sp['ret'], ss['ret'], sp['hgrn'], ss['hgrn'], sp['gla'], ss['gla'],
            sp['rwkv'], ss['rwkv'], sp['shift'], ss['shift'], mem_k_p, mem_v_p)
```

```python
import functools

import jax
import jax.numpy as jnp
from jax import lax
from jax.experimental import pallas as pl
from jax.experimental.pallas import tpu as pltpu

F32 = jnp.float32
BF16 = jnp.bfloat16

D_MODEL = 1024
DEPTH = 2
PAST_LEN = 16384
N_HEADS = 4
HEAD_DIM = 64
BRANCH = N_HEADS * HEAD_DIM
GLA_KEY_DIM = 32
GLA_KEY_WIDTH = N_HEADS * GLA_KEY_DIM
GLA_TAU = 16.0
N_MEM = 256
ROPE_BASE = 10000.0
RMS_EPS = 1e-6
RET_GN_EPS = 1e-5
RWKV_GN_EPS = 64e-5
LB_FLOOR = 1e-20
RWKV_W = 1088
CHUNK = 16

RET_OFF = 0
HGRN_OFF = 1024
GLA_OFF = 2048
RWKV_OFF = 2944
RWKV_PW = 1152
D_IN_PAD = 4096

V7X_VMEM_LIMIT = 60 * 1024 * 1024
SUBLANES = 8
NEG_BIG = -1e30

(V_LB_A, V_LB_B, V_LB_OM, V_HGRN_G, V_GLA_G, V_W0, V_A0, V_V0, V_KK, V_KA, V_RK,
 V_GN_G, V_GN_B) = range(13)
N_VEC = 16


def _rms(x, g):
    return x * lax.rsqrt(jnp.mean(x * x, axis=-1, keepdims=True) + RMS_EPS) * g


def _sigmoid(x):
    return 0.5 * jnp.tanh(0.5 * x) + 0.5


def _silu(x):
    return x * _sigmoid(x)


def _log1p_exp(x):
    return jnp.log(1.0 + jnp.exp(x))


def _log_sigmoid(x):
    return jnp.minimum(x, 0.0) - _log1p_exp(-jnp.abs(x))


def _dot(a, b):
    return jnp.dot(a.astype(BF16), b.astype(BF16), preferred_element_type=F32)


def _dot_nt(a, b):
    return lax.dot_general(a.astype(BF16), b.astype(BF16), (((1,), (1,)), ((), ())),
                           preferred_element_type=F32)


def _dot_tn(a, b):
    return lax.dot_general(a.astype(BF16), b.astype(BF16), (((0,), (0,)), ((), ())),
                           preferred_element_type=F32)


def _row_tiles(x, n):
    return [x[i * SUBLANES:(i + 1) * SUBLANES, :] for i in range(n)]


def _in_proj_kernel(x_ref, g_ref, w_ref, o_ref):
    o_ref[...] = _dot(_rms(x_ref[...], g_ref[...]), w_ref[...])


def _in_proj(x2d, gain, w_bf16, tm):
    m, d = x2d.shape
    n = w_bf16.shape[1]
    return pl.pallas_call(
        _in_proj_kernel,
        grid=(m // tm,),
        in_specs=[pl.BlockSpec((tm, d), lambda i: (i, 0)),
                  pl.BlockSpec((1, d), lambda i: (0, 0)),
                  pl.BlockSpec((d, n), lambda i: (0, 0))],
        out_specs=pl.BlockSpec((tm, n), lambda i: (i, 0)),
        out_shape=jax.ShapeDtypeStruct((m, n), F32),
        compiler_params=pltpu.CompilerParams(dimension_semantics=("arbitrary",),
                                             vmem_limit_bytes=V7X_VMEM_LIMIT),
        name="in_proj",
    )(x2d, gain, w_bf16)


def _chunk_cumsum(x, c):
    t_local = jnp.bitwise_and(lax.broadcasted_iota(jnp.int32, x.shape, 0), c - 1)
    s = 1
    while s < c:
        x = x + jnp.where(t_local >= s, pltpu.roll(x, s, 0), 0.0)
        s *= 2
    return x


def _gl_tiles(q, k, b, c):
    n_rt = c // SUBLANES
    rows = lax.broadcasted_iota(jnp.int32, (SUBLANES, q.shape[1]), 0)
    q_t, b_t = _row_tiles(q, n_rt), _row_tiles(b, n_rt)
    tiles, where_ = [], {}
    for m in range(c):
        for rt in range(m // SUBLANES, n_rt):
            arg = b_t[rt] - b[m:m + 1, :]
            if rt == m // SUBLANES:
                arg = jnp.where(rows >= m % SUBLANES, arg, NEG_BIG)
            where_[m, rt] = len(tiles) * SUBLANES
            tiles.append(q_t[rt] * k[m:m + 1, :] * jnp.exp(arg))
    return tiles, where_


def _gl_output(scores, o_state, v, where_, c):
    o = _row_tiles(o_state, c // SUBLANES)
    for (m, rt), off in where_.items():
        o[rt] = o[rt] + scores[off:off + SUBLANES, :] * v[m:m + 1, :]
    return jnp.concatenate(o, axis=0)


def _gl_state(k, v, b, st, mbd, c):
    b_last = b[c - 1:c, :]
    return st * jnp.exp(b_last) + _dot_tn(v, k * jnp.exp(b_last - b)) * mbd


def _head_stack(x):
    lane_head = jnp.right_shift(lax.broadcasted_iota(jnp.int32, x.shape, 1), 6)
    return jnp.concatenate([jnp.where(lane_head == h, x, 0.0) for h in range(N_HEADS)], axis=0)


def _head_unstack(x4, n):
    lane_head = jnp.right_shift(lax.broadcasted_iota(jnp.int32, (n, BRANCH), 1), 6)
    out = jnp.where(lane_head == 0, x4[0:n, :], 0.0)
    for h in range(1, N_HEADS):
        out = out + jnp.where(lane_head == h, x4[h * n:(h + 1) * n, :], 0.0)
    return out


def _rwkv_tiles(kkt, ahat, c):
    n_rt = c // SUBLANES
    rows = lax.broadcasted_iota(jnp.int32, (SUBLANES, BRANCH), 0)
    kk_t = _row_tiles(kkt, n_rt)
    tiles, where_ = [], {}
    for m in range(c - 1):
        for rt_i in range(m // SUBLANES, n_rt):
            on_diag = rt_i == m // SUBLANES
            if on_diag and m % SUBLANES == SUBLANES - 1:
                continue
            kk_m = jnp.where(rows > m % SUBLANES, kk_t[rt_i], 0.0) if on_diag else kk_t[rt_i]
            where_[m, rt_i] = len(tiles) * SUBLANES
            tiles.append(kk_m * ahat[m:m + 1, :])
    return tiles, where_


def _rwkv_compact(kkt, rt, khat, ahat):
    return _dot_nt(_head_stack(jnp.concatenate([kkt, rt], axis=0)),
                   jnp.concatenate([khat, ahat], axis=0))


def _rwkv_masks(sc, c):
    row = lax.broadcasted_iota(jnp.int32, (2 * N_HEADS * c, c), 0)
    col = lax.broadcasted_iota(jnp.int32, (2 * N_HEADS * c, c), 1)
    keep = col < jnp.bitwise_and(row, c - 1) + jnp.where(jnp.bitwise_and(row, c) != 0, 1, 0)
    r_a = jnp.concatenate([sc[(2 * h + 1) * c:(2 * h + 2) * c, c:2 * c] for h in range(N_HEADS)], axis=0)
    row4 = lax.broadcasted_iota(jnp.int32, (N_HEADS * c, c), 0)
    col4 = lax.broadcasted_iota(jnp.int32, (N_HEADS * c, c), 1)
    return jnp.where(keep, sc[:, 0:c], 0.0), jnp.where(col4 <= jnp.bitwise_and(row4, c - 1), r_a, 0.0)


def _rwkv_solve(sc_a, rhs, where_, c):
    n_rt = c // SUBLANES
    u = _row_tiles(rhs, n_rt)
    for s in range(c - 1):
        u_s = u[s // SUBLANES][s % SUBLANES:s % SUBLANES + 1, :]
        for rt_i in range(s // SUBLANES, n_rt):
            if (s, rt_i) in where_:
                off = where_[s, rt_i]
                u[rt_i] = u[rt_i] - sc_a[off:off + SUBLANES, :] * u_s
    return jnp.concatenate(u, axis=0)


def _rwkv_state(u, v, khat, ahat, p_last, st, mbd):
    upd = _dot_tn(jnp.concatenate([v, -u], axis=0),
                  jnp.concatenate([khat * p_last, ahat * p_last], axis=0))
    return st * p_last + upd * mbd


def _mixers_kernel(*refs, bb, tc, c, layer0, n_t):
    it = iter(refs)
    proj_ref = next(it)
    vfirst_in_ref = None if layer0 else next(it)
    cos_ref, sin_ref = next(it), next(it)
    s_ret0, s_hgrn0, s_gla0, s_rw0, shift0 = next(it), next(it), next(it), next(it), next(it)
    vec_ref, mu_ref, bgate_ref = next(it), next(it), next(it)
    wg_ref, w2_ref, a2_ref = next(it), next(it), next(it)
    v1_ref, v2_ref = (None, None) if layer0 else (next(it), next(it))
    bo_ref, bog_ref, mbd_ref, mbdg_ref = next(it), next(it), next(it), next(it)
    dmat_ref, eq_ref, ek_ref, gblk_ref = next(it), next(it), next(it), next(it)
    ocat_ref = next(it)
    vfirst_out_ref = next(it) if layer0 else None
    o_ret_st, o_hgrn_st, o_gla_st, o_rw_st = next(it), next(it), next(it), next(it)
    st_ret, st_hgrn, st_gla, st_rw, carry = next(it), next(it), next(it), next(it), next(it)
    qkvb = [[next(it) for _ in range(4)] for _ in range(2)]
    o_gl = [next(it) for _ in range(2)]
    ps_ref, orw_ref = next(it), next(it)
    rw = [next(it) for _ in range(6)]

    r = bb * tc
    ti = pl.program_id(1)
    state_io = ((st_ret, s_ret0, o_ret_st, HEAD_DIM), (st_hgrn, s_hgrn0, o_hgrn_st, HEAD_DIM),
                (st_gla, s_gla0, o_gla_st, GLA_KEY_DIM), (st_rw, s_rw0, o_rw_st, HEAD_DIM))

    def vec(i):
        return vec_ref[i:i + 1, :]

    @pl.when(ti == 0)
    def _init():
        for st_ref, s0_ref, _, dk in state_io:
            st_ref[...] = jnp.zeros(st_ref.shape, F32)
            for b in range(bb):
                for h in range(N_HEADS):
                    st_ref[b, h * HEAD_DIM:(h + 1) * HEAD_DIM, h * dk:(h + 1) * dk] = s0_ref[b, h]
        carry[...] = shift0[...]

    def blk(lo, width):
        return proj_ref[:, :, lo:lo + width].reshape(r, width)

    bo = bo_ref[...]
    bo2 = jnp.concatenate([bo, bo], axis=0)

    def head_sum(x):
        hi = x.astype(BF16)
        lo = (x - hi.astype(F32)).astype(BF16)
        return jnp.dot(jnp.concatenate([hi, lo], axis=1), bo2, preferred_element_type=F32)

    def head_mean(x):
        return head_sum(x) * (1.0 / HEAD_DIM)

    cos = jnp.concatenate([cos_ref[...]] * bb, axis=0)
    sin = jnp.concatenate([sin_ref[...]] * bb, axis=0)
    lane = lax.broadcasted_iota(jnp.int32, (r, BRANCH), 1)
    first_half = jnp.bitwise_and(lane, HEAD_DIM - 1) < (HEAD_DIM // 2)

    def rope(x):
        swapped = jnp.where(first_half, pltpu.roll(x, BRANCH - HEAD_DIM // 2, 1),
                            pltpu.roll(x, HEAD_DIM // 2, 1))
        return x * cos + swapped * sin

    q_r = rope(blk(RET_OFF, BRANCH))
    k_r = rope(blk(RET_OFF + 256, BRANCH)) * (HEAD_DIM ** -0.5)
    v_r = blk(RET_OFF + 512, BRANCH)
    seqs = range(bb)
    blocks = [slice(b * tc, (b + 1) * tc) for b in seqs]
    ret_sc = [_dot_nt(_head_stack(q_r[rows, :]), k_r[rows, :]) * dmat_ref[...] for rows in blocks]
    ret_o4 = [_dot(ret_sc[b], v_r[blocks[b], :]) for b in seqs]
    ret_os = [_dot_nt(q_r[blocks[b], :] * eq_ref[...], st_ret[b]) for b in seqs]
    for b in seqs:
        upd = _dot_tn(v_r[blocks[b], :], k_r[blocks[b], :] * ek_ref[...])
        st_ret[b] = st_ret[b] * gblk_ref[...] + upd * mbd_ref[...]
    o_ret = jnp.concatenate([ret_os[b] + _head_unstack(ret_o4[b], tc) for b in seqs], axis=0)

    q_h, k_h, v_h, b_h = qkvb[0]
    q_h[...] = _silu(blk(HGRN_OFF, BRANCH))
    ff = blk(HGRN_OFF + 256, BRANCH)
    y = vec(V_LB_B) + _log_sigmoid(ff)
    a_lb = jnp.broadcast_to(vec(V_LB_A), y.shape)
    mx = jnp.maximum(a_lb, y)
    log_f = mx + _log1p_exp(jnp.minimum(a_lb, y) - mx)
    k_h[...] = vec(V_LB_OM) * _sigmoid(-ff)
    v_h[...] = blk(HGRN_OFF + 512, BRANCH)
    b_h[...] = _chunk_cumsum(log_f, c)

    q_g, k_g, v_g, b_g = qkvb[1]
    q_g[...] = blk(GLA_OFF, GLA_KEY_WIDTH) * (GLA_KEY_DIM ** -0.5)
    k_g[...] = blk(GLA_OFF + 128, GLA_KEY_WIDTH)
    v_g[...] = blk(GLA_OFF + 256, BRANCH)
    z = _dot(blk(GLA_OFF + 768, 128), wg_ref[...]) + bgate_ref[...]
    b_g[...] = _chunk_cumsum(_log_sigmoid(z) / GLA_TAU, c)

    p_rw = blk(RWKV_OFF, RWKV_PW)
    mu = mu_ref[...]
    ps_ref[...] = p_rw + (pltpu.roll(p_rw, 1, 0) - p_rw) * mu
    for b in range(bb):
        p0 = p_rw[b * tc:b * tc + 1, :]
        ps_ref[b * tc:b * tc + 1, :] = p0 + (carry[b:b + 1, :] - p0) * mu
        carry[b:b + 1, :] = p_rw[(b + 1) * tc - 1:(b + 1) * tc, :]

    rr = ps_ref[:, 0:256]
    k_raw = ps_ref[:, 256:512]
    vv = ps_ref[:, 512:768]
    low_rank = ps_ref[:, 1024:1152]
    w_pre = vec(V_W0) + _dot(jnp.tanh(low_rank), w2_ref[...])
    w_log = -(jnp.maximum(-w_pre, 0.0) + _log1p_exp(-jnp.abs(w_pre))) - 0.5
    log_w = -jnp.exp(w_log)
    a = _sigmoid(vec(V_A0) + _dot(low_rank, a2_ref[...]))
    if layer0:
        vfirst_out_ref[...] = vv.reshape(bb, tc, BRANCH)
    else:
        vmix = _sigmoid(vec(V_V0) + _dot(_dot(vv, v1_ref[...]), v2_ref[...]))
        vv = vv + (vfirst_in_ref[...].reshape(r, BRANCH) - vv) * vmix
    kk_raw = k_raw * vec(V_KK)
    kk = kk_raw / jnp.maximum(jnp.sqrt(head_sum(kk_raw * kk_raw)), 1e-12)
    kmod = k_raw * (1.0 + (a - 1.0) * vec(V_KA))
    b_w = _chunk_cumsum(log_w, c)
    inv_p = jnp.exp(-b_w)
    p_w = jnp.exp(b_w)
    for ref, arr in zip(rw, (kk * jnp.exp(b_w - log_w), rr * p_w, kmod * inv_p, kk * a * inv_p, vv, p_w)):
        ref[...] = arr

    n_c = tc // c
    mix = ((qkvb[0], o_gl[0], st_hgrn, bo_ref, mbd_ref), (qkvb[1], o_gl[1], st_gla, bog_ref, mbdg_ref))

    def chunk_body(ci, _):
        rows = [pl.ds(pl.multiple_of(b * tc + ci * c, c), c) for b in seqs]

        def all_scores(tiles, bo_m):
            sc = _dot(jnp.concatenate(tiles, axis=0), bo_m)
            n = len(tiles) // bb * SUBLANES
            return [sc[b * n:(b + 1) * n, :] for b in seqs]

        gl = []
        for (q_s, k_s, v_s, b_s), o_s, st_ref, bo_m, mbd in mix:
            ins = [(q_s[rows[b], :], k_s[rows[b], :], v_s[rows[b], :], b_s[rows[b], :]) for b in seqs]
            tiles = []
            for q, k, _, b_c in ins:
                t, where_ = _gl_tiles(q, k, b_c, c)
                tiles += t
            gl.append((ins, all_scores(tiles, bo_m[...]), where_, o_s, st_ref, mbd))
        rw_in = [tuple(x[rows[b], :] for x in rw) for b in seqs]
        tiles = []
        for kkt, _, _, ahat, _, _ in rw_in:
            t, rw_where = _rwkv_tiles(kkt, ahat, c)
            tiles += t
        rw_sc = all_scores(tiles, bo)
        rw_cmp = [_rwkv_compact(kkt, rt, khat, ahat) for kkt, rt, khat, ahat, _, _ in rw_in]
        gl_os = [[_dot_nt(q * jnp.exp(b_c), st_ref[b]) for b, (q, _, _, b_c) in enumerate(ins)]
                 for ins, _, _, _, st_ref, _ in gl]
        rw_fs = [_dot_nt(jnp.concatenate([rw_in[b][0], rw_in[b][1]], axis=0), st_rw[b]) for b in seqs]
        rw_m = [_rwkv_masks(rw_cmp[b], c) for b in seqs]
        rw_fv = [_head_unstack(_dot(rw_m[b][0], rw_in[b][4]), 2 * c) for b in seqs]
        for ins, _, _, _, st_ref, mbd in gl:
            for b, (_, k, v, b_c) in enumerate(ins):
                st_ref[b] = _gl_state(k, v, b_c, st_ref[b], mbd[...], c)
        for (ins, sc, where_, o_s, _, _), o_state in zip(gl, gl_os):
            for b in seqs:
                o_s[rows[b], :] = _gl_output(sc[b], o_state[b], ins[b][2], where_, c)
        rhs = [rw_fs[b] + rw_fv[b] for b in seqs]
        us = [_rwkv_solve(rw_sc[b], rhs[b][0:c, :], rw_where, c) for b in seqs]
        rw_ou = [_dot(rw_m[b][1], us[b]) for b in seqs]
        for b in seqs:
            _, _, khat, ahat, v_w, p_c = rw_in[b]
            st_rw[b] = _rwkv_state(us[b], v_w, khat, ahat, p_c[c - 1:c, :], st_rw[b], mbd_ref[...])
        for b in seqs:
            orw_ref[rows[b], :] = rhs[b][c:2 * c, :] - _head_unstack(rw_ou[b], c)
        return 0

    lax.fori_loop(0, n_c, chunk_body, 0)

    def group_norm(o, eps):
        cen = o - head_mean(o)
        return cen * lax.rsqrt(head_mean(cen * cen) + eps)

    def head_rms(o, g):
        return o * lax.rsqrt(head_mean(o * o) + RMS_EPS) * g

    def put(lo, val):
        ocat_ref[:, :, lo:lo + BRANCH] = val.reshape(bb, tc, BRANCH)

    put(0, group_norm(o_ret, RET_GN_EPS) * _silu(blk(RET_OFF + 768, BRANCH)))
    put(256, head_rms(o_gl[0][...], vec(V_HGRN_G)) * _silu(blk(HGRN_OFF + 768, BRANCH)))
    put(512, head_rms(o_gl[1][...], vec(V_GLA_G)) * _silu(blk(GLA_OFF + 512, BRANCH)))
    o_rw = group_norm(orw_ref[...], RWKV_GN_EPS) * vec(V_GN_G) + vec(V_GN_B)
    bonus = head_sum(rr * kmod * vec(V_RK))
    put(768, (o_rw + bonus * vv) * _silu(ps_ref[:, 768:1024]))

    @pl.when(ti == n_t - 1)
    def _final():
        for st_ref, _, out_ref, dk in state_io:
            for b in range(bb):
                for h in range(N_HEADS):
                    out_ref[b, h] = st_ref[b, h * HEAD_DIM:(h + 1) * HEAD_DIM, :][:, h * dk:(h + 1) * dk]


def _mixers(proj, vfirst, cos, sin, states, lp, consts, *, bb, tc, c):
    bsz, seq, _ = proj.shape
    layer0 = vfirst is None
    n_t = seq // tc
    r = bb * tc

    def full(arr):
        nd = arr.ndim
        return pl.BlockSpec(arr.shape, lambda bi, ti, _nd=nd: (0,) * _nd)

    def per_b(arr):
        nd = arr.ndim
        return pl.BlockSpec((bb,) + arr.shape[1:], lambda bi, ti, _nd=nd: (bi,) + (0,) * (_nd - 1))

    def tok(width):
        return pl.BlockSpec((bb, tc, width), lambda bi, ti: (bi, ti, 0))

    ins, specs = [proj], [tok(D_IN_PAD)]
    if not layer0:
        ins.append(vfirst)
        specs.append(tok(BRANCH))
    ins += [cos, sin]
    specs += [pl.BlockSpec((tc, BRANCH), lambda bi, ti: (ti, 0))] * 2
    for s in states:
        ins.append(s)
        specs.append(per_b(s))
    small = [lp['vec'], lp['mu'], lp['bgate'], lp['wg'], lp['w2'], lp['a2']]
    if not layer0:
        small += [lp['v1'], lp['v2']]
    small += list(consts) + list(_retention_tables(tc))
    for s in small:
        ins.append(s)
        specs.append(full(s))

    st_shapes = [(bsz, N_HEADS, HEAD_DIM, HEAD_DIM), (bsz, N_HEADS, HEAD_DIM, HEAD_DIM),
                 (bsz, N_HEADS, HEAD_DIM, GLA_KEY_DIM), (bsz, N_HEADS, HEAD_DIM, HEAD_DIM)]
    out_shape = [jax.ShapeDtypeStruct((bsz, seq, D_MODEL), F32)]
    out_specs = [tok(D_MODEL)]
    if layer0:
        out_shape.append(jax.ShapeDtypeStruct((bsz, seq, BRANCH), F32))
        out_specs.append(tok(BRANCH))
    for shp in st_shapes:
        out_shape.append(jax.ShapeDtypeStruct(shp, F32))
        out_specs.append(pl.BlockSpec((bb,) + shp[1:], lambda bi, ti: (bi, 0, 0, 0)))

    scratch = [pltpu.VMEM((bb, BRANCH, BRANCH), F32), pltpu.VMEM((bb, BRANCH, BRANCH), F32),
               pltpu.VMEM((bb, BRANCH, GLA_KEY_WIDTH), F32), pltpu.VMEM((bb, BRANCH, BRANCH), F32),
               pltpu.VMEM((bb, RWKV_PW), F32)]
    for dkt in (BRANCH, GLA_KEY_WIDTH):
        scratch += [pltpu.VMEM((r, dkt), F32), pltpu.VMEM((r, dkt), F32), pltpu.VMEM((r, BRANCH), F32),
                    pltpu.VMEM((r, dkt), F32)]
    scratch += [pltpu.VMEM((r, BRANCH), F32)] * 2
    scratch += [pltpu.VMEM((r, RWKV_PW), F32), pltpu.VMEM((r, BRANCH), F32)]
    scratch += [pltpu.VMEM((r, BRANCH), F32)] * 6

    outs = pl.pallas_call(
        functools.partial(_mixers_kernel, bb=bb, tc=tc, c=c, layer0=layer0, n_t=n_t),
        grid=(bsz // bb, n_t),
        in_specs=specs,
        out_specs=out_specs,
        out_shape=out_shape,
        scratch_shapes=scratch,
        compiler_params=pltpu.CompilerParams(dimension_semantics=("arbitrary", "arbitrary"),
                                             vmem_limit_bytes=V7X_VMEM_LIMIT),
        name="mixers",
    )(*ins)
    if layer0:
        return outs[0], outs[1], outs[2:]
    return outs[0], None, outs[1:]


def _out_attn_kernel(x_ref, oc_ref, mk_ref, mv_ref, wout_ref, wq_ref, wo_ref, gx_ref, gf_ref, out_ref,
                     *, bbc, lc, final):
    r = bbc * lc
    x1 = x_ref[...].reshape(r, D_MODEL) + _dot(oc_ref[...].reshape(r, D_MODEL), wout_ref[...])
    q = _dot(_rms(x1, gx_ref[...]), wq_ref[...])
    lane_head = jnp.right_shift(lax.broadcasted_iota(jnp.int32, (lc, BRANCH), 1), 6)
    outs = []
    for b in range(bbc):
        qb = q[b * lc:(b + 1) * lc, :]
        qs = jnp.concatenate([jnp.where(lane_head == h, qb, 0.0) for h in range(N_HEADS)], axis=0)
        s = _dot_nt(qs, mk_ref[b]) * (HEAD_DIM ** -0.5)
        e = jnp.exp(s - jnp.max(s, axis=-1, keepdims=True))
        p = e / jnp.sum(e, axis=-1, keepdims=True)
        o4 = _dot(p, mv_ref[b])
        ob = jnp.where(lane_head == 0, o4[0:lc, :], 0.0)
        for h in range(1, N_HEADS):
            ob = ob + jnp.where(lane_head == h, o4[h * lc:(h + 1) * lc, :], 0.0)
        outs.append(ob)
    o = outs[0] if bbc == 1 else jnp.concatenate(outs, axis=0)
    x2 = x1 + _dot(o, wo_ref[...])
    if final:
        x2 = _rms(x2, gf_ref[...])
    out_ref[...] = x2.reshape(bbc, lc, D_MODEL)


def _out_attn(x, ocat, mem, lp, norm_f, *, bbc, lc, final):
    bsz, seq, _ = x.shape
    mk, mv, mk_spec, mv_spec = mem

    def tok(width):
        return pl.BlockSpec((bbc, lc, width), lambda bi, li: (bi, li, 0))

    def full(arr):
        nd = arr.ndim
        return pl.BlockSpec(arr.shape, lambda bi, li, _nd=nd: (0,) * _nd)

    small = [lp['w_out'], lp['wq'], lp['wo'], lp['norm_x'], norm_f]
    return pl.pallas_call(
        functools.partial(_out_attn_kernel, bbc=bbc, lc=lc, final=final),
        grid=(bsz // bbc, seq // lc),
        in_specs=[tok(D_MODEL), tok(D_MODEL), mk_spec, mv_spec] + [full(s) for s in small],
        out_specs=tok(D_MODEL),
        out_shape=jax.ShapeDtypeStruct((bsz, seq, D_MODEL), F32),
        compiler_params=pltpu.CompilerParams(dimension_semantics=("arbitrary", "arbitrary"),
                                             vmem_limit_bytes=V7X_VMEM_LIMIT),
        name="out_attn",
    )(x, ocat, mk, mv, *small)


def _rope_tables(pos0, seq):
    half = HEAD_DIM // 2
    inv = ROPE_BASE ** (-jnp.arange(half, dtype=F32) / half)
    pos = (pos0 + jnp.arange(seq, dtype=jnp.int32)).astype(F32)
    ang = pos[:, None] * inv[None, :]
    cos, sin = jnp.cos(ang), jnp.sin(ang)
    return (jnp.tile(jnp.concatenate([cos, cos], axis=-1), (1, N_HEADS)),
            jnp.tile(jnp.concatenate([-sin, sin], axis=-1), (1, N_HEADS)))


def _retention_tables(n):
    log_gamma = jnp.log1p(-jnp.exp2(-5.0 - jnp.arange(N_HEADS, dtype=F32)))
    j = jnp.arange(n, dtype=F32)
    diff = j[:, None] - j[None, :]
    dmat = jnp.where(diff[None] >= 0, jnp.exp(diff[None] * log_gamma[:, None, None]), 0.0)
    lg_lanes = jnp.repeat(log_gamma, HEAD_DIM)[None, :]
    e_q = jnp.exp((j[:, None] + 1.0) * lg_lanes)
    e_k = jnp.exp((n - 1.0 - j[:, None]) * lg_lanes)
    g_blk = jnp.exp(float(n) * lg_lanes)
    return dmat.reshape(N_HEADS * n, n), e_q, e_k, g_blk


def _rwkv_to_padded(t):
    pad = jnp.zeros(t.shape[:-1] + (RWKV_PW - RWKV_W,), t.dtype)
    return jnp.concatenate([t[..., 0:256], t[..., 288:544], t[..., 544:800], t[..., 832:1088],
                            t[..., 256:288], t[..., 800:832], pad], axis=-1)


def _rwkv_from_padded(t):
    return jnp.concatenate([t[..., 0:256], t[..., 1024:1056], t[..., 256:512], t[..., 512:768],
                            t[..., 1056:1088], t[..., 768:1024]], axis=-1)


def _pad_rows(m, rows, at=0):
    out = jnp.zeros((rows, m.shape[1]), m.dtype)
    return out.at[at:at + m.shape[0]].set(m)


def _layer_params(l, P):
    w = P['w_in'][l]
    gla = w[:, 2048:2832]
    w_pad = jnp.concatenate([
        w[:, 0:2048],
        gla[:, 0:512], gla[:, 528:784], gla[:, 512:528], jnp.zeros((D_MODEL, 112), w.dtype),
        _rwkv_to_padded(w[:, 2832:3920])], axis=1).astype(BF16)

    sm = jax.nn.softmax(P['hgrn_lb_logits'].astype(F32), axis=0)
    lb = (jnp.cumsum(sm, axis=0) - sm[0])[l]
    rows = [None] * N_VEC
    rows[V_LB_A] = jnp.log(jnp.maximum(lb, LB_FLOOR))
    rows[V_LB_B] = jnp.log1p(-lb)
    rows[V_LB_OM] = 1.0 - lb
    rows[V_HGRN_G] = jnp.tile(P['hgrn_norm'][l], N_HEADS)
    rows[V_GLA_G] = jnp.tile(P['gla_norm'][l], N_HEADS)
    rows[V_W0] = P['rwkv_w0'][l]
    rows[V_A0] = P['rwkv_a0'][l]
    rows[V_V0] = P['rwkv_v0'][l - 1] if l > 0 else jnp.zeros((BRANCH,), F32)
    rows[V_KK] = P['rwkv_k_k'][l]
    rows[V_KA] = P['rwkv_k_a'][l]
    rows[V_RK] = P['rwkv_r_k'][l].reshape(BRANCH)
    rows[V_GN_G] = P['rwkv_gn_g'][l]
    rows[V_GN_B] = P['rwkv_gn_b'][l]
    zero = jnp.zeros((BRANCH,), F32)
    vec = jnp.stack([zero if x is None else x.astype(F32) for x in rows], axis=0)

    lp = {
        'w_in': w_pad,
        'norm_mix': P['norm_mix'][l][None, :],
        'vec': vec,
        'mu': _rwkv_to_padded(P['rwkv_mu'][l])[None, :],
        'bgate': P['gla_b_gate'][l][None, :],
        'wg': _pad_rows(P['gla_w_gate2'][l], 128).astype(BF16),
        'w2': _pad_rows(P['rwkv_w2'][l], 128, 0).astype(BF16),
        'a2': _pad_rows(P['rwkv_a2'][l], 128, 32).astype(BF16),
        'w_out': P['w_out'][l].astype(BF16),
        'wq': P['wq_x'][l].astype(BF16),
        'wo': P['wo_x'][l].astype(BF16),
        'norm_x': P['norm_x'][l][None, :],
    }
    if l > 0:
        lp['v1'] = jnp.pad(P['rwkv_v1'][l - 1], ((0, 0), (0, 96))).astype(BF16)
        lp['v2'] = _pad_rows(P['rwkv_v2'][l - 1], 128).astype(BF16)
    return lp


def _consts():
    i256 = jnp.arange(BRANCH) // HEAD_DIM
    i128 = jnp.arange(GLA_KEY_WIDTH) // GLA_KEY_DIM
    bo = (i256[:, None] == i256[None, :])
    bog = (i128[:, None] == i256[None, :])
    return bo.astype(BF16), bog.astype(BF16), bo.astype(F32), bog.T.astype(F32)


def _run_trunk(x, pos0, mems, init, lps, norm_f, consts, *, bb, tc, c, bbc, lc, tm):
    bsz, seq, _ = x.shape
    cos, sin = _rope_tables(pos0, seq)
    new = {n: [] for n in ('ret', 'hgrn', 'gla', 'rwkv', 'shift')}
    vfirst = None
    for l in range(DEPTH):
        lp = lps[l]
        proj = _in_proj(x.reshape(bsz * seq, D_MODEL), lp['norm_mix'], lp['w_in'], tm)
        proj = proj.reshape(bsz, seq, D_IN_PAD)
        states = [jnp.swapaxes(init['ret'][l], -1, -2), jnp.swapaxes(init['hgrn'][l], -1, -2),
                  jnp.swapaxes(init['gla'][l], -1, -2), init['rwkv'][l], _rwkv_to_padded(init['shift'][l])]
        ocat, vf, st = _mixers(proj, vfirst, cos, sin, states, lp, consts, bb=bb, tc=tc, c=c)
        if l == 0:
            vfirst = vf
        new['ret'].append(jnp.swapaxes(st[0], -1, -2))
        new['hgrn'].append(jnp.swapaxes(st[1], -1, -2))
        new['gla'].append(jnp.swapaxes(st[2], -1, -2))
        new['rwkv'].append(st[3])
        new['shift'].append(_rwkv_from_padded(proj[:, seq - 1, RWKV_OFF:RWKV_OFF + RWKV_PW]))
        x = _out_attn(x, ocat, mems[l], lp, norm_f, bbc=bbc, lc=lc, final=(l == DEPTH - 1))
    return x, {n: jnp.stack(v, axis=0) for n, v in new.items()}


def kernel(x_prompt, x_sample, mem_prompt, state_ret, state_hgrn, state_gla, state_rwkv, state_rwkv_shift,
           cache_mem_k, cache_mem_v, norm_mix, w_in, hgrn_lb_logits, hgrn_norm, gla_w_gate2, gla_b_gate,
           gla_norm, rwkv_mu, rwkv_w0, rwkv_w2, rwkv_a0, rwkv_a2, rwkv_v0, rwkv_v1, rwkv_v2, rwkv_k_k,
           rwkv_k_a, rwkv_r_k, rwkv_gn_g, rwkv_gn_b, w_out, norm_x, wq_x, wo_x, norm_mem, wk_x, wv_x, norm_f):
    P = {'norm_mix': norm_mix, 'w_in': w_in, 'hgrn_lb_logits': hgrn_lb_logits, 'hgrn_norm': hgrn_norm,
         'gla_w_gate2': gla_w_gate2, 'gla_b_gate': gla_b_gate, 'gla_norm': gla_norm,
         'rwkv_mu': rwkv_mu, 'rwkv_w0': rwkv_w0, 'rwkv_w2': rwkv_w2, 'rwkv_a0': rwkv_a0,
         'rwkv_a2': rwkv_a2, 'rwkv_v0': rwkv_v0, 'rwkv_v1': rwkv_v1, 'rwkv_v2': rwkv_v2,
         'rwkv_k_k': rwkv_k_k, 'rwkv_k_a': rwkv_k_a, 'rwkv_r_k': rwkv_r_k,
         'rwkv_gn_g': rwkv_gn_g, 'rwkv_gn_b': rwkv_gn_b, 'w_out': w_out,
         'norm_x': norm_x, 'wq_x': wq_x, 'wo_x': wo_x}
    lps = [_layer_params(l, P) for l in range(DEPTH)]
    consts = _consts()
    norm_f2 = norm_f[None, :]
    bsz, seq, _ = x_prompt.shape
    dbsz, dseq, _ = x_sample.shape
    p_bbc, s_bbc = 1, 16

    mem2d = mem_prompt.reshape(bsz * N_MEM, D_MODEL)
    kv_l, p_mems = [], []
    for l in range(DEPTH):
        wkv = jnp.concatenate([wk_x[l], wv_x[l]], axis=1).astype(BF16)
        kv = _in_proj(mem2d, norm_mem[l][None, :], wkv, 256).reshape(bsz, N_MEM, 2 * BRANCH)
        kv_l.append(kv)
        p_mems.append((kv, kv, pl.BlockSpec((p_bbc, N_MEM, BRANCH), lambda bi, li: (bi, 0, 0)),
                       pl.BlockSpec((p_bbc, N_MEM, BRANCH), lambda bi, li: (bi, 0, 1))))
    zero_init = {
        'ret': jnp.zeros((DEPTH, bsz, N_HEADS, HEAD_DIM, HEAD_DIM), F32),
        'hgrn': jnp.zeros((DEPTH, bsz, N_HEADS, HEAD_DIM, HEAD_DIM), F32),
        'gla': jnp.zeros((DEPTH, bsz, N_HEADS, GLA_KEY_DIM, HEAD_DIM), F32),
        'rwkv': jnp.zeros((DEPTH, bsz, N_HEADS, HEAD_DIM, HEAD_DIM), F32),
        'shift': jnp.zeros((DEPTH, bsz, RWKV_W), F32),
    }
    y_prompt, sp = _run_trunk(x_prompt, 0, p_mems, zero_init, lps, norm_f2, consts,
                              bb=8, tc=64, c=CHUNK, bbc=p_bbc, lc=512, tm=512)

    cached = {'ret': state_ret, 'hgrn': state_hgrn, 'gla': state_gla, 'rwkv': state_rwkv,
              'shift': state_rwkv_shift}
    cmk = cache_mem_k.reshape(DEPTH, dbsz, N_MEM, BRANCH)
    cmv = cache_mem_v.reshape(DEPTH, dbsz, N_MEM, BRANCH)
    s_mems = []
    for l in range(DEPTH):
        spec = pl.BlockSpec((None, s_bbc, N_MEM, BRANCH), lambda bi, li, _l=l: (_l, bi, 0, 0))
        s_mems.append((cmk, cmv, spec, spec))
    y_sample, ss = _run_trunk(x_sample, PAST_LEN, s_mems, cached, lps, norm_f2, consts,
                              bb=8, tc=dseq, c=dseq, bbc=s_bbc, lc=dseq, tm=256)

    kv_all = jnp.stack(kv_l, axis=0)
    mem_k_p = kv_all[..., :BRANCH].reshape(DEPTH, bsz, N_MEM, N_HEADS, HEAD_DIM)
    mem_v_p = kv_all[..., BRANCH:].reshape(DEPTH, bsz, N_MEM, N_HEADS, HEAD_DIM)
    return (y_prompt, y_sample, sp['ret'], ss['ret'], sp['hgrn'], ss['hgrn'], sp['gla'], ss['gla'],
            sp['rwkv'], ss['rwkv'], sp['shift'], ss['shift'], mem_k_p, mem_v_p)
```

```python
import functools

import jax
import jax.numpy as jnp
from jax import lax
from jax.experimental import pallas as pl
from jax.experimental.pallas import tpu as pltpu

F32 = jnp.float32
BF16 = jnp.bfloat16

D_MODEL = 1024
DEPTH = 2
PAST_LEN = 16384
N_HEADS = 4
HEAD_DIM = 64
BRANCH = N_HEADS * HEAD_DIM
GLA_KEY_DIM = 32
GLA_KEY_WIDTH = N_HEADS * GLA_KEY_DIM
GLA_TAU = 16.0
N_MEM = 256
ROPE_BASE = 10000.0
RMS_EPS = 1e-6
RET_GN_EPS = 1e-5
RWKV_GN_EPS = 64e-5
LB_FLOOR = 1e-20
RWKV_W = 1088
CHUNK = 16

RET_OFF = 0
HGRN_OFF = 1024
GLA_OFF = 2048
RWKV_OFF = 2944
RWKV_PW = 1152
D_IN_PAD = 4096

V7X_VMEM_LIMIT = 60 * 1024 * 1024
SUBLANES = 8
NEG_BIG = -1e30

(V_LB_A, V_LB_B, V_LB_OM, V_HGRN_G, V_GLA_G, V_W0, V_A0, V_V0, V_KK, V_KA, V_RK,
 V_GN_G, V_GN_B) = range(13)
N_VEC = 16


def _rms(x, g):
    return x * lax.rsqrt(jnp.mean(x * x, axis=-1, keepdims=True) + RMS_EPS) * g


def _sigmoid(x):
    return 0.5 * jnp.tanh(0.5 * x) + 0.5


def _silu(x):
    return x * _sigmoid(x)


def _log1p_exp(x):
    return jnp.log(1.0 + jnp.exp(x))


def _log_sigmoid(x):
    return jnp.minimum(x, 0.0) - _log1p_exp(-jnp.abs(x))


def _dot(a, b):
    return jnp.dot(a.astype(BF16), b.astype(BF16), preferred_element_type=F32)


def _dot_nt(a, b):
    return lax.dot_general(a.astype(BF16), b.astype(BF16), (((1,), (1,)), ((), ())),
                           preferred_element_type=F32)


def _dot_tn(a, b):
    return lax.dot_general(a.astype(BF16), b.astype(BF16), (((0,), (0,)), ((), ())),
                           preferred_element_type=F32)


def _row_tiles(x, n):
    return [x[i * SUBLANES:(i + 1) * SUBLANES, :] for i in range(n)]


def _in_proj_kernel(x_ref, g_ref, w_ref, o_ref):
    o_ref[...] = _dot(_rms(x_ref[...], g_ref[...]), w_ref[...])


def _in_proj(x2d, gain, w_bf16, tm):
    m, d = x2d.shape
    n = w_bf16.shape[1]
    return pl.pallas_call(
        _in_proj_kernel,
        grid=(m // tm,),
        in_specs=[pl.BlockSpec((tm, d), lambda i: (i, 0)),
                  pl.BlockSpec((1, d), lambda i: (0, 0)),
                  pl.BlockSpec((d, n), lambda i: (0, 0))],
        out_specs=pl.BlockSpec((tm, n), lambda i: (i, 0)),
        out_shape=jax.ShapeDtypeStruct((m, n), F32),
        compiler_params=pltpu.CompilerParams(dimension_semantics=("arbitrary",),
                                             vmem_limit_bytes=V7X_VMEM_LIMIT),
        name="in_proj",
    )(x2d, gain, w_bf16)


def _chunk_cumsum(x, c):
    t_local = jnp.bitwise_and(lax.broadcasted_iota(jnp.int32, x.shape, 0), c - 1)
    s = 1
    while s < c:
        x = x + jnp.where(t_local >= s, pltpu.roll(x, s, 0), 0.0)
        s *= 2
    return x


def _head_stack(x, dk=HEAD_DIM):
    lane_head = jnp.right_shift(lax.broadcasted_iota(jnp.int32, x.shape, 1), dk.bit_length() - 1)
    return jnp.concatenate([jnp.where(lane_head == h, x, 0.0) for h in range(N_HEADS)], axis=0)


def _head_unstack(x4, n):
    lane_head = jnp.right_shift(lax.broadcasted_iota(jnp.int32, (n, BRANCH), 1), 6)
    out = jnp.where(lane_head == 0, x4[0:n, :], 0.0)
    for h in range(1, N_HEADS):
        out = out + jnp.where(lane_head == h, x4[h * n:(h + 1) * n, :], 0.0)
    return out


def _heads_to_rows(x):
    return jnp.concatenate([x[:, h * HEAD_DIM:(h + 1) * HEAD_DIM] for h in range(N_HEADS)], axis=0)


def _rows_to_heads(x4, n):
    return jnp.concatenate([x4[h * n:(h + 1) * n, :] for h in range(N_HEADS)], axis=1)


def _state_read(lhs_stacked, st, n):
    return _rows_to_heads(_dot_nt(lhs_stacked, st), n)


def _state_outer(values, keys_stacked):
    return _dot_tn(_heads_to_rows(values), keys_stacked)


def _gl_tiles(q, k, b, c):
    n_rt = c // SUBLANES
    rows = lax.broadcasted_iota(jnp.int32, (SUBLANES, q.shape[1]), 0)
    q_t, b_t = _row_tiles(q, n_rt), _row_tiles(b, n_rt)
    tiles, where_ = [], {}
    for m in range(c):
        for rt in range(m // SUBLANES, n_rt):
            arg = b_t[rt] - b[m:m + 1, :]
            if rt == m // SUBLANES:
                arg = jnp.where(rows >= m % SUBLANES, arg, NEG_BIG)
            where_[m, rt] = len(tiles) * SUBLANES
            tiles.append(q_t[rt] * k[m:m + 1, :] * jnp.exp(arg))
    return tiles, where_


def _gl_output(scores, o_state, v, where_, c):
    o = _row_tiles(o_state, c // SUBLANES)
    for (m, rt), off in where_.items():
        o[rt] = o[rt] + scores[off:off + SUBLANES, :] * v[m:m + 1, :]
    return jnp.concatenate(o, axis=0)


def _gl_state(k, v, b, st, c):
    b_last = b[c - 1:c, :]
    dk = k.shape[1] // N_HEADS
    return st * jnp.exp(b_last) + _state_outer(v, _head_stack(k * jnp.exp(b_last - b), dk))


def _rwkv_tiles(kkt, ahat, c):
    n_rt = c // SUBLANES
    rows = lax.broadcasted_iota(jnp.int32, (SUBLANES, BRANCH), 0)
    kk_t = _row_tiles(kkt, n_rt)
    tiles, where_ = [], {}
    for m in range(c - 1):
        for rt_i in range(m // SUBLANES, n_rt):
            on_diag = rt_i == m // SUBLANES
            if on_diag and m % SUBLANES == SUBLANES - 1:
                continue
            kk_m = jnp.where(rows > m % SUBLANES, kk_t[rt_i], 0.0) if on_diag else kk_t[rt_i]
            where_[m, rt_i] = len(tiles) * SUBLANES
            tiles.append(kk_m * ahat[m:m + 1, :])
    return tiles, where_


def _rwkv_compact(lhs_stacked, khat, ahat):
    return _dot_nt(lhs_stacked, jnp.concatenate([khat, ahat], axis=0))


def _rwkv_masks(sc, c):
    row = lax.broadcasted_iota(jnp.int32, (2 * N_HEADS * c, c), 0)
    col = lax.broadcasted_iota(jnp.int32, (2 * N_HEADS * c, c), 1)
    keep = col < jnp.bitwise_and(row, c - 1) + jnp.where(jnp.bitwise_and(row, c) != 0, 1, 0)
    r_a = jnp.concatenate([sc[(2 * h + 1) * c:(2 * h + 2) * c, c:2 * c] for h in range(N_HEADS)], axis=0)
    row4 = lax.broadcasted_iota(jnp.int32, (N_HEADS * c, c), 0)
    col4 = lax.broadcasted_iota(jnp.int32, (N_HEADS * c, c), 1)
    return jnp.where(keep, sc[:, 0:c], 0.0), jnp.where(col4 <= jnp.bitwise_and(row4, c - 1), r_a, 0.0)


def _rwkv_solve(sc_a, rhs, where_, c):
    n_rt = c // SUBLANES
    u = _row_tiles(rhs, n_rt)
    for s in range(c - 1):
        u_s = u[s // SUBLANES][s % SUBLANES:s % SUBLANES + 1, :]
        for rt_i in range(s // SUBLANES, n_rt):
            if (s, rt_i) in where_:
                off = where_[s, rt_i]
                u[rt_i] = u[rt_i] - sc_a[off:off + SUBLANES, :] * u_s
    return jnp.concatenate(u, axis=0)


def _rwkv_state(u, v, khat, ahat, p_last, st):
    return st * p_last + _state_outer(jnp.concatenate([v, -u], axis=0),
                                      _head_stack(jnp.concatenate([khat * p_last, ahat * p_last], axis=0)))


def _mixers_kernel(*refs, bb, tc, c, layer0, n_t):
    it = iter(refs)
    proj_ref = next(it)
    vfirst_in_ref = None if layer0 else next(it)
    cos_ref, sin_ref = next(it), next(it)
    s_ret0, s_hgrn0, s_gla0, s_rw0, shift0 = next(it), next(it), next(it), next(it), next(it)
    vec_ref, mu_ref, bgate_ref = next(it), next(it), next(it)
    wg_ref, w2_ref, a2_ref = next(it), next(it), next(it)
    v1_ref, v2_ref = (None, None) if layer0 else (next(it), next(it))
    bo_ref, bog_ref = next(it), next(it)
    dmat_ref, eq_ref, ek_ref, gblk_ref = next(it), next(it), next(it), next(it)
    ocat_ref = next(it)
    vfirst_out_ref = next(it) if layer0 else None
    o_ret_st, o_hgrn_st, o_gla_st, o_rw_st = next(it), next(it), next(it), next(it)
    st_ret, st_hgrn, st_gla, st_rw, carry = next(it), next(it), next(it), next(it), next(it)
    qkvb = [[next(it) for _ in range(4)] for _ in range(2)]
    o_gl = [next(it) for _ in range(2)]
    ps_ref, orw_ref = next(it), next(it)
    rw = [next(it) for _ in range(6)]

    r = bb * tc
    ti = pl.program_id(1)
    state_io = ((st_ret, s_ret0, o_ret_st, HEAD_DIM), (st_hgrn, s_hgrn0, o_hgrn_st, HEAD_DIM),
                (st_gla, s_gla0, o_gla_st, GLA_KEY_DIM), (st_rw, s_rw0, o_rw_st, HEAD_DIM))

    def vec(i):
        return vec_ref[i:i + 1, :]

    @pl.when(ti == 0)
    def _init():
        for st_ref, s0_ref, _, dk in state_io:
            for b in range(bb):
                for h in range(N_HEADS):
                    st_ref[b, :, h * dk:(h + 1) * dk] = s0_ref[b, h]
        carry[...] = shift0[...]

    def blk(lo, width):
        return proj_ref[:, :, lo:lo + width].reshape(r, width)

    bo = bo_ref[...]
    bo2 = jnp.concatenate([bo, bo], axis=0)

    def head_sum(x):
        hi = x.astype(BF16)
        lo = (x - hi.astype(F32)).astype(BF16)
        return jnp.dot(jnp.concatenate([hi, lo], axis=1), bo2, preferred_element_type=F32)

    def head_mean(x):
        return head_sum(x) * (1.0 / HEAD_DIM)

    cos = jnp.concatenate([cos_ref[...]] * bb, axis=0)
    sin = jnp.concatenate([sin_ref[...]] * bb, axis=0)
    lane = lax.broadcasted_iota(jnp.int32, (r, BRANCH), 1)
    first_half = jnp.bitwise_and(lane, HEAD_DIM - 1) < (HEAD_DIM // 2)

    def rope(x):
        swapped = jnp.where(first_half, pltpu.roll(x, BRANCH - HEAD_DIM // 2, 1),
                            pltpu.roll(x, HEAD_DIM // 2, 1))
        return x * cos + swapped * sin

    q_r = rope(blk(RET_OFF, BRANCH))
    k_r = rope(blk(RET_OFF + 256, BRANCH)) * (HEAD_DIM ** -0.5)
    v_r = blk(RET_OFF + 512, BRANCH)
    seqs = range(bb)
    blocks = [slice(b * tc, (b + 1) * tc) for b in seqs]
    ret_sc = [_dot_nt(_head_stack(q_r[rows, :]), k_r[rows, :]) * dmat_ref[...] for rows in blocks]
    ret_o4 = [_dot(ret_sc[b], v_r[blocks[b], :]) for b in seqs]
    ret_os = [_state_read(_head_stack(q_r[blocks[b], :] * eq_ref[...]), st_ret[b], tc) for b in seqs]
    for b in seqs:
        upd = _state_outer(v_r[blocks[b], :], _head_stack(k_r[blocks[b], :] * ek_ref[...]))
        st_ret[b] = st_ret[b] * gblk_ref[...] + upd
    o_ret = jnp.concatenate([ret_os[b] + _head_unstack(ret_o4[b], tc) for b in seqs], axis=0)

    q_h, k_h, v_h, b_h = qkvb[0]
    q_h[...] = _silu(blk(HGRN_OFF, BRANCH))
    ff = blk(HGRN_OFF + 256, BRANCH)
    y = vec(V_LB_B) + _log_sigmoid(ff)
    a_lb = jnp.broadcast_to(vec(V_LB_A), y.shape)
    mx = jnp.maximum(a_lb, y)
    log_f = mx + _log1p_exp(jnp.minimum(a_lb, y) - mx)
    k_h[...] = vec(V_LB_OM) * _sigmoid(-ff)
    v_h[...] = blk(HGRN_OFF + 512, BRANCH)
    b_h[...] = _chunk_cumsum(log_f, c)

    q_g, k_g, v_g, b_g = qkvb[1]
    q_g[...] = blk(GLA_OFF, GLA_KEY_WIDTH) * (GLA_KEY_DIM ** -0.5)
    k_g[...] = blk(GLA_OFF + 128, GLA_KEY_WIDTH)
    v_g[...] = blk(GLA_OFF + 256, BRANCH)
    z = _dot(blk(GLA_OFF + 768, 128), wg_ref[...]) + bgate_ref[...]
    b_g[...] = _chunk_cumsum(_log_sigmoid(z) / GLA_TAU, c)

    p_rw = blk(RWKV_OFF, RWKV_PW)
    mu = mu_ref[...]
    ps_ref[...] = p_rw + (pltpu.roll(p_rw, 1, 0) - p_rw) * mu
    for b in range(bb):
        p0 = p_rw[b * tc:b * tc + 1, :]
        ps_ref[b * tc:b * tc + 1, :] = p0 + (carry[b:b + 1, :] - p0) * mu
        carry[b:b + 1, :] = p_rw[(b + 1) * tc - 1:(b + 1) * tc, :]

    rr = ps_ref[:, 0:256]
    k_raw = ps_ref[:, 256:512]
    vv = ps_ref[:, 512:768]
    low_rank = ps_ref[:, 1024:1152]
    w_pre = vec(V_W0) + _dot(jnp.tanh(low_rank), w2_ref[...])
    w_log = -(jnp.maximum(-w_pre, 0.0) + _log1p_exp(-jnp.abs(w_pre))) - 0.5
    log_w = -jnp.exp(w_log)
    a = _sigmoid(vec(V_A0) + _dot(low_rank, a2_ref[...]))
    if layer0:
        vfirst_out_ref[...] = vv.reshape(bb, tc, BRANCH)
    else:
        vmix = _sigmoid(vec(V_V0) + _dot(_dot(vv, v1_ref[...]), v2_ref[...]))
        vv = vv + (vfirst_in_ref[...].reshape(r, BRANCH) - vv) * vmix
    kk_raw = k_raw * vec(V_KK)
    kk = kk_raw / jnp.maximum(jnp.sqrt(head_sum(kk_raw * kk_raw)), 1e-12)
    kmod = k_raw * (1.0 + (a - 1.0) * vec(V_KA))
    b_w = _chunk_cumsum(log_w, c)
    inv_p = jnp.exp(-b_w)
    p_w = jnp.exp(b_w)
    for ref, arr in zip(rw, (kk * jnp.exp(b_w - log_w), rr * p_w, kmod * inv_p, kk * a * inv_p, vv, p_w)):
        ref[...] = arr

    n_c = tc // c
    mix = ((qkvb[0], o_gl[0], st_hgrn, bo_ref, HEAD_DIM), (qkvb[1], o_gl[1], st_gla, bog_ref, GLA_KEY_DIM))

    def chunk_body(ci, _):
        rows = [pl.ds(pl.multiple_of(b * tc + ci * c, c), c) for b in seqs]

        def all_scores(tiles, bo_m):
            sc = _dot(jnp.concatenate(tiles, axis=0), bo_m)
            n = len(tiles) // bb * SUBLANES
            return [sc[b * n:(b + 1) * n, :] for b in seqs]

        gl = []
        for (q_s, k_s, v_s, b_s), o_s, st_ref, bo_m, dk in mix:
            ins = [(q_s[rows[b], :], k_s[rows[b], :], v_s[rows[b], :], b_s[rows[b], :]) for b in seqs]
            tiles = []
            for q, k, _, b_c in ins:
                t, where_ = _gl_tiles(q, k, b_c, c)
                tiles += t
            gl.append((ins, all_scores(tiles, bo_m[...]), where_, o_s, st_ref, dk))
        rw_in = [tuple(x[rows[b], :] for x in rw) for b in seqs]
        tiles = []
        for kkt, _, _, ahat, _, _ in rw_in:
            t, rw_where = _rwkv_tiles(kkt, ahat, c)
            tiles += t
        rw_sc = all_scores(tiles, bo)
        rw_lhs = [_head_stack(jnp.concatenate([kkt, rt], axis=0)) for kkt, rt, _, _, _, _ in rw_in]
        rw_cmp = [_rwkv_compact(rw_lhs[b], rw_in[b][2], rw_in[b][3]) for b in seqs]
        gl_os = [[_state_read(_head_stack(q * jnp.exp(b_c), dk), st_ref[b], c)
                  for b, (q, _, _, b_c) in enumerate(ins)] for ins, _, _, _, st_ref, dk in gl]
        rw_fs = [_state_read(rw_lhs[b], st_rw[b], 2 * c) for b in seqs]
        rw_m = [_rwkv_masks(rw_cmp[b], c) for b in seqs]
        rw_fv = [_head_unstack(_dot(rw_m[b][0], rw_in[b][4]), 2 * c) for b in seqs]
        for ins, _, _, _, st_ref, _ in gl:
            for b, (_, k, v, b_c) in enumerate(ins):
                st_ref[b] = _gl_state(k, v, b_c, st_ref[b], c)
        for (ins, sc, where_, o_s, _, _), o_state in zip(gl, gl_os):
            for b in seqs:
                o_s[rows[b], :] = _gl_output(sc[b], o_state[b], ins[b][2], where_, c)
        rhs = [rw_fs[b] + rw_fv[b] for b in seqs]
        us = [_rwkv_solve(rw_sc[b], rhs[b][0:c, :], rw_where, c) for b in seqs]
        rw_ou = [_dot(rw_m[b][1], us[b]) for b in seqs]
        for b in seqs:
            _, _, khat, ahat, v_w, p_c = rw_in[b]
            st_rw[b] = _rwkv_state(us[b], v_w, khat, ahat, p_c[c - 1:c, :], st_rw[b])
        for b in seqs:
            orw_ref[rows[b], :] = rhs[b][c:2 * c, :] - _head_unstack(rw_ou[b], c)
        return 0

    lax.fori_loop(0, n_c, chunk_body, 0)

    def group_norm(o, eps):
        cen = o - head_mean(o)
        return cen * lax.rsqrt(head_mean(cen * cen) + eps)

    def head_rms(o, g):
        return o * lax.rsqrt(head_mean(o * o) + RMS_EPS) * g

    def put(lo, val):
        ocat_ref[:, :, lo:lo + BRANCH] = val.reshape(bb, tc, BRANCH)

    put(0, group_norm(o_ret, RET_GN_EPS) * _silu(blk(RET_OFF + 768, BRANCH)))
    put(256, head_rms(o_gl[0][...], vec(V_HGRN_G)) * _silu(blk(HGRN_OFF + 768, BRANCH)))
    put(512, head_rms(o_gl[1][...], vec(V_GLA_G)) * _silu(blk(GLA_OFF + 512, BRANCH)))
    o_rw = group_norm(orw_ref[...], RWKV_GN_EPS) * vec(V_GN_G) + vec(V_GN_B)
    bonus = head_sum(rr * kmod * vec(V_RK))
    put(768, (o_rw + bonus * vv) * _silu(ps_ref[:, 768:1024]))

    @pl.when(ti == n_t - 1)
    def _final():
        for st_ref, _, out_ref, dk in state_io:
            for b in range(bb):
                for h in range(N_HEADS):
                    out_ref[b, h] = st_ref[b][:, h * dk:(h + 1) * dk]


def _mixers(proj, vfirst, cos, sin, states, lp, consts, *, bb, tc, c):
    bsz, seq, _ = proj.shape
    layer0 = vfirst is None
    n_t = seq // tc
    r = bb * tc

    def full(arr):
        nd = arr.ndim
        return pl.BlockSpec(arr.shape, lambda bi, ti, _nd=nd: (0,) * _nd)

    def per_b(arr):
        nd = arr.ndim
        return pl.BlockSpec((bb,) + arr.shape[1:], lambda bi, ti, _nd=nd: (bi,) + (0,) * (_nd - 1))

    def tok(width):
        return pl.BlockSpec((bb, tc, width), lambda bi, ti: (bi, ti, 0))

    ins, specs = [proj], [tok(D_IN_PAD)]
    if not layer0:
        ins.append(vfirst)
        specs.append(tok(BRANCH))
    ins += [cos, sin]
    specs += [pl.BlockSpec((tc, BRANCH), lambda bi, ti: (ti, 0))] * 2
    for s in states:
        ins.append(s)
        specs.append(per_b(s))
    small = [lp['vec'], lp['mu'], lp['bgate'], lp['wg'], lp['w2'], lp['a2']]
    if not layer0:
        small += [lp['v1'], lp['v2']]
    small += list(consts) + list(_retention_tables(tc))
    for s in small:
        ins.append(s)
        specs.append(full(s))

    st_shapes = [(bsz, N_HEADS, HEAD_DIM, HEAD_DIM), (bsz, N_HEADS, HEAD_DIM, HEAD_DIM),
                 (bsz, N_HEADS, HEAD_DIM, GLA_KEY_DIM), (bsz, N_HEADS, HEAD_DIM, HEAD_DIM)]
    out_shape = [jax.ShapeDtypeStruct((bsz, seq, D_MODEL), F32)]
    out_specs = [tok(D_MODEL)]
    if layer0:
        out_shape.append(jax.ShapeDtypeStruct((bsz, seq, BRANCH), F32))
        out_specs.append(tok(BRANCH))
    for shp in st_shapes:
        out_shape.append(jax.ShapeDtypeStruct(shp, F32))
        out_specs.append(pl.BlockSpec((bb,) + shp[1:], lambda bi, ti: (bi, 0, 0, 0)))

    scratch = [pltpu.VMEM((bb, HEAD_DIM, BRANCH), F32), pltpu.VMEM((bb, HEAD_DIM, BRANCH), F32),
               pltpu.VMEM((bb, HEAD_DIM, GLA_KEY_WIDTH), F32), pltpu.VMEM((bb, HEAD_DIM, BRANCH), F32),
               pltpu.VMEM((bb, RWKV_PW), F32)]
    for dkt in (BRANCH, GLA_KEY_WIDTH):
        scratch += [pltpu.VMEM((r, dkt), F32), pltpu.VMEM((r, dkt), F32), pltpu.VMEM((r, BRANCH), F32),
                    pltpu.VMEM((r, dkt), F32)]
    scratch += [pltpu.VMEM((r, BRANCH), F32)] * 2
    scratch += [pltpu.VMEM((r, RWKV_PW), F32), pltpu.VMEM((r, BRANCH), F32)]
    scratch += [pltpu.VMEM((r, BRANCH), F32)] * 6

    outs = pl.pallas_call(
        functools.partial(_mixers_kernel, bb=bb, tc=tc, c=c, layer0=layer0, n_t=n_t),
        grid=(bsz // bb, n_t),
        in_specs=specs,
        out_specs=out_specs,
        out_shape=out_shape,
        scratch_shapes=scratch,
        compiler_params=pltpu.CompilerParams(dimension_semantics=("arbitrary", "arbitrary"),
                                             vmem_limit_bytes=V7X_VMEM_LIMIT),
        name="mixers",
    )(*ins)
    if layer0:
        return outs[0], outs[1], outs[2:]
    return outs[0], None, outs[1:]


def _out_attn_kernel(x_ref, oc_ref, mk_ref, mv_ref, wout_ref, wq_ref, wo_ref, gx_ref, gf_ref, out_ref,
                     *, bbc, lc, final):
    r = bbc * lc
    x1 = x_ref[...].reshape(r, D_MODEL) + _dot(oc_ref[...].reshape(r, D_MODEL), wout_ref[...])
    q = _dot(_rms(x1, gx_ref[...]), wq_ref[...])
    lane_head = jnp.right_shift(lax.broadcasted_iota(jnp.int32, (lc, BRANCH), 1), 6)
    outs = []
    for b in range(bbc):
        qb = q[b * lc:(b + 1) * lc, :]
        qs = jnp.concatenate([jnp.where(lane_head == h, qb, 0.0) for h in range(N_HEADS)], axis=0)
        s = _dot_nt(qs, mk_ref[b]) * (HEAD_DIM ** -0.5)
        e = jnp.exp(s - jnp.max(s, axis=-1, keepdims=True))
        p = e / jnp.sum(e, axis=-1, keepdims=True)
        o4 = _dot(p, mv_ref[b])
        ob = jnp.where(lane_head == 0, o4[0:lc, :], 0.0)
        for h in range(1, N_HEADS):
            ob = ob + jnp.where(lane_head == h, o4[h * lc:(h + 1) * lc, :], 0.0)
        outs.append(ob)
    o = outs[0] if bbc == 1 else jnp.concatenate(outs, axis=0)
    x2 = x1 + _dot(o, wo_ref[...])
    if final:
        x2 = _rms(x2, gf_ref[...])
    out_ref[...] = x2.reshape(bbc, lc, D_MODEL)


def _out_attn(x, ocat, mem, lp, norm_f, *, bbc, lc, final):
    bsz, seq, _ = x.shape
    mk, mv, mk_spec, mv_spec = mem

    def tok(width):
        return pl.BlockSpec((bbc, lc, width), lambda bi, li: (bi, li, 0))

    def full(arr):
        nd = arr.ndim
        return pl.BlockSpec(arr.shape, lambda bi, li, _nd=nd: (0,) * _nd)

    small = [lp['w_out'], lp['wq'], lp['wo'], lp['norm_x'], norm_f]
    return pl.pallas_call(
        functools.partial(_out_attn_kernel, bbc=bbc, lc=lc, final=final),
        grid=(bsz // bbc, seq // lc),
        in_specs=[tok(D_MODEL), tok(D_MODEL), mk_spec, mv_spec] + [full(s) for s in small],
        out_specs=tok(D_MODEL),
        out_shape=jax.ShapeDtypeStruct((bsz, seq, D_MODEL), F32),
        compiler_params=pltpu.CompilerParams(dimension_semantics=("arbitrary", "arbitrary"),
                                             vmem_limit_bytes=V7X_VMEM_LIMIT),
        name="out_attn",
    )(x, ocat, mk, mv, *small)


def _rope_tables(pos0, seq):
    half = HEAD_DIM // 2
    inv = ROPE_BASE ** (-jnp.arange(half, dtype=F32) / half)
    pos = (pos0 + jnp.arange(seq, dtype=jnp.int32)).astype(F32)
    ang = pos[:, None] * inv[None, :]
    cos, sin = jnp.cos(ang), jnp.sin(ang)
    return (jnp.tile(jnp.concatenate([cos, cos], axis=-1), (1, N_HEADS)),
            jnp.tile(jnp.concatenate([-sin, sin], axis=-1), (1, N_HEADS)))


def _retention_tables(n):
    log_gamma = jnp.log1p(-jnp.exp2(-5.0 - jnp.arange(N_HEADS, dtype=F32)))
    j = jnp.arange(n, dtype=F32)
    diff = j[:, None] - j[None, :]
    dmat = jnp.where(diff[None] >= 0, jnp.exp(diff[None] * log_gamma[:, None, None]), 0.0)
    lg_lanes = jnp.repeat(log_gamma, HEAD_DIM)[None, :]
    e_q = jnp.exp((j[:, None] + 1.0) * lg_lanes)
    e_k = jnp.exp((n - 1.0 - j[:, None]) * lg_lanes)
    g_blk = jnp.exp(float(n) * lg_lanes)
    return dmat.reshape(N_HEADS * n, n), e_q, e_k, g_blk


def _rwkv_to_padded(t):
    pad = jnp.zeros(t.shape[:-1] + (RWKV_PW - RWKV_W,), t.dtype)
    return jnp.concatenate([t[..., 0:256], t[..., 288:544], t[..., 544:800], t[..., 832:1088],
                            t[..., 256:288], t[..., 800:832], pad], axis=-1)


def _rwkv_from_padded(t):
    return jnp.concatenate([t[..., 0:256], t[..., 1024:1056], t[..., 256:512], t[..., 512:768],
                            t[..., 1056:1088], t[..., 768:1024]], axis=-1)


def _pad_rows(m, rows, at=0):
    out = jnp.zeros((rows, m.shape[1]), m.dtype)
    return out.at[at:at + m.shape[0]].set(m)


def _layer_params(l, P):
    w = P['w_in'][l]
    gla = w[:, 2048:2832]
    w_pad = jnp.concatenate([
        w[:, 0:2048],
        gla[:, 0:512], gla[:, 528:784], gla[:, 512:528], jnp.zeros((D_MODEL, 112), w.dtype),
        _rwkv_to_padded(w[:, 2832:3920])], axis=1).astype(BF16)

    sm = jax.nn.softmax(P['hgrn_lb_logits'].astype(F32), axis=0)
    lb = (jnp.cumsum(sm, axis=0) - sm[0])[l]
    rows = [None] * N_VEC
    rows[V_LB_A] = jnp.log(jnp.maximum(lb, LB_FLOOR))
    rows[V_LB_B] = jnp.log1p(-lb)
    rows[V_LB_OM] = 1.0 - lb
    rows[V_HGRN_G] = jnp.tile(P['hgrn_norm'][l], N_HEADS)
    rows[V_GLA_G] = jnp.tile(P['gla_norm'][l], N_HEADS)
    rows[V_W0] = P['rwkv_w0'][l]
    rows[V_A0] = P['rwkv_a0'][l]
    rows[V_V0] = P['rwkv_v0'][l - 1] if l > 0 else jnp.zeros((BRANCH,), F32)
    rows[V_KK] = P['rwkv_k_k'][l]
    rows[V_KA] = P['rwkv_k_a'][l]
    rows[V_RK] = P['rwkv_r_k'][l].reshape(BRANCH)
    rows[V_GN_G] = P['rwkv_gn_g'][l]
    rows[V_GN_B] = P['rwkv_gn_b'][l]
    zero = jnp.zeros((BRANCH,), F32)
    vec = jnp.stack([zero if x is None else x.astype(F32) for x in rows], axis=0)

    lp = {
        'w_in': w_pad,
        'norm_mix': P['norm_mix'][l][None, :],
        'vec': vec,
        'mu': _rwkv_to_padded(P['rwkv_mu'][l])[None, :],
        'bgate': P['gla_b_gate'][l][None, :],
        'wg': _pad_rows(P['gla_w_gate2'][l], 128).astype(BF16),
        'w2': _pad_rows(P['rwkv_w2'][l], 128, 0).astype(BF16),
        'a2': _pad_rows(P['rwkv_a2'][l], 128, 32).astype(BF16),
        'w_out': P['w_out'][l].astype(BF16),
        'wq': P['wq_x'][l].astype(BF16),
        'wo': P['wo_x'][l].astype(BF16),
        'norm_x': P['norm_x'][l][None, :],
    }
    if l > 0:
        lp['v1'] = jnp.pad(P['rwkv_v1'][l - 1], ((0, 0), (0, 96))).astype(BF16)
        lp['v2'] = _pad_rows(P['rwkv_v2'][l - 1], 128).astype(BF16)
    return lp


def _consts():
    i256 = jnp.arange(BRANCH) // HEAD_DIM
    i128 = jnp.arange(GLA_KEY_WIDTH) // GLA_KEY_DIM
    bo = (i256[:, None] == i256[None, :])
    bog = (i128[:, None] == i256[None, :])
    return bo.astype(BF16), bog.astype(BF16)


def _run_trunk(x, pos0, mems, init, lps, norm_f, consts, *, bb, tc, c, bbc, lc, tm):
    bsz, seq, _ = x.shape
    cos, sin = _rope_tables(pos0, seq)
    new = {n: [] for n in ('ret', 'hgrn', 'gla', 'rwkv', 'shift')}
    vfirst = None
    for l in range(DEPTH):
        lp = lps[l]
        proj = _in_proj(x.reshape(bsz * seq, D_MODEL), lp['norm_mix'], lp['w_in'], tm)
        proj = proj.reshape(bsz, seq, D_IN_PAD)
        states = [jnp.swapaxes(init['ret'][l], -1, -2), jnp.swapaxes(init['hgrn'][l], -1, -2),
                  jnp.swapaxes(init['gla'][l], -1, -2), init['rwkv'][l], _rwkv_to_padded(init['shift'][l])]
        ocat, vf, st = _mixers(proj, vfirst, cos, sin, states, lp, consts, bb=bb, tc=tc, c=c)
        if l == 0:
            vfirst = vf
        new['ret'].append(jnp.swapaxes(st[0], -1, -2))
        new['hgrn'].append(jnp.swapaxes(st[1], -1, -2))
        new['gla'].append(jnp.swapaxes(st[2], -1, -2))
        new['rwkv'].append(st[3])
        new['shift'].append(_rwkv_from_padded(proj[:, seq - 1, RWKV_OFF:RWKV_OFF + RWKV_PW]))
        x = _out_attn(x, ocat, mems[l], lp, norm_f, bbc=bbc, lc=lc, final=(l == DEPTH - 1))
    return x, {n: jnp.stack(v, axis=0) for n, v in new.items()}


def kernel(x_prompt, x_sample, mem_prompt, state_ret, state_hgrn, state_gla, state_rwkv, state_rwkv_shift,
           cache_mem_k, cache_mem_v, norm_mix, w_in, hgrn_lb_logits, hgrn_norm, gla_w_gate2, gla_b_gate,
           gla_norm, rwkv_mu, rwkv_w0, rwkv_w2, rwkv_a0, rwkv_a2, rwkv_v0, rwkv_v1, rwkv_v2, rwkv_k_k,
           rwkv_k_a, rwkv_r_k, rwkv_gn_g, rwkv_gn_b, w_out, norm_x, wq_x, wo_x, norm_mem, wk_x, wv_x, norm_f):
    P = {'norm_mix': norm_mix, 'w_in': w_in, 'hgrn_lb_logits': hgrn_lb_logits, 'hgrn_norm': hgrn_norm,
         'gla_w_gate2': gla_w_gate2, 'gla_b_gate': gla_b_gate, 'gla_norm': gla_norm,
         'rwkv_mu': rwkv_mu, 'rwkv_w0': rwkv_w0, 'rwkv_w2': rwkv_w2, 'rwkv_a0': rwkv_a0,
         'rwkv_a2': rwkv_a2, 'rwkv_v0': rwkv_v0, 'rwkv_v1': rwkv_v1, 'rwkv_v2': rwkv_v2,
         'rwkv_k_k': rwkv_k_k, 'rwkv_k_a': rwkv_k_a, 'rwkv_r_k': rwkv_r_k,
         'rwkv_gn_g': rwkv_gn_g, 'rwkv_gn_b': rwkv_gn_b, 'w_out': w_out,
         'norm_x': norm_x, 'wq_x': wq_x, 'wo_x': wo_x}
    lps = [_layer_params(l, P) for l in range(DEPTH)]
    consts = _consts()
    norm_f2 = norm_f[None, :]
    bsz, seq, _ = x_prompt.shape
    dbsz, dseq, _ = x_sample.shape
    p_bbc, s_bbc = 1, 16

    mem2d = mem_prompt.reshape(bsz * N_MEM, D_MODEL)
    kv_l, p_mems = [], []
    for l in range(DEPTH):
        wkv = jnp.concatenate([wk_x[l], wv_x[l]], axis=1).astype(BF16)
        kv = _in_proj(mem2d, norm_mem[l][None, :], wkv, 256).reshape(bsz, N_MEM, 2 * BRANCH)
        kv_l.append(kv)
        p_mems.append((kv, kv, pl.BlockSpec((p_bbc, N_MEM, BRANCH), lambda bi, li: (bi, 0, 0)),
                       pl.BlockSpec((p_bbc, N_MEM, BRANCH), lambda bi, li: (bi, 0, 1))))
    zero_init = {
        'ret': jnp.zeros((DEPTH, bsz, N_HEADS, HEAD_DIM, HEAD_DIM), F32),
        'hgrn': jnp.zeros((DEPTH, bsz, N_HEADS, HEAD_DIM, HEAD_DIM), F32),
        'gla': jnp.zeros((DEPTH, bsz, N_HEADS, GLA_KEY_DIM, HEAD_DIM), F32),
        'rwkv': jnp.zeros((DEPTH, bsz, N_HEADS, HEAD_DIM, HEAD_DIM), F32),
        'shift': jnp.zeros((DEPTH, bsz, RWKV_W), F32),
    }
    y_prompt, sp = _run_trunk(x_prompt, 0, p_mems, zero_init, lps, norm_f2, consts,
                              bb=8, tc=64, c=CHUNK, bbc=p_bbc, lc=512, tm=512)

    cached = {'ret': state_ret, 'hgrn': state_hgrn, 'gla': state_gla, 'rwkv': state_rwkv,
              'shift': state_rwkv_shift}
    cmk = cache_mem_k.reshape(DEPTH, dbsz, N_MEM, BRANCH)
    cmv = cache_mem_v.reshape(DEPTH, dbsz, N_MEM, BRANCH)
    s_mems = []
    for l in range(DEPTH):
        spec = pl.BlockSpec((None, s_bbc, N_MEM, BRANCH), lambda bi, li, _l=l: (_l, bi, 0, 0))
        s_mems.append((cmk, cmv, spec, spec))
    y_sample, ss = _run_trunk(x_sample, PAST_LEN, s_mems, cached, lps, norm_f2, consts,
                              bb=8, tc=dseq, c=dseq, bbc=s_bbc, lc=dseq, tm=256)

    kv_all = jnp.stack(kv_l, axis=0)
    mem_k_p = kv_all[..., :BRANCH].reshape(DEPTH, bsz, N_MEM, N_HEADS, HEAD_DIM)
    mem_v_p = kv_all[..., BRANCH:].reshape(DEPTH, bsz, N_MEM, N_HEADS, HEAD_DIM)
    return (y_prompt, y_sample, sp['ret'], ss['ret'], sp['hgrn'], ss['hgrn'], sp['gla'], ss['gla'],
            sp['rwkv'], ss['rwkv'], sp['shift'], ss['shift'], mem_k_p, mem_v_p)
```

```python
import functools

import jax
import jax.numpy as jnp
from jax import lax
from jax.experimental import pallas as pl
from jax.experimental.pallas import tpu as pltpu

F32 = jnp.float32
BF16 = jnp.bfloat16

D_MODEL = 1024
DEPTH = 2
PAST_LEN = 16384
N_HEADS = 4
HEAD_DIM = 64
BRANCH = N_HEADS * HEAD_DIM
GLA_KEY_DIM = 32
GLA_KEY_WIDTH = N_HEADS * GLA_KEY_DIM
GLA_TAU = 16.0
N_MEM = 256
ROPE_BASE = 10000.0
RMS_EPS = 1e-6
RET_GN_EPS = 1e-5
RWKV_GN_EPS = 64e-5
LB_FLOOR = 1e-20
RWKV_W = 1088
CHUNK = 16

RET_OFF = 0
HGRN_OFF = 1024
GLA_OFF = 2048
RWKV_OFF = 2944
RWKV_PW = 1152
D_IN_PAD = 4096

V7X_VMEM_LIMIT = 60 * 1024 * 1024
SUBLANES = 8
NEG_BIG = -1e30

(V_LB_A, V_LB_B, V_LB_OM, V_HGRN_G, V_GLA_G, V_W0, V_A0, V_V0, V_KK, V_KA, V_RK,
 V_GN_G, V_GN_B) = range(13)
N_VEC = 16


def _rms(x, g):
    return x * lax.rsqrt(jnp.mean(x * x, axis=-1, keepdims=True) + RMS_EPS) * g


def _sigmoid(x):
    return 0.5 * jnp.tanh(0.5 * x) + 0.5


def _silu(x):
    h = 0.5 * x
    return h * jnp.tanh(h) + h


def _log1p_exp(x):
    return jnp.log(1.0 + jnp.exp(x))


def _log_sigmoid(x):
    return jnp.minimum(x, 0.0) - _log1p_exp(-jnp.abs(x))


def _dot(a, b):
    return jnp.dot(a.astype(BF16), b.astype(BF16), preferred_element_type=F32)


def _dot_nt(a, b):
    return lax.dot_general(a.astype(BF16), b.astype(BF16), (((1,), (1,)), ((), ())),
                           preferred_element_type=F32)


def _dot_tn(a, b):
    return lax.dot_general(a.astype(BF16), b.astype(BF16), (((0,), (0,)), ((), ())),
                           preferred_element_type=F32)


def _row_tiles(x, n):
    return [x[i * SUBLANES:(i + 1) * SUBLANES, :] for i in range(n)]


def _in_proj_kernel(x_ref, g_ref, w_ref, o_ref):
    o_ref[...] = _dot(_rms(x_ref[...], g_ref[...]), w_ref[...])


def _in_proj(x2d, gain, w_bf16, tm):
    m, d = x2d.shape
    n = w_bf16.shape[1]
    return pl.pallas_call(
        _in_proj_kernel,
        grid=(m // tm,),
        in_specs=[pl.BlockSpec((tm, d), lambda i: (i, 0)),
                  pl.BlockSpec((1, d), lambda i: (0, 0)),
                  pl.BlockSpec((d, n), lambda i: (0, 0))],
        out_specs=pl.BlockSpec((tm, n), lambda i: (i, 0)),
        out_shape=jax.ShapeDtypeStruct((m, n), F32),
        compiler_params=pltpu.CompilerParams(dimension_semantics=("arbitrary",),
                                             vmem_limit_bytes=V7X_VMEM_LIMIT),
        name="in_proj",
    )(x2d, gain, w_bf16)


def _chunk_cumsum(x, c):
    t_local = jnp.bitwise_and(lax.broadcasted_iota(jnp.int32, x.shape, 0), c - 1)
    s = 1
    while s < c:
        x = x + jnp.where(t_local >= s, pltpu.roll(x, s, 0), 0.0)
        s *= 2
    return x


def _head_stack(x, dk=HEAD_DIM):
    lane_head = jnp.right_shift(lax.broadcasted_iota(jnp.int32, x.shape, 1), dk.bit_length() - 1)
    return jnp.concatenate([jnp.where(lane_head == h, x, 0.0) for h in range(N_HEADS)], axis=0)


def _head_unstack(x4, n):
    lane_head = jnp.right_shift(lax.broadcasted_iota(jnp.int32, (n, BRANCH), 1), 6)
    out = jnp.where(lane_head == 0, x4[0:n, :], 0.0)
    for h in range(1, N_HEADS):
        out = out + jnp.where(lane_head == h, x4[h * n:(h + 1) * n, :], 0.0)
    return out


def _heads_to_rows(x):
    return jnp.concatenate([x[:, h * HEAD_DIM:(h + 1) * HEAD_DIM] for h in range(N_HEADS)], axis=0)


def _rows_to_heads(x4, n):
    return jnp.concatenate([x4[h * n:(h + 1) * n, :] for h in range(N_HEADS)], axis=1)


def _state_read(lhs_stacked, st, n):
    return _rows_to_heads(_dot_nt(lhs_stacked, st), n)


def _state_outer(values, keys_stacked):
    return _dot_tn(_heads_to_rows(values), keys_stacked)


def _gl_tiles(q, k, b, c):
    n_rt = c // SUBLANES
    rows = lax.broadcasted_iota(jnp.int32, (SUBLANES, q.shape[1]), 0)
    q_t, b_t = _row_tiles(q, n_rt), _row_tiles(b, n_rt)
    tiles, where_ = [], {}
    for m in range(c):
        for rt in range(m // SUBLANES, n_rt):
            arg = b_t[rt] - b[m:m + 1, :]
            if rt == m // SUBLANES:
                arg = jnp.where(rows >= m % SUBLANES, arg, NEG_BIG)
            where_[m, rt] = len(tiles) * SUBLANES
            tiles.append(q_t[rt] * k[m:m + 1, :] * jnp.exp(arg))
    return tiles, where_


def _gl_output(scores, o_state, v, where_, c):
    o = _row_tiles(o_state, c // SUBLANES)
    for (m, rt), off in where_.items():
        o[rt] = o[rt] + scores[off:off + SUBLANES, :] * v[m:m + 1, :]
    return jnp.concatenate(o, axis=0)


def _gl_state(k, v, b, st, c):
    b_last = b[c - 1:c, :]
    dk = k.shape[1] // N_HEADS
    return st * jnp.exp(b_last) + _state_outer(v, _head_stack(k * jnp.exp(b_last - b), dk))


def _rwkv_tiles(kkt, ahat, c):
    n_rt = c // SUBLANES
    rows = lax.broadcasted_iota(jnp.int32, (SUBLANES, BRANCH), 0)
    kk_t = _row_tiles(kkt, n_rt)
    tiles, where_ = [], {}
    for m in range(c - 1):
        for rt_i in range(m // SUBLANES, n_rt):
            on_diag = rt_i == m // SUBLANES
            if on_diag and m % SUBLANES == SUBLANES - 1:
                continue
            kk_m = jnp.where(rows > m % SUBLANES, kk_t[rt_i], 0.0) if on_diag else kk_t[rt_i]
            where_[m, rt_i] = len(tiles) * SUBLANES
            tiles.append(kk_m * ahat[m:m + 1, :])
    return tiles, where_


def _rwkv_compact(lhs_stacked, khat, ahat):
    return _dot_nt(lhs_stacked, jnp.concatenate([khat, ahat], axis=0))


def _rwkv_masks(sc, c):
    row = lax.broadcasted_iota(jnp.int32, (2 * N_HEADS * c, c), 0)
    col = lax.broadcasted_iota(jnp.int32, (2 * N_HEADS * c, c), 1)
    keep = col < jnp.bitwise_and(row, c - 1) + jnp.where(jnp.bitwise_and(row, c) != 0, 1, 0)
    r_a = jnp.concatenate([sc[(2 * h + 1) * c:(2 * h + 2) * c, c:2 * c] for h in range(N_HEADS)], axis=0)
    row4 = lax.broadcasted_iota(jnp.int32, (N_HEADS * c, c), 0)
    col4 = lax.broadcasted_iota(jnp.int32, (N_HEADS * c, c), 1)
    return jnp.where(keep, sc[:, 0:c], 0.0), jnp.where(col4 <= jnp.bitwise_and(row4, c - 1), r_a, 0.0)


def _rwkv_solve(sc_a, rhs, where_, c):
    n_rt = c // SUBLANES
    u = _row_tiles(rhs, n_rt)
    for s in range(c - 1):
        u_s = u[s // SUBLANES][s % SUBLANES:s % SUBLANES + 1, :]
        for rt_i in range(s // SUBLANES, n_rt):
            if (s, rt_i) in where_:
                off = where_[s, rt_i]
                u[rt_i] = u[rt_i] - sc_a[off:off + SUBLANES, :] * u_s
    return jnp.concatenate(u, axis=0)


def _rwkv_state(u, v, khat, ahat, p_last, st):
    return st * p_last + _state_outer(jnp.concatenate([v, -u], axis=0),
                                      _head_stack(jnp.concatenate([khat * p_last, ahat * p_last], axis=0)))


def _mixers_kernel(*refs, bb, tc, c, layer0, n_t):
    it = iter(refs)
    proj_ref = next(it)
    vfirst_in_ref = None if layer0 else next(it)
    cos_ref, sin_ref = next(it), next(it)
    s_ret0, s_hgrn0, s_gla0, s_rw0, shift0 = next(it), next(it), next(it), next(it), next(it)
    vec_ref, mu_ref, bgate_ref = next(it), next(it), next(it)
    wg_ref, w2_ref, a2_ref = next(it), next(it), next(it)
    v1_ref, v2_ref = (None, None) if layer0 else (next(it), next(it))
    bo_ref, bog_ref, swap_ref = next(it), next(it), next(it)
    dmat_ref, eq_ref, ek_ref, gblk_ref = next(it), next(it), next(it), next(it)
    ocat_ref = next(it)
    vfirst_out_ref = next(it) if layer0 else None
    o_ret_st, o_hgrn_st, o_gla_st, o_rw_st = next(it), next(it), next(it), next(it)
    st_ret, st_hgrn, st_gla, st_rw, carry = next(it), next(it), next(it), next(it), next(it)
    qkvb = [[next(it) for _ in range(4)] for _ in range(2)]
    o_gl = [next(it) for _ in range(2)]
    ps_ref, orw_ref = next(it), next(it)
    rw = [next(it) for _ in range(6)]

    r = bb * tc
    ti = pl.program_id(1)
    state_io = ((st_ret, s_ret0, o_ret_st, HEAD_DIM), (st_hgrn, s_hgrn0, o_hgrn_st, HEAD_DIM),
                (st_gla, s_gla0, o_gla_st, GLA_KEY_DIM), (st_rw, s_rw0, o_rw_st, HEAD_DIM))

    def vec(i):
        return vec_ref[i:i + 1, :]

    @pl.when(ti == 0)
    def _init():
        for st_ref, s0_ref, _, dk in state_io:
            for b in range(bb):
                for h in range(N_HEADS):
                    st_ref[b, :, h * dk:(h + 1) * dk] = s0_ref[b, h]
        carry[...] = shift0[...]

    def blk(lo, width):
        return proj_ref[:, :, lo:lo + width].reshape(r, width)

    bo = bo_ref[...]
    bo2 = jnp.concatenate([bo, bo], axis=0)

    def head_sum(x, two_pass=False):
        if not two_pass:
            return _dot(x, bo)
        hi = x.astype(BF16)
        lo = (x - hi.astype(F32)).astype(BF16)
        return jnp.dot(jnp.concatenate([hi, lo], axis=1), bo2, preferred_element_type=F32)

    def head_mean(x, two_pass=False):
        return head_sum(x, two_pass) * (1.0 / HEAD_DIM)

    cos = jnp.concatenate([cos_ref[...]] * bb, axis=0)
    sin = jnp.concatenate([sin_ref[...]] * bb, axis=0)
    def rope(x):
        return x * cos + _dot(x, swap_ref[...]) * sin

    q_r = rope(blk(RET_OFF, BRANCH))
    k_r = rope(blk(RET_OFF + 256, BRANCH)) * (HEAD_DIM ** -0.5)
    v_r = blk(RET_OFF + 512, BRANCH)
    seqs = range(bb)
    blocks = [slice(b * tc, (b + 1) * tc) for b in seqs]
    ret_sc = [_dot_nt(_head_stack(q_r[rows, :]), k_r[rows, :]) * dmat_ref[...] for rows in blocks]
    ret_o4 = [_dot(ret_sc[b], v_r[blocks[b], :]) for b in seqs]
    ret_os = [_state_read(_head_stack(q_r[blocks[b], :] * eq_ref[...]), st_ret[b], tc) for b in seqs]
    for b in seqs:
        upd = _state_outer(v_r[blocks[b], :], _head_stack(k_r[blocks[b], :] * ek_ref[...]))
        st_ret[b] = st_ret[b] * gblk_ref[...] + upd
    o_ret = jnp.concatenate([ret_os[b] + _head_unstack(ret_o4[b], tc) for b in seqs], axis=0)

    q_h, k_h, v_h, b_h = qkvb[0]
    q_h[...] = _silu(blk(HGRN_OFF, BRANCH))
    ff = blk(HGRN_OFF + 256, BRANCH)
    y = vec(V_LB_B) + _log_sigmoid(ff)
    a_lb = jnp.broadcast_to(vec(V_LB_A), y.shape)
    mx = jnp.maximum(a_lb, y)
    log_f = mx + _log1p_exp(jnp.minimum(a_lb, y) - mx)
    k_h[...] = vec(V_LB_OM) * _sigmoid(-ff)
    v_h[...] = blk(HGRN_OFF + 512, BRANCH)
    b_h[...] = _chunk_cumsum(log_f, c)

    q_g, k_g, v_g, b_g = qkvb[1]
    q_g[...] = blk(GLA_OFF, GLA_KEY_WIDTH) * (GLA_KEY_DIM ** -0.5)
    k_g[...] = blk(GLA_OFF + 128, GLA_KEY_WIDTH)
    v_g[...] = blk(GLA_OFF + 256, BRANCH)
    z = _dot(blk(GLA_OFF + 768, 128), wg_ref[...]) + bgate_ref[...]
    b_g[...] = _chunk_cumsum(_log_sigmoid(z) / GLA_TAU, c)

    p_rw = blk(RWKV_OFF, RWKV_PW)
    mu = mu_ref[...]
    ps_ref[...] = p_rw + (pltpu.roll(p_rw, 1, 0) - p_rw) * mu
    for b in range(bb):
        p0 = p_rw[b * tc:b * tc + 1, :]
        ps_ref[b * tc:b * tc + 1, :] = p0 + (carry[b:b + 1, :] - p0) * mu
        carry[b:b + 1, :] = p_rw[(b + 1) * tc - 1:(b + 1) * tc, :]

    rr = ps_ref[:, 0:256]
    k_raw = ps_ref[:, 256:512]
    vv = ps_ref[:, 512:768]
    low_rank = ps_ref[:, 1024:1152]
    w_pre = vec(V_W0) + _dot(jnp.tanh(low_rank), w2_ref[...])
    w_log = -(jnp.maximum(-w_pre, 0.0) + _log1p_exp(-jnp.abs(w_pre))) - 0.5
    log_w = -jnp.exp(w_log)
    a = _sigmoid(vec(V_A0) + _dot(low_rank, a2_ref[...]))
    if layer0:
        vfirst_out_ref[...] = vv.reshape(bb, tc, BRANCH)
    else:
        vmix = _sigmoid(vec(V_V0) + _dot(_dot(vv, v1_ref[...]), v2_ref[...]))
        vv = vv + (vfirst_in_ref[...].reshape(r, BRANCH) - vv) * vmix
    kk_raw = k_raw * vec(V_KK)
    kk = kk_raw / jnp.maximum(jnp.sqrt(head_sum(kk_raw * kk_raw, two_pass=True)), 1e-12)
    kmod = k_raw * (1.0 + (a - 1.0) * vec(V_KA))
    b_w = _chunk_cumsum(log_w, c)
    inv_p = jnp.exp(-b_w)
    p_w = jnp.exp(b_w)
    for ref, arr in zip(rw, (kk * jnp.exp(b_w - log_w), rr * p_w, kmod * inv_p, kk * a * inv_p, vv, p_w)):
        ref[...] = arr

    n_c = tc // c
    mix = ((qkvb[0], o_gl[0], st_hgrn, bo_ref, HEAD_DIM), (qkvb[1], o_gl[1], st_gla, bog_ref, GLA_KEY_DIM))

    def chunk_body(ci, _):
        rows = [pl.ds(pl.multiple_of(b * tc + ci * c, c), c) for b in seqs]

        def all_scores(tiles, bo_m):
            sc = _dot(jnp.concatenate(tiles, axis=0), bo_m)
            n = len(tiles) // bb * SUBLANES
            return [sc[b * n:(b + 1) * n, :] for b in seqs]

        gl = []
        for (q_s, k_s, v_s, b_s), o_s, st_ref, bo_m, dk in mix:
            ins = [(q_s[rows[b], :], k_s[rows[b], :], v_s[rows[b], :], b_s[rows[b], :]) for b in seqs]
            tiles = []
            for q, k, _, b_c in ins:
                t, where_ = _gl_tiles(q, k, b_c, c)
                tiles += t
            gl.append((ins, all_scores(tiles, bo_m[...]), where_, o_s, st_ref, dk))
        rw_in = [tuple(x[rows[b], :] for x in rw) for b in seqs]
        tiles = []
        for kkt, _, _, ahat, _, _ in rw_in:
            t, rw_where = _rwkv_tiles(kkt, ahat, c)
            tiles += t
        rw_sc = all_scores(tiles, bo)
        rw_lhs = [_head_stack(jnp.concatenate([kkt, rt], axis=0)) for kkt, rt, _, _, _, _ in rw_in]
        rw_cmp = [_rwkv_compact(rw_lhs[b], rw_in[b][2], rw_in[b][3]) for b in seqs]
        gl_os = [[_state_read(_head_stack(q * jnp.exp(b_c), dk), st_ref[b], c)
                  for b, (q, _, _, b_c) in enumerate(ins)] for ins, _, _, _, st_ref, dk in gl]
        rw_fs = [_state_read(rw_lhs[b], st_rw[b], 2 * c) for b in seqs]
        rw_m = [_rwkv_masks(rw_cmp[b], c) for b in seqs]
        rw_fv = [_head_unstack(_dot(rw_m[b][0], rw_in[b][4]), 2 * c) for b in seqs]
        for ins, _, _, _, st_ref, _ in gl:
            for b, (_, k, v, b_c) in enumerate(ins):
                st_ref[b] = _gl_state(k, v, b_c, st_ref[b], c)
        for (ins, sc, where_, o_s, _, _), o_state in zip(gl, gl_os):
            for b in seqs:
                o_s[rows[b], :] = _gl_output(sc[b], o_state[b], ins[b][2], where_, c)
        rhs = [rw_fs[b] + rw_fv[b] for b in seqs]
        us = [_rwkv_solve(rw_sc[b], rhs[b][0:c, :], rw_where, c) for b in seqs]
        rw_ou = [_dot(rw_m[b][1], us[b]) for b in seqs]
        for b in seqs:
            _, _, khat, ahat, v_w, p_c = rw_in[b]
            st_rw[b] = _rwkv_state(us[b], v_w, khat, ahat, p_c[c - 1:c, :], st_rw[b])
        for b in seqs:
            orw_ref[rows[b], :] = rhs[b][c:2 * c, :] - _head_unstack(rw_ou[b], c)
        return 0

    lax.fori_loop(0, n_c, chunk_body, 0)

    def group_norm(o, eps):
        cen = o - head_mean(o, two_pass=True)
        return cen * lax.rsqrt(head_mean(cen * cen) + eps)

    def head_rms(o, g):
        return o * lax.rsqrt(head_mean(o * o) + RMS_EPS) * g

    def put(lo, val):
        ocat_ref[:, :, lo:lo + BRANCH] = val.reshape(bb, tc, BRANCH)

    put(0, group_norm(o_ret, RET_GN_EPS) * _silu(blk(RET_OFF + 768, BRANCH)))
    put(256, head_rms(o_gl[0][...], vec(V_HGRN_G)) * _silu(blk(HGRN_OFF + 768, BRANCH)))
    put(512, head_rms(o_gl[1][...], vec(V_GLA_G)) * _silu(blk(GLA_OFF + 512, BRANCH)))
    o_rw = group_norm(orw_ref[...], RWKV_GN_EPS) * vec(V_GN_G) + vec(V_GN_B)
    bonus = head_sum(rr * kmod * vec(V_RK))
    put(768, (o_rw + bonus * vv) * _silu(ps_ref[:, 768:1024]))

    @pl.when(ti == n_t - 1)
    def _final():
        for st_ref, _, out_ref, dk in state_io:
            for b in range(bb):
                for h in range(N_HEADS):
                    out_ref[b, h] = st_ref[b][:, h * dk:(h + 1) * dk]


def _mixers(proj, vfirst, cos, sin, states, lp, consts, *, bb, tc, c):
    bsz, seq, _ = proj.shape
    layer0 = vfirst is None
    n_t = seq // tc
    r = bb * tc

    def full(arr):
        nd = arr.ndim
        return pl.BlockSpec(arr.shape, lambda bi, ti, _nd=nd: (0,) * _nd)

    def per_b(arr):
        nd = arr.ndim
        return pl.BlockSpec((bb,) + arr.shape[1:], lambda bi, ti, _nd=nd: (bi,) + (0,) * (_nd - 1))

    def tok(width):
        return pl.BlockSpec((bb, tc, width), lambda bi, ti: (bi, ti, 0))

    ins, specs = [proj], [tok(D_IN_PAD)]
    if not layer0:
        ins.append(vfirst)
        specs.append(tok(BRANCH))
    ins += [cos, sin]
    specs += [pl.BlockSpec((tc, BRANCH), lambda bi, ti: (ti, 0))] * 2
    for s in states:
        ins.append(s)
        specs.append(per_b(s))
    small = [lp['vec'], lp['mu'], lp['bgate'], lp['wg'], lp['w2'], lp['a2']]
    if not layer0:
        small += [lp['v1'], lp['v2']]
    small += list(consts) + list(_retention_tables(tc))
    for s in small:
        ins.append(s)
        specs.append(full(s))

    st_shapes = [(bsz, N_HEADS, HEAD_DIM, HEAD_DIM), (bsz, N_HEADS, HEAD_DIM, HEAD_DIM),
                 (bsz, N_HEADS, HEAD_DIM, GLA_KEY_DIM), (bsz, N_HEADS, HEAD_DIM, HEAD_DIM)]
    out_shape = [jax.ShapeDtypeStruct((bsz, seq, D_MODEL), F32)]
    out_specs = [tok(D_MODEL)]
    if layer0:
        out_shape.append(jax.ShapeDtypeStruct((bsz, seq, BRANCH), F32))
        out_specs.append(tok(BRANCH))
    for shp in st_shapes:
        out_shape.append(jax.ShapeDtypeStruct(shp, F32))
        out_specs.append(pl.BlockSpec((bb,) + shp[1:], lambda bi, ti: (bi, 0, 0, 0)))

    scratch = [pltpu.VMEM((bb, HEAD_DIM, BRANCH), F32), pltpu.VMEM((bb, HEAD_DIM, BRANCH), F32),
               pltpu.VMEM((bb, HEAD_DIM, GLA_KEY_WIDTH), F32), pltpu.VMEM((bb, HEAD_DIM, BRANCH), F32),
               pltpu.VMEM((bb, RWKV_PW), F32)]
    for dkt in (BRANCH, GLA_KEY_WIDTH):
        scratch += [pltpu.VMEM((r, dkt), F32), pltpu.VMEM((r, dkt), F32), pltpu.VMEM((r, BRANCH), F32),
                    pltpu.VMEM((r, dkt), F32)]
    scratch += [pltpu.VMEM((r, BRANCH), F32)] * 2
    scratch += [pltpu.VMEM((r, RWKV_PW), F32), pltpu.VMEM((r, BRANCH), F32)]
    scratch += [pltpu.VMEM((r, BRANCH), F32)] * 6

    outs = pl.pallas_call(
        functools.partial(_mixers_kernel, bb=bb, tc=tc, c=c, layer0=layer0, n_t=n_t),
        grid=(bsz // bb, n_t),
        in_specs=specs,
        out_specs=out_specs,
        out_shape=out_shape,
        scratch_shapes=scratch,
        compiler_params=pltpu.CompilerParams(dimension_semantics=("arbitrary", "arbitrary"),
                                             vmem_limit_bytes=V7X_VMEM_LIMIT),
        name="mixers",
    )(*ins)
    if layer0:
        return outs[0], outs[1], outs[2:]
    return outs[0], None, outs[1:]


def _out_attn_kernel(x_ref, oc_ref, mk_ref, mv_ref, wout_ref, wq_ref, wo_ref, gx_ref, gf_ref, out_ref,
                     *, bbc, lc, final):
    r = bbc * lc
    x1 = x_ref[...].reshape(r, D_MODEL) + _dot(oc_ref[...].reshape(r, D_MODEL), wout_ref[...])
    q = _dot(_rms(x1, gx_ref[...]), wq_ref[...])
    lane_head = jnp.right_shift(lax.broadcasted_iota(jnp.int32, (lc, BRANCH), 1), 6)
    outs = []
    for b in range(bbc):
        qb = q[b * lc:(b + 1) * lc, :]
        qs = jnp.concatenate([jnp.where(lane_head == h, qb, 0.0) for h in range(N_HEADS)], axis=0)
        s = _dot_nt(qs, mk_ref[b]) * (HEAD_DIM ** -0.5)
        e = jnp.exp(s - jnp.max(s, axis=-1, keepdims=True))
        p = e / jnp.sum(e, axis=-1, keepdims=True)
        o4 = _dot(p, mv_ref[b])
        ob = jnp.where(lane_head == 0, o4[0:lc, :], 0.0)
        for h in range(1, N_HEADS):
            ob = ob + jnp.where(lane_head == h, o4[h * lc:(h + 1) * lc, :], 0.0)
        outs.append(ob)
    o = outs[0] if bbc == 1 else jnp.concatenate(outs, axis=0)
    x2 = x1 + _dot(o, wo_ref[...])
    if final:
        x2 = _rms(x2, gf_ref[...])
    out_ref[...] = x2.reshape(bbc, lc, D_MODEL)


def _out_attn(x, ocat, mem, lp, norm_f, *, bbc, lc, final):
    bsz, seq, _ = x.shape
    mk, mv, mk_spec, mv_spec = mem

    def tok(width):
        return pl.BlockSpec((bbc, lc, width), lambda bi, li: (bi, li, 0))

    def full(arr):
        nd = arr.ndim
        return pl.BlockSpec(arr.shape, lambda bi, li, _nd=nd: (0,) * _nd)

    small = [lp['w_out'], lp['wq'], lp['wo'], lp['norm_x'], norm_f]
    return pl.pallas_call(
        functools.partial(_out_attn_kernel, bbc=bbc, lc=lc, final=final),
        grid=(bsz // bbc, seq // lc),
        in_specs=[tok(D_MODEL), tok(D_MODEL), mk_spec, mv_spec] + [full(s) for s in small],
        out_specs=tok(D_MODEL),
        out_shape=jax.ShapeDtypeStruct((bsz, seq, D_MODEL), F32),
        compiler_params=pltpu.CompilerParams(dimension_semantics=("arbitrary", "arbitrary"),
                                             vmem_limit_bytes=V7X_VMEM_LIMIT),
        name="out_attn",
    )(x, ocat, mk, mv, *small)


def _rope_tables(pos0, seq):
    half = HEAD_DIM // 2
    inv = ROPE_BASE ** (-jnp.arange(half, dtype=F32) / half)
    pos = (pos0 + jnp.arange(seq, dtype=jnp.int32)).astype(F32)
    ang = pos[:, None] * inv[None, :]
    cos, sin = jnp.cos(ang), jnp.sin(ang)
    return (jnp.tile(jnp.concatenate([cos, cos], axis=-1), (1, N_HEADS)),
            jnp.tile(jnp.concatenate([-sin, sin], axis=-1), (1, N_HEADS)))


def _retention_tables(n):
    log_gamma = jnp.log1p(-jnp.exp2(-5.0 - jnp.arange(N_HEADS, dtype=F32)))
    j = jnp.arange(n, dtype=F32)
    diff = j[:, None] - j[None, :]
    dmat = jnp.where(diff[None] >= 0, jnp.exp(diff[None] * log_gamma[:, None, None]), 0.0)
    lg_lanes = jnp.repeat(log_gamma, HEAD_DIM)[None, :]
    e_q = jnp.exp((j[:, None] + 1.0) * lg_lanes)
    e_k = jnp.exp((n - 1.0 - j[:, None]) * lg_lanes)
    g_blk = jnp.exp(float(n) * lg_lanes)
    return dmat.reshape(N_HEADS * n, n), e_q, e_k, g_blk


def _rwkv_to_padded(t):
    pad = jnp.zeros(t.shape[:-1] + (RWKV_PW - RWKV_W,), t.dtype)
    return jnp.concatenate([t[..., 0:256], t[..., 288:544], t[..., 544:800], t[..., 832:1088],
                            t[..., 256:288], t[..., 800:832], pad], axis=-1)


def _rwkv_from_padded(t):
    return jnp.concatenate([t[..., 0:256], t[..., 1024:1056], t[..., 256:512], t[..., 512:768],
                            t[..., 1056:1088], t[..., 768:1024]], axis=-1)


def _pad_rows(m, rows, at=0):
    out = jnp.zeros((rows, m.shape[1]), m.dtype)
    return out.at[at:at + m.shape[0]].set(m)


def _layer_params(l, P):
    w = P['w_in'][l].astype(BF16)
    gla = w[:, 2048:2832]
    w_pad = jnp.concatenate([
        w[:, 0:2048],
        gla[:, 0:512], gla[:, 528:784], gla[:, 512:528], jnp.zeros((D_MODEL, 112), w.dtype),
        _rwkv_to_padded(w[:, 2832:3920])], axis=1)

    sm = jax.nn.softmax(P['hgrn_lb_logits'].astype(F32), axis=0)
    lb = (jnp.cumsum(sm, axis=0) - sm[0])[l]
    rows = [None] * N_VEC
    rows[V_LB_A] = jnp.log(jnp.maximum(lb, LB_FLOOR))
    rows[V_LB_B] = jnp.log1p(-lb)
    rows[V_LB_OM] = 1.0 - lb
    rows[V_HGRN_G] = jnp.tile(P['hgrn_norm'][l], N_HEADS)
    rows[V_GLA_G] = jnp.tile(P['gla_norm'][l], N_HEADS)
    rows[V_W0] = P['rwkv_w0'][l]
    rows[V_A0] = P['rwkv_a0'][l]
    rows[V_V0] = P['rwkv_v0'][l - 1] if l > 0 else jnp.zeros((BRANCH,), F32)
    rows[V_KK] = P['rwkv_k_k'][l]
    rows[V_KA] = P['rwkv_k_a'][l]
    rows[V_RK] = P['rwkv_r_k'][l].reshape(BRANCH)
    rows[V_GN_G] = P['rwkv_gn_g'][l]
    rows[V_GN_B] = P['rwkv_gn_b'][l]
    zero = jnp.zeros((BRANCH,), F32)
    vec = jnp.stack([zero if x is None else x.astype(F32) for x in rows], axis=0)

    lp = {
        'w_in': w_pad,
        'norm_mix': P['norm_mix'][l][None, :],
        'vec': vec,
        'mu': _rwkv_to_padded(P['rwkv_mu'][l])[None, :],
        'bgate': P['gla_b_gate'][l][None, :],
        'wg': _pad_rows(P['gla_w_gate2'][l], 128).astype(BF16),
        'w2': _pad_rows(P['rwkv_w2'][l], 128, 0).astype(BF16),
        'a2': _pad_rows(P['rwkv_a2'][l], 128, 32).astype(BF16),
        'w_out': P['w_out'][l].astype(BF16),
        'wq': P['wq_x'][l].astype(BF16),
        'wo': P['wo_x'][l].astype(BF16),
        'norm_x': P['norm_x'][l][None, :],
    }
    if l > 0:
        lp['v1'] = jnp.pad(P['rwkv_v1'][l - 1], ((0, 0), (0, 96))).astype(BF16)
        lp['v2'] = _pad_rows(P['rwkv_v2'][l - 1], 128).astype(BF16)
    return lp


def _consts():
    i256 = jnp.arange(BRANCH) // HEAD_DIM
    i128 = jnp.arange(GLA_KEY_WIDTH) // GLA_KEY_DIM
    bo = (i256[:, None] == i256[None, :])
    bog = (i128[:, None] == i256[None, :])
    lane = jnp.arange(BRANCH)
    partner = jnp.where(lane % HEAD_DIM < HEAD_DIM // 2, lane + HEAD_DIM // 2, lane - HEAD_DIM // 2)
    swap = (lane[:, None] == partner[None, :])
    return bo.astype(BF16), bog.astype(BF16), swap.astype(BF16)


def _run_trunk(x, pos0, mems, init, lps, norm_f, consts, *, bb, tc, c, bbc, lc, tm):
    bsz, seq, _ = x.shape
    cos, sin = _rope_tables(pos0, seq)
    new = {n: [] for n in ('ret', 'hgrn', 'gla', 'rwkv', 'shift')}
    vfirst = None
    for l in range(DEPTH):
        lp = lps[l]
        proj = _in_proj(x.reshape(bsz * seq, D_MODEL), lp['norm_mix'], lp['w_in'], tm)
        proj = proj.reshape(bsz, seq, D_IN_PAD)
        states = [jnp.swapaxes(init['ret'][l], -1, -2), jnp.swapaxes(init['hgrn'][l], -1, -2),
                  jnp.swapaxes(init['gla'][l], -1, -2), init['rwkv'][l], _rwkv_to_padded(init['shift'][l])]
        ocat, vf, st = _mixers(proj, vfirst, cos, sin, states, lp, consts, bb=bb, tc=tc, c=c)
        if l == 0:
            vfirst = vf
        new['ret'].append(jnp.swapaxes(st[0], -1, -2))
        new['hgrn'].append(jnp.swapaxes(st[1], -1, -2))
        new['gla'].append(jnp.swapaxes(st[2], -1, -2))
        new['rwkv'].append(st[3])
        new['shift'].append(_rwkv_from_padded(proj[:, seq - 1, RWKV_OFF:RWKV_OFF + RWKV_PW]))
        x = _out_attn(x, ocat, mems[l], lp, norm_f, bbc=bbc, lc=lc, final=(l == DEPTH - 1))
    return x, {n: jnp.stack(v, axis=0) for n, v in new.items()}


def kernel(x_prompt, x_sample, mem_prompt, state_ret, state_hgrn, state_gla, state_rwkv, state_rwkv_shift,
           cache_mem_k, cache_mem_v, norm_mix, w_in, hgrn_lb_logits, hgrn_norm, gla_w_gate2, gla_b_gate,
           gla_norm, rwkv_mu, rwkv_w0, rwkv_w2, rwkv_a0, rwkv_a2, rwkv_v0, rwkv_v1, rwkv_v2, rwkv_k_k,
           rwkv_k_a, rwkv_r_k, rwkv_gn_g, rwkv_gn_b, w_out, norm_x, wq_x, wo_x, norm_mem, wk_x, wv_x, norm_f):
    P = {'norm_mix': norm_mix, 'w_in': w_in, 'hgrn_lb_logits': hgrn_lb_logits, 'hgrn_norm': hgrn_norm,
         'gla_w_gate2': gla_w_gate2, 'gla_b_gate': gla_b_gate, 'gla_norm': gla_norm,
         'rwkv_mu': rwkv_mu, 'rwkv_w0': rwkv_w0, 'rwkv_w2': rwkv_w2, 'rwkv_a0': rwkv_a0,
         'rwkv_a2': rwkv_a2, 'rwkv_v0': rwkv_v0, 'rwkv_v1': rwkv_v1, 'rwkv_v2': rwkv_v2,
         'rwkv_k_k': rwkv_k_k, 'rwkv_k_a': rwkv_k_a, 'rwkv_r_k': rwkv_r_k,
         'rwkv_gn_g': rwkv_gn_g, 'rwkv_gn_b': rwkv_gn_b, 'w_out': w_out,
         'norm_x': norm_x, 'wq_x': wq_x, 'wo_x': wo_x}
    lps = [_layer_params(l, P) for l in range(DEPTH)]
    consts = _consts()
    norm_f2 = norm_f[None, :]
    bsz, seq, _ = x_prompt.shape
    dbsz, dseq, _ = x_sample.shape
    p_bbc, s_bbc = 1, 16

    mem2d = mem_prompt.reshape(bsz * N_MEM, D_MODEL)
    kv_l, p_mems = [], []
    for l in range(DEPTH):
        wkv = jnp.concatenate([wk_x[l], wv_x[l]], axis=1).astype(BF16)
        kv = _in_proj(mem2d, norm_mem[l][None, :], wkv, 256).reshape(bsz, N_MEM, 2 * BRANCH)
        kv_l.append(kv)
        p_mems.append((kv, kv, pl.BlockSpec((p_bbc, N_MEM, BRANCH), lambda bi, li: (bi, 0, 0)),
                       pl.BlockSpec((p_bbc, N_MEM, BRANCH), lambda bi, li: (bi, 0, 1))))
    zero_init = {
        'ret': jnp.zeros((DEPTH, bsz, N_HEADS, HEAD_DIM, HEAD_DIM), F32),
        'hgrn': jnp.zeros((DEPTH, bsz, N_HEADS, HEAD_DIM, HEAD_DIM), F32),
        'gla': jnp.zeros((DEPTH, bsz, N_HEADS, GLA_KEY_DIM, HEAD_DIM), F32),
        'rwkv': jnp.zeros((DEPTH, bsz, N_HEADS, HEAD_DIM, HEAD_DIM), F32),
        'shift': jnp.zeros((DEPTH, bsz, RWKV_W), F32),
    }
    y_prompt, sp = _run_trunk(x_prompt, 0, p_mems, zero_init, lps, norm_f2, consts,
                              bb=8, tc=64, c=CHUNK, bbc=p_bbc, lc=1024, tm=512)

    cached = {'ret': state_ret, 'hgrn': state_hgrn, 'gla': state_gla, 'rwkv': state_rwkv,
              'shift': state_rwkv_shift}
    cmk = cache_mem_k.reshape(DEPTH, dbsz, N_MEM, BRANCH)
    cmv = cache_mem_v.reshape(DEPTH, dbsz, N_MEM, BRANCH)
    s_mems = []
    for l in range(DEPTH):
        spec = pl.BlockSpec((None, s_bbc, N_MEM, BRANCH), lambda bi, li, _l=l: (_l, bi, 0, 0))
        s_mems.append((cmk, cmv, spec, spec))
    y_sample, ss = _run_trunk(x_sample, PAST_LEN, s_mems, cached, lps, norm_f2, consts,
                              bb=8, tc=dseq, c=dseq, bbc=s_bbc, lc=dseq, tm=256)

    kv_all = jnp.stack(kv_l, axis=0)
    mem_k_p = kv_all[..., :BRANCH].reshape(DEPTH, bsz, N_MEM, N_HEADS, HEAD_DIM)
    mem_v_p = kv_all[..., BRANCH:].reshape(DEPTH, bsz, N_MEM, N_HEADS, HEAD_DIM)
    return (y_prompt, y_sample, sp['ret'], ss['ret'], sp['hgrn'], ss['hgrn'], sp['gla'], ss['gla'],
            sp['rwkv'], ss['rwkv'], sp['shift'], ss['shift'], mem_k_p, mem_v_p)
```

```python
import functools

import jax
import jax.numpy as jnp
from jax import lax
from jax.experimental import pallas as pl
from jax.experimental.pallas import tpu as pltpu

F32 = jnp.float32
BF16 = jnp.bfloat16

D_MODEL = 1024
DEPTH = 2
PAST_LEN = 16384
N_HEADS = 4
HEAD_DIM = 64
BRANCH = N_HEADS * HEAD_DIM
GLA_KEY_DIM = 32
GLA_KEY_WIDTH = N_HEADS * GLA_KEY_DIM
GLA_TAU = 16.0
N_MEM = 256
ROPE_BASE = 10000.0
RMS_EPS = 1e-6
RET_GN_EPS = 1e-5
RWKV_GN_EPS = 64e-5
LB_FLOOR = 1e-20
RWKV_W = 1088
CHUNK = 16

RET_OFF = 0
HGRN_OFF = 1024
GLA_OFF = 2048
RWKV_OFF = 2944
RWKV_PW = 1152
D_IN_PAD = 4096

V7X_VMEM_LIMIT = 60 * 1024 * 1024
SUBLANES = 8
NEG_BIG = -1e30

(V_LB_A, V_LB_B, V_LB_OM, V_HGRN_G, V_GLA_G, V_W0, V_A0, V_V0, V_KK, V_KA, V_RK,
 V_GN_G, V_GN_B) = range(13)
N_VEC = 16


def _rms(x, g):
    return x * lax.rsqrt(jnp.mean(x * x, axis=-1, keepdims=True) + RMS_EPS) * g


def _sigmoid(x):
    return 0.5 * jnp.tanh(0.5 * x) + 0.5


def _silu(x):
    h = 0.5 * x
    return h * jnp.tanh(h) + h


def _log1p_exp(x):
    return jnp.log(1.0 + jnp.exp(x))


def _log_sigmoid(x):
    return jnp.minimum(x, 0.0) - _log1p_exp(-jnp.abs(x))


def _dot(a, b):
    return jnp.dot(a.astype(BF16), b.astype(BF16), preferred_element_type=F32)


def _dot_nt(a, b):
    return lax.dot_general(a.astype(BF16), b.astype(BF16), (((1,), (1,)), ((), ())),
                           preferred_element_type=F32)


def _dot_tn(a, b):
    return lax.dot_general(a.astype(BF16), b.astype(BF16), (((0,), (0,)), ((), ())),
                           preferred_element_type=F32)


def _row_tiles(x, n):
    return [x[i * SUBLANES:(i + 1) * SUBLANES, :] for i in range(n)]


def _w_in_layout_kernel(w_ref, o_ref):
    w = w_ref[...]
    zeros = lambda n: jnp.zeros((w.shape[0], n), w.dtype)
    o_ref[...] = jnp.concatenate([
        w[:, 0:2560],
        w[:, 2576:2832], w[:, 2560:2576], zeros(112),
        w[:, 2832:3088], w[:, 3120:3376], w[:, 3376:3632], w[:, 3664:3920],
        w[:, 3088:3120], w[:, 3632:3664], zeros(64)], axis=1).astype(BF16)


def _w_in_layout(w_in):
    depth, d, n = w_in.shape
    rows = 128
    return pl.pallas_call(
        _w_in_layout_kernel,
        grid=(depth, d // rows),
        in_specs=[pl.BlockSpec((None, rows, n), lambda l, i: (l, i, 0))],
        out_specs=pl.BlockSpec((None, rows, D_IN_PAD), lambda l, i: (l, i, 0)),
        out_shape=jax.ShapeDtypeStruct((depth, d, D_IN_PAD), BF16),
        compiler_params=pltpu.CompilerParams(dimension_semantics=("arbitrary", "arbitrary")),
        name="w_in_layout",
    )(w_in)


def _in_proj_kernel(x_ref, g_ref, w_ref, o_ref):
    o_ref[...] = _dot(_rms(x_ref[...], g_ref[...]), w_ref[...])


def _in_proj(x2d, gain, w_bf16, tm, layer=None):
    m, d = x2d.shape
    n = w_bf16.shape[-1]
    w_spec = (pl.BlockSpec((d, n), lambda i: (0, 0)) if layer is None
              else pl.BlockSpec((None, d, n), lambda i: (layer, 0, 0)))
    return pl.pallas_call(
        _in_proj_kernel,
        grid=(m // tm,),
        in_specs=[pl.BlockSpec((tm, d), lambda i: (i, 0)),
                  pl.BlockSpec((1, d), lambda i: (0, 0)),
                  w_spec],
        out_specs=pl.BlockSpec((tm, n), lambda i: (i, 0)),
        out_shape=jax.ShapeDtypeStruct((m, n), F32),
        compiler_params=pltpu.CompilerParams(dimension_semantics=("arbitrary",),
                                             vmem_limit_bytes=V7X_VMEM_LIMIT),
        name="in_proj",
    )(x2d, gain, w_bf16)


def _chunk_cumsum(x, c):
    t_local = jnp.bitwise_and(lax.broadcasted_iota(jnp.int32, x.shape, 0), c - 1)
    s = 1
    while s < c:
        x = x + jnp.where(t_local >= s, pltpu.roll(x, s, 0), 0.0)
        s *= 2
    return x


def _head_stack(x, dk=HEAD_DIM):
    lane_head = jnp.right_shift(lax.broadcasted_iota(jnp.int32, x.shape, 1), dk.bit_length() - 1)
    return jnp.concatenate([jnp.where(lane_head == h, x, 0.0) for h in range(N_HEADS)], axis=0)


def _head_unstack(x4, n):
    lane_head = jnp.right_shift(lax.broadcasted_iota(jnp.int32, (n, BRANCH), 1), 6)
    out = jnp.where(lane_head == 0, x4[0:n, :], 0.0)
    for h in range(1, N_HEADS):
        out = out + jnp.where(lane_head == h, x4[h * n:(h + 1) * n, :], 0.0)
    return out


def _heads_to_rows(x):
    return jnp.concatenate([x[:, h * HEAD_DIM:(h + 1) * HEAD_DIM] for h in range(N_HEADS)], axis=0)


def _rows_to_heads(x4, n):
    return jnp.concatenate([x4[h * n:(h + 1) * n, :] for h in range(N_HEADS)], axis=1)


def _state_read(lhs_stacked, st, n):
    return _rows_to_heads(_dot_nt(lhs_stacked, st), n)


def _state_outer(values, keys_stacked):
    return _dot_tn(_heads_to_rows(values), keys_stacked)


def _gl_tiles(q, k, b, c):
    n_rt = c // SUBLANES
    rows = lax.broadcasted_iota(jnp.int32, (SUBLANES, q.shape[1]), 0)
    q_t, b_t = _row_tiles(q, n_rt), _row_tiles(b, n_rt)
    tiles, where_ = [], {}
    for m in range(c):
        for rt in range(m // SUBLANES, n_rt):
            arg = b_t[rt] - b[m:m + 1, :]
            if rt == m // SUBLANES:
                arg = jnp.where(rows >= m % SUBLANES, arg, NEG_BIG)
            where_[m, rt] = len(tiles) * SUBLANES
            tiles.append(q_t[rt] * k[m:m + 1, :] * jnp.exp(arg))
    return tiles, where_


def _gl_output(scores, o_state, v, where_, c):
    o = _row_tiles(o_state, c // SUBLANES)
    for (m, rt), off in where_.items():
        o[rt] = o[rt] + scores[off:off + SUBLANES, :] * v[m:m + 1, :]
    return jnp.concatenate(o, axis=0)


def _gl_state(k, v, b, st, c):
    b_last = b[c - 1:c, :]
    dk = k.shape[1] // N_HEADS
    return st * jnp.exp(b_last) + _state_outer(v, _head_stack(k * jnp.exp(b_last - b), dk))


def _rwkv_tiles(kkt, ahat, c):
    n_rt = c // SUBLANES
    rows = lax.broadcasted_iota(jnp.int32, (SUBLANES, BRANCH), 0)
    kk_t = _row_tiles(kkt, n_rt)
    tiles, where_ = [], {}
    for m in range(c - 1):
        for rt_i in range(m // SUBLANES, n_rt):
            on_diag = rt_i == m // SUBLANES
            if on_diag and m % SUBLANES == SUBLANES - 1:
                continue
            kk_m = jnp.where(rows > m % SUBLANES, kk_t[rt_i], 0.0) if on_diag else kk_t[rt_i]
            where_[m, rt_i] = len(tiles) * SUBLANES
            tiles.append(kk_m * ahat[m:m + 1, :])
    return tiles, where_


def _rwkv_compact(lhs_stacked, khat, ahat):
    return _dot_nt(lhs_stacked, jnp.concatenate([khat, ahat], axis=0))


def _rwkv_masks(sc, c):
    row = lax.broadcasted_iota(jnp.int32, (2 * N_HEADS * c, c), 0)
    col = lax.broadcasted_iota(jnp.int32, (2 * N_HEADS * c, c), 1)
    keep = col < jnp.bitwise_and(row, c - 1) + jnp.where(jnp.bitwise_and(row, c) != 0, 1, 0)
    r_a = jnp.concatenate([sc[(2 * h + 1) * c:(2 * h + 2) * c, c:2 * c] for h in range(N_HEADS)], axis=0)
    row4 = lax.broadcasted_iota(jnp.int32, (N_HEADS * c, c), 0)
    col4 = lax.broadcasted_iota(jnp.int32, (N_HEADS * c, c), 1)
    return jnp.where(keep, sc[:, 0:c], 0.0), jnp.where(col4 <= jnp.bitwise_and(row4, c - 1), r_a, 0.0)


def _rwkv_solve(sc_a, rhs, where_, c):
    n_rt = c // SUBLANES
    u = _row_tiles(rhs, n_rt)
    for s in range(c - 1):
        u_s = u[s // SUBLANES][s % SUBLANES:s % SUBLANES + 1, :]
        for rt_i in range(s // SUBLANES, n_rt):
            if (s, rt_i) in where_:
                off = where_[s, rt_i]
                u[rt_i] = u[rt_i] - sc_a[off:off + SUBLANES, :] * u_s
    return jnp.concatenate(u, axis=0)


def _rwkv_state(u, v, khat, ahat, p_last, st):
    return st * p_last + _state_outer(jnp.concatenate([v, -u], axis=0),
                                      _head_stack(jnp.concatenate([khat * p_last, ahat * p_last], axis=0)))


def _mixers_kernel(*refs, bb, tc, c, layer0, n_t):
    it = iter(refs)
    proj_ref = next(it)
    vfirst_in_ref = None if layer0 else next(it)
    cos_ref, sin_ref = next(it), next(it)
    s_ret0, s_hgrn0, s_gla0, s_rw0, shift0 = next(it), next(it), next(it), next(it), next(it)
    vec_ref, mu_ref, bgate_ref = next(it), next(it), next(it)
    wg_ref, w2_ref, a2_ref = next(it), next(it), next(it)
    v1_ref, v2_ref = (None, None) if layer0 else (next(it), next(it))
    bo_ref, bog_ref, swap_ref = next(it), next(it), next(it)
    dmat_ref, eq_ref, ek_ref, gblk_ref = next(it), next(it), next(it), next(it)
    ocat_ref = next(it)
    vfirst_out_ref = next(it) if layer0 else None
    o_ret_st, o_hgrn_st, o_gla_st, o_rw_st = next(it), next(it), next(it), next(it)
    st_ret, st_hgrn, st_gla, st_rw, carry = next(it), next(it), next(it), next(it), next(it)
    qkvb = [[next(it) for _ in range(4)] for _ in range(2)]
    o_gl = [next(it) for _ in range(2)]
    ps_ref, orw_ref = next(it), next(it)
    rw = [next(it) for _ in range(6)]

    r = bb * tc
    ti = pl.program_id(1)
    state_io = ((st_ret, s_ret0, o_ret_st, HEAD_DIM), (st_hgrn, s_hgrn0, o_hgrn_st, HEAD_DIM),
                (st_gla, s_gla0, o_gla_st, GLA_KEY_DIM), (st_rw, s_rw0, o_rw_st, HEAD_DIM))

    def vec(i):
        return vec_ref[i:i + 1, :]

    @pl.when(ti == 0)
    def _init():
        for st_ref, s0_ref, _, dk in state_io:
            for b in range(bb):
                for h in range(N_HEADS):
                    st_ref[b, :, h * dk:(h + 1) * dk] = s0_ref[b, h]
        carry[...] = shift0[...]

    def blk(lo, width):
        return proj_ref[:, :, lo:lo + width].reshape(r, width)

    bo = bo_ref[...]
    bo2 = jnp.concatenate([bo, bo], axis=0)

    def head_sum(x, two_pass=False):
        if not two_pass:
            return _dot(x, bo)
        hi = x.astype(BF16)
        lo = (x - hi.astype(F32)).astype(BF16)
        return jnp.dot(jnp.concatenate([hi, lo], axis=1), bo2, preferred_element_type=F32)

    def head_mean(x, two_pass=False):
        return head_sum(x, two_pass) * (1.0 / HEAD_DIM)

    cos = jnp.concatenate([cos_ref[...]] * bb, axis=0)
    sin = jnp.concatenate([sin_ref[...]] * bb, axis=0)
    def rope(x):
        return x * cos + _dot(x, swap_ref[...]) * sin

    q_r = rope(blk(RET_OFF, BRANCH))
    k_r = rope(blk(RET_OFF + 256, BRANCH)) * (HEAD_DIM ** -0.5)
    v_r = blk(RET_OFF + 512, BRANCH)
    seqs = range(bb)
    blocks = [slice(b * tc, (b + 1) * tc) for b in seqs]
    ret_sc = [_dot_nt(_head_stack(q_r[rows, :]), k_r[rows, :]) * dmat_ref[...] for rows in blocks]
    ret_o4 = [_dot(ret_sc[b], v_r[blocks[b], :]) for b in seqs]
    ret_os = [_state_read(_head_stack(q_r[blocks[b], :] * eq_ref[...]), st_ret[b], tc) for b in seqs]
    for b in seqs:
        upd = _state_outer(v_r[blocks[b], :], _head_stack(k_r[blocks[b], :] * ek_ref[...]))
        st_ret[b] = st_ret[b] * gblk_ref[...] + upd
    o_ret = jnp.concatenate([ret_os[b] + _head_unstack(ret_o4[b], tc) for b in seqs], axis=0)

    q_h, k_h, v_h, b_h = qkvb[0]
    q_h[...] = _silu(blk(HGRN_OFF, BRANCH))
    ff = blk(HGRN_OFF + 256, BRANCH)
    y = vec(V_LB_B) + _log_sigmoid(ff)
    a_lb = jnp.broadcast_to(vec(V_LB_A), y.shape)
    mx = jnp.maximum(a_lb, y)
    log_f = mx + _log1p_exp(jnp.minimum(a_lb, y) - mx)
    k_h[...] = vec(V_LB_OM) * _sigmoid(-ff)
    v_h[...] = blk(HGRN_OFF + 512, BRANCH)
    b_h[...] = _chunk_cumsum(log_f, c)

    q_g, k_g, v_g, b_g = qkvb[1]
    q_g[...] = blk(GLA_OFF, GLA_KEY_WIDTH) * (GLA_KEY_DIM ** -0.5)
    k_g[...] = blk(GLA_OFF + 128, GLA_KEY_WIDTH)
    v_g[...] = blk(GLA_OFF + 256, BRANCH)
    z = _dot(blk(GLA_OFF + 768, 128), wg_ref[...]) + bgate_ref[...]
    b_g[...] = _chunk_cumsum(_log_sigmoid(z) / GLA_TAU, c)

    p_rw = blk(RWKV_OFF, RWKV_PW)
    mu = mu_ref[...]
    ps_ref[...] = p_rw + (pltpu.roll(p_rw, 1, 0) - p_rw) * mu
    for b in range(bb):
        p0 = p_rw[b * tc:b * tc + 1, :]
        ps_ref[b * tc:b * tc + 1, :] = p0 + (carry[b:b + 1, :] - p0) * mu
        carry[b:b + 1, :] = p_rw[(b + 1) * tc - 1:(b + 1) * tc, :]

    rr = ps_ref[:, 0:256]
    k_raw = ps_ref[:, 256:512]
    vv = ps_ref[:, 512:768]
    low_rank = ps_ref[:, 1024:1152]
    w_pre = vec(V_W0) + _dot(jnp.tanh(low_rank), w2_ref[...])
    w_log = -(jnp.maximum(-w_pre, 0.0) + _log1p_exp(-jnp.abs(w_pre))) - 0.5
    log_w = -jnp.exp(w_log)
    a = _sigmoid(vec(V_A0) + _dot(low_rank, a2_ref[...]))
    if layer0:
        vfirst_out_ref[...] = vv.reshape(bb, tc, BRANCH)
    else:
        vmix = _sigmoid(vec(V_V0) + _dot(_dot(vv, v1_ref[...]), v2_ref[...]))
        vv = vv + (vfirst_in_ref[...].reshape(r, BRANCH) - vv) * vmix
    kk_raw = k_raw * vec(V_KK)
    kk = kk_raw / jnp.maximum(jnp.sqrt(head_sum(kk_raw * kk_raw, two_pass=True)), 1e-12)
    kmod = k_raw * (1.0 + (a - 1.0) * vec(V_KA))
    b_w = _chunk_cumsum(log_w, c)
    inv_p = jnp.exp(-b_w)
    p_w = jnp.exp(b_w)
    for ref, arr in zip(rw, (kk * jnp.exp(b_w - log_w), rr * p_w, kmod * inv_p, kk * a * inv_p, vv, p_w)):
        ref[...] = arr

    n_c = tc // c
    mix = ((qkvb[0], o_gl[0], st_hgrn, bo_ref, HEAD_DIM), (qkvb[1], o_gl[1], st_gla, bog_ref, GLA_KEY_DIM))

    def chunk_body(ci, _):
        rows = [pl.ds(pl.multiple_of(b * tc + ci * c, c), c) for b in seqs]

        def all_scores(tiles, bo_m):
            sc = _dot(jnp.concatenate(tiles, axis=0), bo_m)
            n = len(tiles) // bb * SUBLANES
            return [sc[b * n:(b + 1) * n, :] for b in seqs]

        gl = []
        for (q_s, k_s, v_s, b_s), o_s, st_ref, bo_m, dk in mix:
            ins = [(q_s[rows[b], :], k_s[rows[b], :], v_s[rows[b], :], b_s[rows[b], :]) for b in seqs]
            tiles = []
            for q, k, _, b_c in ins:
                t, where_ = _gl_tiles(q, k, b_c, c)
                tiles += t
            gl.append((ins, all_scores(tiles, bo_m[...]), where_, o_s, st_ref, dk))
        rw_in = [tuple(x[rows[b], :] for x in rw) for b in seqs]
        tiles = []
        for kkt, _, _, ahat, _, _ in rw_in:
            t, rw_where = _rwkv_tiles(kkt, ahat, c)
            tiles += t
        rw_sc = all_scores(tiles, bo)
        rw_lhs = [_head_stack(jnp.concatenate([kkt, rt], axis=0)) for kkt, rt, _, _, _, _ in rw_in]
        rw_cmp = [_rwkv_compact(rw_lhs[b], rw_in[b][2], rw_in[b][3]) for b in seqs]
        gl_os = [[_state_read(_head_stack(q * jnp.exp(b_c), dk), st_ref[b], c)
                  for b, (q, _, _, b_c) in enumerate(ins)] for ins, _, _, _, st_ref, dk in gl]
        rw_fs = [_state_read(rw_lhs[b], st_rw[b], 2 * c) for b in seqs]
        rw_m = [_rwkv_masks(rw_cmp[b], c) for b in seqs]
        rw_fv = [_head_unstack(_dot(rw_m[b][0], rw_in[b][4]), 2 * c) for b in seqs]
        for ins, _, _, _, st_ref, _ in gl:
            for b, (_, k, v, b_c) in enumerate(ins):
                st_ref[b] = _gl_state(k, v, b_c, st_ref[b], c)
        for (ins, sc, where_, o_s, _, _), o_state in zip(gl, gl_os):
            for b in seqs:
                o_s[rows[b], :] = _gl_output(sc[b], o_state[b], ins[b][2], where_, c)
        rhs = [rw_fs[b] + rw_fv[b] for b in seqs]
        us = [_rwkv_solve(rw_sc[b], rhs[b][0:c, :], rw_where, c) for b in seqs]
        rw_ou = [_dot(rw_m[b][1], us[b]) for b in seqs]
        for b in seqs:
            _, _, khat, ahat, v_w, p_c = rw_in[b]
            st_rw[b] = _rwkv_state(us[b], v_w, khat, ahat, p_c[c - 1:c, :], st_rw[b])
        for b in seqs:
            orw_ref[rows[b], :] = rhs[b][c:2 * c, :] - _head_unstack(rw_ou[b], c)
        return 0

    lax.fori_loop(0, n_c, chunk_body, 0)

    def group_norm(o, eps):
        cen = o - head_mean(o, two_pass=True)
        return cen * lax.rsqrt(head_mean(cen * cen) + eps)

    def head_rms(o, g):
        return o * lax.rsqrt(head_mean(o * o) + RMS_EPS) * g

    def put(lo, val):
        ocat_ref[:, :, lo:lo + BRANCH] = val.reshape(bb, tc, BRANCH)

    put(0, group_norm(o_ret, RET_GN_EPS) * _silu(blk(RET_OFF + 768, BRANCH)))
    put(256, head_rms(o_gl[0][...], vec(V_HGRN_G)) * _silu(blk(HGRN_OFF + 768, BRANCH)))
    put(512, head_rms(o_gl[1][...], vec(V_GLA_G)) * _silu(blk(GLA_OFF + 512, BRANCH)))
    o_rw = group_norm(orw_ref[...], RWKV_GN_EPS) * vec(V_GN_G) + vec(V_GN_B)
    bonus = head_sum(rr * kmod * vec(V_RK))
    put(768, (o_rw + bonus * vv) * _silu(ps_ref[:, 768:1024]))

    @pl.when(ti == n_t - 1)
    def _final():
        for st_ref, _, out_ref, dk in state_io:
            for b in range(bb):
                for h in range(N_HEADS):
                    out_ref[b, h] = st_ref[b][:, h * dk:(h + 1) * dk]


def _mixers(proj, vfirst, cos, sin, states, lp, consts, *, bb, tc, c):
    bsz, seq, _ = proj.shape
    layer0 = vfirst is None
    n_t = seq // tc
    r = bb * tc

    def full(arr):
        nd = arr.ndim
        return pl.BlockSpec(arr.shape, lambda bi, ti, _nd=nd: (0,) * _nd)

    def per_b(arr):
        nd = arr.ndim
        return pl.BlockSpec((bb,) + arr.shape[1:], lambda bi, ti, _nd=nd: (bi,) + (0,) * (_nd - 1))

    def tok(width):
        return pl.BlockSpec((bb, tc, width), lambda bi, ti: (bi, ti, 0))

    ins, specs = [proj], [tok(D_IN_PAD)]
    if not layer0:
        ins.append(vfirst)
        specs.append(tok(BRANCH))
    ins += [cos, sin]
    specs += [pl.BlockSpec((tc, BRANCH), lambda bi, ti: (ti, 0))] * 2
    for s in states:
        ins.append(s)
        specs.append(per_b(s))
    small = [lp['vec'], lp['mu'], lp['bgate'], lp['wg'], lp['w2'], lp['a2']]
    if not layer0:
        small += [lp['v1'], lp['v2']]
    small += list(consts) + list(_retention_tables(tc))
    for s in small:
        ins.append(s)
        specs.append(full(s))

    st_shapes = [(bsz, N_HEADS, HEAD_DIM, HEAD_DIM), (bsz, N_HEADS, HEAD_DIM, HEAD_DIM),
                 (bsz, N_HEADS, HEAD_DIM, GLA_KEY_DIM), (bsz, N_HEADS, HEAD_DIM, HEAD_DIM)]
    out_shape = [jax.ShapeDtypeStruct((bsz, seq, D_MODEL), F32)]
    out_specs = [tok(D_MODEL)]
    if layer0:
        out_shape.append(jax.ShapeDtypeStruct((bsz, seq, BRANCH), F32))
        out_specs.append(tok(BRANCH))
    for shp in st_shapes:
        out_shape.append(jax.ShapeDtypeStruct(shp, F32))
        out_specs.append(pl.BlockSpec((bb,) + shp[1:], lambda bi, ti: (bi, 0, 0, 0)))

    scratch = [pltpu.VMEM((bb, HEAD_DIM, BRANCH), F32), pltpu.VMEM((bb, HEAD_DIM, BRANCH), F32),
               pltpu.VMEM((bb, HEAD_DIM, GLA_KEY_WIDTH), F32), pltpu.VMEM((bb, HEAD_DIM, BRANCH), F32),
               pltpu.VMEM((bb, RWKV_PW), F32)]
    for dkt in (BRANCH, GLA_KEY_WIDTH):
        scratch += [pltpu.VMEM((r, dkt), F32), pltpu.VMEM((r, dkt), F32), pltpu.VMEM((r, BRANCH), F32),
                    pltpu.VMEM((r, dkt), F32)]
    scratch += [pltpu.VMEM((r, BRANCH), F32)] * 2
    scratch += [pltpu.VMEM((r, RWKV_PW), F32), pltpu.VMEM((r, BRANCH), F32)]
    scratch += [pltpu.VMEM((r, BRANCH), F32)] * 6

    outs = pl.pallas_call(
        functools.partial(_mixers_kernel, bb=bb, tc=tc, c=c, layer0=layer0, n_t=n_t),
        grid=(bsz // bb, n_t),
        in_specs=specs,
        out_specs=out_specs,
        out_shape=out_shape,
        scratch_shapes=scratch,
        compiler_params=pltpu.CompilerParams(dimension_semantics=("arbitrary", "arbitrary"),
                                             vmem_limit_bytes=V7X_VMEM_LIMIT),
        name="mixers",
    )(*ins)
    if layer0:
        return outs[0], outs[1], outs[2:]
    return outs[0], None, outs[1:]


def _out_attn_kernel(x_ref, oc_ref, mk_ref, mv_ref, wout_ref, wq_ref, wo_ref, gx_ref, gf_ref, out_ref,
                     *, bbc, lc, final):
    r = bbc * lc
    x1 = x_ref[...].reshape(r, D_MODEL) + _dot(oc_ref[...].reshape(r, D_MODEL), wout_ref[...])
    q = _dot(_rms(x1, gx_ref[...]), wq_ref[...])
    seqs = range(bbc)
    scores = [_dot_nt(_head_stack(q[b * lc:(b + 1) * lc, :]), mk_ref[b]) * (HEAD_DIM ** -0.5) for b in seqs]
    probs = []
    for s in scores:
        e = jnp.exp(s - jnp.max(s, axis=-1, keepdims=True))
        probs.append(e / jnp.sum(e, axis=-1, keepdims=True))
    o4 = [_dot(probs[b], mv_ref[b]) for b in seqs]
    outs = [_head_unstack(o4[b], lc) for b in seqs]
    o = outs[0] if bbc == 1 else jnp.concatenate(outs, axis=0)
    x2 = x1 + _dot(o, wo_ref[...])
    if final:
        x2 = _rms(x2, gf_ref[...])
    out_ref[...] = x2.reshape(bbc, lc, D_MODEL)


def _out_attn(x, ocat, mem, lp, norm_f, *, bbc, lc, final):
    bsz, seq, _ = x.shape
    mk, mv, mk_spec, mv_spec = mem

    def tok(width):
        return pl.BlockSpec((bbc, lc, width), lambda bi, li: (bi, li, 0))

    def full(arr):
        nd = arr.ndim
        return pl.BlockSpec(arr.shape, lambda bi, li, _nd=nd: (0,) * _nd)

    small = [lp['w_out'], lp['wq'], lp['wo'], lp['norm_x'], norm_f]
    return pl.pallas_call(
        functools.partial(_out_attn_kernel, bbc=bbc, lc=lc, final=final),
        grid=(bsz // bbc, seq // lc),
        in_specs=[tok(D_MODEL), tok(D_MODEL), mk_spec, mv_spec] + [full(s) for s in small],
        out_specs=tok(D_MODEL),
        out_shape=jax.ShapeDtypeStruct((bsz, seq, D_MODEL), F32),
        compiler_params=pltpu.CompilerParams(dimension_semantics=("arbitrary", "arbitrary"),
                                             vmem_limit_bytes=V7X_VMEM_LIMIT),
        name="out_attn",
    )(x, ocat, mk, mv, *small)


def _rope_tables(pos0, seq):
    half = HEAD_DIM // 2
    inv = ROPE_BASE ** (-jnp.arange(half, dtype=F32) / half)
    pos = (pos0 + jnp.arange(seq, dtype=jnp.int32)).astype(F32)
    ang = pos[:, None] * inv[None, :]
    cos, sin = jnp.cos(ang), jnp.sin(ang)
    return (jnp.tile(jnp.concatenate([cos, cos], axis=-1), (1, N_HEADS)),
            jnp.tile(jnp.concatenate([-sin, sin], axis=-1), (1, N_HEADS)))


def _retention_tables(n):
    log_gamma = jnp.log1p(-jnp.exp2(-5.0 - jnp.arange(N_HEADS, dtype=F32)))
    j = jnp.arange(n, dtype=F32)
    diff = j[:, None] - j[None, :]
    dmat = jnp.where(diff[None] >= 0, jnp.exp(diff[None] * log_gamma[:, None, None]), 0.0)
    lg_lanes = jnp.repeat(log_gamma, HEAD_DIM)[None, :]
    e_q = jnp.exp((j[:, None] + 1.0) * lg_lanes)
    e_k = jnp.exp((n - 1.0 - j[:, None]) * lg_lanes)
    g_blk = jnp.exp(float(n) * lg_lanes)
    return dmat.reshape(N_HEADS * n, n), e_q, e_k, g_blk


def _rwkv_to_padded(t):
    pad = jnp.zeros(t.shape[:-1] + (RWKV_PW - RWKV_W,), t.dtype)
    return jnp.concatenate([t[..., 0:256], t[..., 288:544], t[..., 544:800], t[..., 832:1088],
                            t[..., 256:288], t[..., 800:832], pad], axis=-1)


def _rwkv_from_padded(t):
    return jnp.concatenate([t[..., 0:256], t[..., 1024:1056], t[..., 256:512], t[..., 512:768],
                            t[..., 1056:1088], t[..., 768:1024]], axis=-1)


def _pad_rows(m, rows, at=0):
    out = jnp.zeros((rows, m.shape[1]), m.dtype)
    return out.at[at:at + m.shape[0]].set(m)


def _layer_params(l, P):
    sm = jax.nn.softmax(P['hgrn_lb_logits'].astype(F32), axis=0)
    lb = (jnp.cumsum(sm, axis=0) - sm[0])[l]
    rows = [None] * N_VEC
    rows[V_LB_A] = jnp.log(jnp.maximum(lb, LB_FLOOR))
    rows[V_LB_B] = jnp.log1p(-lb)
    rows[V_LB_OM] = 1.0 - lb
    rows[V_HGRN_G] = jnp.tile(P['hgrn_norm'][l], N_HEADS)
    rows[V_GLA_G] = jnp.tile(P['gla_norm'][l], N_HEADS)
    rows[V_W0] = P['rwkv_w0'][l]
    rows[V_A0] = P['rwkv_a0'][l]
    rows[V_V0] = P['rwkv_v0'][l - 1] if l > 0 else jnp.zeros((BRANCH,), F32)
    rows[V_KK] = P['rwkv_k_k'][l]
    rows[V_KA] = P['rwkv_k_a'][l]
    rows[V_RK] = P['rwkv_r_k'][l].reshape(BRANCH)
    rows[V_GN_G] = P['rwkv_gn_g'][l]
    rows[V_GN_B] = P['rwkv_gn_b'][l]
    zero = jnp.zeros((BRANCH,), F32)
    vec = jnp.stack([zero if x is None else x.astype(F32) for x in rows], axis=0)

    lp = {
        'norm_mix': P['norm_mix'][l][None, :],
        'vec': vec,
        'mu': _rwkv_to_padded(P['rwkv_mu'][l])[None, :],
        'bgate': P['gla_b_gate'][l][None, :],
        'wg': _pad_rows(P['gla_w_gate2'][l], 128).astype(BF16),
        'w2': _pad_rows(P['rwkv_w2'][l], 128, 0).astype(BF16),
        'a2': _pad_rows(P['rwkv_a2'][l], 128, 32).astype(BF16),
        'w_out': P['w_out'][l].astype(BF16),
        'wq': P['wq_x'][l].astype(BF16),
        'wo': P['wo_x'][l].astype(BF16),
        'norm_x': P['norm_x'][l][None, :],
    }
    if l > 0:
        lp['v1'] = jnp.pad(P['rwkv_v1'][l - 1], ((0, 0), (0, 96))).astype(BF16)
        lp['v2'] = _pad_rows(P['rwkv_v2'][l - 1], 128).astype(BF16)
    return lp


def _consts():
    i256 = jnp.arange(BRANCH) // HEAD_DIM
    i128 = jnp.arange(GLA_KEY_WIDTH) // GLA_KEY_DIM
    bo = (i256[:, None] == i256[None, :])
    bog = (i128[:, None] == i256[None, :])
    lane = jnp.arange(BRANCH)
    partner = jnp.where(lane % HEAD_DIM < HEAD_DIM // 2, lane + HEAD_DIM // 2, lane - HEAD_DIM // 2)
    swap = (lane[:, None] == partner[None, :])
    return bo.astype(BF16), bog.astype(BF16), swap.astype(BF16)


def _run_trunk(x, pos0, mems, init, lps, w_in_all, norm_f, consts, *, bb, tc, c, bbc, lc, tm):
    bsz, seq, _ = x.shape
    cos, sin = _rope_tables(pos0, seq)
    new = {n: [] for n in ('ret', 'hgrn', 'gla', 'rwkv', 'shift')}
    vfirst = None
    for l in range(DEPTH):
        lp = lps[l]
        proj = _in_proj(x.reshape(bsz * seq, D_MODEL), lp['norm_mix'], w_in_all, tm, layer=l)
        proj = proj.reshape(bsz, seq, D_IN_PAD)
        states = [jnp.swapaxes(init['ret'][l], -1, -2), jnp.swapaxes(init['hgrn'][l], -1, -2),
                  jnp.swapaxes(init['gla'][l], -1, -2), init['rwkv'][l], _rwkv_to_padded(init['shift'][l])]
        ocat, vf, st = _mixers(proj, vfirst, cos, sin, states, lp, consts, bb=bb, tc=tc, c=c)
        if l == 0:
            vfirst = vf
        new['ret'].append(jnp.swapaxes(st[0], -1, -2))
        new['hgrn'].append(jnp.swapaxes(st[1], -1, -2))
        new['gla'].append(jnp.swapaxes(st[2], -1, -2))
        new['rwkv'].append(st[3])
        new['shift'].append(_rwkv_from_padded(proj[:, seq - 1, RWKV_OFF:RWKV_OFF + RWKV_PW]))
        x = _out_attn(x, ocat, mems[l], lp, norm_f, bbc=bbc, lc=lc, final=(l == DEPTH - 1))
    return x, {n: jnp.stack(v, axis=0) for n, v in new.items()}


def kernel(x_prompt, x_sample, mem_prompt, state_ret, state_hgrn, state_gla, state_rwkv, state_rwkv_shift,
           cache_mem_k, cache_mem_v, norm_mix, w_in, hgrn_lb_logits, hgrn_norm, gla_w_gate2, gla_b_gate,
           gla_norm, rwkv_mu, rwkv_w0, rwkv_w2, rwkv_a0, rwkv_a2, rwkv_v0, rwkv_v1, rwkv_v2, rwkv_k_k,
           rwkv_k_a, rwkv_r_k, rwkv_gn_g, rwkv_gn_b, w_out, norm_x, wq_x, wo_x, norm_mem, wk_x, wv_x, norm_f):
    P = {'norm_mix': norm_mix, 'hgrn_lb_logits': hgrn_lb_logits, 'hgrn_norm': hgrn_norm,
         'gla_w_gate2': gla_w_gate2, 'gla_b_gate': gla_b_gate, 'gla_norm': gla_norm,
         'rwkv_mu': rwkv_mu, 'rwkv_w0': rwkv_w0, 'rwkv_w2': rwkv_w2, 'rwkv_a0': rwkv_a0,
         'rwkv_a2': rwkv_a2, 'rwkv_v0': rwkv_v0, 'rwkv_v1': rwkv_v1, 'rwkv_v2': rwkv_v2,
         'rwkv_k_k': rwkv_k_k, 'rwkv_k_a': rwkv_k_a, 'rwkv_r_k': rwkv_r_k,
         'rwkv_gn_g': rwkv_gn_g, 'rwkv_gn_b': rwkv_gn_b, 'w_out': w_out,
         'norm_x': norm_x, 'wq_x': wq_x, 'wo_x': wo_x}
    lps = [_layer_params(l, P) for l in range(DEPTH)]
    w_in_all = _w_in_layout(w_in)
    consts = _consts()
    norm_f2 = norm_f[None, :]
    bsz, seq, _ = x_prompt.shape
    dbsz, dseq, _ = x_sample.shape
    p_bbc, s_bbc = 2, 16

    mem2d = mem_prompt.reshape(bsz * N_MEM, D_MODEL)
    kv_l, p_mems = [], []
    for l in range(DEPTH):
        wkv = jnp.concatenate([wk_x[l], wv_x[l]], axis=1).astype(BF16)
        kv = _in_proj(mem2d, norm_mem[l][None, :], wkv, 256).reshape(bsz, N_MEM, 2 * BRANCH)
        kv_l.append(kv)
        p_mems.append((kv, kv, pl.BlockSpec((p_bbc, N_MEM, BRANCH), lambda bi, li: (bi, 0, 0)),
                       pl.BlockSpec((p_bbc, N_MEM, BRANCH), lambda bi, li: (bi, 0, 1))))
    zero_init = {
        'ret': jnp.zeros((DEPTH, bsz, N_HEADS, HEAD_DIM, HEAD_DIM), F32),
        'hgrn': jnp.zeros((DEPTH, bsz, N_HEADS, HEAD_DIM, HEAD_DIM), F32),
        'gla': jnp.zeros((DEPTH, bsz, N_HEADS, GLA_KEY_DIM, HEAD_DIM), F32),
        'rwkv': jnp.zeros((DEPTH, bsz, N_HEADS, HEAD_DIM, HEAD_DIM), F32),
        'shift': jnp.zeros((DEPTH, bsz, RWKV_W), F32),
    }
    y_prompt, sp = _run_trunk(x_prompt, 0, p_mems, zero_init, lps, w_in_all, norm_f2, consts,
                              bb=8, tc=64, c=CHUNK, bbc=p_bbc, lc=512, tm=512)

    cached = {'ret': state_ret, 'hgrn': state_hgrn, 'gla': state_gla, 'rwkv': state_rwkv,
              'shift': state_rwkv_shift}
    cmk = cache_mem_k.reshape(DEPTH, dbsz, N_MEM, BRANCH)
    cmv = cache_mem_v.reshape(DEPTH, dbsz, N_MEM, BRANCH)
    s_mems = []
    for l in range(DEPTH):
        spec = pl.BlockSpec((None, s_bbc, N_MEM, BRANCH), lambda bi, li, _l=l: (_l, bi, 0, 0))
        s_mems.append((cmk, cmv, spec, spec))
    y_sample, ss = _run_trunk(x_sample, PAST_LEN, s_mems, cached, lps, w_in_all, norm_f2, consts,
                              bb=8, tc=dseq, c=dseq, bbc=s_bbc, lc=dseq, tm=256)

    kv_all = jnp.stack(kv_l, axis=0)
    mem_k_p = kv_all[..., :BRANCH].reshape(DEPTH, bsz, N_MEM, N_HEADS, HEAD_DIM)
    mem_v_p = kv_all[..., BRANCH:].reshape(DEPTH, bsz, N_MEM, N_HEADS, HEAD_DIM)
    return (y_prompt, y_sample, sp['ret'], ss['ret'], sp['hgrn'], ss['hgrn'], sp['gla'], ss['gla'],
            sp['rwkv'], ss['rwkv'], sp['shift'], ss['shift'], mem_k_p, mem_v_p)
```

```python
import functools

import jax
import jax.numpy as jnp
from jax import lax
from jax.experimental import pallas as pl
from jax.experimental.pallas import tpu as pltpu

F32 = jnp.float32
BF16 = jnp.bfloat16

D_MODEL = 1024
DEPTH = 2
PAST_LEN = 16384
N_HEADS = 4
HEAD_DIM = 64
BRANCH = N_HEADS * HEAD_DIM
GLA_KEY_DIM = 32
GLA_KEY_WIDTH = N_HEADS * GLA_KEY_DIM
GLA_TAU = 16.0
N_MEM = 256
ROPE_BASE = 10000.0
RMS_EPS = 1e-6
RET_GN_EPS = 1e-5
RWKV_GN_EPS = 64e-5
LB_FLOOR = 1e-20
RWKV_W = 1088
CHUNK = 16

RET_OFF = 0
HGRN_OFF = 1024
GLA_OFF = 2048
RWKV_OFF = 2944
RWKV_PW = 1152
D_IN_PAD = 4096

V7X_VMEM_LIMIT = 60 * 1024 * 1024
SUBLANES = 8
NEG_BIG = -1e30

(V_LB_A, V_LB_B, V_LB_OM, V_HGRN_G, V_GLA_G, V_W0, V_A0, V_V0, V_KK, V_KA, V_RK,
 V_GN_G, V_GN_B) = range(13)
N_VEC = 16


def _rms(x, g):
    return x * lax.rsqrt(jnp.mean(x * x, axis=-1, keepdims=True) + RMS_EPS) * g


def _sigmoid(x):
    return 0.5 * jnp.tanh(0.5 * x) + 0.5


def _silu(x):
    h = 0.5 * x
    return h * jnp.tanh(h) + h


def _log1p_exp(x):
    return jnp.log(1.0 + jnp.exp(x))


def _log_sigmoid(x):
    return jnp.minimum(x, 0.0) - _log1p_exp(-jnp.abs(x))


def _dot(a, b):
    return jnp.dot(a.astype(BF16), b.astype(BF16), preferred_element_type=F32)


def _dot_nt(a, b):
    return lax.dot_general(a.astype(BF16), b.astype(BF16), (((1,), (1,)), ((), ())),
                           preferred_element_type=F32)


def _dot_tn(a, b):
    return lax.dot_general(a.astype(BF16), b.astype(BF16), (((0,), (0,)), ((), ())),
                           preferred_element_type=F32)


def _row_tiles(x, n):
    return [x[i * SUBLANES:(i + 1) * SUBLANES, :] for i in range(n)]


def _w_in_layout_kernel(w_ref, o_ref):
    w = w_ref[...]
    zeros = lambda n: jnp.zeros((w.shape[0], n), w.dtype)
    o_ref[...] = jnp.concatenate([
        w[:, 0:2560],
        w[:, 2576:2832], w[:, 2560:2576], zeros(112),
        w[:, 2832:3088], w[:, 3120:3376], w[:, 3376:3632], w[:, 3664:3920],
        w[:, 3088:3120], w[:, 3632:3664], zeros(64)], axis=1).astype(BF16)


def _w_in_layout(w_in):
    depth, d, n = w_in.shape
    rows = 128
    return pl.pallas_call(
        _w_in_layout_kernel,
        grid=(depth, d // rows),
        in_specs=[pl.BlockSpec((None, rows, n), lambda l, i: (l, i, 0))],
        out_specs=pl.BlockSpec((None, rows, D_IN_PAD), lambda l, i: (l, i, 0)),
        out_shape=jax.ShapeDtypeStruct((depth, d, D_IN_PAD), BF16),
        compiler_params=pltpu.CompilerParams(dimension_semantics=("arbitrary", "arbitrary")),
        name="w_in_layout",
    )(w_in)


def _in_proj_kernel(x_ref, g_ref, w_ref, o_ref):
    o_ref[...] = _dot(_rms(x_ref[...], g_ref[...]), w_ref[...])


def _in_proj(x2d, gain, w_bf16, tm, layer=None):
    m, d = x2d.shape
    n = w_bf16.shape[-1]
    w_spec = (pl.BlockSpec((d, n), lambda i: (0, 0)) if layer is None
              else pl.BlockSpec((None, d, n), lambda i: (layer, 0, 0)))
    return pl.pallas_call(
        _in_proj_kernel,
        grid=(m // tm,),
        in_specs=[pl.BlockSpec((tm, d), lambda i: (i, 0)),
                  pl.BlockSpec((1, d), lambda i: (0, 0)),
                  w_spec],
        out_specs=pl.BlockSpec((tm, n), lambda i: (i, 0)),
        out_shape=jax.ShapeDtypeStruct((m, n), F32),
        compiler_params=pltpu.CompilerParams(dimension_semantics=("arbitrary",),
                                             vmem_limit_bytes=V7X_VMEM_LIMIT),
        name="in_proj",
    )(x2d, gain, w_bf16)


def _chunk_cumsum(x, c):
    t_local = jnp.bitwise_and(lax.broadcasted_iota(jnp.int32, x.shape, 0), c - 1)
    s = 1
    while s < c:
        x = x + jnp.where(t_local >= s, pltpu.roll(x, s, 0), 0.0)
        s *= 2
    return x


def _head_stack(x, dk=HEAD_DIM):
    lane_head = jnp.right_shift(lax.broadcasted_iota(jnp.int32, x.shape, 1), dk.bit_length() - 1)
    return jnp.concatenate([jnp.where(lane_head == h, x, 0.0) for h in range(N_HEADS)], axis=0)


def _head_unstack(x4, n):
    lane_head = jnp.right_shift(lax.broadcasted_iota(jnp.int32, (n, BRANCH), 1), 6)
    out = jnp.where(lane_head == 0, x4[0:n, :], 0.0)
    for h in range(1, N_HEADS):
        out = out + jnp.where(lane_head == h, x4[h * n:(h + 1) * n, :], 0.0)
    return out


def _heads_to_rows(x):
    return jnp.concatenate([x[:, h * HEAD_DIM:(h + 1) * HEAD_DIM] for h in range(N_HEADS)], axis=0)


def _rows_to_heads(x4, n):
    return jnp.concatenate([x4[h * n:(h + 1) * n, :] for h in range(N_HEADS)], axis=1)


def _state_read(lhs_stacked, st, n):
    return _rows_to_heads(_dot_nt(lhs_stacked, st), n)


def _state_outer(values, keys_stacked):
    return _dot_tn(_heads_to_rows(values), keys_stacked)


def _gl_tiles(q, k, b, c):
    n_rt = c // SUBLANES
    rows = lax.broadcasted_iota(jnp.int32, (SUBLANES, q.shape[1]), 0)
    q_t, b_t = _row_tiles(q, n_rt), _row_tiles(b, n_rt)
    tiles, where_ = [], {}
    for m in range(c):
        for rt in range(m // SUBLANES, n_rt):
            arg = b_t[rt] - b[m:m + 1, :]
            if rt == m // SUBLANES:
                arg = jnp.where(rows >= m % SUBLANES, arg, NEG_BIG)
            where_[m, rt] = len(tiles) * SUBLANES
            tiles.append(q_t[rt] * k[m:m + 1, :] * jnp.exp(arg))
    return tiles, where_


def _gl_output(scores, o_state, v, where_, c):
    o = _row_tiles(o_state, c // SUBLANES)
    for (m, rt), off in where_.items():
        o[rt] = o[rt] + scores[off:off + SUBLANES, :] * v[m:m + 1, :]
    return jnp.concatenate(o, axis=0)


def _gl_state(k, v, b, st, c):
    b_last = b[c - 1:c, :]
    dk = k.shape[1] // N_HEADS
    return st * jnp.exp(b_last) + _state_outer(v, _head_stack(k * jnp.exp(b_last - b), dk))


def _rwkv_tiles(kkt, ahat, c):
    n_rt = c // SUBLANES
    rows = lax.broadcasted_iota(jnp.int32, (SUBLANES, BRANCH), 0)
    kk_t = _row_tiles(kkt, n_rt)
    tiles, where_ = [], {}
    for m in range(c - 1):
        for rt_i in range(m // SUBLANES, n_rt):
            on_diag = rt_i == m // SUBLANES
            if on_diag and m % SUBLANES == SUBLANES - 1:
                continue
            kk_m = jnp.where(rows > m % SUBLANES, kk_t[rt_i], 0.0) if on_diag else kk_t[rt_i]
            where_[m, rt_i] = len(tiles) * SUBLANES
            tiles.append(kk_m * ahat[m:m + 1, :])
    return tiles, where_


def _rwkv_compact(lhs_stacked, khat, ahat):
    return _dot_nt(lhs_stacked, jnp.concatenate([khat, ahat], axis=0))


def _rwkv_masks(sc, c):
    row = lax.broadcasted_iota(jnp.int32, (2 * N_HEADS * c, c), 0)
    col = lax.broadcasted_iota(jnp.int32, (2 * N_HEADS * c, c), 1)
    keep = col < jnp.bitwise_and(row, c - 1) + jnp.where(jnp.bitwise_and(row, c) != 0, 1, 0)
    r_a = jnp.concatenate([sc[(2 * h + 1) * c:(2 * h + 2) * c, c:2 * c] for h in range(N_HEADS)], axis=0)
    row4 = lax.broadcasted_iota(jnp.int32, (N_HEADS * c, c), 0)
    col4 = lax.broadcasted_iota(jnp.int32, (N_HEADS * c, c), 1)
    return jnp.where(keep, sc[:, 0:c], 0.0), jnp.where(col4 <= jnp.bitwise_and(row4, c - 1), r_a, 0.0)


def _rwkv_solve(sc_a, rhs, where_, c):
    n_rt = c // SUBLANES
    u = _row_tiles(rhs, n_rt)
    for s in range(c - 1):
        u_s = u[s // SUBLANES][s % SUBLANES:s % SUBLANES + 1, :]
        for rt_i in range(s // SUBLANES, n_rt):
            if (s, rt_i) in where_:
                off = where_[s, rt_i]
                u[rt_i] = u[rt_i] - sc_a[off:off + SUBLANES, :] * u_s
    return jnp.concatenate(u, axis=0)


def _rwkv_state(u, v, khat, ahat, p_last, st):
    return st * p_last + _state_outer(jnp.concatenate([v, -u], axis=0),
                                      _head_stack(jnp.concatenate([khat * p_last, ahat * p_last], axis=0)))


def _mixers_kernel(*refs, bb, tc, c, layer0, n_t):
    it = iter(refs)
    proj_ref = next(it)
    vfirst_in_ref = None if layer0 else next(it)
    cos_ref, sin_ref = next(it), next(it)
    s_ret0, s_hgrn0, s_gla0, s_rw0, shift0 = next(it), next(it), next(it), next(it), next(it)
    vec_ref, mu_ref, bgate_ref = next(it), next(it), next(it)
    wg_ref, w2_ref, a2_ref = next(it), next(it), next(it)
    v1_ref, v2_ref = (None, None) if layer0 else (next(it), next(it))
    bo_ref, bog_ref, swap_ref = next(it), next(it), next(it)
    dmat_ref, eq_ref, ek_ref, gblk_ref = next(it), next(it), next(it), next(it)
    ocat_ref = next(it)
    vfirst_out_ref = next(it) if layer0 else None
    o_ret_st, o_hgrn_st, o_gla_st, o_rw_st = next(it), next(it), next(it), next(it)
    st_ret, st_hgrn, st_gla, st_rw, carry = next(it), next(it), next(it), next(it), next(it)
    qkvb = [[next(it) for _ in range(4)] for _ in range(2)]
    o_gl = [next(it) for _ in range(2)]
    ps_ref, orw_ref = next(it), next(it)
    rw = [next(it) for _ in range(6)]

    r = bb * tc
    ti = pl.program_id(1)
    key_value_io = ((st_ret, s_ret0, o_ret_st), (st_hgrn, s_hgrn0, o_hgrn_st), (st_gla, s_gla0, o_gla_st))

    def vec(i):
        return vec_ref[i:i + 1, :]

    @pl.when(ti == 0)
    def _init():
        for b in range(bb):
            for st_ref, s0_ref, _ in key_value_io:
                st_ref[b] = s0_ref[b].T
            for h in range(N_HEADS):
                st_rw[b, :, h * HEAD_DIM:(h + 1) * HEAD_DIM] = s_rw0[b, h]
        carry[...] = shift0[...]

    def blk(lo, width):
        return proj_ref[:, :, lo:lo + width].reshape(r, width)

    bo = bo_ref[...]
    bo2 = jnp.concatenate([bo, bo], axis=0)

    def head_sum(x, two_pass=False):
        if not two_pass:
            return _dot(x, bo)
        hi = x.astype(BF16)
        lo = (x - hi.astype(F32)).astype(BF16)
        return jnp.dot(jnp.concatenate([hi, lo], axis=1), bo2, preferred_element_type=F32)

    def head_mean(x, two_pass=False):
        return head_sum(x, two_pass) * (1.0 / HEAD_DIM)

    cos = jnp.concatenate([cos_ref[...]] * bb, axis=0)
    sin = jnp.concatenate([sin_ref[...]] * bb, axis=0)
    def rope(x):
        return x * cos + _dot(x, swap_ref[...]) * sin

    q_r = rope(blk(RET_OFF, BRANCH))
    k_r = rope(blk(RET_OFF + 256, BRANCH)) * (HEAD_DIM ** -0.5)
    v_r = blk(RET_OFF + 512, BRANCH)
    seqs = range(bb)
    blocks = [slice(b * tc, (b + 1) * tc) for b in seqs]
    ret_sc = [_dot_nt(_head_stack(q_r[rows, :]), k_r[rows, :]) * dmat_ref[...] for rows in blocks]
    ret_o4 = [_dot(ret_sc[b], v_r[blocks[b], :]) for b in seqs]
    ret_os = [_state_read(_head_stack(q_r[blocks[b], :] * eq_ref[...]), st_ret[b], tc) for b in seqs]
    for b in seqs:
        upd = _state_outer(v_r[blocks[b], :], _head_stack(k_r[blocks[b], :] * ek_ref[...]))
        st_ret[b] = st_ret[b] * gblk_ref[...] + upd
    o_ret = jnp.concatenate([ret_os[b] + _head_unstack(ret_o4[b], tc) for b in seqs], axis=0)

    q_h, k_h, v_h, b_h = qkvb[0]
    q_h[...] = _silu(blk(HGRN_OFF, BRANCH))
    ff = blk(HGRN_OFF + 256, BRANCH)
    y = vec(V_LB_B) + _log_sigmoid(ff)
    a_lb = jnp.broadcast_to(vec(V_LB_A), y.shape)
    mx = jnp.maximum(a_lb, y)
    log_f = mx + _log1p_exp(jnp.minimum(a_lb, y) - mx)
    k_h[...] = vec(V_LB_OM) * _sigmoid(-ff)
    v_h[...] = blk(HGRN_OFF + 512, BRANCH)
    b_h[...] = _chunk_cumsum(log_f, c)

    q_g, k_g, v_g, b_g = qkvb[1]
    q_g[...] = blk(GLA_OFF, GLA_KEY_WIDTH) * (GLA_KEY_DIM ** -0.5)
    k_g[...] = blk(GLA_OFF + 128, GLA_KEY_WIDTH)
    v_g[...] = blk(GLA_OFF + 256, BRANCH)
    z = _dot(blk(GLA_OFF + 768, 128), wg_ref[...]) + bgate_ref[...]
    b_g[...] = _chunk_cumsum(_log_sigmoid(z) / GLA_TAU, c)

    p_rw = blk(RWKV_OFF, RWKV_PW)
    mu = mu_ref[...]
    ps_ref[...] = p_rw + (pltpu.roll(p_rw, 1, 0) - p_rw) * mu
    for b in range(bb):
        p0 = p_rw[b * tc:b * tc + 1, :]
        ps_ref[b * tc:b * tc + 1, :] = p0 + (carry[b:b + 1, :] - p0) * mu
        carry[b:b + 1, :] = p_rw[(b + 1) * tc - 1:(b + 1) * tc, :]

    rr = ps_ref[:, 0:256]
    k_raw = ps_ref[:, 256:512]
    vv = ps_ref[:, 512:768]
    low_rank = ps_ref[:, 1024:1152]
    w_pre = vec(V_W0) + _dot(jnp.tanh(low_rank), w2_ref[...])
    w_log = -(jnp.maximum(-w_pre, 0.0) + _log1p_exp(-jnp.abs(w_pre))) - 0.5
    log_w = -jnp.exp(w_log)
    a = _sigmoid(vec(V_A0) + _dot(low_rank, a2_ref[...]))
    if layer0:
        vfirst_out_ref[...] = vv.reshape(bb, tc, BRANCH)
    else:
        vmix = _sigmoid(vec(V_V0) + _dot(_dot(vv, v1_ref[...]), v2_ref[...]))
        vv = vv + (vfirst_in_ref[...].reshape(r, BRANCH) - vv) * vmix
    kk_raw = k_raw * vec(V_KK)
    kk = kk_raw / jnp.maximum(jnp.sqrt(head_sum(kk_raw * kk_raw, two_pass=True)), 1e-12)
    kmod = k_raw * (1.0 + (a - 1.0) * vec(V_KA))
    b_w = _chunk_cumsum(log_w, c)
    inv_p = jnp.exp(-b_w)
    p_w = jnp.exp(b_w)
    for ref, arr in zip(rw, (kk * jnp.exp(b_w - log_w), rr * p_w, kmod * inv_p, kk * a * inv_p, vv, p_w)):
        ref[...] = arr

    n_c = tc // c
    mix = ((qkvb[0], o_gl[0], st_hgrn, bo_ref, HEAD_DIM), (qkvb[1], o_gl[1], st_gla, bog_ref, GLA_KEY_DIM))

    def chunk_body(ci, _):
        rows = [pl.ds(pl.multiple_of(b * tc + ci * c, c), c) for b in seqs]

        def all_scores(tiles, bo_m):
            sc = _dot(jnp.concatenate(tiles, axis=0), bo_m)
            n = len(tiles) // bb * SUBLANES
            return [sc[b * n:(b + 1) * n, :] for b in seqs]

        gl = []
        for (q_s, k_s, v_s, b_s), o_s, st_ref, bo_m, dk in mix:
            ins = [(q_s[rows[b], :], k_s[rows[b], :], v_s[rows[b], :], b_s[rows[b], :]) for b in seqs]
            tiles = []
            for q, k, _, b_c in ins:
                t, where_ = _gl_tiles(q, k, b_c, c)
                tiles += t
            gl.append((ins, all_scores(tiles, bo_m[...]), where_, o_s, st_ref, dk))
        rw_in = [tuple(x[rows[b], :] for x in rw) for b in seqs]
        tiles = []
        for kkt, _, _, ahat, _, _ in rw_in:
            t, rw_where = _rwkv_tiles(kkt, ahat, c)
            tiles += t
        rw_sc = all_scores(tiles, bo)
        rw_lhs = [_head_stack(jnp.concatenate([kkt, rt], axis=0)) for kkt, rt, _, _, _, _ in rw_in]
        rw_cmp = [_rwkv_compact(rw_lhs[b], rw_in[b][2], rw_in[b][3]) for b in seqs]
        gl_os = [[_state_read(_head_stack(q * jnp.exp(b_c), dk), st_ref[b], c)
                  for b, (q, _, _, b_c) in enumerate(ins)] for ins, _, _, _, st_ref, dk in gl]
        rw_fs = [_state_read(rw_lhs[b], st_rw[b], 2 * c) for b in seqs]
        rw_m = [_rwkv_masks(rw_cmp[b], c) for b in seqs]
        rw_fv = [_head_unstack(_dot(rw_m[b][0], rw_in[b][4]), 2 * c) for b in seqs]
        for ins, _, _, _, st_ref, _ in gl:
            for b, (_, k, v, b_c) in enumerate(ins):
                st_ref[b] = _gl_state(k, v, b_c, st_ref[b], c)
        for (ins, sc, where_, o_s, _, _), o_state in zip(gl, gl_os):
            for b in seqs:
                o_s[rows[b], :] = _gl_output(sc[b], o_state[b], ins[b][2], where_, c)
        rhs = [rw_fs[b] + rw_fv[b] for b in seqs]
        us = [_rwkv_solve(rw_sc[b], rhs[b][0:c, :], rw_where, c) for b in seqs]
        rw_ou = [_dot(rw_m[b][1], us[b]) for b in seqs]
        for b in seqs:
            _, _, khat, ahat, v_w, p_c = rw_in[b]
            st_rw[b] = _rwkv_state(us[b], v_w, khat, ahat, p_c[c - 1:c, :], st_rw[b])
        for b in seqs:
            orw_ref[rows[b], :] = rhs[b][c:2 * c, :] - _head_unstack(rw_ou[b], c)
        return 0

    lax.fori_loop(0, n_c, chunk_body, 0)

    def group_norm(o, eps):
        cen = o - head_mean(o, two_pass=True)
        return cen * lax.rsqrt(head_mean(cen * cen) + eps)

    def head_rms(o, g):
        return o * lax.rsqrt(head_mean(o * o) + RMS_EPS) * g

    def put(lo, val):
        ocat_ref[:, :, lo:lo + BRANCH] = val.reshape(bb, tc, BRANCH)

    put(0, group_norm(o_ret, RET_GN_EPS) * _silu(blk(RET_OFF + 768, BRANCH)))
    put(256, head_rms(o_gl[0][...], vec(V_HGRN_G)) * _silu(blk(HGRN_OFF + 768, BRANCH)))
    put(512, head_rms(o_gl[1][...], vec(V_GLA_G)) * _silu(blk(GLA_OFF + 512, BRANCH)))
    o_rw = group_norm(orw_ref[...], RWKV_GN_EPS) * vec(V_GN_G) + vec(V_GN_B)
    bonus = head_sum(rr * kmod * vec(V_RK))
    put(768, (o_rw + bonus * vv) * _silu(ps_ref[:, 768:1024]))

    @pl.when(ti == n_t - 1)
    def _final():
        for b in range(bb):
            for st_ref, _, out_ref in key_value_io:
                out_ref[b] = st_ref[b].T
            for h in range(N_HEADS):
                o_rw_st[b, h] = st_rw[b][:, h * HEAD_DIM:(h + 1) * HEAD_DIM]


def _mixers(proj, vfirst, cos, sin, states, layer, lp, consts, *, bb, tc, c):
    bsz, seq, _ = proj.shape
    layer0 = vfirst is None
    n_t = seq // tc
    r = bb * tc

    def full(arr):
        nd = arr.ndim
        return pl.BlockSpec(arr.shape, lambda bi, ti, _nd=nd: (0,) * _nd)

    def per_b(arr):
        nd = arr.ndim
        return pl.BlockSpec((None, bb) + arr.shape[2:], lambda bi, ti, _nd=nd: (layer, bi) + (0,) * (_nd - 2))

    def tok(width):
        return pl.BlockSpec((bb, tc, width), lambda bi, ti: (bi, ti, 0))

    ins, specs = [proj], [tok(D_IN_PAD)]
    if not layer0:
        ins.append(vfirst)
        specs.append(tok(BRANCH))
    ins += [cos, sin]
    specs += [pl.BlockSpec((tc, BRANCH), lambda bi, ti: (ti, 0))] * 2
    for s in states:
        ins.append(s)
        specs.append(per_b(s))
    small = [lp['vec'], lp['mu'], lp['bgate'], lp['wg'], lp['w2'], lp['a2']]
    if not layer0:
        small += [lp['v1'], lp['v2']]
    small += list(consts) + list(_retention_tables(tc))
    for s in small:
        ins.append(s)
        specs.append(full(s))

    st_shapes = [(bsz, BRANCH, HEAD_DIM), (bsz, BRANCH, HEAD_DIM), (bsz, GLA_KEY_WIDTH, HEAD_DIM),
                 (bsz, N_HEADS, HEAD_DIM, HEAD_DIM)]
    out_shape = [jax.ShapeDtypeStruct((bsz, seq, D_MODEL), F32)]
    out_specs = [tok(D_MODEL)]
    if layer0:
        out_shape.append(jax.ShapeDtypeStruct((bsz, seq, BRANCH), F32))
        out_specs.append(tok(BRANCH))
    for shp in st_shapes:
        out_shape.append(jax.ShapeDtypeStruct(shp, F32))
        out_specs.append(pl.BlockSpec((bb,) + shp[1:], lambda bi, ti, _n=len(shp): (bi,) + (0,) * (_n - 1)))

    scratch = [pltpu.VMEM((bb, HEAD_DIM, BRANCH), F32), pltpu.VMEM((bb, HEAD_DIM, BRANCH), F32),
               pltpu.VMEM((bb, HEAD_DIM, GLA_KEY_WIDTH), F32), pltpu.VMEM((bb, HEAD_DIM, BRANCH), F32),
               pltpu.VMEM((bb, RWKV_PW), F32)]
    for dkt in (BRANCH, GLA_KEY_WIDTH):
        scratch += [pltpu.VMEM((r, dkt), F32), pltpu.VMEM((r, dkt), F32), pltpu.VMEM((r, BRANCH), F32),
                    pltpu.VMEM((r, dkt), F32)]
    scratch += [pltpu.VMEM((r, BRANCH), F32)] * 2
    scratch += [pltpu.VMEM((r, RWKV_PW), F32), pltpu.VMEM((r, BRANCH), F32)]
    scratch += [pltpu.VMEM((r, BRANCH), F32)] * 6

    outs = pl.pallas_call(
        functools.partial(_mixers_kernel, bb=bb, tc=tc, c=c, layer0=layer0, n_t=n_t),
        grid=(bsz // bb, n_t),
        in_specs=specs,
        out_specs=out_specs,
        out_shape=out_shape,
        scratch_shapes=scratch,
        compiler_params=pltpu.CompilerParams(dimension_semantics=("arbitrary", "arbitrary"),
                                             vmem_limit_bytes=V7X_VMEM_LIMIT),
        name="mixers",
    )(*ins)
    if layer0:
        return outs[0], outs[1], outs[2:]
    return outs[0], None, outs[1:]


def _out_attn_kernel(x_ref, oc_ref, mk_ref, mv_ref, wout_ref, wq_ref, wo_ref, gx_ref, gf_ref, out_ref,
                     *, bbc, lc, final):
    r = bbc * lc
    x1 = x_ref[...].reshape(r, D_MODEL) + _dot(oc_ref[...].reshape(r, D_MODEL), wout_ref[...])
    q = _dot(_rms(x1, gx_ref[...]), wq_ref[...])
    seqs = range(bbc)
    scores = [_dot_nt(_head_stack(q[b * lc:(b + 1) * lc, :]), mk_ref[b]) * (HEAD_DIM ** -0.5) for b in seqs]
    probs = []
    for s in scores:
        e = jnp.exp(s - jnp.max(s, axis=-1, keepdims=True))
        probs.append(e / jnp.sum(e, axis=-1, keepdims=True))
    o4 = [_dot(probs[b], mv_ref[b]) for b in seqs]
    outs = [_head_unstack(o4[b], lc) for b in seqs]
    o = outs[0] if bbc == 1 else jnp.concatenate(outs, axis=0)
    x2 = x1 + _dot(o, wo_ref[...])
    if final:
        x2 = _rms(x2, gf_ref[...])
    out_ref[...] = x2.reshape(bbc, lc, D_MODEL)


def _out_attn(x, ocat, mem, lp, norm_f, *, bbc, lc, final):
    bsz, seq, _ = x.shape
    mk, mv, mk_spec, mv_spec = mem

    def tok(width):
        return pl.BlockSpec((bbc, lc, width), lambda bi, li: (bi, li, 0))

    def full(arr):
        nd = arr.ndim
        return pl.BlockSpec(arr.shape, lambda bi, li, _nd=nd: (0,) * _nd)

    small = [lp['w_out'], lp['wq'], lp['wo'], lp['norm_x'], norm_f]
    return pl.pallas_call(
        functools.partial(_out_attn_kernel, bbc=bbc, lc=lc, final=final),
        grid=(bsz // bbc, seq // lc),
        in_specs=[tok(D_MODEL), tok(D_MODEL), mk_spec, mv_spec] + [full(s) for s in small],
        out_specs=tok(D_MODEL),
        out_shape=jax.ShapeDtypeStruct((bsz, seq, D_MODEL), F32),
        compiler_params=pltpu.CompilerParams(dimension_semantics=("arbitrary", "arbitrary"),
                                             vmem_limit_bytes=V7X_VMEM_LIMIT),
        name="out_attn",
    )(x, ocat, mk, mv, *small)


def _rope_tables(pos0, seq):
    half = HEAD_DIM // 2
    inv = ROPE_BASE ** (-jnp.arange(half, dtype=F32) / half)
    pos = (pos0 + jnp.arange(seq, dtype=jnp.int32)).astype(F32)
    ang = pos[:, None] * inv[None, :]
    cos, sin = jnp.cos(ang), jnp.sin(ang)
    return (jnp.tile(jnp.concatenate([cos, cos], axis=-1), (1, N_HEADS)),
            jnp.tile(jnp.concatenate([-sin, sin], axis=-1), (1, N_HEADS)))


def _retention_tables(n):
    log_gamma = jnp.log1p(-jnp.exp2(-5.0 - jnp.arange(N_HEADS, dtype=F32)))
    j = jnp.arange(n, dtype=F32)
    diff = j[:, None] - j[None, :]
    dmat = jnp.where(diff[None] >= 0, jnp.exp(diff[None] * log_gamma[:, None, None]), 0.0)
    lg_lanes = jnp.repeat(log_gamma, HEAD_DIM)[None, :]
    e_q = jnp.exp((j[:, None] + 1.0) * lg_lanes)
    e_k = jnp.exp((n - 1.0 - j[:, None]) * lg_lanes)
    g_blk = jnp.exp(float(n) * lg_lanes)
    return dmat.reshape(N_HEADS * n, n), e_q, e_k, g_blk


def _rwkv_to_padded(t):
    pad = jnp.zeros(t.shape[:-1] + (RWKV_PW - RWKV_W,), t.dtype)
    return jnp.concatenate([t[..., 0:256], t[..., 288:544], t[..., 544:800], t[..., 832:1088],
                            t[..., 256:288], t[..., 800:832], pad], axis=-1)


def _rwkv_from_padded(t):
    return jnp.concatenate([t[..., 0:256], t[..., 1024:1056], t[..., 256:512], t[..., 512:768],
                            t[..., 1056:1088], t[..., 768:1024]], axis=-1)


def _pad_rows(m, rows, at=0):
    out = jnp.zeros((rows, m.shape[1]), m.dtype)
    return out.at[at:at + m.shape[0]].set(m)


def _layer_params(l, P):
    sm = jax.nn.softmax(P['hgrn_lb_logits'].astype(F32), axis=0)
    lb = (jnp.cumsum(sm, axis=0) - sm[0])[l]
    rows = [None] * N_VEC
    rows[V_LB_A] = jnp.log(jnp.maximum(lb, LB_FLOOR))
    rows[V_LB_B] = jnp.log1p(-lb)
    rows[V_LB_OM] = 1.0 - lb
    rows[V_HGRN_G] = jnp.tile(P['hgrn_norm'][l], N_HEADS)
    rows[V_GLA_G] = jnp.tile(P['gla_norm'][l], N_HEADS)
    rows[V_W0] = P['rwkv_w0'][l]
    rows[V_A0] = P['rwkv_a0'][l]
    rows[V_V0] = P['rwkv_v0'][l - 1] if l > 0 else jnp.zeros((BRANCH,), F32)
    rows[V_KK] = P['rwkv_k_k'][l]
    rows[V_KA] = P['rwkv_k_a'][l]
    rows[V_RK] = P['rwkv_r_k'][l].reshape(BRANCH)
    rows[V_GN_G] = P['rwkv_gn_g'][l]
    rows[V_GN_B] = P['rwkv_gn_b'][l]
    zero = jnp.zeros((BRANCH,), F32)
    vec = jnp.stack([zero if x is None else x.astype(F32) for x in rows], axis=0)

    lp = {
        'norm_mix': P['norm_mix'][l][None, :],
        'vec': vec,
        'mu': _rwkv_to_padded(P['rwkv_mu'][l])[None, :],
        'bgate': P['gla_b_gate'][l][None, :],
        'wg': _pad_rows(P['gla_w_gate2'][l], 128).astype(BF16),
        'w2': _pad_rows(P['rwkv_w2'][l], 128, 0).astype(BF16),
        'a2': _pad_rows(P['rwkv_a2'][l], 128, 32).astype(BF16),
        'w_out': P['w_out'][l].astype(BF16),
        'wq': P['wq_x'][l].astype(BF16),
        'wo': P['wo_x'][l].astype(BF16),
        'norm_x': P['norm_x'][l][None, :],
    }
    if l > 0:
        lp['v1'] = jnp.pad(P['rwkv_v1'][l - 1], ((0, 0), (0, 96))).astype(BF16)
        lp['v2'] = _pad_rows(P['rwkv_v2'][l - 1], 128).astype(BF16)
    return lp


def _consts():
    i256 = jnp.arange(BRANCH) // HEAD_DIM
    i128 = jnp.arange(GLA_KEY_WIDTH) // GLA_KEY_DIM
    bo = (i256[:, None] == i256[None, :])
    bog = (i128[:, None] == i256[None, :])
    lane = jnp.arange(BRANCH)
    partner = jnp.where(lane % HEAD_DIM < HEAD_DIM // 2, lane + HEAD_DIM // 2, lane - HEAD_DIM // 2)
    swap = (lane[:, None] == partner[None, :])
    return bo.astype(BF16), bog.astype(BF16), swap.astype(BF16)


def _run_trunk(x, pos0, mems, init, lps, w_in_all, norm_f, consts, *, bb, tc, c, bbc, lc, tm):
    bsz, seq, _ = x.shape
    cos, sin = _rope_tables(pos0, seq)
    new = {n: [] for n in ('ret', 'hgrn', 'gla', 'rwkv', 'shift')}
    states = [init['ret'].reshape(DEPTH, bsz, BRANCH, HEAD_DIM), init['hgrn'].reshape(DEPTH, bsz, BRANCH, HEAD_DIM),
              init['gla'].reshape(DEPTH, bsz, GLA_KEY_WIDTH, HEAD_DIM), init['rwkv'],
              _rwkv_to_padded(init['shift'])]
    vfirst = None
    for l in range(DEPTH):
        lp = lps[l]
        proj = _in_proj(x.reshape(bsz * seq, D_MODEL), lp['norm_mix'], w_in_all, tm, layer=l)
        proj = proj.reshape(bsz, seq, D_IN_PAD)
        ocat, vf, st = _mixers(proj, vfirst, cos, sin, states, l, lp, consts, bb=bb, tc=tc, c=c)
        if l == 0:
            vfirst = vf
        new['ret'].append(st[0].reshape(bsz, N_HEADS, HEAD_DIM, HEAD_DIM))
        new['hgrn'].append(st[1].reshape(bsz, N_HEADS, HEAD_DIM, HEAD_DIM))
        new['gla'].append(st[2].reshape(bsz, N_HEADS, GLA_KEY_DIM, HEAD_DIM))
        new['rwkv'].append(st[3])
        new['shift'].append(_rwkv_from_padded(proj[:, seq - 1, RWKV_OFF:RWKV_OFF + RWKV_PW]))
        x = _out_attn(x, ocat, mems[l], lp, norm_f, bbc=bbc, lc=lc, final=(l == DEPTH - 1))
    return x, {n: jnp.stack(v, axis=0) for n, v in new.items()}


def kernel(x_prompt, x_sample, mem_prompt, state_ret, state_hgrn, state_gla, state_rwkv, state_rwkv_shift,
           cache_mem_k, cache_mem_v, norm_mix, w_in, hgrn_lb_logits, hgrn_norm, gla_w_gate2, gla_b_gate,
           gla_norm, rwkv_mu, rwkv_w0, rwkv_w2, rwkv_a0, rwkv_a2, rwkv_v0, rwkv_v1, rwkv_v2, rwkv_k_k,
           rwkv_k_a, rwkv_r_k, rwkv_gn_g, rwkv_gn_b, w_out, norm_x, wq_x, wo_x, norm_mem, wk_x, wv_x, norm_f):
    P = {'norm_mix': norm_mix, 'hgrn_lb_logits': hgrn_lb_logits, 'hgrn_norm': hgrn_norm,
         'gla_w_gate2': gla_w_gate2, 'gla_b_gate': gla_b_gate, 'gla_norm': gla_norm,
         'rwkv_mu': rwkv_mu, 'rwkv_w0': rwkv_w0, 'rwkv_w2': rwkv_w2, 'rwkv_a0': rwkv_a0,
         'rwkv_a2': rwkv_a2, 'rwkv_v0': rwkv_v0, 'rwkv_v1': rwkv_v1, 'rwkv_v2': rwkv_v2,
         'rwkv_k_k': rwkv_k_k, 'rwkv_k_a': rwkv_k_a, 'rwkv_r_k': rwkv_r_k,
         'rwkv_gn_g': rwkv_gn_g, 'rwkv_gn_b': rwkv_gn_b, 'w_out': w_out,
         'norm_x': norm_x, 'wq_x': wq_x, 'wo_x': wo_x}
    lps = [_layer_params(l, P) for l in range(DEPTH)]
    w_in_all = _w_in_layout(w_in)
    consts = _consts()
    norm_f2 = norm_f[None, :]
    bsz, seq, _ = x_prompt.shape
    dbsz, dseq, _ = x_sample.shape
    p_bbc, s_bbc = 2, 16

    mem2d = mem_prompt.reshape(bsz * N_MEM, D_MODEL)
    kv_l, p_mems = [], []
    for l in range(DEPTH):
        wkv = jnp.concatenate([wk_x[l], wv_x[l]], axis=1).astype(BF16)
        kv = _in_proj(mem2d, norm_mem[l][None, :], wkv, 256).reshape(bsz, N_MEM, 2 * BRANCH)
        kv_l.append(kv)
        p_mems.append((kv, kv, pl.BlockSpec((p_bbc, N_MEM, BRANCH), lambda bi, li: (bi, 0, 0)),
                       pl.BlockSpec((p_bbc, N_MEM, BRANCH), lambda bi, li: (bi, 0, 1))))
    zero_init = {
        'ret': jnp.zeros((DEPTH, bsz, N_HEADS, HEAD_DIM, HEAD_DIM), F32),
        'hgrn': jnp.zeros((DEPTH, bsz, N_HEADS, HEAD_DIM, HEAD_DIM), F32),
        'gla': jnp.zeros((DEPTH, bsz, N_HEADS, GLA_KEY_DIM, HEAD_DIM), F32),
        'rwkv': jnp.zeros((DEPTH, bsz, N_HEADS, HEAD_DIM, HEAD_DIM), F32),
        'shift': jnp.zeros((DEPTH, bsz, RWKV_W), F32),
    }
    y_prompt, sp = _run_trunk(x_prompt, 0, p_mems, zero_init, lps, w_in_all, norm_f2, consts,
                              bb=8, tc=64, c=CHUNK, bbc=p_bbc, lc=512, tm=512)

    cached = {'ret': state_ret, 'hgrn': state_hgrn, 'gla': state_gla, 'rwkv': state_rwkv,
              'shift': state_rwkv_shift}
    cmk = cache_mem_k.reshape(DEPTH, dbsz, N_MEM, BRANCH)
    cmv = cache_mem_v.reshape(DEPTH, dbsz, N_MEM, BRANCH)
    s_mems = []
    for l in range(DEPTH):
        spec = pl.BlockSpec((None, s_bbc, N_MEM, BRANCH), lambda bi, li, _l=l: (_l, bi, 0, 0))
        s_mems.append((cmk, cmv, spec, spec))
    y_sample, ss = _run_trunk(x_sample, PAST_LEN, s_mems, cached, lps, w_in_all, norm_f2, consts,
                              bb=8, tc=dseq, c=dseq, bbc=s_bbc, lc=dseq, tm=256)

    kv_all = jnp.stack(kv_l, axis=0)
    mem_k_p = kv_all[..., :BRANCH].reshape(DEPTH, bsz, N_MEM, N_HEADS, HEAD_DIM)
    mem_v_p = kv_all[..., BRANCH:].reshape(DEPTH, bsz, N_MEM, N_HEADS, HEAD_DIM)
    return (y_prompt, y_sample, sp['ret'], ss['ret'], sp['hgrn'], ss['hgrn'], sp['gla'], ss['gla'],
            sp['rwkv'], ss['rwkv'], sp['shift'], ss['shift'], mem_k_p, mem_v_p)
```

```python
import functools

import jax
import jax.numpy as jnp
from jax import lax
from jax.experimental import pallas as pl
from jax.experimental.pallas import tpu as pltpu

F32 = jnp.float32
BF16 = jnp.bfloat16

D_MODEL = 1024
DEPTH = 2
PAST_LEN = 16384
N_HEADS = 4
HEAD_DIM = 64
BRANCH = N_HEADS * HEAD_DIM
GLA_KEY_DIM = 32
GLA_KEY_WIDTH = N_HEADS * GLA_KEY_DIM
GLA_TAU = 16.0
N_MEM = 256
ROPE_BASE = 10000.0
RMS_EPS = 1e-6
RET_GN_EPS = 1e-5
RWKV_GN_EPS = 64e-5
LB_FLOOR = 1e-20
RWKV_W = 1088
CHUNK = 16

RET_OFF = 0
HGRN_OFF = 1024
GLA_OFF = 2048
RWKV_OFF = 2944
RWKV_PW = 1152
D_IN_PAD = 4096

V7X_VMEM_LIMIT = 60 * 1024 * 1024
SUBLANES = 8
NEG_BIG = -1e30

(V_LB_A, V_LB_B, V_LB_OM, V_HGRN_G, V_GLA_G, V_W0, V_A0, V_V0, V_KK, V_KA, V_RK,
 V_GN_G, V_GN_B) = range(13)
N_VEC = 16


def _rms(x, g):
    return x * lax.rsqrt(jnp.mean(x * x, axis=-1, keepdims=True) + RMS_EPS) * g


def _sigmoid(x):
    return 0.5 * jnp.tanh(0.5 * x) + 0.5


def _silu(x):
    h = 0.5 * x
    return h * jnp.tanh(h) + h


def _log1p_exp(x):
    return jnp.log(1.0 + jnp.exp(x))


def _log_sigmoid(x):
    return jnp.minimum(x, 0.0) - _log1p_exp(-jnp.abs(x))


def _dot(a, b):
    return jnp.dot(a.astype(BF16), b.astype(BF16), preferred_element_type=F32)


def _dot_nt(a, b):
    return lax.dot_general(a.astype(BF16), b.astype(BF16), (((1,), (1,)), ((), ())),
                           preferred_element_type=F32)


def _dot_tn(a, b):
    return lax.dot_general(a.astype(BF16), b.astype(BF16), (((0,), (0,)), ((), ())),
                           preferred_element_type=F32)


def _row_tiles(x, n):
    return [x[i * SUBLANES:(i + 1) * SUBLANES, :] for i in range(n)]


def _w_in_layout_kernel(w_ref, o_ref):
    w = w_ref[...]
    zeros = lambda n: jnp.zeros((w.shape[0], n), w.dtype)
    o_ref[...] = jnp.concatenate([
        w[:, 0:2560],
        w[:, 2576:2832], w[:, 2560:2576], zeros(112),
        w[:, 2832:3088], w[:, 3120:3376], w[:, 3376:3632], w[:, 3664:3920],
        w[:, 3088:3120], w[:, 3632:3664], zeros(64)], axis=1).astype(BF16)


def _w_in_layout(w_in):
    depth, d, n = w_in.shape
    rows = 128
    return pl.pallas_call(
        _w_in_layout_kernel,
        grid=(depth, d // rows),
        in_specs=[pl.BlockSpec((None, rows, n), lambda l, i: (l, i, 0))],
        out_specs=pl.BlockSpec((None, rows, D_IN_PAD), lambda l, i: (l, i, 0)),
        out_shape=jax.ShapeDtypeStruct((depth, d, D_IN_PAD), BF16),
        compiler_params=pltpu.CompilerParams(dimension_semantics=("arbitrary", "arbitrary")),
        name="w_in_layout",
    )(w_in)


def _in_proj_kernel(x_ref, g_ref, w_ref, o_ref):
    o_ref[...] = _dot(_rms(x_ref[...], g_ref[...]), w_ref[...])


def _in_proj(x2d, gain, w_bf16, tm, layer=None):
    m, d = x2d.shape
    n = w_bf16.shape[-1]
    w_spec = (pl.BlockSpec((d, n), lambda i: (0, 0)) if layer is None
              else pl.BlockSpec((None, d, n), lambda i: (layer, 0, 0)))
    return pl.pallas_call(
        _in_proj_kernel,
        grid=(m // tm,),
        in_specs=[pl.BlockSpec((tm, d), lambda i: (i, 0)),
                  pl.BlockSpec((1, d), lambda i: (0, 0)),
                  w_spec],
        out_specs=pl.BlockSpec((tm, n), lambda i: (i, 0)),
        out_shape=jax.ShapeDtypeStruct((m, n), F32),
        compiler_params=pltpu.CompilerParams(dimension_semantics=("arbitrary",),
                                             vmem_limit_bytes=V7X_VMEM_LIMIT),
        name="in_proj",
    )(x2d, gain, w_bf16)


def _chunk_cumsum(x, c):
    t_local = jnp.bitwise_and(lax.broadcasted_iota(jnp.int32, x.shape, 0), c - 1)
    s = 1
    while s < c:
        x = x + jnp.where(t_local >= s, pltpu.roll(x, s, 0), 0.0)
        s *= 2
    return x


def _head_stack(x, dk=HEAD_DIM):
    lane_head = jnp.right_shift(lax.broadcasted_iota(jnp.int32, x.shape, 1), dk.bit_length() - 1)
    return jnp.concatenate([jnp.where(lane_head == h, x, 0.0) for h in range(N_HEADS)], axis=0)


def _head_unstack(x4, n):
    lane_head = jnp.right_shift(lax.broadcasted_iota(jnp.int32, (n, BRANCH), 1), 6)
    out = jnp.where(lane_head == 0, x4[0:n, :], 0.0)
    for h in range(1, N_HEADS):
        out = out + jnp.where(lane_head == h, x4[h * n:(h + 1) * n, :], 0.0)
    return out


def _heads_to_rows(x):
    return jnp.concatenate([x[:, h * HEAD_DIM:(h + 1) * HEAD_DIM] for h in range(N_HEADS)], axis=0)


def _rows_to_heads(x4, n):
    return jnp.concatenate([x4[h * n:(h + 1) * n, :] for h in range(N_HEADS)], axis=1)


def _state_read(lhs_stacked, st, n):
    return _rows_to_heads(_dot_nt(lhs_stacked, st), n)


def _state_outer(values, keys_stacked):
    return _dot_tn(_heads_to_rows(values), keys_stacked)


def _gl_tiles(q, k, b, c):
    n_rt = c // SUBLANES
    rows = lax.broadcasted_iota(jnp.int32, (SUBLANES, q.shape[1]), 0)
    q_t, b_t = _row_tiles(q, n_rt), _row_tiles(b, n_rt)
    tiles, where_ = [], {}
    for m in range(c):
        for rt in range(m // SUBLANES, n_rt):
            arg = b_t[rt] - b[m:m + 1, :]
            if rt == m // SUBLANES:
                arg = jnp.where(rows >= m % SUBLANES, arg, NEG_BIG)
            where_[m, rt] = len(tiles) * SUBLANES
            tiles.append(q_t[rt] * k[m:m + 1, :] * jnp.exp(arg))
    return tiles, where_


def _gl_output(scores, o_state, v, where_, c):
    o = _row_tiles(o_state, c // SUBLANES)
    for (m, rt), off in where_.items():
        o[rt] = o[rt] + scores[off:off + SUBLANES, :] * v[m:m + 1, :]
    return jnp.concatenate(o, axis=0)


def _gl_state(k, v, b, st, c):
    b_last = b[c - 1:c, :]
    dk = k.shape[1] // N_HEADS
    return st * jnp.exp(b_last) + _state_outer(v, _head_stack(k * jnp.exp(b_last - b), dk))


def _rwkv_tiles(kkt, ahat, c):
    n_rt = c // SUBLANES
    rows = lax.broadcasted_iota(jnp.int32, (SUBLANES, BRANCH), 0)
    kk_t = _row_tiles(kkt, n_rt)
    tiles, where_ = [], {}
    for m in range(c - 1):
        for rt_i in range(m // SUBLANES, n_rt):
            on_diag = rt_i == m // SUBLANES
            if on_diag and m % SUBLANES == SUBLANES - 1:
                continue
            kk_m = jnp.where(rows > m % SUBLANES, kk_t[rt_i], 0.0) if on_diag else kk_t[rt_i]
            where_[m, rt_i] = len(tiles) * SUBLANES
            tiles.append(kk_m * ahat[m:m + 1, :])
    return tiles, where_


def _rwkv_compact(lhs_stacked, khat, ahat):
    return _dot_nt(lhs_stacked, jnp.concatenate([khat, ahat], axis=0))


def _rwkv_masks(sc, c):
    row = lax.broadcasted_iota(jnp.int32, (2 * N_HEADS * c, c), 0)
    col = lax.broadcasted_iota(jnp.int32, (2 * N_HEADS * c, c), 1)
    keep = col < jnp.bitwise_and(row, c - 1) + jnp.where(jnp.bitwise_and(row, c) != 0, 1, 0)
    r_a = jnp.concatenate([sc[(2 * h + 1) * c:(2 * h + 2) * c, c:2 * c] for h in range(N_HEADS)], axis=0)
    row4 = lax.broadcasted_iota(jnp.int32, (N_HEADS * c, c), 0)
    col4 = lax.broadcasted_iota(jnp.int32, (N_HEADS * c, c), 1)
    return jnp.where(keep, sc[:, 0:c], 0.0), jnp.where(col4 <= jnp.bitwise_and(row4, c - 1), r_a, 0.0)


def _rwkv_solve(sc_a, rhs, where_, c):
    n_rt = c // SUBLANES
    u = _row_tiles(rhs, n_rt)
    for s in range(c - 1):
        u_s = u[s // SUBLANES][s % SUBLANES:s % SUBLANES + 1, :]
        for rt_i in range(s // SUBLANES, n_rt):
            if (s, rt_i) in where_:
                off = where_[s, rt_i]
                u[rt_i] = u[rt_i] - sc_a[off:off + SUBLANES, :] * u_s
    return jnp.concatenate(u, axis=0)


def _rwkv_state(u, v, khat, ahat, p_last, st):
    return st * p_last + _state_outer(jnp.concatenate([v, -u], axis=0),
                                      _head_stack(jnp.concatenate([khat * p_last, ahat * p_last], axis=0)))


def _mixers_kernel(*refs, bb, tc, c, layer0, n_t):
    it = iter(refs)
    proj_ref = next(it)
    vfirst_in_ref = None if layer0 else next(it)
    cos_ref, sin_ref = next(it), next(it)
    s_ret0, s_hgrn0, s_gla0, s_rw0, shift0 = next(it), next(it), next(it), next(it), next(it)
    for _ in range(4):
        next(it)
    vec_ref, mu_ref, bgate_ref = next(it), next(it), next(it)
    wg_ref, w2_ref, a2_ref = next(it), next(it), next(it)
    v1_ref, v2_ref = (None, None) if layer0 else (next(it), next(it))
    bo_ref, bog_ref, swap_ref = next(it), next(it), next(it)
    dmat_ref, eq_ref, ek_ref, gblk_ref = next(it), next(it), next(it), next(it)
    ocat_ref = next(it)
    vfirst_out_ref = next(it) if layer0 else None
    o_ret_st, o_hgrn_st, o_gla_st, o_rw_st = next(it), next(it), next(it), next(it)
    st_ret, st_hgrn, st_gla, st_rw, carry = next(it), next(it), next(it), next(it), next(it)
    qkvb = [[next(it) for _ in range(4)] for _ in range(2)]
    o_gl = [next(it) for _ in range(2)]
    ps_ref, orw_ref = next(it), next(it)
    rw = [next(it) for _ in range(6)]

    r = bb * tc
    ti = pl.program_id(1)
    key_value_io = ((st_ret, s_ret0, o_ret_st), (st_hgrn, s_hgrn0, o_hgrn_st), (st_gla, s_gla0, o_gla_st))

    def vec(i):
        return vec_ref[i:i + 1, :]

    @pl.when(ti == 0)
    def _init():
        for b in range(bb):
            for st_ref, s0_ref, _ in key_value_io:
                st_ref[b] = s0_ref[b].T
            for h in range(N_HEADS):
                st_rw[b, :, h * HEAD_DIM:(h + 1) * HEAD_DIM] = s_rw0[b, h]
        carry[...] = shift0[...]

    def blk(lo, width):
        return proj_ref[:, :, lo:lo + width].reshape(r, width)

    bo = bo_ref[...]
    bo2 = jnp.concatenate([bo, bo], axis=0)

    def head_sum(x, two_pass=False):
        if not two_pass:
            return _dot(x, bo)
        hi = x.astype(BF16)
        lo = (x - hi.astype(F32)).astype(BF16)
        return jnp.dot(jnp.concatenate([hi, lo], axis=1), bo2, preferred_element_type=F32)

    def head_mean(x, two_pass=False):
        return head_sum(x, two_pass) * (1.0 / HEAD_DIM)

    cos = jnp.concatenate([cos_ref[...]] * bb, axis=0)
    sin = jnp.concatenate([sin_ref[...]] * bb, axis=0)
    def rope(x):
        return x * cos + _dot(x, swap_ref[...]) * sin

    q_r = rope(blk(RET_OFF, BRANCH))
    k_r = rope(blk(RET_OFF + 256, BRANCH)) * (HEAD_DIM ** -0.5)
    v_r = blk(RET_OFF + 512, BRANCH)
    seqs = range(bb)
    blocks = [slice(b * tc, (b + 1) * tc) for b in seqs]
    ret_sc = [_dot_nt(_head_stack(q_r[rows, :]), k_r[rows, :]) * dmat_ref[...] for rows in blocks]
    ret_o4 = [_dot(ret_sc[b], v_r[blocks[b], :]) for b in seqs]
    ret_os = [_state_read(_head_stack(q_r[blocks[b], :] * eq_ref[...]), st_ret[b], tc) for b in seqs]
    for b in seqs:
        upd = _state_outer(v_r[blocks[b], :], _head_stack(k_r[blocks[b], :] * ek_ref[...]))
        st_ret[b] = st_ret[b] * gblk_ref[...] + upd
    o_ret = jnp.concatenate([ret_os[b] + _head_unstack(ret_o4[b], tc) for b in seqs], axis=0)

    q_h, k_h, v_h, b_h = qkvb[0]
    q_h[...] = _silu(blk(HGRN_OFF, BRANCH))
    ff = blk(HGRN_OFF + 256, BRANCH)
    y = vec(V_LB_B) + _log_sigmoid(ff)
    a_lb = jnp.broadcast_to(vec(V_LB_A), y.shape)
    mx = jnp.maximum(a_lb, y)
    log_f = mx + _log1p_exp(jnp.minimum(a_lb, y) - mx)
    k_h[...] = vec(V_LB_OM) * _sigmoid(-ff)
    v_h[...] = blk(HGRN_OFF + 512, BRANCH)
    b_h[...] = _chunk_cumsum(log_f, c)

    q_g, k_g, v_g, b_g = qkvb[1]
    q_g[...] = blk(GLA_OFF, GLA_KEY_WIDTH) * (GLA_KEY_DIM ** -0.5)
    k_g[...] = blk(GLA_OFF + 128, GLA_KEY_WIDTH)
    v_g[...] = blk(GLA_OFF + 256, BRANCH)
    z = _dot(blk(GLA_OFF + 768, 128), wg_ref[...]) + bgate_ref[...]
    b_g[...] = _chunk_cumsum(_log_sigmoid(z) / GLA_TAU, c)

    p_rw = blk(RWKV_OFF, RWKV_PW)
    mu = mu_ref[...]
    ps_ref[...] = p_rw + (pltpu.roll(p_rw, 1, 0) - p_rw) * mu
    for b in range(bb):
        p0 = p_rw[b * tc:b * tc + 1, :]
        ps_ref[b * tc:b * tc + 1, :] = p0 + (carry[b:b + 1, :] - p0) * mu
        carry[b:b + 1, :] = p_rw[(b + 1) * tc - 1:(b + 1) * tc, :]

    rr = ps_ref[:, 0:256]
    k_raw = ps_ref[:, 256:512]
    vv = ps_ref[:, 512:768]
    low_rank = ps_ref[:, 1024:1152]
    w_pre = vec(V_W0) + _dot(jnp.tanh(low_rank), w2_ref[...])
    w_log = -(jnp.maximum(-w_pre, 0.0) + _log1p_exp(-jnp.abs(w_pre))) - 0.5
    log_w = -jnp.exp(w_log)
    a = _sigmoid(vec(V_A0) + _dot(low_rank, a2_ref[...]))
    if layer0:
        vfirst_out_ref[...] = vv.reshape(bb, tc, BRANCH)
    else:
        vmix = _sigmoid(vec(V_V0) + _dot(_dot(vv, v1_ref[...]), v2_ref[...]))
        vv = vv + (vfirst_in_ref[...].reshape(r, BRANCH) - vv) * vmix
    kk_raw = k_raw * vec(V_KK)
    kk = kk_raw / jnp.maximum(jnp.sqrt(head_sum(kk_raw * kk_raw, two_pass=True)), 1e-12)
    kmod = k_raw * (1.0 + (a - 1.0) * vec(V_KA))
    b_w = _chunk_cumsum(log_w, c)
    inv_p = jnp.exp(-b_w)
    p_w = jnp.exp(b_w)
    for ref, arr in zip(rw, (kk * jnp.exp(b_w - log_w), rr * p_w, kmod * inv_p, kk * a * inv_p, vv, p_w)):
        ref[...] = arr

    n_c = tc // c
    mix = ((qkvb[0], o_gl[0], st_hgrn, bo_ref, HEAD_DIM), (qkvb[1], o_gl[1], st_gla, bog_ref, GLA_KEY_DIM))

    def chunk_body(ci, _):
        rows = [pl.ds(pl.multiple_of(b * tc + ci * c, c), c) for b in seqs]

        def all_scores(tiles, bo_m):
            sc = _dot(jnp.concatenate(tiles, axis=0), bo_m)
            n = len(tiles) // bb * SUBLANES
            return [sc[b * n:(b + 1) * n, :] for b in seqs]

        gl = []
        for (q_s, k_s, v_s, b_s), o_s, st_ref, bo_m, dk in mix:
            ins = [(q_s[rows[b], :], k_s[rows[b], :], v_s[rows[b], :], b_s[rows[b], :]) for b in seqs]
            tiles = []
            for q, k, _, b_c in ins:
                t, where_ = _gl_tiles(q, k, b_c, c)
                tiles += t
            gl.append((ins, all_scores(tiles, bo_m[...]), where_, o_s, st_ref, dk))
        rw_in = [tuple(x[rows[b], :] for x in rw) for b in seqs]
        tiles = []
        for kkt, _, _, ahat, _, _ in rw_in:
            t, rw_where = _rwkv_tiles(kkt, ahat, c)
            tiles += t
        rw_sc = all_scores(tiles, bo)
        rw_lhs = [_head_stack(jnp.concatenate([kkt, rt], axis=0)) for kkt, rt, _, _, _, _ in rw_in]
        rw_cmp = [_rwkv_compact(rw_lhs[b], rw_in[b][2], rw_in[b][3]) for b in seqs]
        gl_os = [[_state_read(_head_stack(q * jnp.exp(b_c), dk), st_ref[b], c)
                  for b, (q, _, _, b_c) in enumerate(ins)] for ins, _, _, _, st_ref, dk in gl]
        rw_fs = [_state_read(rw_lhs[b], st_rw[b], 2 * c) for b in seqs]
        rw_m = [_rwkv_masks(rw_cmp[b], c) for b in seqs]
        rw_fv = [_head_unstack(_dot(rw_m[b][0], rw_in[b][4]), 2 * c) for b in seqs]
        for ins, _, _, _, st_ref, _ in gl:
            for b, (_, k, v, b_c) in enumerate(ins):
                st_ref[b] = _gl_state(k, v, b_c, st_ref[b], c)
        for (ins, sc, where_, o_s, _, _), o_state in zip(gl, gl_os):
            for b in seqs:
                o_s[rows[b], :] = _gl_output(sc[b], o_state[b], ins[b][2], where_, c)
        rhs = [rw_fs[b] + rw_fv[b] for b in seqs]
        us = [_rwkv_solve(rw_sc[b], rhs[b][0:c, :], rw_where, c) for b in seqs]
        rw_ou = [_dot(rw_m[b][1], us[b]) for b in seqs]
        for b in seqs:
            _, _, khat, ahat, v_w, p_c = rw_in[b]
            st_rw[b] = _rwkv_state(us[b], v_w, khat, ahat, p_c[c - 1:c, :], st_rw[b])
        for b in seqs:
            orw_ref[rows[b], :] = rhs[b][c:2 * c, :] - _head_unstack(rw_ou[b], c)
        return 0

    lax.fori_loop(0, n_c, chunk_body, 0)

    def group_norm(o, eps):
        cen = o - head_mean(o, two_pass=True)
        return cen * lax.rsqrt(head_mean(cen * cen) + eps)

    def head_rms(o, g):
        return o * lax.rsqrt(head_mean(o * o) + RMS_EPS) * g

    def put(lo, val):
        ocat_ref[:, :, lo:lo + BRANCH] = val.reshape(bb, tc, BRANCH)

    put(0, group_norm(o_ret, RET_GN_EPS) * _silu(blk(RET_OFF + 768, BRANCH)))
    put(256, head_rms(o_gl[0][...], vec(V_HGRN_G)) * _silu(blk(HGRN_OFF + 768, BRANCH)))
    put(512, head_rms(o_gl[1][...], vec(V_GLA_G)) * _silu(blk(GLA_OFF + 512, BRANCH)))
    o_rw = group_norm(orw_ref[...], RWKV_GN_EPS) * vec(V_GN_G) + vec(V_GN_B)
    bonus = head_sum(rr * kmod * vec(V_RK))
    put(768, (o_rw + bonus * vv) * _silu(ps_ref[:, 768:1024]))

    @pl.when(ti == n_t - 1)
    def _final():
        for b in range(bb):
            for st_ref, _, out_ref in key_value_io:
                out_ref[b] = st_ref[b].T
            for h in range(N_HEADS):
                o_rw_st[b, h] = st_rw[b][:, h * HEAD_DIM:(h + 1) * HEAD_DIM]


def _mixers(proj, vfirst, cos, sin, states, new_states, layer, lp, consts, *, bb, tc, c):
    bsz, seq, _ = proj.shape
    layer0 = vfirst is None
    n_t = seq // tc
    r = bb * tc

    def full(arr):
        nd = arr.ndim
        return pl.BlockSpec(arr.shape, lambda bi, ti, _nd=nd: (0,) * _nd)

    def per_b(arr):
        nd = arr.ndim
        return pl.BlockSpec((None, bb) + arr.shape[2:], lambda bi, ti, _nd=nd: (layer, bi) + (0,) * (_nd - 2))

    def tok(width):
        return pl.BlockSpec((bb, tc, width), lambda bi, ti: (bi, ti, 0))

    ins, specs = [proj], [tok(D_IN_PAD)]
    if not layer0:
        ins.append(vfirst)
        specs.append(tok(BRANCH))
    ins += [cos, sin]
    specs += [pl.BlockSpec((tc, BRANCH), lambda bi, ti: (ti, 0))] * 2
    for s in states:
        ins.append(s)
        specs.append(per_b(s))
    new_state_inputs = list(range(len(ins), len(ins) + len(new_states)))
    for s in new_states:
        ins.append(s)
        specs.append(pl.BlockSpec(memory_space=pl.ANY))
    small = [lp['vec'], lp['mu'], lp['bgate'], lp['wg'], lp['w2'], lp['a2']]
    if not layer0:
        small += [lp['v1'], lp['v2']]
    small += list(consts) + list(_retention_tables(tc))
    for s in small:
        ins.append(s)
        specs.append(full(s))

    out_shape = [jax.ShapeDtypeStruct((bsz, seq, D_MODEL), F32)]
    out_specs = [tok(D_MODEL)]
    if layer0:
        out_shape.append(jax.ShapeDtypeStruct((bsz, seq, BRANCH), F32))
        out_specs.append(tok(BRANCH))
    aliases = {}
    for idx, s in zip(new_state_inputs, new_states):
        aliases[idx] = len(out_shape)
        out_shape.append(jax.ShapeDtypeStruct(s.shape, F32))
        out_specs.append(per_b(s))

    scratch = [pltpu.VMEM((bb, HEAD_DIM, BRANCH), F32), pltpu.VMEM((bb, HEAD_DIM, BRANCH), F32),
               pltpu.VMEM((bb, HEAD_DIM, GLA_KEY_WIDTH), F32), pltpu.VMEM((bb, HEAD_DIM, BRANCH), F32),
               pltpu.VMEM((bb, RWKV_PW), F32)]
    for dkt in (BRANCH, GLA_KEY_WIDTH):
        scratch += [pltpu.VMEM((r, dkt), F32), pltpu.VMEM((r, dkt), F32), pltpu.VMEM((r, BRANCH), F32),
                    pltpu.VMEM((r, dkt), F32)]
    scratch += [pltpu.VMEM((r, BRANCH), F32)] * 2
    scratch += [pltpu.VMEM((r, RWKV_PW), F32), pltpu.VMEM((r, BRANCH), F32)]
    scratch += [pltpu.VMEM((r, BRANCH), F32)] * 6

    outs = pl.pallas_call(
        functools.partial(_mixers_kernel, bb=bb, tc=tc, c=c, layer0=layer0, n_t=n_t),
        grid=(bsz // bb, n_t),
        in_specs=specs,
        out_specs=out_specs,
        out_shape=out_shape,
        scratch_shapes=scratch,
        input_output_aliases=aliases,
        compiler_params=pltpu.CompilerParams(dimension_semantics=("arbitrary", "arbitrary"),
                                             vmem_limit_bytes=V7X_VMEM_LIMIT),
        name="mixers",
    )(*ins)
    if layer0:
        return outs[0], outs[1], list(outs[2:])
    return outs[0], None, list(outs[1:])


def _out_attn_kernel(x_ref, oc_ref, mk_ref, mv_ref, wout_ref, wq_ref, wo_ref, gx_ref, gf_ref, out_ref,
                     *, bbc, lc, final):
    r = bbc * lc
    x1 = x_ref[...].reshape(r, D_MODEL) + _dot(oc_ref[...].reshape(r, D_MODEL), wout_ref[...])
    q = _dot(_rms(x1, gx_ref[...]), wq_ref[...])
    seqs = range(bbc)
    scores = [_dot_nt(_head_stack(q[b * lc:(b + 1) * lc, :]), mk_ref[b]) * (HEAD_DIM ** -0.5) for b in seqs]
    probs = []
    for s in scores:
        e = jnp.exp(s - jnp.max(s, axis=-1, keepdims=True))
        probs.append(e / jnp.sum(e, axis=-1, keepdims=True))
    o4 = [_dot(probs[b], mv_ref[b]) for b in seqs]
    outs = [_head_unstack(o4[b], lc) for b in seqs]
    o = outs[0] if bbc == 1 else jnp.concatenate(outs, axis=0)
    x2 = x1 + _dot(o, wo_ref[...])
    if final:
        x2 = _rms(x2, gf_ref[...])
    out_ref[...] = x2.reshape(bbc, lc, D_MODEL)


def _out_attn(x, ocat, mem, lp, norm_f, *, bbc, lc, final):
    bsz, seq, _ = x.shape
    mk, mv, mk_spec, mv_spec = mem

    def tok(width):
        return pl.BlockSpec((bbc, lc, width), lambda bi, li: (bi, li, 0))

    def full(arr):
        nd = arr.ndim
        return pl.BlockSpec(arr.shape, lambda bi, li, _nd=nd: (0,) * _nd)

    small = [lp['w_out'], lp['wq'], lp['wo'], lp['norm_x'], norm_f]
    return pl.pallas_call(
        functools.partial(_out_attn_kernel, bbc=bbc, lc=lc, final=final),
        grid=(bsz // bbc, seq // lc),
        in_specs=[tok(D_MODEL), tok(D_MODEL), mk_spec, mv_spec] + [full(s) for s in small],
        out_specs=tok(D_MODEL),
        out_shape=jax.ShapeDtypeStruct((bsz, seq, D_MODEL), F32),
        compiler_params=pltpu.CompilerParams(dimension_semantics=("arbitrary", "arbitrary"),
                                             vmem_limit_bytes=V7X_VMEM_LIMIT),
        name="out_attn",
    )(x, ocat, mk, mv, *small)


def _rope_tables(pos0, seq):
    half = HEAD_DIM // 2
    inv = ROPE_BASE ** (-jnp.arange(half, dtype=F32) / half)
    pos = (pos0 + jnp.arange(seq, dtype=jnp.int32)).astype(F32)
    ang = pos[:, None] * inv[None, :]
    cos, sin = lax.optimization_barrier((jnp.cos(ang), jnp.sin(ang)))
    return (jnp.tile(jnp.concatenate([cos, cos], axis=-1), (1, N_HEADS)),
            jnp.tile(jnp.concatenate([-sin, sin], axis=-1), (1, N_HEADS)))


def _retention_tables(n):
    log_gamma = jnp.log1p(-jnp.exp2(-5.0 - jnp.arange(N_HEADS, dtype=F32)))
    j = jnp.arange(n, dtype=F32)
    diff = j[:, None] - j[None, :]
    dmat = jnp.where(diff[None] >= 0, jnp.exp(diff[None] * log_gamma[:, None, None]), 0.0)
    lg_lanes = jnp.repeat(log_gamma, HEAD_DIM)[None, :]
    e_q = jnp.exp((j[:, None] + 1.0) * lg_lanes)
    e_k = jnp.exp((n - 1.0 - j[:, None]) * lg_lanes)
    g_blk = jnp.exp(float(n) * lg_lanes)
    return dmat.reshape(N_HEADS * n, n), e_q, e_k, g_blk


def _rwkv_to_padded(t):
    pad = jnp.zeros(t.shape[:-1] + (RWKV_PW - RWKV_W,), t.dtype)
    return jnp.concatenate([t[..., 0:256], t[..., 288:544], t[..., 544:800], t[..., 832:1088],
                            t[..., 256:288], t[..., 800:832], pad], axis=-1)


def _rwkv_from_padded(t):
    return jnp.concatenate([t[..., 0:256], t[..., 1024:1056], t[..., 256:512], t[..., 512:768],
                            t[..., 1056:1088], t[..., 768:1024]], axis=-1)


def _pad_rows(m, rows, at=0):
    out = jnp.zeros((rows, m.shape[1]), m.dtype)
    return out.at[at:at + m.shape[0]].set(m)


def _layer_params(l, P):
    sm = jax.nn.softmax(P['hgrn_lb_logits'].astype(F32), axis=0)
    lb = (jnp.cumsum(sm, axis=0) - sm[0])[l]
    rows = [None] * N_VEC
    rows[V_LB_A] = jnp.log(jnp.maximum(lb, LB_FLOOR))
    rows[V_LB_B] = jnp.log1p(-lb)
    rows[V_LB_OM] = 1.0 - lb
    rows[V_HGRN_G] = jnp.tile(P['hgrn_norm'][l], N_HEADS)
    rows[V_GLA_G] = jnp.tile(P['gla_norm'][l], N_HEADS)
    rows[V_W0] = P['rwkv_w0'][l]
    rows[V_A0] = P['rwkv_a0'][l]
    rows[V_V0] = P['rwkv_v0'][l - 1] if l > 0 else jnp.zeros((BRANCH,), F32)
    rows[V_KK] = P['rwkv_k_k'][l]
    rows[V_KA] = P['rwkv_k_a'][l]
    rows[V_RK] = P['rwkv_r_k'][l].reshape(BRANCH)
    rows[V_GN_G] = P['rwkv_gn_g'][l]
    rows[V_GN_B] = P['rwkv_gn_b'][l]
    zero = jnp.zeros((BRANCH,), F32)
    vec = jnp.stack([zero if x is None else x.astype(F32) for x in rows], axis=0)

    lp = {
        'norm_mix': P['norm_mix'][l][None, :],
        'vec': vec,
        'mu': _rwkv_to_padded(P['rwkv_mu'][l])[None, :],
        'bgate': P['gla_b_gate'][l][None, :],
        'wg': _pad_rows(P['gla_w_gate2'][l], 128).astype(BF16),
        'w2': _pad_rows(P['rwkv_w2'][l], 128, 0).astype(BF16),
        'a2': _pad_rows(P['rwkv_a2'][l], 128, 32).astype(BF16),
        'w_out': P['w_out'][l].astype(BF16),
        'wq': P['wq_x'][l].astype(BF16),
        'wo': P['wo_x'][l].astype(BF16),
        'norm_x': P['norm_x'][l][None, :],
    }
    if l > 0:
        lp['v1'] = jnp.pad(P['rwkv_v1'][l - 1], ((0, 0), (0, 96))).astype(BF16)
        lp['v2'] = _pad_rows(P['rwkv_v2'][l - 1], 128).astype(BF16)
    return lp


def _consts():
    i256 = jnp.arange(BRANCH) // HEAD_DIM
    i128 = jnp.arange(GLA_KEY_WIDTH) // GLA_KEY_DIM
    bo = (i256[:, None] == i256[None, :])
    bog = (i128[:, None] == i256[None, :])
    lane = jnp.arange(BRANCH)
    partner = jnp.where(lane % HEAD_DIM < HEAD_DIM // 2, lane + HEAD_DIM // 2, lane - HEAD_DIM // 2)
    swap = (lane[:, None] == partner[None, :])
    return bo.astype(BF16), bog.astype(BF16), swap.astype(BF16)


def _run_trunk(x, pos0, mems, init, lps, w_in_all, norm_f, consts, *, bb, tc, c, bbc, lc, tm):
    bsz, seq, _ = x.shape
    cos, sin = _rope_tables(pos0, seq)
    new_shift = []
    new_states = [jnp.zeros((DEPTH, bsz, BRANCH, HEAD_DIM), F32), jnp.zeros((DEPTH, bsz, BRANCH, HEAD_DIM), F32),
                  jnp.zeros((DEPTH, bsz, GLA_KEY_WIDTH, HEAD_DIM), F32),
                  jnp.zeros((DEPTH, bsz, N_HEADS, HEAD_DIM, HEAD_DIM), F32)]
    states = [init['ret'].reshape(DEPTH, bsz, BRANCH, HEAD_DIM), init['hgrn'].reshape(DEPTH, bsz, BRANCH, HEAD_DIM),
              init['gla'].reshape(DEPTH, bsz, GLA_KEY_WIDTH, HEAD_DIM), init['rwkv'],
              _rwkv_to_padded(init['shift'])]
    vfirst = None
    for l in range(DEPTH):
        lp = lps[l]
        proj = _in_proj(x.reshape(bsz * seq, D_MODEL), lp['norm_mix'], w_in_all, tm, layer=l)
        proj = proj.reshape(bsz, seq, D_IN_PAD)
        ocat, vf, new_states = _mixers(proj, vfirst, cos, sin, states, new_states, l, lp, consts,
                                       bb=bb, tc=tc, c=c)
        if l == 0:
            vfirst = vf
        new_shift.append(_rwkv_from_padded(proj[:, seq - 1, RWKV_OFF:RWKV_OFF + RWKV_PW]))
        x = _out_attn(x, ocat, mems[l], lp, norm_f, bbc=bbc, lc=lc, final=(l == DEPTH - 1))
    return x, {'ret': new_states[0].reshape(DEPTH, bsz, N_HEADS, HEAD_DIM, HEAD_DIM),
               'hgrn': new_states[1].reshape(DEPTH, bsz, N_HEADS, HEAD_DIM, HEAD_DIM),
               'gla': new_states[2].reshape(DEPTH, bsz, N_HEADS, GLA_KEY_DIM, HEAD_DIM),
               'rwkv': new_states[3], 'shift': jnp.stack(new_shift, axis=0)}


def kernel(x_prompt, x_sample, mem_prompt, state_ret, state_hgrn, state_gla, state_rwkv, state_rwkv_shift,
           cache_mem_k, cache_mem_v, norm_mix, w_in, hgrn_lb_logits, hgrn_norm, gla_w_gate2, gla_b_gate,
           gla_norm, rwkv_mu, rwkv_w0, rwkv_w2, rwkv_a0, rwkv_a2, rwkv_v0, rwkv_v1, rwkv_v2, rwkv_k_k,
           rwkv_k_a, rwkv_r_k, rwkv_gn_g, rwkv_gn_b, w_out, norm_x, wq_x, wo_x, norm_mem, wk_x, wv_x, norm_f):
    P = {'norm_mix': norm_mix, 'hgrn_lb_logits': hgrn_lb_logits, 'hgrn_norm': hgrn_norm,
         'gla_w_gate2': gla_w_gate2, 'gla_b_gate': gla_b_gate, 'gla_norm': gla_norm,
         'rwkv_mu': rwkv_mu, 'rwkv_w0': rwkv_w0, 'rwkv_w2': rwkv_w2, 'rwkv_a0': rwkv_a0,
         'rwkv_a2': rwkv_a2, 'rwkv_v0': rwkv_v0, 'rwkv_v1': rwkv_v1, 'rwkv_v2': rwkv_v2,
         'rwkv_k_k': rwkv_k_k, 'rwkv_k_a': rwkv_k_a, 'rwkv_r_k': rwkv_r_k,
         'rwkv_gn_g': rwkv_gn_g, 'rwkv_gn_b': rwkv_gn_b, 'w_out': w_out,
         'norm_x': norm_x, 'wq_x': wq_x, 'wo_x': wo_x}
    lps = [_layer_params(l, P) for l in range(DEPTH)]
    w_in_all = _w_in_layout(w_in)
    consts = _consts()
    norm_f2 = norm_f[None, :]
    bsz, seq, _ = x_prompt.shape
    dbsz, dseq, _ = x_sample.shape
    p_bbc, s_bbc = 2, 16

    mem2d = mem_prompt.reshape(bsz * N_MEM, D_MODEL)
    kv_l, p_mems = [], []
    for l in range(DEPTH):
        wkv = jnp.concatenate([wk_x[l], wv_x[l]], axis=1).astype(BF16)
        kv = _in_proj(mem2d, norm_mem[l][None, :], wkv, 256).reshape(bsz, N_MEM, 2 * BRANCH)
        kv_l.append(kv)
        p_mems.append((kv, kv, pl.BlockSpec((p_bbc, N_MEM, BRANCH), lambda bi, li: (bi, 0, 0)),
                       pl.BlockSpec((p_bbc, N_MEM, BRANCH), lambda bi, li: (bi, 0, 1))))
    zero_init = {
        'ret': jnp.zeros((DEPTH, bsz, N_HEADS, HEAD_DIM, HEAD_DIM), F32),
        'hgrn': jnp.zeros((DEPTH, bsz, N_HEADS, HEAD_DIM, HEAD_DIM), F32),
        'gla': jnp.zeros((DEPTH, bsz, N_HEADS, GLA_KEY_DIM, HEAD_DIM), F32),
        'rwkv': jnp.zeros((DEPTH, bsz, N_HEADS, HEAD_DIM, HEAD_DIM), F32),
        'shift': jnp.zeros((DEPTH, bsz, RWKV_W), F32),
    }
    y_prompt, sp = _run_trunk(x_prompt, 0, p_mems, zero_init, lps, w_in_all, norm_f2, consts,
                              bb=8, tc=64, c=CHUNK, bbc=p_bbc, lc=512, tm=512)

    cached = {'ret': state_ret, 'hgrn': state_hgrn, 'gla': state_gla, 'rwkv': state_rwkv,
              'shift': state_rwkv_shift}
    cmk = cache_mem_k.reshape(DEPTH, dbsz, N_MEM, BRANCH)
    cmv = cache_mem_v.reshape(DEPTH, dbsz, N_MEM, BRANCH)
    s_mems = []
    for l in range(DEPTH):
        spec = pl.BlockSpec((None, s_bbc, N_MEM, BRANCH), lambda bi, li, _l=l: (_l, bi, 0, 0))
        s_mems.append((cmk, cmv, spec, spec))
    y_sample, ss = _run_trunk(x_sample, PAST_LEN, s_mems, cached, lps, w_in_all, norm_f2, consts,
                              bb=8, tc=dseq, c=dseq, bbc=s_bbc, lc=dseq, tm=256)

    kv_all = jnp.stack(kv_l, axis=0)
    mem_k_p = kv_all[..., :BRANCH].reshape(DEPTH, bsz, N_MEM, N_HEADS, HEAD_DIM)
    mem_v_p = kv_all[..., BRANCH:].reshape(DEPTH, bsz, N_MEM, N_HEADS, HEAD_DIM)
    return (y_prompt, y_sample, sp['ret'], ss['ret'], sp['hgrn'], ss['hgrn'], sp['gla'], ss['gla'],
            sp['rwkv'], ss['rwkv'], sp['shift'], ss['shift'], mem_k_p, mem_v_p)
```

```python
import functools

import jax
import jax.numpy as jnp
from jax import lax
from jax.experimental import pallas as pl
from jax.experimental.pallas import tpu as pltpu

F32 = jnp.float32
BF16 = jnp.bfloat16

D_MODEL = 1024
DEPTH = 2
PAST_LEN = 16384
N_HEADS = 4
HEAD_DIM = 64
BRANCH = N_HEADS * HEAD_DIM
GLA_KEY_DIM = 32
GLA_KEY_WIDTH = N_HEADS * GLA_KEY_DIM
GLA_TAU = 16.0
N_MEM = 256
ROPE_BASE = 10000.0
RMS_EPS = 1e-6
RET_GN_EPS = 1e-5
RWKV_GN_EPS = 64e-5
LB_FLOOR = 1e-20
RWKV_W = 1088
CHUNK = 16

RET_OFF = 0
HGRN_OFF = 1024
GLA_OFF = 2048
RWKV_OFF = 2944
RWKV_PW = 1152
D_IN_PAD = 4096

V7X_VMEM_LIMIT = 60 * 1024 * 1024
SUBLANES = 8
NEG_BIG = -1e30

(V_LB_A, V_LB_B, V_LB_OM, V_HGRN_G, V_GLA_G, V_W0, V_A0, V_V0, V_KK, V_KA, V_RK,
 V_GN_G, V_GN_B) = range(13)
N_VEC = 16


def _rms(x, g):
    return x * lax.rsqrt(jnp.mean(x * x, axis=-1, keepdims=True) + RMS_EPS) * g


def _sigmoid(x):
    return 0.5 * jnp.tanh(0.5 * x) + 0.5


def _silu(x):
    h = 0.5 * x
    return h * jnp.tanh(h) + h


def _log1p_exp(x):
    return jnp.log(1.0 + jnp.exp(x))


def _log_sigmoid(x):
    return jnp.minimum(x, 0.0) - _log1p_exp(-jnp.abs(x))


def _dot(a, b):
    return jnp.dot(a.astype(BF16), b.astype(BF16), preferred_element_type=F32)


def _dot_nt(a, b):
    return lax.dot_general(a.astype(BF16), b.astype(BF16), (((1,), (1,)), ((), ())),
                           preferred_element_type=F32)


def _dot_tn(a, b):
    return lax.dot_general(a.astype(BF16), b.astype(BF16), (((0,), (0,)), ((), ())),
                           preferred_element_type=F32)


def _row_tiles(x, n):
    return [x[i * SUBLANES:(i + 1) * SUBLANES, :] for i in range(n)]


def _w_in_layout_kernel(w_ref, o_ref):
    w = w_ref[...]
    zeros = lambda n: jnp.zeros((w.shape[0], n), w.dtype)
    o_ref[...] = jnp.concatenate([
        w[:, 0:2560],
        w[:, 2576:2832], w[:, 2560:2576], zeros(112),
        w[:, 2832:3088], w[:, 3120:3376], w[:, 3376:3632], w[:, 3664:3920],
        w[:, 3088:3120], w[:, 3632:3664], zeros(64)], axis=1).astype(BF16)


def _w_in_layout(w_in):
    depth, d, n = w_in.shape
    rows = 128
    return pl.pallas_call(
        _w_in_layout_kernel,
        grid=(depth, d // rows),
        in_specs=[pl.BlockSpec((None, rows, n), lambda l, i: (l, i, 0))],
        out_specs=pl.BlockSpec((None, rows, D_IN_PAD), lambda l, i: (l, i, 0)),
        out_shape=jax.ShapeDtypeStruct((depth, d, D_IN_PAD), BF16),
        compiler_params=pltpu.CompilerParams(dimension_semantics=("arbitrary", "arbitrary")),
        name="w_in_layout",
    )(w_in)


def _in_proj_kernel(x_ref, g_ref, w_ref, o_ref):
    o_ref[...] = _dot(_rms(x_ref[...], g_ref[...]), w_ref[...])


def _in_proj(x2d, gain, w_bf16, tm, layer=None):
    m, d = x2d.shape
    n = w_bf16.shape[-1]
    w_spec = (pl.BlockSpec((d, n), lambda i: (0, 0)) if layer is None
              else pl.BlockSpec((None, d, n), lambda i: (layer, 0, 0)))
    return pl.pallas_call(
        _in_proj_kernel,
        grid=(m // tm,),
        in_specs=[pl.BlockSpec((tm, d), lambda i: (i, 0)),
                  pl.BlockSpec((1, d), lambda i: (0, 0)),
                  w_spec],
        out_specs=pl.BlockSpec((tm, n), lambda i: (i, 0)),
        out_shape=jax.ShapeDtypeStruct((m, n), F32),
        compiler_params=pltpu.CompilerParams(dimension_semantics=("arbitrary",),
                                             vmem_limit_bytes=V7X_VMEM_LIMIT),
        name="in_proj",
    )(x2d, gain, w_bf16)


def _chunk_cumsum(x, c):
    t_local = jnp.bitwise_and(lax.broadcasted_iota(jnp.int32, x.shape, 0), c - 1)
    s = 1
    while s < c:
        x = x + jnp.where(t_local >= s, pltpu.roll(x, s, 0), 0.0)
        s *= 2
    return x


def _head_stack(x, dk=HEAD_DIM):
    lane_head = jnp.right_shift(lax.broadcasted_iota(jnp.int32, x.shape, 1), dk.bit_length() - 1)
    return jnp.concatenate([jnp.where(lane_head == h, x, 0.0) for h in range(N_HEADS)], axis=0)


def _head_unstack(x4, n):
    lane_head = jnp.right_shift(lax.broadcasted_iota(jnp.int32, (n, BRANCH), 1), 6)
    out = jnp.where(lane_head == 0, x4[0:n, :], 0.0)
    for h in range(1, N_HEADS):
        out = out + jnp.where(lane_head == h, x4[h * n:(h + 1) * n, :], 0.0)
    return out


def _heads_to_rows(x):
    return jnp.concatenate([x[:, h * HEAD_DIM:(h + 1) * HEAD_DIM] for h in range(N_HEADS)], axis=0)


def _rows_to_heads(x4, n):
    return jnp.concatenate([x4[h * n:(h + 1) * n, :] for h in range(N_HEADS)], axis=1)


def _state_read(lhs_stacked, st, n):
    return _rows_to_heads(_dot_nt(lhs_stacked, st), n)


def _state_outer(values, keys_stacked):
    return _dot_tn(_heads_to_rows(values), keys_stacked)


def _gl_tiles(q, k, b, c):
    n_rt = c // SUBLANES
    rows = lax.broadcasted_iota(jnp.int32, (SUBLANES, q.shape[1]), 0)
    q_t, b_t = _row_tiles(q, n_rt), _row_tiles(b, n_rt)
    tiles, where_ = [], {}
    for m in range(c):
        for rt in range(m // SUBLANES, n_rt):
            arg = b_t[rt] - b[m:m + 1, :]
            if rt == m // SUBLANES:
                arg = jnp.where(rows >= m % SUBLANES, arg, NEG_BIG)
            where_[m, rt] = len(tiles) * SUBLANES
            tiles.append(q_t[rt] * k[m:m + 1, :] * jnp.exp(arg))
    return tiles, where_


def _gl_output(scores, o_state, v, where_, c):
    o = _row_tiles(o_state, c // SUBLANES)
    for (m, rt), off in where_.items():
        o[rt] = o[rt] + scores[off:off + SUBLANES, :] * v[m:m + 1, :]
    return jnp.concatenate(o, axis=0)


def _gl_state(k, v, b, st, c):
    b_last = b[c - 1:c, :]
    dk = k.shape[1] // N_HEADS
    return st * jnp.exp(b_last) + _state_outer(v, _head_stack(k * jnp.exp(b_last - b), dk))


def _rwkv_tiles(kkt, ahat, c):
    n_rt = c // SUBLANES
    rows = lax.broadcasted_iota(jnp.int32, (SUBLANES, BRANCH), 0)
    kk_t = _row_tiles(kkt, n_rt)
    tiles, where_ = [], {}
    for m in range(c - 1):
        for rt_i in range(m // SUBLANES, n_rt):
            on_diag = rt_i == m // SUBLANES
            if on_diag and m % SUBLANES == SUBLANES - 1:
                continue
            kk_m = jnp.where(rows > m % SUBLANES, kk_t[rt_i], 0.0) if on_diag else kk_t[rt_i]
            where_[m, rt_i] = len(tiles) * SUBLANES
            tiles.append(kk_m * ahat[m:m + 1, :])
    return tiles, where_


def _rwkv_compact(lhs_stacked, khat, ahat):
    return _dot_nt(lhs_stacked, jnp.concatenate([khat, ahat], axis=0))


def _rwkv_masks(sc, c):
    row = lax.broadcasted_iota(jnp.int32, (2 * N_HEADS * c, c), 0)
    col = lax.broadcasted_iota(jnp.int32, (2 * N_HEADS * c, c), 1)
    keep = col < jnp.bitwise_and(row, c - 1) + jnp.where(jnp.bitwise_and(row, c) != 0, 1, 0)
    r_a = jnp.concatenate([sc[(2 * h + 1) * c:(2 * h + 2) * c, c:2 * c] for h in range(N_HEADS)], axis=0)
    row4 = lax.broadcasted_iota(jnp.int32, (N_HEADS * c, c), 0)
    col4 = lax.broadcasted_iota(jnp.int32, (N_HEADS * c, c), 1)
    return jnp.where(keep, sc[:, 0:c], 0.0), jnp.where(col4 <= jnp.bitwise_and(row4, c - 1), r_a, 0.0)


def _rwkv_solve(sc_a, rhs, where_, c):
    n_rt = c // SUBLANES
    u = _row_tiles(rhs, n_rt)
    for s in range(c - 1):
        u_s = u[s // SUBLANES][s % SUBLANES:s % SUBLANES + 1, :]
        for rt_i in range(s // SUBLANES, n_rt):
            if (s, rt_i) in where_:
                off = where_[s, rt_i]
                u[rt_i] = u[rt_i] - sc_a[off:off + SUBLANES, :] * u_s
    return jnp.concatenate(u, axis=0)


def _rwkv_state(u, v, khat, ahat, p_last, st):
    return st * p_last + _state_outer(jnp.concatenate([v, -u], axis=0),
                                      _head_stack(jnp.concatenate([khat * p_last, ahat * p_last], axis=0)))


def _mixers_kernel(*refs, bb, tc, c, layer0, n_t):
    it = iter(refs)
    proj_ref = next(it)
    vfirst_in_ref = None if layer0 else next(it)
    cos_ref, sin_ref = next(it), next(it)
    s_ret0, s_hgrn0, s_gla0, s_rw0, shift0 = next(it), next(it), next(it), next(it), next(it)
    for _ in range(4):
        next(it)
    vec_ref, mu_ref, bgate_ref = next(it), next(it), next(it)
    wg_ref, w2_ref, a2_ref = next(it), next(it), next(it)
    v1_ref, v2_ref = (None, None) if layer0 else (next(it), next(it))
    bo_ref, bog_ref, swap_ref = next(it), next(it), next(it)
    dmat_ref, eq_ref, ek_ref, gblk_ref = next(it), next(it), next(it), next(it)
    ocat_ref = next(it)
    vfirst_out_ref = next(it) if layer0 else None
    o_ret_st, o_hgrn_st, o_gla_st, o_rw_st = next(it), next(it), next(it), next(it)
    st_ret, st_hgrn, st_gla, st_rw, carry = next(it), next(it), next(it), next(it), next(it)
    qkvb = [[next(it) for _ in range(4)] for _ in range(2)]
    o_gl = [next(it) for _ in range(2)]
    ps_ref, orw_ref = next(it), next(it)
    rw = [next(it) for _ in range(6)]

    r = bb * tc
    ti = pl.program_id(1)
    key_value_io = ((st_ret, s_ret0, o_ret_st), (st_hgrn, s_hgrn0, o_hgrn_st), (st_gla, s_gla0, o_gla_st))

    def vec(i):
        return vec_ref[i:i + 1, :]

    @pl.when(ti == 0)
    def _init():
        for b in range(bb):
            for st_ref, s0_ref, _ in key_value_io:
                st_ref[b] = s0_ref[b].reshape(st_ref.shape[2], HEAD_DIM).T
            for h in range(N_HEADS):
                st_rw[b, :, h * HEAD_DIM:(h + 1) * HEAD_DIM] = s_rw0[b, h]
        carry[...] = shift0[...]

    def blk(lo, width):
        return proj_ref[:, :, lo:lo + width].reshape(r, width)

    bo = bo_ref[...]
    bo2 = jnp.concatenate([bo, bo], axis=0)

    def head_sum(x, two_pass=False):
        if not two_pass:
            return _dot(x, bo)
        hi = x.astype(BF16)
        lo = (x - hi.astype(F32)).astype(BF16)
        return jnp.dot(jnp.concatenate([hi, lo], axis=1), bo2, preferred_element_type=F32)

    def head_mean(x, two_pass=False):
        return head_sum(x, two_pass) * (1.0 / HEAD_DIM)

    cos = jnp.concatenate([cos_ref[...]] * bb, axis=0)
    sin = jnp.concatenate([sin_ref[...]] * bb, axis=0)
    def rope(x):
        return x * cos + _dot(x, swap_ref[...]) * sin

    q_r = rope(blk(RET_OFF, BRANCH))
    k_r = rope(blk(RET_OFF + 256, BRANCH)) * (HEAD_DIM ** -0.5)
    v_r = blk(RET_OFF + 512, BRANCH)
    seqs = range(bb)
    blocks = [slice(b * tc, (b + 1) * tc) for b in seqs]
    ret_sc = [_dot_nt(_head_stack(q_r[rows, :]), k_r[rows, :]) * dmat_ref[...] for rows in blocks]
    ret_o4 = [_dot(ret_sc[b], v_r[blocks[b], :]) for b in seqs]
    ret_os = [_state_read(_head_stack(q_r[blocks[b], :] * eq_ref[...]), st_ret[b], tc) for b in seqs]
    for b in seqs:
        upd = _state_outer(v_r[blocks[b], :], _head_stack(k_r[blocks[b], :] * ek_ref[...]))
        st_ret[b] = st_ret[b] * gblk_ref[...] + upd
    o_ret = jnp.concatenate([ret_os[b] + _head_unstack(ret_o4[b], tc) for b in seqs], axis=0)

    q_h, k_h, v_h, b_h = qkvb[0]
    q_h[...] = _silu(blk(HGRN_OFF, BRANCH))
    ff = blk(HGRN_OFF + 256, BRANCH)
    y = vec(V_LB_B) + _log_sigmoid(ff)
    a_lb = jnp.broadcast_to(vec(V_LB_A), y.shape)
    mx = jnp.maximum(a_lb, y)
    log_f = mx + _log1p_exp(jnp.minimum(a_lb, y) - mx)
    k_h[...] = vec(V_LB_OM) * _sigmoid(-ff)
    v_h[...] = blk(HGRN_OFF + 512, BRANCH)
    b_h[...] = _chunk_cumsum(log_f, c)

    q_g, k_g, v_g, b_g = qkvb[1]
    q_g[...] = blk(GLA_OFF, GLA_KEY_WIDTH) * (GLA_KEY_DIM ** -0.5)
    k_g[...] = blk(GLA_OFF + 128, GLA_KEY_WIDTH)
    v_g[...] = blk(GLA_OFF + 256, BRANCH)
    z = _dot(blk(GLA_OFF + 768, 128), wg_ref[...]) + bgate_ref[...]
    b_g[...] = _chunk_cumsum(_log_sigmoid(z) / GLA_TAU, c)

    p_rw = blk(RWKV_OFF, RWKV_PW)
    mu = mu_ref[...]
    ps_ref[...] = p_rw + (pltpu.roll(p_rw, 1, 0) - p_rw) * mu
    for b in range(bb):
        p0 = p_rw[b * tc:b * tc + 1, :]
        ps_ref[b * tc:b * tc + 1, :] = p0 + (carry[b:b + 1, :] - p0) * mu
        carry[b:b + 1, :] = p_rw[(b + 1) * tc - 1:(b + 1) * tc, :]

    rr = ps_ref[:, 0:256]
    k_raw = ps_ref[:, 256:512]
    vv = ps_ref[:, 512:768]
    low_rank = ps_ref[:, 1024:1152]
    w_pre = vec(V_W0) + _dot(jnp.tanh(low_rank), w2_ref[...])
    w_log = -(jnp.maximum(-w_pre, 0.0) + _log1p_exp(-jnp.abs(w_pre))) - 0.5
    log_w = -jnp.exp(w_log)
    a = _sigmoid(vec(V_A0) + _dot(low_rank, a2_ref[...]))
    if layer0:
        vfirst_out_ref[...] = vv.reshape(bb, tc, BRANCH)
    else:
        vmix = _sigmoid(vec(V_V0) + _dot(_dot(vv, v1_ref[...]), v2_ref[...]))
        vv = vv + (vfirst_in_ref[...].reshape(r, BRANCH) - vv) * vmix
    kk_raw = k_raw * vec(V_KK)
    kk = kk_raw / jnp.maximum(jnp.sqrt(head_sum(kk_raw * kk_raw, two_pass=True)), 1e-12)
    kmod = k_raw * (1.0 + (a - 1.0) * vec(V_KA))
    b_w = _chunk_cumsum(log_w, c)
    inv_p = jnp.exp(-b_w)
    p_w = jnp.exp(b_w)
    for ref, arr in zip(rw, (kk * jnp.exp(b_w - log_w), rr * p_w, kmod * inv_p, kk * a * inv_p, vv, p_w)):
        ref[...] = arr

    n_c = tc // c
    mix = ((qkvb[0], o_gl[0], st_hgrn, bo_ref, HEAD_DIM), (qkvb[1], o_gl[1], st_gla, bog_ref, GLA_KEY_DIM))

    def chunk_body(ci, _):
        rows = [pl.ds(pl.multiple_of(b * tc + ci * c, c), c) for b in seqs]

        def all_scores(tiles, bo_m):
            sc = _dot(jnp.concatenate(tiles, axis=0), bo_m)
            n = len(tiles) // bb * SUBLANES
            return [sc[b * n:(b + 1) * n, :] for b in seqs]

        gl = []
        for (q_s, k_s, v_s, b_s), o_s, st_ref, bo_m, dk in mix:
            ins = [(q_s[rows[b], :], k_s[rows[b], :], v_s[rows[b], :], b_s[rows[b], :]) for b in seqs]
            tiles = []
            for q, k, _, b_c in ins:
                t, where_ = _gl_tiles(q, k, b_c, c)
                tiles += t
            gl.append((ins, all_scores(tiles, bo_m[...]), where_, o_s, st_ref, dk))
        rw_in = [tuple(x[rows[b], :] for x in rw) for b in seqs]
        tiles = []
        for kkt, _, _, ahat, _, _ in rw_in:
            t, rw_where = _rwkv_tiles(kkt, ahat, c)
            tiles += t
        rw_sc = all_scores(tiles, bo)
        rw_lhs = [_head_stack(jnp.concatenate([kkt, rt], axis=0)) for kkt, rt, _, _, _, _ in rw_in]
        rw_cmp = [_rwkv_compact(rw_lhs[b], rw_in[b][2], rw_in[b][3]) for b in seqs]
        gl_os = [[_state_read(_head_stack(q * jnp.exp(b_c), dk), st_ref[b], c)
                  for b, (q, _, _, b_c) in enumerate(ins)] for ins, _, _, _, st_ref, dk in gl]
        rw_fs = [_state_read(rw_lhs[b], st_rw[b], 2 * c) for b in seqs]
        rw_m = [_rwkv_masks(rw_cmp[b], c) for b in seqs]
        rw_fv = [_head_unstack(_dot(rw_m[b][0], rw_in[b][4]), 2 * c) for b in seqs]
        for ins, _, _, _, st_ref, _ in gl:
            for b, (_, k, v, b_c) in enumerate(ins):
                st_ref[b] = _gl_state(k, v, b_c, st_ref[b], c)
        for (ins, sc, where_, o_s, _, _), o_state in zip(gl, gl_os):
            for b in seqs:
                o_s[rows[b], :] = _gl_output(sc[b], o_state[b], ins[b][2], where_, c)
        rhs = [rw_fs[b] + rw_fv[b] for b in seqs]
        us = [_rwkv_solve(rw_sc[b], rhs[b][0:c, :], rw_where, c) for b in seqs]
        rw_ou = [_dot(rw_m[b][1], us[b]) for b in seqs]
        for b in seqs:
            _, _, khat, ahat, v_w, p_c = rw_in[b]
            st_rw[b] = _rwkv_state(us[b], v_w, khat, ahat, p_c[c - 1:c, :], st_rw[b])
        for b in seqs:
            orw_ref[rows[b], :] = rhs[b][c:2 * c, :] - _head_unstack(rw_ou[b], c)
        return 0

    lax.fori_loop(0, n_c, chunk_body, 0)

    def group_norm(o, eps):
        cen = o - head_mean(o, two_pass=True)
        return cen * lax.rsqrt(head_mean(cen * cen) + eps)

    def head_rms(o, g):
        return o * lax.rsqrt(head_mean(o * o) + RMS_EPS) * g

    def put(lo, val):
        ocat_ref[:, :, lo:lo + BRANCH] = val.reshape(bb, tc, BRANCH)

    put(0, group_norm(o_ret, RET_GN_EPS) * _silu(blk(RET_OFF + 768, BRANCH)))
    put(256, head_rms(o_gl[0][...], vec(V_HGRN_G)) * _silu(blk(HGRN_OFF + 768, BRANCH)))
    put(512, head_rms(o_gl[1][...], vec(V_GLA_G)) * _silu(blk(GLA_OFF + 512, BRANCH)))
    o_rw = group_norm(orw_ref[...], RWKV_GN_EPS) * vec(V_GN_G) + vec(V_GN_B)
    bonus = head_sum(rr * kmod * vec(V_RK))
    put(768, (o_rw + bonus * vv) * _silu(ps_ref[:, 768:1024]))

    @pl.when(ti == n_t - 1)
    def _final():
        for b in range(bb):
            for st_ref, _, out_ref in key_value_io:
                out_ref[b] = st_ref[b].T.reshape(out_ref.shape[1:])
            for h in range(N_HEADS):
                o_rw_st[b, h] = st_rw[b][:, h * HEAD_DIM:(h + 1) * HEAD_DIM]


def _mixers(proj, vfirst, cos, sin, states, new_states, layer, lp, consts, *, bb, tc, c):
    bsz, seq, _ = proj.shape
    layer0 = vfirst is None
    n_t = seq // tc
    r = bb * tc

    def full(arr):
        nd = arr.ndim
        return pl.BlockSpec(arr.shape, lambda bi, ti, _nd=nd: (0,) * _nd)

    def per_b(arr):
        nd = arr.ndim
        return pl.BlockSpec((None, bb) + arr.shape[2:], lambda bi, ti, _nd=nd: (layer, bi) + (0,) * (_nd - 2))

    def tok(width):
        return pl.BlockSpec((bb, tc, width), lambda bi, ti: (bi, ti, 0))

    ins, specs = [proj], [tok(D_IN_PAD)]
    if not layer0:
        ins.append(vfirst)
        specs.append(tok(BRANCH))
    ins += [cos, sin]
    specs += [pl.BlockSpec((tc, BRANCH), lambda bi, ti: (ti, 0))] * 2
    for s in states:
        ins.append(s)
        specs.append(per_b(s))
    new_state_inputs = list(range(len(ins), len(ins) + len(new_states)))
    for s in new_states:
        ins.append(s)
        specs.append(pl.BlockSpec(memory_space=pl.ANY))
    small = [lp['vec'], lp['mu'], lp['bgate'], lp['wg'], lp['w2'], lp['a2']]
    if not layer0:
        small += [lp['v1'], lp['v2']]
    small += list(consts) + list(_retention_tables(tc))
    for s in small:
        ins.append(s)
        specs.append(full(s))

    out_shape = [jax.ShapeDtypeStruct((bsz, seq, D_MODEL), F32)]
    out_specs = [tok(D_MODEL)]
    if layer0:
        out_shape.append(jax.ShapeDtypeStruct((bsz, seq, BRANCH), F32))
        out_specs.append(tok(BRANCH))
    aliases = {}
    for idx, s in zip(new_state_inputs, new_states):
        aliases[idx] = len(out_shape)
        out_shape.append(jax.ShapeDtypeStruct(s.shape, F32))
        out_specs.append(per_b(s))

    scratch = [pltpu.VMEM((bb, HEAD_DIM, BRANCH), F32), pltpu.VMEM((bb, HEAD_DIM, BRANCH), F32),
               pltpu.VMEM((bb, HEAD_DIM, GLA_KEY_WIDTH), F32), pltpu.VMEM((bb, HEAD_DIM, BRANCH), F32),
               pltpu.VMEM((bb, RWKV_PW), F32)]
    for dkt in (BRANCH, GLA_KEY_WIDTH):
        scratch += [pltpu.VMEM((r, dkt), F32), pltpu.VMEM((r, dkt), F32), pltpu.VMEM((r, BRANCH), F32),
                    pltpu.VMEM((r, dkt), F32)]
    scratch += [pltpu.VMEM((r, BRANCH), F32)] * 2
    scratch += [pltpu.VMEM((r, RWKV_PW), F32), pltpu.VMEM((r, BRANCH), F32)]
    scratch += [pltpu.VMEM((r, BRANCH), F32)] * 6

    outs = pl.pallas_call(
        functools.partial(_mixers_kernel, bb=bb, tc=tc, c=c, layer0=layer0, n_t=n_t),
        grid=(bsz // bb, n_t),
        in_specs=specs,
        out_specs=out_specs,
        out_shape=out_shape,
        scratch_shapes=scratch,
        input_output_aliases=aliases,
        compiler_params=pltpu.CompilerParams(dimension_semantics=("arbitrary", "arbitrary"),
                                             vmem_limit_bytes=V7X_VMEM_LIMIT),
        name="mixers",
    )(*ins)
    if layer0:
        return outs[0], outs[1], list(outs[2:])
    return outs[0], None, list(outs[1:])


def _out_attn_kernel(x_ref, oc_ref, mk_ref, mv_ref, wout_ref, wq_ref, wo_ref, gx_ref, gf_ref, out_ref,
                     *, bbc, lc, final):
    r = bbc * lc
    x1 = x_ref[...].reshape(r, D_MODEL) + _dot(oc_ref[...].reshape(r, D_MODEL), wout_ref[...])
    q = _dot(_rms(x1, gx_ref[...]), wq_ref[...])
    seqs = range(bbc)
    scores = [_dot_nt(_head_stack(q[b * lc:(b + 1) * lc, :]), mk_ref[b]) * (HEAD_DIM ** -0.5) for b in seqs]
    probs = []
    for s in scores:
        e = jnp.exp(s - jnp.max(s, axis=-1, keepdims=True))
        probs.append(e / jnp.sum(e, axis=-1, keepdims=True))
    o4 = [_dot(probs[b], mv_ref[b]) for b in seqs]
    outs = [_head_unstack(o4[b], lc) for b in seqs]
    o = outs[0] if bbc == 1 else jnp.concatenate(outs, axis=0)
    x2 = x1 + _dot(o, wo_ref[...])
    if final:
        x2 = _rms(x2, gf_ref[...])
    out_ref[...] = x2.reshape(bbc, lc, D_MODEL)


def _out_attn(x, ocat, mem, lp, norm_f, *, bbc, lc, final):
    bsz, seq, _ = x.shape
    mk, mv, mk_spec, mv_spec = mem

    def tok(width):
        return pl.BlockSpec((bbc, lc, width), lambda bi, li: (bi, li, 0))

    def full(arr):
        nd = arr.ndim
        return pl.BlockSpec(arr.shape, lambda bi, li, _nd=nd: (0,) * _nd)

    small = [lp['w_out'], lp['wq'], lp['wo'], lp['norm_x'], norm_f]
    return pl.pallas_call(
        functools.partial(_out_attn_kernel, bbc=bbc, lc=lc, final=final),
        grid=(bsz // bbc, seq // lc),
        in_specs=[tok(D_MODEL), tok(D_MODEL), mk_spec, mv_spec] + [full(s) for s in small],
        out_specs=tok(D_MODEL),
        out_shape=jax.ShapeDtypeStruct((bsz, seq, D_MODEL), F32),
        compiler_params=pltpu.CompilerParams(dimension_semantics=("arbitrary", "arbitrary"),
                                             vmem_limit_bytes=V7X_VMEM_LIMIT),
        name="out_attn",
    )(x, ocat, mk, mv, *small)


def _rope_tables(pos0, seq):
    half = HEAD_DIM // 2
    inv = ROPE_BASE ** (-jnp.arange(half, dtype=F32) / half)
    pos = (pos0 + jnp.arange(seq, dtype=jnp.int32)).astype(F32)
    ang = pos[:, None] * inv[None, :]
    cos, sin = lax.optimization_barrier((jnp.cos(ang), jnp.sin(ang)))
    return (jnp.tile(jnp.concatenate([cos, cos], axis=-1), (1, N_HEADS)),
            jnp.tile(jnp.concatenate([-sin, sin], axis=-1), (1, N_HEADS)))


def _retention_tables(n):
    log_gamma = jnp.log1p(-jnp.exp2(-5.0 - jnp.arange(N_HEADS, dtype=F32)))
    j = jnp.arange(n, dtype=F32)
    diff = j[:, None] - j[None, :]
    dmat = jnp.where(diff[None] >= 0, jnp.exp(diff[None] * log_gamma[:, None, None]), 0.0)
    lg_lanes = jnp.repeat(log_gamma, HEAD_DIM)[None, :]
    e_q = jnp.exp((j[:, None] + 1.0) * lg_lanes)
    e_k = jnp.exp((n - 1.0 - j[:, None]) * lg_lanes)
    g_blk = jnp.exp(float(n) * lg_lanes)
    return dmat.reshape(N_HEADS * n, n), e_q, e_k, g_blk


def _rwkv_to_padded(t):
    pad = jnp.zeros(t.shape[:-1] + (RWKV_PW - RWKV_W,), t.dtype)
    return jnp.concatenate([t[..., 0:256], t[..., 288:544], t[..., 544:800], t[..., 832:1088],
                            t[..., 256:288], t[..., 800:832], pad], axis=-1)


def _rwkv_from_padded(t):
    return jnp.concatenate([t[..., 0:256], t[..., 1024:1056], t[..., 256:512], t[..., 512:768],
                            t[..., 1056:1088], t[..., 768:1024]], axis=-1)


def _pad_rows(m, rows, at=0):
    out = jnp.zeros((rows, m.shape[1]), m.dtype)
    return out.at[at:at + m.shape[0]].set(m)


def _layer_params(l, P):
    sm = jax.nn.softmax(P['hgrn_lb_logits'].astype(F32), axis=0)
    lb = (jnp.cumsum(sm, axis=0) - sm[0])[l]
    rows = [None] * N_VEC
    rows[V_LB_A] = jnp.log(jnp.maximum(lb, LB_FLOOR))
    rows[V_LB_B] = jnp.log1p(-lb)
    rows[V_LB_OM] = 1.0 - lb
    rows[V_HGRN_G] = jnp.tile(P['hgrn_norm'][l], N_HEADS)
    rows[V_GLA_G] = jnp.tile(P['gla_norm'][l], N_HEADS)
    rows[V_W0] = P['rwkv_w0'][l]
    rows[V_A0] = P['rwkv_a0'][l]
    rows[V_V0] = P['rwkv_v0'][l - 1] if l > 0 else jnp.zeros((BRANCH,), F32)
    rows[V_KK] = P['rwkv_k_k'][l]
    rows[V_KA] = P['rwkv_k_a'][l]
    rows[V_RK] = P['rwkv_r_k'][l].reshape(BRANCH)
    rows[V_GN_G] = P['rwkv_gn_g'][l]
    rows[V_GN_B] = P['rwkv_gn_b'][l]
    zero = jnp.zeros((BRANCH,), F32)
    vec = jnp.stack([zero if x is None else x.astype(F32) for x in rows], axis=0)

    lp = {
        'norm_mix': P['norm_mix'][l][None, :],
        'vec': vec,
        'mu': _rwkv_to_padded(P['rwkv_mu'][l])[None, :],
        'bgate': P['gla_b_gate'][l][None, :],
        'wg': _pad_rows(P['gla_w_gate2'][l], 128).astype(BF16),
        'w2': _pad_rows(P['rwkv_w2'][l], 128, 0).astype(BF16),
        'a2': _pad_rows(P['rwkv_a2'][l], 128, 32).astype(BF16),
        'w_out': P['w_out'][l].astype(BF16),
        'wq': P['wq_x'][l].astype(BF16),
        'wo': P['wo_x'][l].astype(BF16),
        'norm_x': P['norm_x'][l][None, :],
    }
    if l > 0:
        lp['v1'] = jnp.pad(P['rwkv_v1'][l - 1], ((0, 0), (0, 96))).astype(BF16)
        lp['v2'] = _pad_rows(P['rwkv_v2'][l - 1], 128).astype(BF16)
    return lp


def _consts():
    i256 = jnp.arange(BRANCH) // HEAD_DIM
    i128 = jnp.arange(GLA_KEY_WIDTH) // GLA_KEY_DIM
    bo = (i256[:, None] == i256[None, :])
    bog = (i128[:, None] == i256[None, :])
    lane = jnp.arange(BRANCH)
    partner = jnp.where(lane % HEAD_DIM < HEAD_DIM // 2, lane + HEAD_DIM // 2, lane - HEAD_DIM // 2)
    swap = (lane[:, None] == partner[None, :])
    return bo.astype(BF16), bog.astype(BF16), swap.astype(BF16)


def _run_trunk(x, pos0, mems, init, lps, w_in_all, norm_f, consts, *, bb, tc, c, bbc, lc, tm):
    bsz, seq, _ = x.shape
    cos, sin = _rope_tables(pos0, seq)
    new_shift = []
    states = [init['ret'], init['hgrn'], init['gla'], init['rwkv'], _rwkv_to_padded(init['shift'])]
    new_states = [jnp.zeros(s.shape, F32) for s in states[:4]]
    vfirst = None
    for l in range(DEPTH):
        lp = lps[l]
        proj = _in_proj(x.reshape(bsz * seq, D_MODEL), lp['norm_mix'], w_in_all, tm, layer=l)
        proj = proj.reshape(bsz, seq, D_IN_PAD)
        ocat, vf, new_states = _mixers(proj, vfirst, cos, sin, states, new_states, l, lp, consts,
                                       bb=bb, tc=tc, c=c)
        if l == 0:
            vfirst = vf
        new_shift.append(_rwkv_from_padded(proj[:, seq - 1, RWKV_OFF:RWKV_OFF + RWKV_PW]))
        x = _out_attn(x, ocat, mems[l], lp, norm_f, bbc=bbc, lc=lc, final=(l == DEPTH - 1))
    return x, {'ret': new_states[0], 'hgrn': new_states[1], 'gla': new_states[2], 'rwkv': new_states[3],
               'shift': jnp.stack(new_shift, axis=0)}


def kernel(x_prompt, x_sample, mem_prompt, state_ret, state_hgrn, state_gla, state_rwkv, state_rwkv_shift,
           cache_mem_k, cache_mem_v, norm_mix, w_in, hgrn_lb_logits, hgrn_norm, gla_w_gate2, gla_b_gate,
           gla_norm, rwkv_mu, rwkv_w0, rwkv_w2, rwkv_a0, rwkv_a2, rwkv_v0, rwkv_v1, rwkv_v2, rwkv_k_k,
           rwkv_k_a, rwkv_r_k, rwkv_gn_g, rwkv_gn_b, w_out, norm_x, wq_x, wo_x, norm_mem, wk_x, wv_x, norm_f):
    P = {'norm_mix': norm_mix, 'hgrn_lb_logits': hgrn_lb_logits, 'hgrn_norm': hgrn_norm,
         'gla_w_gate2': gla_w_gate2, 'gla_b_gate': gla_b_gate, 'gla_norm': gla_norm,
         'rwkv_mu': rwkv_mu, 'rwkv_w0': rwkv_w0, 'rwkv_w2': rwkv_w2, 'rwkv_a0': rwkv_a0,
         'rwkv_a2': rwkv_a2, 'rwkv_v0': rwkv_v0, 'rwkv_v1': rwkv_v1, 'rwkv_v2': rwkv_v2,
         'rwkv_k_k': rwkv_k_k, 'rwkv_k_a': rwkv_k_a, 'rwkv_r_k': rwkv_r_k,
         'rwkv_gn_g': rwkv_gn_g, 'rwkv_gn_b': rwkv_gn_b, 'w_out': w_out,
         'norm_x': norm_x, 'wq_x': wq_x, 'wo_x': wo_x}
    lps = [_layer_params(l, P) for l in range(DEPTH)]
    w_in_all = _w_in_layout(w_in)
    consts = _consts()
    norm_f2 = norm_f[None, :]
    bsz, seq, _ = x_prompt.shape
    dbsz, dseq, _ = x_sample.shape
    p_bbc, s_bbc = 2, 16

    mem2d = mem_prompt.reshape(bsz * N_MEM, D_MODEL)
    kv_l, p_mems = [], []
    for l in range(DEPTH):
        wkv = jnp.concatenate([wk_x[l], wv_x[l]], axis=1).astype(BF16)
        kv = _in_proj(mem2d, norm_mem[l][None, :], wkv, 256).reshape(bsz, N_MEM, 2 * BRANCH)
        kv_l.append(kv)
        p_mems.append((kv, kv, pl.BlockSpec((p_bbc, N_MEM, BRANCH), lambda bi, li: (bi, 0, 0)),
                       pl.BlockSpec((p_bbc, N_MEM, BRANCH), lambda bi, li: (bi, 0, 1))))
    zero_init = {
        'ret': jnp.zeros((DEPTH, bsz, N_HEADS, HEAD_DIM, HEAD_DIM), F32),
        'hgrn': jnp.zeros((DEPTH, bsz, N_HEADS, HEAD_DIM, HEAD_DIM), F32),
        'gla': jnp.zeros((DEPTH, bsz, N_HEADS, GLA_KEY_DIM, HEAD_DIM), F32),
        'rwkv': jnp.zeros((DEPTH, bsz, N_HEADS, HEAD_DIM, HEAD_DIM), F32),
        'shift': jnp.zeros((DEPTH, bsz, RWKV_W), F32),
    }
    y_prompt, sp = _run_trunk(x_prompt, 0, p_mems, zero_init, lps, w_in_all, norm_f2, consts,
                              bb=8, tc=64, c=CHUNK, bbc=p_bbc, lc=512, tm=512)

    cached = {'ret': state_ret, 'hgrn': state_hgrn, 'gla': state_gla, 'rwkv': state_rwkv,
              'shift': state_rwkv_shift}
    cmk = cache_mem_k.reshape(DEPTH, dbsz, N_MEM, BRANCH)
    cmv = cache_mem_v.reshape(DEPTH, dbsz, N_MEM, BRANCH)
    s_mems = []
    for l in range(DEPTH):
        spec = pl.BlockSpec((None, s_bbc, N_MEM, BRANCH), lambda bi, li, _l=l: (_l, bi, 0, 0))
        s_mems.append((cmk, cmv, spec, spec))
    y_sample, ss = _run_trunk(x_sample, PAST_LEN, s_mems, cached, lps, w_in_all, norm_f2, consts,
                              bb=8, tc=dseq, c=dseq, bbc=s_bbc, lc=dseq, tm=256)

    kv_all = jnp.stack(kv_l, axis=0)
    mem_k_p = kv_all[..., :BRANCH].reshape(DEPTH, bsz, N_MEM, N_HEADS, HEAD_DIM)
    mem_v_p = kv_all[..., BRANCH:].reshape(DEPTH, bsz, N_MEM, N_HEADS, HEAD_DIM)
    return (y_prompt, y_sample, sp['ret'], ss['ret'], sp['hgrn'], ss['hgrn'], sp['gla'], ss['gla'],
            sp['rwkv'], ss['rwkv'], sp['shift'], ss['shift'], mem_k_p, mem_v_p)
```

```python
import functools

import jax
import jax.numpy as jnp
from jax import lax
from jax.experimental import pallas as pl
from jax.experimental.pallas import tpu as pltpu

F32 = jnp.float32
BF16 = jnp.bfloat16

D_MODEL = 1024
DEPTH = 2
PAST_LEN = 16384
N_HEADS = 4
HEAD_DIM = 64
BRANCH = N_HEADS * HEAD_DIM
GLA_KEY_DIM = 32
GLA_KEY_WIDTH = N_HEADS * GLA_KEY_DIM
GLA_TAU = 16.0
N_MEM = 256
ROPE_BASE = 10000.0
RMS_EPS = 1e-6
RET_GN_EPS = 1e-5
RWKV_GN_EPS = 64e-5
LB_FLOOR = 1e-20
RWKV_W = 1088
CHUNK = 16

RET_OFF = 0
HGRN_OFF = 1024
GLA_OFF = 2048
RWKV_OFF = 2944
RWKV_PW = 1152
D_IN_PAD = 4096

V7X_VMEM_LIMIT = 60 * 1024 * 1024
SUBLANES = 8
NEG_BIG = -1e30

(V_LB_A, V_LB_B, V_LB_OM, V_HGRN_G, V_GLA_G, V_W0, V_A0, V_V0, V_KK, V_KA, V_RK,
 V_GN_G, V_GN_B) = range(13)
N_VEC = 16


def _rms(x, g):
    return x * lax.rsqrt(jnp.mean(x * x, axis=-1, keepdims=True) + RMS_EPS) * g


def _sigmoid(x):
    return 0.5 * jnp.tanh(0.5 * x) + 0.5


def _silu(x):
    h = 0.5 * x
    return h * jnp.tanh(h) + h


def _log1p_exp(x):
    return jnp.log(1.0 + jnp.exp(x))


def _log_sigmoid(x):
    return jnp.minimum(x, 0.0) - _log1p_exp(-jnp.abs(x))


def _dot(a, b):
    return jnp.dot(a.astype(BF16), b.astype(BF16), preferred_element_type=F32)


def _dot_nt(a, b):
    return lax.dot_general(a.astype(BF16), b.astype(BF16), (((1,), (1,)), ((), ())),
                           preferred_element_type=F32)


def _dot_tn(a, b):
    return lax.dot_general(a.astype(BF16), b.astype(BF16), (((0,), (0,)), ((), ())),
                           preferred_element_type=F32)


def _row_tiles(x, n):
    return [x[i * SUBLANES:(i + 1) * SUBLANES, :] for i in range(n)]


def _w_in_layout_kernel(w_ref, o_ref):
    w = w_ref[...]
    zeros = lambda n: jnp.zeros((w.shape[0], n), w.dtype)
    o_ref[...] = jnp.concatenate([
        w[:, 0:2560],
        w[:, 2576:2832], w[:, 2560:2576], zeros(112),
        w[:, 2832:3088], w[:, 3120:3376], w[:, 3376:3632], w[:, 3664:3920],
        w[:, 3088:3120], w[:, 3632:3664], zeros(64)], axis=1).astype(BF16)


def _w_in_layout(w_in):
    depth, d, n = w_in.shape
    rows = 128
    return pl.pallas_call(
        _w_in_layout_kernel,
        grid=(depth, d // rows),
        in_specs=[pl.BlockSpec((None, rows, n), lambda l, i: (l, i, 0))],
        out_specs=pl.BlockSpec((None, rows, D_IN_PAD), lambda l, i: (l, i, 0)),
        out_shape=jax.ShapeDtypeStruct((depth, d, D_IN_PAD), BF16),
        compiler_params=pltpu.CompilerParams(dimension_semantics=("arbitrary", "arbitrary")),
        name="w_in_layout",
    )(w_in)


def _in_proj_kernel(x_ref, g_ref, w_ref, o_ref):
    o_ref[...] = _dot(_rms(x_ref[...], g_ref[...]), w_ref[...])


def _in_proj(x2d, gain, w_bf16, tm, layer=None):
    m, d = x2d.shape
    n = w_bf16.shape[-1]
    w_spec = (pl.BlockSpec((d, n), lambda i: (0, 0)) if layer is None
              else pl.BlockSpec((None, d, n), lambda i: (layer, 0, 0)))
    return pl.pallas_call(
        _in_proj_kernel,
        grid=(m // tm,),
        in_specs=[pl.BlockSpec((tm, d), lambda i: (i, 0)),
                  pl.BlockSpec((1, d), lambda i: (0, 0)),
                  w_spec],
        out_specs=pl.BlockSpec((tm, n), lambda i: (i, 0)),
        out_shape=jax.ShapeDtypeStruct((m, n), F32),
        compiler_params=pltpu.CompilerParams(dimension_semantics=("arbitrary",),
                                             vmem_limit_bytes=V7X_VMEM_LIMIT),
        name="in_proj",
    )(x2d, gain, w_bf16)


def _chunk_cumsum(x, c):
    t_local = jnp.bitwise_and(lax.broadcasted_iota(jnp.int32, x.shape, 0), c - 1)
    s = 1
    while s < c:
        x = x + jnp.where(t_local >= s, pltpu.roll(x, s, 0), 0.0)
        s *= 2
    return x


def _head_stack(x, dk=HEAD_DIM):
    lane_head = jnp.right_shift(lax.broadcasted_iota(jnp.int32, x.shape, 1), dk.bit_length() - 1)
    return jnp.concatenate([jnp.where(lane_head == h, x, 0.0) for h in range(N_HEADS)], axis=0)


def _head_unstack(x4, n):
    lane_head = jnp.right_shift(lax.broadcasted_iota(jnp.int32, (n, BRANCH), 1), 6)
    out = jnp.where(lane_head == 0, x4[0:n, :], 0.0)
    for h in range(1, N_HEADS):
        out = out + jnp.where(lane_head == h, x4[h * n:(h + 1) * n, :], 0.0)
    return out


def _heads_to_rows(x):
    return jnp.concatenate([x[:, h * HEAD_DIM:(h + 1) * HEAD_DIM] for h in range(N_HEADS)], axis=0)


def _rows_to_heads(x4, n):
    return jnp.concatenate([x4[h * n:(h + 1) * n, :] for h in range(N_HEADS)], axis=1)


def _state_read(lhs_stacked, st, n):
    return _rows_to_heads(_dot_nt(lhs_stacked, st), n)


def _state_outer(values, keys_stacked):
    return _dot_tn(_heads_to_rows(values), keys_stacked)


def _gl_tiles(q, k, b, c):
    n_rt = c // SUBLANES
    rows = lax.broadcasted_iota(jnp.int32, (SUBLANES, q.shape[1]), 0)
    q_t, b_t = _row_tiles(q, n_rt), _row_tiles(b, n_rt)
    tiles, where_ = [], {}
    for m in range(c):
        for rt in range(m // SUBLANES, n_rt):
            arg = b_t[rt] - b[m:m + 1, :]
            if rt == m // SUBLANES:
                arg = jnp.where(rows >= m % SUBLANES, arg, NEG_BIG)
            where_[m, rt] = len(tiles) * SUBLANES
            tiles.append(q_t[rt] * k[m:m + 1, :] * jnp.exp(arg))
    return tiles, where_


def _gl_output(scores, o_state, v, where_, c):
    o = _row_tiles(o_state, c // SUBLANES)
    for (m, rt), off in where_.items():
        o[rt] = o[rt] + scores[off:off + SUBLANES, :] * v[m:m + 1, :]
    return jnp.concatenate(o, axis=0)


def _gl_state(k, v, b, st, c):
    b_last = b[c - 1:c, :]
    dk = k.shape[1] // N_HEADS
    return st * jnp.exp(b_last) + _state_outer(v, _head_stack(k * jnp.exp(b_last - b), dk))


def _rwkv_tiles(kkt, ahat, c):
    n_rt = c // SUBLANES
    rows = lax.broadcasted_iota(jnp.int32, (SUBLANES, BRANCH), 0)
    kk_t = _row_tiles(kkt, n_rt)
    tiles, where_ = [], {}
    for m in range(c - 1):
        for rt_i in range(m // SUBLANES, n_rt):
            on_diag = rt_i == m // SUBLANES
            if on_diag and m % SUBLANES == SUBLANES - 1:
                continue
            kk_m = jnp.where(rows > m % SUBLANES, kk_t[rt_i], 0.0) if on_diag else kk_t[rt_i]
            where_[m, rt_i] = len(tiles) * SUBLANES
            tiles.append(kk_m * ahat[m:m + 1, :])
    return tiles, where_


def _rwkv_compact(lhs_stacked, khat, ahat):
    return _dot_nt(lhs_stacked, jnp.concatenate([khat, ahat], axis=0))


def _rwkv_masks(sc, c):
    row = lax.broadcasted_iota(jnp.int32, (2 * N_HEADS * c, c), 0)
    col = lax.broadcasted_iota(jnp.int32, (2 * N_HEADS * c, c), 1)
    keep = col < jnp.bitwise_and(row, c - 1) + jnp.where(jnp.bitwise_and(row, c) != 0, 1, 0)
    r_a = jnp.concatenate([sc[(2 * h + 1) * c:(2 * h + 2) * c, c:2 * c] for h in range(N_HEADS)], axis=0)
    row4 = lax.broadcasted_iota(jnp.int32, (N_HEADS * c, c), 0)
    col4 = lax.broadcasted_iota(jnp.int32, (N_HEADS * c, c), 1)
    return jnp.where(keep, sc[:, 0:c], 0.0), jnp.where(col4 <= jnp.bitwise_and(row4, c - 1), r_a, 0.0)


def _rwkv_solve(sc_a, rhs, where_, c):
    n_rt = c // SUBLANES
    u = _row_tiles(rhs, n_rt)
    for s in range(c - 1):
        u_s = u[s // SUBLANES][s % SUBLANES:s % SUBLANES + 1, :]
        for rt_i in range(s // SUBLANES, n_rt):
            if (s, rt_i) in where_:
                off = where_[s, rt_i]
                u[rt_i] = u[rt_i] - sc_a[off:off + SUBLANES, :] * u_s
    return jnp.concatenate(u, axis=0)


def _rwkv_state(u, v, khat, ahat, p_last, st):
    return st * p_last + _state_outer(jnp.concatenate([v, -u], axis=0),
                                      _head_stack(jnp.concatenate([khat * p_last, ahat * p_last], axis=0)))


def _mixers_kernel(*refs, bb, tc, c, layer0, n_t):
    it = iter(refs)
    proj_ref = next(it)
    vfirst_in_ref = None if layer0 else next(it)
    cos_ref, sin_ref = next(it), next(it)
    s_ret0, s_hgrn0, s_gla0, s_rw0, shift0 = next(it), next(it), next(it), next(it), next(it)
    vec_ref, mu_ref, bgate_ref = next(it), next(it), next(it)
    wg_ref, w2_ref, a2_ref = next(it), next(it), next(it)
    v1_ref, v2_ref = (None, None) if layer0 else (next(it), next(it))
    bo_ref, bog_ref, swap_ref = next(it), next(it), next(it)
    dmat_ref, eq_ref, ek_ref, gblk_ref = next(it), next(it), next(it), next(it)
    ocat_ref = next(it)
    vfirst_out_ref = next(it) if layer0 else None
    o_ret_st, o_hgrn_st, o_gla_st, o_rw_st = next(it), next(it), next(it), next(it)
    st_ret, st_hgrn, st_gla, st_rw, carry = next(it), next(it), next(it), next(it), next(it)
    qkvb = [[next(it) for _ in range(4)] for _ in range(2)]
    o_gl = [next(it) for _ in range(2)]
    ps_ref, orw_ref = next(it), next(it)
    rw = [next(it) for _ in range(6)]

    r = bb * tc
    ti = pl.program_id(1)
    key_value_io = ((st_ret, s_ret0, o_ret_st), (st_hgrn, s_hgrn0, o_hgrn_st), (st_gla, s_gla0, o_gla_st))

    def vec(i):
        return vec_ref[i:i + 1, :]

    @pl.when(ti == 0)
    def _init():
        for b in range(bb):
            for st_ref, s0_ref, _ in key_value_io:
                st_ref[b] = s0_ref[b].reshape(st_ref.shape[2], HEAD_DIM).T
            for h in range(N_HEADS):
                st_rw[b, :, h * HEAD_DIM:(h + 1) * HEAD_DIM] = s_rw0[b, h]
        carry[...] = shift0[...]

    def blk(lo, width):
        return proj_ref[:, :, lo:lo + width].reshape(r, width)

    bo = bo_ref[...]
    bo2 = jnp.concatenate([bo, bo], axis=0)

    def head_sum(x, two_pass=False):
        if not two_pass:
            return _dot(x, bo)
        hi = x.astype(BF16)
        lo = (x - hi.astype(F32)).astype(BF16)
        return jnp.dot(jnp.concatenate([hi, lo], axis=1), bo2, preferred_element_type=F32)

    def head_mean(x, two_pass=False):
        return head_sum(x, two_pass) * (1.0 / HEAD_DIM)

    cos = jnp.concatenate([cos_ref[...]] * bb, axis=0)
    sin = jnp.concatenate([sin_ref[...]] * bb, axis=0)
    def rope(x):
        return x * cos + _dot(x, swap_ref[...]) * sin

    q_r = rope(blk(RET_OFF, BRANCH))
    k_r = rope(blk(RET_OFF + 256, BRANCH)) * (HEAD_DIM ** -0.5)
    v_r = blk(RET_OFF + 512, BRANCH)
    seqs = range(bb)
    blocks = [slice(b * tc, (b + 1) * tc) for b in seqs]
    ret_sc = [_dot_nt(_head_stack(q_r[rows, :]), k_r[rows, :]) * dmat_ref[...] for rows in blocks]
    ret_o4 = [_dot(ret_sc[b], v_r[blocks[b], :]) for b in seqs]
    ret_os = [_state_read(_head_stack(q_r[blocks[b], :] * eq_ref[...]), st_ret[b], tc) for b in seqs]
    for b in seqs:
        upd = _state_outer(v_r[blocks[b], :], _head_stack(k_r[blocks[b], :] * ek_ref[...]))
        st_ret[b] = st_ret[b] * gblk_ref[...] + upd
    o_ret = jnp.concatenate([ret_os[b] + _head_unstack(ret_o4[b], tc) for b in seqs], axis=0)

    q_h, k_h, v_h, b_h = qkvb[0]
    q_h[...] = _silu(blk(HGRN_OFF, BRANCH))
    ff = blk(HGRN_OFF + 256, BRANCH)
    y = vec(V_LB_B) + _log_sigmoid(ff)
    a_lb = jnp.broadcast_to(vec(V_LB_A), y.shape)
    mx = jnp.maximum(a_lb, y)
    log_f = mx + _log1p_exp(jnp.minimum(a_lb, y) - mx)
    k_h[...] = vec(V_LB_OM) * _sigmoid(-ff)
    v_h[...] = blk(HGRN_OFF + 512, BRANCH)
    b_h[...] = _chunk_cumsum(log_f, c)

    q_g, k_g, v_g, b_g = qkvb[1]
    q_g[...] = blk(GLA_OFF, GLA_KEY_WIDTH) * (GLA_KEY_DIM ** -0.5)
    k_g[...] = blk(GLA_OFF + 128, GLA_KEY_WIDTH)
    v_g[...] = blk(GLA_OFF + 256, BRANCH)
    z = _dot(blk(GLA_OFF + 768, 128), wg_ref[...]) + bgate_ref[...]
    b_g[...] = _chunk_cumsum(_log_sigmoid(z) / GLA_TAU, c)

    p_rw = blk(RWKV_OFF, RWKV_PW)
    mu = mu_ref[...]
    ps_ref[...] = p_rw + (pltpu.roll(p_rw, 1, 0) - p_rw) * mu
    for b in range(bb):
        p0 = p_rw[b * tc:b * tc + 1, :]
        ps_ref[b * tc:b * tc + 1, :] = p0 + (carry[b:b + 1, :] - p0) * mu
        carry[b:b + 1, :] = p_rw[(b + 1) * tc - 1:(b + 1) * tc, :]

    rr = ps_ref[:, 0:256]
    k_raw = ps_ref[:, 256:512]
    vv = ps_ref[:, 512:768]
    low_rank = ps_ref[:, 1024:1152]
    w_pre = vec(V_W0) + _dot(jnp.tanh(low_rank), w2_ref[...])
    w_log = -(jnp.maximum(-w_pre, 0.0) + _log1p_exp(-jnp.abs(w_pre))) - 0.5
    log_w = -jnp.exp(w_log)
    a = _sigmoid(vec(V_A0) + _dot(low_rank, a2_ref[...]))
    if layer0:
        vfirst_out_ref[...] = vv.reshape(bb, tc, BRANCH)
    else:
        vmix = _sigmoid(vec(V_V0) + _dot(_dot(vv, v1_ref[...]), v2_ref[...]))
        vv = vv + (vfirst_in_ref[...].reshape(r, BRANCH) - vv) * vmix
    kk_raw = k_raw * vec(V_KK)
    kk = kk_raw / jnp.maximum(jnp.sqrt(head_sum(kk_raw * kk_raw, two_pass=True)), 1e-12)
    kmod = k_raw * (1.0 + (a - 1.0) * vec(V_KA))
    b_w = _chunk_cumsum(log_w, c)
    inv_p = jnp.exp(-b_w)
    p_w = jnp.exp(b_w)
    for ref, arr in zip(rw, (kk * jnp.exp(b_w - log_w), rr * p_w, kmod * inv_p, kk * a * inv_p, vv, p_w)):
        ref[...] = arr

    n_c = tc // c
    mix = ((qkvb[0], o_gl[0], st_hgrn, bo_ref, HEAD_DIM), (qkvb[1], o_gl[1], st_gla, bog_ref, GLA_KEY_DIM))

    def chunk_body(ci, _):
        rows = [pl.ds(pl.multiple_of(b * tc + ci * c, c), c) for b in seqs]

        def all_scores(tiles, bo_m):
            sc = _dot(jnp.concatenate(tiles, axis=0), bo_m)
            n = len(tiles) // bb * SUBLANES
            return [sc[b * n:(b + 1) * n, :] for b in seqs]

        gl = []
        for (q_s, k_s, v_s, b_s), o_s, st_ref, bo_m, dk in mix:
            ins = [(q_s[rows[b], :], k_s[rows[b], :], v_s[rows[b], :], b_s[rows[b], :]) for b in seqs]
            tiles = []
            for q, k, _, b_c in ins:
                t, where_ = _gl_tiles(q, k, b_c, c)
                tiles += t
            gl.append((ins, all_scores(tiles, bo_m[...]), where_, o_s, st_ref, dk))
        rw_in = [tuple(x[rows[b], :] for x in rw) for b in seqs]
        tiles = []
        for kkt, _, _, ahat, _, _ in rw_in:
            t, rw_where = _rwkv_tiles(kkt, ahat, c)
            tiles += t
        rw_sc = all_scores(tiles, bo)
        rw_lhs = [_head_stack(jnp.concatenate([kkt, rt], axis=0)) for kkt, rt, _, _, _, _ in rw_in]
        rw_cmp = [_rwkv_compact(rw_lhs[b], rw_in[b][2], rw_in[b][3]) for b in seqs]
        gl_os = [[_state_read(_head_stack(q * jnp.exp(b_c), dk), st_ref[b], c)
                  for b, (q, _, _, b_c) in enumerate(ins)] for ins, _, _, _, st_ref, dk in gl]
        rw_fs = [_state_read(rw_lhs[b], st_rw[b], 2 * c) for b in seqs]
        rw_m = [_rwkv_masks(rw_cmp[b], c) for b in seqs]
        rw_fv = [_head_unstack(_dot(rw_m[b][0], rw_in[b][4]), 2 * c) for b in seqs]
        for ins, _, _, _, st_ref, _ in gl:
            for b, (_, k, v, b_c) in enumerate(ins):
                st_ref[b] = _gl_state(k, v, b_c, st_ref[b], c)
        for (ins, sc, where_, o_s, _, _), o_state in zip(gl, gl_os):
            for b in seqs:
                o_s[rows[b], :] = _gl_output(sc[b], o_state[b], ins[b][2], where_, c)
        rhs = [rw_fs[b] + rw_fv[b] for b in seqs]
        us = [_rwkv_solve(rw_sc[b], rhs[b][0:c, :], rw_where, c) for b in seqs]
        rw_ou = [_dot(rw_m[b][1], us[b]) for b in seqs]
        for b in seqs:
            _, _, khat, ahat, v_w, p_c = rw_in[b]
            st_rw[b] = _rwkv_state(us[b], v_w, khat, ahat, p_c[c - 1:c, :], st_rw[b])
        for b in seqs:
            orw_ref[rows[b], :] = rhs[b][c:2 * c, :] - _head_unstack(rw_ou[b], c)
        return 0

    lax.fori_loop(0, n_c, chunk_body, 0)

    def group_norm(o, eps):
        cen = o - head_mean(o, two_pass=True)
        return cen * lax.rsqrt(head_mean(cen * cen) + eps)

    def head_rms(o, g):
        return o * lax.rsqrt(head_mean(o * o) + RMS_EPS) * g

    def put(lo, val):
        ocat_ref[:, :, lo:lo + BRANCH] = val.reshape(bb, tc, BRANCH)

    put(0, group_norm(o_ret, RET_GN_EPS) * _silu(blk(RET_OFF + 768, BRANCH)))
    put(256, head_rms(o_gl[0][...], vec(V_HGRN_G)) * _silu(blk(HGRN_OFF + 768, BRANCH)))
    put(512, head_rms(o_gl[1][...], vec(V_GLA_G)) * _silu(blk(GLA_OFF + 512, BRANCH)))
    o_rw = group_norm(orw_ref[...], RWKV_GN_EPS) * vec(V_GN_G) + vec(V_GN_B)
    bonus = head_sum(rr * kmod * vec(V_RK))
    put(768, (o_rw + bonus * vv) * _silu(ps_ref[:, 768:1024]))

    @pl.when(ti == n_t - 1)
    def _final():
        for b in range(bb):
            for st_ref, _, out_ref in key_value_io:
                out_ref[b] = st_ref[b].T.reshape(out_ref.shape[1:])
            for h in range(N_HEADS):
                o_rw_st[b, h] = st_rw[b][:, h * HEAD_DIM:(h + 1) * HEAD_DIM]


def _mixers(proj, vfirst, cos, sin, states, layer, lp, consts, *, bb, tc, c):
    bsz, seq, _ = proj.shape
    layer0 = vfirst is None
    n_t = seq // tc
    r = bb * tc

    def full(arr):
        nd = arr.ndim
        return pl.BlockSpec(arr.shape, lambda bi, ti, _nd=nd: (0,) * _nd)

    def per_b(arr):
        nd = arr.ndim
        return pl.BlockSpec((None, bb) + arr.shape[2:], lambda bi, ti, _nd=nd: (layer, bi) + (0,) * (_nd - 2))

    def tok(width):
        return pl.BlockSpec((bb, tc, width), lambda bi, ti: (bi, ti, 0))

    ins, specs = [proj], [tok(D_IN_PAD)]
    if not layer0:
        ins.append(vfirst)
        specs.append(tok(BRANCH))
    ins += [cos, sin]
    specs += [pl.BlockSpec((tc, BRANCH), lambda bi, ti: (ti, 0))] * 2
    state_inputs = list(range(len(ins), len(ins) + 4))
    for s in states:
        ins.append(s)
        specs.append(per_b(s))
    small = [lp['vec'], lp['mu'], lp['bgate'], lp['wg'], lp['w2'], lp['a2']]
    if not layer0:
        small += [lp['v1'], lp['v2']]
    small += list(consts) + list(_retention_tables(tc))
    for s in small:
        ins.append(s)
        specs.append(full(s))

    out_shape = [jax.ShapeDtypeStruct((bsz, seq, D_MODEL), F32)]
    out_specs = [tok(D_MODEL)]
    if layer0:
        out_shape.append(jax.ShapeDtypeStruct((bsz, seq, BRANCH), F32))
        out_specs.append(tok(BRANCH))
    aliases = {}
    for idx, s in zip(state_inputs, states[:4]):
        aliases[idx] = len(out_shape)
        out_shape.append(jax.ShapeDtypeStruct(s.shape, F32))
        out_specs.append(per_b(s))

    scratch = [pltpu.VMEM((bb, HEAD_DIM, BRANCH), F32), pltpu.VMEM((bb, HEAD_DIM, BRANCH), F32),
               pltpu.VMEM((bb, HEAD_DIM, GLA_KEY_WIDTH), F32), pltpu.VMEM((bb, HEAD_DIM, BRANCH), F32),
               pltpu.VMEM((bb, RWKV_PW), F32)]
    for dkt in (BRANCH, GLA_KEY_WIDTH):
        scratch += [pltpu.VMEM((r, dkt), F32), pltpu.VMEM((r, dkt), F32), pltpu.VMEM((r, BRANCH), F32),
                    pltpu.VMEM((r, dkt), F32)]
    scratch += [pltpu.VMEM((r, BRANCH), F32)] * 2
    scratch += [pltpu.VMEM((r, RWKV_PW), F32), pltpu.VMEM((r, BRANCH), F32)]
    scratch += [pltpu.VMEM((r, BRANCH), F32)] * 6

    outs = pl.pallas_call(
        functools.partial(_mixers_kernel, bb=bb, tc=tc, c=c, layer0=layer0, n_t=n_t),
        grid=(bsz // bb, n_t),
        in_specs=specs,
        out_specs=out_specs,
        out_shape=out_shape,
        scratch_shapes=scratch,
        input_output_aliases=aliases,
        compiler_params=pltpu.CompilerParams(dimension_semantics=("arbitrary", "arbitrary"),
                                             vmem_limit_bytes=V7X_VMEM_LIMIT),
        name="mixers",
    )(*ins)
    if layer0:
        return outs[0], outs[1], list(outs[2:])
    return outs[0], None, list(outs[1:])


def _out_attn_kernel(x_ref, oc_ref, mk_ref, mv_ref, wout_ref, wq_ref, wo_ref, gx_ref, gf_ref, out_ref,
                     *, bbc, lc, final):
    r = bbc * lc
    x1 = x_ref[...].reshape(r, D_MODEL) + _dot(oc_ref[...].reshape(r, D_MODEL), wout_ref[...])
    q = _dot(_rms(x1, gx_ref[...]), wq_ref[...])
    seqs = range(bbc)
    scores = [_dot_nt(_head_stack(q[b * lc:(b + 1) * lc, :]), mk_ref[b]) * (HEAD_DIM ** -0.5) for b in seqs]
    probs = []
    for s in scores:
        e = jnp.exp(s - jnp.max(s, axis=-1, keepdims=True))
        probs.append(e / jnp.sum(e, axis=-1, keepdims=True))
    o4 = [_dot(probs[b], mv_ref[b]) for b in seqs]
    outs = [_head_unstack(o4[b], lc) for b in seqs]
    o = outs[0] if bbc == 1 else jnp.concatenate(outs, axis=0)
    x2 = x1 + _dot(o, wo_ref[...])
    if final:
        x2 = _rms(x2, gf_ref[...])
    out_ref[...] = x2.reshape(bbc, lc, D_MODEL)


def _out_attn(x, ocat, mem, lp, norm_f, *, bbc, lc, final):
    bsz, seq, _ = x.shape
    mk, mv, mk_spec, mv_spec = mem

    def tok(width):
        return pl.BlockSpec((bbc, lc, width), lambda bi, li: (bi, li, 0))

    def full(arr):
        nd = arr.ndim
        return pl.BlockSpec(arr.shape, lambda bi, li, _nd=nd: (0,) * _nd)

    small = [lp['w_out'], lp['wq'], lp['wo'], lp['norm_x'], norm_f]
    return pl.pallas_call(
        functools.partial(_out_attn_kernel, bbc=bbc, lc=lc, final=final),
        grid=(bsz // bbc, seq // lc),
        in_specs=[tok(D_MODEL), tok(D_MODEL), mk_spec, mv_spec] + [full(s) for s in small],
        out_specs=tok(D_MODEL),
        out_shape=jax.ShapeDtypeStruct((bsz, seq, D_MODEL), F32),
        compiler_params=pltpu.CompilerParams(dimension_semantics=("arbitrary", "arbitrary"),
                                             vmem_limit_bytes=V7X_VMEM_LIMIT),
        name="out_attn",
    )(x, ocat, mk, mv, *small)


def _rope_tables(pos0, seq):
    half = HEAD_DIM // 2
    inv = ROPE_BASE ** (-jnp.arange(half, dtype=F32) / half)
    pos = (pos0 + jnp.arange(seq, dtype=jnp.int32)).astype(F32)
    ang = pos[:, None] * inv[None, :]
    cos, sin = lax.optimization_barrier((jnp.cos(ang), jnp.sin(ang)))
    return (jnp.tile(jnp.concatenate([cos, cos], axis=-1), (1, N_HEADS)),
            jnp.tile(jnp.concatenate([-sin, sin], axis=-1), (1, N_HEADS)))


def _retention_tables(n):
    log_gamma = jnp.log1p(-jnp.exp2(-5.0 - jnp.arange(N_HEADS, dtype=F32)))
    j = jnp.arange(n, dtype=F32)
    diff = j[:, None] - j[None, :]
    dmat = jnp.where(diff[None] >= 0, jnp.exp(diff[None] * log_gamma[:, None, None]), 0.0)
    lg_lanes = jnp.repeat(log_gamma, HEAD_DIM)[None, :]
    e_q = jnp.exp((j[:, None] + 1.0) * lg_lanes)
    e_k = jnp.exp((n - 1.0 - j[:, None]) * lg_lanes)
    g_blk = jnp.exp(float(n) * lg_lanes)
    return dmat.reshape(N_HEADS * n, n), e_q, e_k, g_blk


def _rwkv_to_padded(t):
    pad = jnp.zeros(t.shape[:-1] + (RWKV_PW - RWKV_W,), t.dtype)
    return jnp.concatenate([t[..., 0:256], t[..., 288:544], t[..., 544:800], t[..., 832:1088],
                            t[..., 256:288], t[..., 800:832], pad], axis=-1)


def _rwkv_from_padded(t):
    return jnp.concatenate([t[..., 0:256], t[..., 1024:1056], t[..., 256:512], t[..., 512:768],
                            t[..., 1056:1088], t[..., 768:1024]], axis=-1)


def _pad_rows(m, rows, at=0):
    out = jnp.zeros((rows, m.shape[1]), m.dtype)
    return out.at[at:at + m.shape[0]].set(m)


def _layer_params(l, P):
    sm = jax.nn.softmax(P['hgrn_lb_logits'].astype(F32), axis=0)
    lb = (jnp.cumsum(sm, axis=0) - sm[0])[l]
    rows = [None] * N_VEC
    rows[V_LB_A] = jnp.log(jnp.maximum(lb, LB_FLOOR))
    rows[V_LB_B] = jnp.log1p(-lb)
    rows[V_LB_OM] = 1.0 - lb
    rows[V_HGRN_G] = jnp.tile(P['hgrn_norm'][l], N_HEADS)
    rows[V_GLA_G] = jnp.tile(P['gla_norm'][l], N_HEADS)
    rows[V_W0] = P['rwkv_w0'][l]
    rows[V_A0] = P['rwkv_a0'][l]
    rows[V_V0] = P['rwkv_v0'][l - 1] if l > 0 else jnp.zeros((BRANCH,), F32)
    rows[V_KK] = P['rwkv_k_k'][l]
    rows[V_KA] = P['rwkv_k_a'][l]
    rows[V_RK] = P['rwkv_r_k'][l].reshape(BRANCH)
    rows[V_GN_G] = P['rwkv_gn_g'][l]
    rows[V_GN_B] = P['rwkv_gn_b'][l]
    zero = jnp.zeros((BRANCH,), F32)
    vec = jnp.stack([zero if x is None else x.astype(F32) for x in rows], axis=0)

    lp = {
        'norm_mix': P['norm_mix'][l][None, :],
        'vec': vec,
        'mu': _rwkv_to_padded(P['rwkv_mu'][l])[None, :],
        'bgate': P['gla_b_gate'][l][None, :],
        'wg': _pad_rows(P['gla_w_gate2'][l], 128).astype(BF16),
        'w2': _pad_rows(P['rwkv_w2'][l], 128, 0).astype(BF16),
        'a2': _pad_rows(P['rwkv_a2'][l], 128, 32).astype(BF16),
        'w_out': P['w_out'][l].astype(BF16),
        'wq': P['wq_x'][l].astype(BF16),
        'wo': P['wo_x'][l].astype(BF16),
        'norm_x': P['norm_x'][l][None, :],
    }
    if l > 0:
        lp['v1'] = jnp.pad(P['rwkv_v1'][l - 1], ((0, 0), (0, 96))).astype(BF16)
        lp['v2'] = _pad_rows(P['rwkv_v2'][l - 1], 128).astype(BF16)
    return lp


def _consts():
    i256 = jnp.arange(BRANCH) // HEAD_DIM
    i128 = jnp.arange(GLA_KEY_WIDTH) // GLA_KEY_DIM
    bo = (i256[:, None] == i256[None, :])
    bog = (i128[:, None] == i256[None, :])
    lane = jnp.arange(BRANCH)
    partner = jnp.where(lane % HEAD_DIM < HEAD_DIM // 2, lane + HEAD_DIM // 2, lane - HEAD_DIM // 2)
    swap = (lane[:, None] == partner[None, :])
    return bo.astype(BF16), bog.astype(BF16), swap.astype(BF16)


def _run_trunk(x, pos0, mems, init, lps, w_in_all, norm_f, consts, *, bb, tc, c, bbc, lc, tm):
    bsz, seq, _ = x.shape
    cos, sin = _rope_tables(pos0, seq)
    new_shift = []
    states = [init['ret'], init['hgrn'], init['gla'], init['rwkv'], _rwkv_to_padded(init['shift'])]
    vfirst = None
    for l in range(DEPTH):
        lp = lps[l]
        proj = _in_proj(x.reshape(bsz * seq, D_MODEL), lp['norm_mix'], w_in_all, tm, layer=l)
        proj = proj.reshape(bsz, seq, D_IN_PAD)
        ocat, vf, updated = _mixers(proj, vfirst, cos, sin, states, l, lp, consts, bb=bb, tc=tc, c=c)
        states = updated + states[4:]
        if l == 0:
            vfirst = vf
        new_shift.append(_rwkv_from_padded(proj[:, seq - 1, RWKV_OFF:RWKV_OFF + RWKV_PW]))
        x = _out_attn(x, ocat, mems[l], lp, norm_f, bbc=bbc, lc=lc, final=(l == DEPTH - 1))
    return x, {'ret': states[0], 'hgrn': states[1], 'gla': states[2], 'rwkv': states[3],
               'shift': jnp.stack(new_shift, axis=0)}


def kernel(x_prompt, x_sample, mem_prompt, state_ret, state_hgrn, state_gla, state_rwkv, state_rwkv_shift,
           cache_mem_k, cache_mem_v, norm_mix, w_in, hgrn_lb_logits, hgrn_norm, gla_w_gate2, gla_b_gate,
           gla_norm, rwkv_mu, rwkv_w0, rwkv_w2, rwkv_a0, rwkv_a2, rwkv_v0, rwkv_v1, rwkv_v2, rwkv_k_k,
           rwkv_k_a, rwkv_r_k, rwkv_gn_g, rwkv_gn_b, w_out, norm_x, wq_x, wo_x, norm_mem, wk_x, wv_x, norm_f):
    P = {'norm_mix': norm_mix, 'hgrn_lb_logits': hgrn_lb_logits, 'hgrn_norm': hgrn_norm,
         'gla_w_gate2': gla_w_gate2, 'gla_b_gate': gla_b_gate, 'gla_norm': gla_norm,
         'rwkv_mu': rwkv_mu, 'rwkv_w0': rwkv_w0, 'rwkv_w2': rwkv_w2, 'rwkv_a0': rwkv_a0,
         'rwkv_a2': rwkv_a2, 'rwkv_v0': rwkv_v0, 'rwkv_v1': rwkv_v1, 'rwkv_v2': rwkv_v2,
         'rwkv_k_k': rwkv_k_k, 'rwkv_k_a': rwkv_k_a, 'rwkv_r_k': rwkv_r_k,
         'rwkv_gn_g': rwkv_gn_g, 'rwkv_gn_b': rwkv_gn_b, 'w_out': w_out,
         'norm_x': norm_x, 'wq_x': wq_x, 'wo_x': wo_x}
    lps = [_layer_params(l, P) for l in range(DEPTH)]
    w_in_all = _w_in_layout(w_in)
    consts = _consts()
    norm_f2 = norm_f[None, :]
    bsz, seq, _ = x_prompt.shape
    dbsz, dseq, _ = x_sample.shape
    p_bbc, s_bbc = 2, 16

    mem2d = mem_prompt.reshape(bsz * N_MEM, D_MODEL)
    kv_l, p_mems = [], []
    for l in range(DEPTH):
        wkv = jnp.concatenate([wk_x[l], wv_x[l]], axis=1).astype(BF16)
        kv = _in_proj(mem2d, norm_mem[l][None, :], wkv, 256).reshape(bsz, N_MEM, 2 * BRANCH)
        kv_l.append(kv)
        p_mems.append((kv, kv, pl.BlockSpec((p_bbc, N_MEM, BRANCH), lambda bi, li: (bi, 0, 0)),
                       pl.BlockSpec((p_bbc, N_MEM, BRANCH), lambda bi, li: (bi, 0, 1))))
    zero_init = {
        'ret': jnp.zeros((DEPTH, bsz, N_HEADS, HEAD_DIM, HEAD_DIM), F32),
        'hgrn': jnp.zeros((DEPTH, bsz, N_HEADS, HEAD_DIM, HEAD_DIM), F32),
        'gla': jnp.zeros((DEPTH, bsz, N_HEADS, GLA_KEY_DIM, HEAD_DIM), F32),
        'rwkv': jnp.zeros((DEPTH, bsz, N_HEADS, HEAD_DIM, HEAD_DIM), F32),
        'shift': jnp.zeros((DEPTH, bsz, RWKV_W), F32),
    }
    y_prompt, sp = _run_trunk(x_prompt, 0, p_mems, zero_init, lps, w_in_all, norm_f2, consts,
                              bb=8, tc=64, c=CHUNK, bbc=p_bbc, lc=512, tm=512)

    cached = {'ret': state_ret, 'hgrn': state_hgrn, 'gla': state_gla, 'rwkv': state_rwkv,
              'shift': state_rwkv_shift}
    cmk = cache_mem_k.reshape(DEPTH, dbsz, N_MEM, BRANCH)
    cmv = cache_mem_v.reshape(DEPTH, dbsz, N_MEM, BRANCH)
    s_mems = []
    for l in range(DEPTH):
        spec = pl.BlockSpec((None, s_bbc, N_MEM, BRANCH), lambda bi, li, _l=l: (_l, bi, 0, 0))
        s_mems.append((cmk, cmv, spec, spec))
    y_sample, ss = _run_trunk(x_sample, PAST_LEN, s_mems, cached, lps, w_in_all, norm_f2, consts,
                              bb=8, tc=dseq, c=dseq, bbc=s_bbc, lc=dseq, tm=256)

    kv_all = jnp.stack(kv_l, axis=0)
    mem_k_p = kv_all[..., :BRANCH].reshape(DEPTH, bsz, N_MEM, N_HEADS, HEAD_DIM)
    mem_v_p = kv_all[..., BRANCH:].reshape(DEPTH, bsz, N_MEM, N_HEADS, HEAD_DIM)
    return (y_prompt, y_sample, sp['ret'], ss['ret'], sp['hgrn'], ss['hgrn'], sp['gla'], ss['gla'],
            sp['rwkv'], ss['rwkv'], sp['shift'], ss['shift'], mem_k_p, mem_v_p)
```

```python
import functools

import jax
import jax.numpy as jnp
from jax import lax
from jax.experimental import pallas as pl
from jax.experimental.pallas import tpu as pltpu

F32 = jnp.float32
BF16 = jnp.bfloat16

D_MODEL = 1024
DEPTH = 2
PAST_LEN = 16384
N_HEADS = 4
HEAD_DIM = 64
BRANCH = N_HEADS * HEAD_DIM
GLA_KEY_DIM = 32
GLA_KEY_WIDTH = N_HEADS * GLA_KEY_DIM
GLA_TAU = 16.0
N_MEM = 256
ROPE_BASE = 10000.0
RMS_EPS = 1e-6
RET_GN_EPS = 1e-5
RWKV_GN_EPS = 64e-5
LB_FLOOR = 1e-20
RWKV_W = 1088
CHUNK = 16

RET_OFF = 0
HGRN_OFF = 1024
GLA_OFF = 2048
RWKV_OFF = 2944
RWKV_PW = 1152
D_IN_PAD = 4096

V7X_VMEM_LIMIT = 60 * 1024 * 1024
SUBLANES = 8
NEG_BIG = -1e30

(V_LB_A, V_LB_B, V_LB_OM, V_HGRN_G, V_GLA_G, V_W0, V_A0, V_V0, V_KK, V_KA, V_RK,
 V_GN_G, V_GN_B) = range(13)
N_VEC = 16


def _rms(x, g):
    return x * lax.rsqrt(jnp.mean(x * x, axis=-1, keepdims=True) + RMS_EPS) * g


def _sigmoid(x):
    return 0.5 * jnp.tanh(0.5 * x) + 0.5


def _silu(x):
    h = 0.5 * x
    return h * jnp.tanh(h) + h


def _log1p_exp(x):
    return jnp.log(1.0 + jnp.exp(x))


def _log_sigmoid(x):
    return jnp.minimum(x, 0.0) - _log1p_exp(-jnp.abs(x))


def _dot(a, b):
    return jnp.dot(a.astype(BF16), b.astype(BF16), preferred_element_type=F32)


def _dot_nt(a, b):
    return lax.dot_general(a.astype(BF16), b.astype(BF16), (((1,), (1,)), ((), ())),
                           preferred_element_type=F32)


def _dot_tn(a, b):
    return lax.dot_general(a.astype(BF16), b.astype(BF16), (((0,), (0,)), ((), ())),
                           preferred_element_type=F32)


def _row_tiles(x, n):
    return [x[i * SUBLANES:(i + 1) * SUBLANES, :] for i in range(n)]


def _w_in_layout_kernel(w_ref, o_ref):
    w = w_ref[...]
    zeros = lambda n: jnp.zeros((w.shape[0], n), w.dtype)
    o_ref[...] = jnp.concatenate([
        w[:, 0:2560],
        w[:, 2576:2832], w[:, 2560:2576], zeros(112),
        w[:, 2832:3088], w[:, 3120:3376], w[:, 3376:3632], w[:, 3664:3920],
        w[:, 3088:3120], w[:, 3632:3664], zeros(64)], axis=1).astype(BF16)


def _w_in_layout(w_in):
    depth, d, n = w_in.shape
    rows = 128
    return pl.pallas_call(
        _w_in_layout_kernel,
        grid=(depth, d // rows),
        in_specs=[pl.BlockSpec((None, rows, n), lambda l, i: (l, i, 0))],
        out_specs=pl.BlockSpec((None, rows, D_IN_PAD), lambda l, i: (l, i, 0)),
        out_shape=jax.ShapeDtypeStruct((depth, d, D_IN_PAD), BF16),
        compiler_params=pltpu.CompilerParams(dimension_semantics=("arbitrary", "arbitrary")),
        name="w_in_layout",
    )(w_in)


def _in_proj_kernel(x_ref, g_ref, w_ref, o_ref):
    o_ref[...] = _dot(_rms(x_ref[...], g_ref[...]), w_ref[...])


def _in_proj(x2d, gain, w_bf16, tm, layer=None):
    m, d = x2d.shape
    n = w_bf16.shape[-1]
    w_spec = (pl.BlockSpec((d, n), lambda i: (0, 0)) if layer is None
              else pl.BlockSpec((None, d, n), lambda i: (layer, 0, 0)))
    return pl.pallas_call(
        _in_proj_kernel,
        grid=(m // tm,),
        in_specs=[pl.BlockSpec((tm, d), lambda i: (i, 0)),
                  pl.BlockSpec((1, d), lambda i: (0, 0)),
                  w_spec],
        out_specs=pl.BlockSpec((tm, n), lambda i: (i, 0)),
        out_shape=jax.ShapeDtypeStruct((m, n), F32),
        compiler_params=pltpu.CompilerParams(dimension_semantics=("arbitrary",),
                                             vmem_limit_bytes=V7X_VMEM_LIMIT),
        name="in_proj",
    )(x2d, gain, w_bf16)


def _chunk_cumsum(x, c):
    t_local = jnp.bitwise_and(lax.broadcasted_iota(jnp.int32, x.shape, 0), c - 1)
    s = 1
    while s < c:
        x = x + jnp.where(t_local >= s, pltpu.roll(x, s, 0), 0.0)
        s *= 2
    return x


def _head_stack(x, dk=HEAD_DIM):
    lane_head = jnp.right_shift(lax.broadcasted_iota(jnp.int32, x.shape, 1), dk.bit_length() - 1)
    return jnp.concatenate([jnp.where(lane_head == h, x, 0.0) for h in range(N_HEADS)], axis=0)


def _head_unstack(x4, n):
    lane_head = jnp.right_shift(lax.broadcasted_iota(jnp.int32, (n, BRANCH), 1), 6)
    out = jnp.where(lane_head == 0, x4[0:n, :], 0.0)
    for h in range(1, N_HEADS):
        out = out + jnp.where(lane_head == h, x4[h * n:(h + 1) * n, :], 0.0)
    return out


def _heads_to_rows(x):
    return jnp.concatenate([x[:, h * HEAD_DIM:(h + 1) * HEAD_DIM] for h in range(N_HEADS)], axis=0)


def _rows_to_heads(x4, n):
    return jnp.concatenate([x4[h * n:(h + 1) * n, :] for h in range(N_HEADS)], axis=1)


def _state_read(lhs_stacked, st, n):
    return _rows_to_heads(_dot_nt(lhs_stacked, st), n)


def _state_outer(values, keys_stacked):
    return _dot_tn(_heads_to_rows(values), keys_stacked)


def _gl_tiles(q, k, b, c):
    n_rt = c // SUBLANES
    rows = lax.broadcasted_iota(jnp.int32, (SUBLANES, q.shape[1]), 0)
    q_t, b_t = _row_tiles(q, n_rt), _row_tiles(b, n_rt)
    tiles, where_ = [], {}
    for m in range(c):
        for rt in range(m // SUBLANES, n_rt):
            arg = b_t[rt] - b[m:m + 1, :]
            if rt == m // SUBLANES:
                arg = jnp.where(rows >= m % SUBLANES, arg, NEG_BIG)
            where_[m, rt] = len(tiles) * SUBLANES
            tiles.append(q_t[rt] * k[m:m + 1, :] * jnp.exp(arg))
    return tiles, where_


def _gl_output(scores, o_state, v, where_, c):
    o = _row_tiles(o_state, c // SUBLANES)
    for (m, rt), off in where_.items():
        o[rt] = o[rt] + scores[off:off + SUBLANES, :] * v[m:m + 1, :]
    return jnp.concatenate(o, axis=0)


def _gl_state(k, v, b, st, c):
    b_last = b[c - 1:c, :]
    dk = k.shape[1] // N_HEADS
    return st * jnp.exp(b_last) + _state_outer(v, _head_stack(k * jnp.exp(b_last - b), dk))


def _rwkv_tiles(kkt, ahat, c):
    n_rt = c // SUBLANES
    rows = lax.broadcasted_iota(jnp.int32, (SUBLANES, BRANCH), 0)
    kk_t = _row_tiles(kkt, n_rt)
    tiles, where_ = [], {}
    for m in range(c - 1):
        for rt_i in range(m // SUBLANES, n_rt):
            on_diag = rt_i == m // SUBLANES
            if on_diag and m % SUBLANES == SUBLANES - 1:
                continue
            kk_m = jnp.where(rows > m % SUBLANES, kk_t[rt_i], 0.0) if on_diag else kk_t[rt_i]
            where_[m, rt_i] = len(tiles) * SUBLANES
            tiles.append(kk_m * ahat[m:m + 1, :])
    return tiles, where_


def _rwkv_compact(lhs_stacked, khat, ahat):
    return _dot_nt(lhs_stacked, jnp.concatenate([khat, ahat], axis=0))


def _rwkv_masks(sc, c):
    row = lax.broadcasted_iota(jnp.int32, (2 * N_HEADS * c, c), 0)
    col = lax.broadcasted_iota(jnp.int32, (2 * N_HEADS * c, c), 1)
    keep = col < jnp.bitwise_and(row, c - 1) + jnp.where(jnp.bitwise_and(row, c) != 0, 1, 0)
    r_a = jnp.concatenate([sc[(2 * h + 1) * c:(2 * h + 2) * c, c:2 * c] for h in range(N_HEADS)], axis=0)
    row4 = lax.broadcasted_iota(jnp.int32, (N_HEADS * c, c), 0)
    col4 = lax.broadcasted_iota(jnp.int32, (N_HEADS * c, c), 1)
    return jnp.where(keep, sc[:, 0:c], 0.0), jnp.where(col4 <= jnp.bitwise_and(row4, c - 1), r_a, 0.0)


def _rwkv_solve(sc_a, rhs, where_, c):
    n_rt = c // SUBLANES
    u = _row_tiles(rhs, n_rt)
    for s in range(c - 1):
        u_s = u[s // SUBLANES][s % SUBLANES:s % SUBLANES + 1, :]
        for rt_i in range(s // SUBLANES, n_rt):
            if (s, rt_i) in where_:
                off = where_[s, rt_i]
                u[rt_i] = u[rt_i] - sc_a[off:off + SUBLANES, :] * u_s
    return jnp.concatenate(u, axis=0)


def _rwkv_state(u, v, khat, ahat, p_last, st):
    return st * p_last + _state_outer(jnp.concatenate([v, -u], axis=0),
                                      _head_stack(jnp.concatenate([khat * p_last, ahat * p_last], axis=0)))


def _mixers_kernel(*refs, bb, tc, c, layer0, n_t):
    it = iter(refs)
    proj_ref = next(it)
    vfirst_in_ref = None if layer0 else next(it)
    cos_ref, sin_ref = next(it), next(it)
    s_ret0, s_hgrn0, s_gla0, s_rw0, shift0 = next(it), next(it), next(it), next(it), next(it)
    vec_ref, mu_ref, bgate_ref = next(it), next(it), next(it)
    wg_ref, w2_ref, a2_ref = next(it), next(it), next(it)
    v1_ref, v2_ref = (None, None) if layer0 else (next(it), next(it))
    bo_ref, bog_ref, swap_ref = next(it), next(it), next(it)
    dmat_ref, eq_ref, ek_ref, gblk_ref = next(it), next(it), next(it), next(it)
    ocat_ref = next(it)
    vfirst_out_ref = next(it) if layer0 else None
    o_ret_st, o_hgrn_st, o_gla_st, o_rw_st = next(it), next(it), next(it), next(it)
    st_ret, st_hgrn, st_gla, st_rw, carry = next(it), next(it), next(it), next(it), next(it)
    qkvb = [[next(it) for _ in range(4)] for _ in range(2)]
    o_gl = [next(it) for _ in range(2)]
    ps_ref, orw_ref = next(it), next(it)
    rw = [next(it) for _ in range(6)]

    r = bb * tc
    ti = pl.program_id(1)
    key_value_io = ((st_ret, s_ret0, o_ret_st), (st_hgrn, s_hgrn0, o_hgrn_st), (st_gla, s_gla0, o_gla_st))

    def vec(i):
        return vec_ref[i:i + 1, :]

    @pl.when(ti == 0)
    def _init():
        for b in range(bb):
            for st_ref, s0_ref, _ in key_value_io:
                st_ref[b] = s0_ref[b].reshape(st_ref.shape[2], HEAD_DIM).T
            for h in range(N_HEADS):
                st_rw[b, :, h * HEAD_DIM:(h + 1) * HEAD_DIM] = s_rw0[b, h]
        carry[...] = shift0[...]

    def blk(lo, width):
        return proj_ref[:, :, lo:lo + width].reshape(r, width)

    bo = bo_ref[...]
    bo2 = jnp.concatenate([bo, bo], axis=0)

    def head_sum(x, two_pass=False):
        if not two_pass:
            return _dot(x, bo)
        hi = x.astype(BF16)
        lo = (x - hi.astype(F32)).astype(BF16)
        return jnp.dot(jnp.concatenate([hi, lo], axis=1), bo2, preferred_element_type=F32)

    def head_mean(x, two_pass=False):
        return head_sum(x, two_pass) * (1.0 / HEAD_DIM)

    cos = jnp.concatenate([cos_ref[...]] * bb, axis=0)
    sin = jnp.concatenate([sin_ref[...]] * bb, axis=0)
    def rope(x):
        return x * cos + _dot(x, swap_ref[...]) * sin

    q_r = rope(blk(RET_OFF, BRANCH))
    k_r = rope(blk(RET_OFF + 256, BRANCH)) * (HEAD_DIM ** -0.5)
    v_r = blk(RET_OFF + 512, BRANCH)
    seqs = range(bb)
    blocks = [slice(b * tc, (b + 1) * tc) for b in seqs]
    ret_sc = [_dot_nt(_head_stack(q_r[rows, :]), k_r[rows, :]) * dmat_ref[...] for rows in blocks]
    ret_o4 = [_dot(ret_sc[b], v_r[blocks[b], :]) for b in seqs]
    ret_os = [_state_read(_head_stack(q_r[blocks[b], :] * eq_ref[...]), st_ret[b], tc) for b in seqs]
    for b in seqs:
        upd = _state_outer(v_r[blocks[b], :], _head_stack(k_r[blocks[b], :] * ek_ref[...]))
        st_ret[b] = st_ret[b] * gblk_ref[...] + upd
    o_ret = jnp.concatenate([ret_os[b] + _head_unstack(ret_o4[b], tc) for b in seqs], axis=0)

    q_h, k_h, v_h, b_h = qkvb[0]
    q_h[...] = _silu(blk(HGRN_OFF, BRANCH))
    ff = blk(HGRN_OFF + 256, BRANCH)
    y = vec(V_LB_B) + _log_sigmoid(ff)
    a_lb = jnp.broadcast_to(vec(V_LB_A), y.shape)
    mx = jnp.maximum(a_lb, y)
    log_f = mx + _log1p_exp(jnp.minimum(a_lb, y) - mx)
    k_h[...] = vec(V_LB_OM) * _sigmoid(-ff)
    v_h[...] = blk(HGRN_OFF + 512, BRANCH)
    b_h[...] = _chunk_cumsum(log_f, c)

    q_g, k_g, v_g, b_g = qkvb[1]
    q_g[...] = blk(GLA_OFF, GLA_KEY_WIDTH) * (GLA_KEY_DIM ** -0.5)
    k_g[...] = blk(GLA_OFF + 128, GLA_KEY_WIDTH)
    v_g[...] = blk(GLA_OFF + 256, BRANCH)
    z = _dot(blk(GLA_OFF + 768, 128), wg_ref[...]) + bgate_ref[...]
    b_g[...] = _chunk_cumsum(_log_sigmoid(z) / GLA_TAU, c)

    p_rw = blk(RWKV_OFF, RWKV_PW)
    mu = mu_ref[...]
    ps_ref[...] = p_rw + (pltpu.roll(p_rw, 1, 0) - p_rw) * mu
    for b in range(bb):
        p0 = p_rw[b * tc:b * tc + 1, :]
        ps_ref[b * tc:b * tc + 1, :] = p0 + (carry[b:b + 1, :] - p0) * mu
        carry[b:b + 1, :] = p_rw[(b + 1) * tc - 1:(b + 1) * tc, :]

    rr = ps_ref[:, 0:256]
    k_raw = ps_ref[:, 256:512]
    vv = ps_ref[:, 512:768]
    low_rank = ps_ref[:, 1024:1152]
    w_pre = vec(V_W0) + _dot(jnp.tanh(low_rank), w2_ref[...])
    w_log = -(jnp.maximum(-w_pre, 0.0) + _log1p_exp(-jnp.abs(w_pre))) - 0.5
    log_w = -jnp.exp(w_log)
    a = _sigmoid(vec(V_A0) + _dot(low_rank, a2_ref[...]))
    if layer0:
        vfirst_out_ref[...] = vv.reshape(bb, tc, BRANCH)
    else:
        vmix = _sigmoid(vec(V_V0) + _dot(_dot(vv, v1_ref[...]), v2_ref[...]))
        vv = vv + (vfirst_in_ref[...].reshape(r, BRANCH) - vv) * vmix
    kk_raw = k_raw * vec(V_KK)
    kk = kk_raw / jnp.maximum(jnp.sqrt(head_sum(kk_raw * kk_raw, two_pass=True)), 1e-12)
    kmod = k_raw * (1.0 + (a - 1.0) * vec(V_KA))
    b_w = _chunk_cumsum(log_w, c)
    inv_p = jnp.exp(-b_w)
    p_w = jnp.exp(b_w)
    for ref, arr in zip(rw, (kk * jnp.exp(b_w - log_w), rr * p_w, kmod * inv_p, kk * a * inv_p, vv, p_w)):
        ref[...] = arr

    n_c = tc // c
    mix = ((qkvb[0], o_gl[0], st_hgrn, bo_ref, HEAD_DIM), (qkvb[1], o_gl[1], st_gla, bog_ref, GLA_KEY_DIM))

    def chunk_body(ci, _):
        rows = [pl.ds(pl.multiple_of(b * tc + ci * c, c), c) for b in seqs]

        def all_scores(tiles, bo_m):
            sc = _dot(jnp.concatenate(tiles, axis=0), bo_m)
            n = len(tiles) // bb * SUBLANES
            return [sc[b * n:(b + 1) * n, :] for b in seqs]

        gl = []
        for (q_s, k_s, v_s, b_s), o_s, st_ref, bo_m, dk in mix:
            ins = [(q_s[rows[b], :], k_s[rows[b], :], v_s[rows[b], :], b_s[rows[b], :]) for b in seqs]
            tiles = []
            for q, k, _, b_c in ins:
                t, where_ = _gl_tiles(q, k, b_c, c)
                tiles += t
            gl.append((ins, all_scores(tiles, bo_m[...]), where_, o_s, st_ref, dk))
        rw_in = [tuple(x[rows[b], :] for x in rw) for b in seqs]
        tiles = []
        for kkt, _, _, ahat, _, _ in rw_in:
            t, rw_where = _rwkv_tiles(kkt, ahat, c)
            tiles += t
        rw_sc = all_scores(tiles, bo)
        rw_lhs = [_head_stack(jnp.concatenate([kkt, rt], axis=0)) for kkt, rt, _, _, _, _ in rw_in]
        rw_cmp = [_rwkv_compact(rw_lhs[b], rw_in[b][2], rw_in[b][3]) for b in seqs]
        gl_os = [[_state_read(_head_stack(q * jnp.exp(b_c), dk), st_ref[b], c)
                  for b, (q, _, _, b_c) in enumerate(ins)] for ins, _, _, _, st_ref, dk in gl]
        rw_fs = [_state_read(rw_lhs[b], st_rw[b], 2 * c) for b in seqs]
        rw_m = [_rwkv_masks(rw_cmp[b], c) for b in seqs]
        rw_fv = [_head_unstack(_dot(rw_m[b][0], rw_in[b][4]), 2 * c) for b in seqs]
        for ins, _, _, _, st_ref, _ in gl:
            for b, (_, k, v, b_c) in enumerate(ins):
                st_ref[b] = _gl_state(k, v, b_c, st_ref[b], c)
        for (ins, sc, where_, o_s, _, _), o_state in zip(gl, gl_os):
            for b in seqs:
                o_s[rows[b], :] = _gl_output(sc[b], o_state[b], ins[b][2], where_, c)
        rhs = [rw_fs[b] + rw_fv[b] for b in seqs]
        us = [_rwkv_solve(rw_sc[b], rhs[b][0:c, :], rw_where, c) for b in seqs]
        rw_ou = [_dot(rw_m[b][1], us[b]) for b in seqs]
        for b in seqs:
            _, _, khat, ahat, v_w, p_c = rw_in[b]
            st_rw[b] = _rwkv_state(us[b], v_w, khat, ahat, p_c[c - 1:c, :], st_rw[b])
        for b in seqs:
            orw_ref[rows[b], :] = rhs[b][c:2 * c, :] - _head_unstack(rw_ou[b], c)
        return 0

    lax.fori_loop(0, n_c, chunk_body, 0)

    def group_norm(o, eps):
        cen = o - head_mean(o, two_pass=True)
        return cen * lax.rsqrt(head_mean(cen * cen) + eps)

    def head_rms(o, g):
        return o * lax.rsqrt(head_mean(o * o) + RMS_EPS) * g

    def put(lo, val):
        ocat_ref[:, :, lo:lo + BRANCH] = val.reshape(bb, tc, BRANCH)

    put(0, group_norm(o_ret, RET_GN_EPS) * _silu(blk(RET_OFF + 768, BRANCH)))
    put(256, head_rms(o_gl[0][...], vec(V_HGRN_G)) * _silu(blk(HGRN_OFF + 768, BRANCH)))
    put(512, head_rms(o_gl[1][...], vec(V_GLA_G)) * _silu(blk(GLA_OFF + 512, BRANCH)))
    o_rw = group_norm(orw_ref[...], RWKV_GN_EPS) * vec(V_GN_G) + vec(V_GN_B)
    bonus = head_sum(rr * kmod * vec(V_RK))
    put(768, (o_rw + bonus * vv) * _silu(ps_ref[:, 768:1024]))

    @pl.when(ti == n_t - 1)
    def _final():
        for b in range(bb):
            for st_ref, _, out_ref in key_value_io:
                out_ref[b] = st_ref[b].T.reshape(out_ref.shape[1:])
            for h in range(N_HEADS):
                o_rw_st[b, h] = st_rw[b][:, h * HEAD_DIM:(h + 1) * HEAD_DIM]


def _mixers(proj, vfirst, cos, sin, states, layer, lp, consts, *, bb, tc, c):
    bsz, seq, _ = proj.shape
    layer0 = vfirst is None
    n_t = seq // tc
    r = bb * tc

    def full(arr):
        nd = arr.ndim
        return pl.BlockSpec(arr.shape, lambda bi, ti, _nd=nd: (0,) * _nd)

    def per_b(arr):
        nd = arr.ndim
        return pl.BlockSpec((None, bb) + arr.shape[2:], lambda bi, ti, _nd=nd: (layer, bi) + (0,) * (_nd - 2))

    def tok(width):
        return pl.BlockSpec((bb, tc, width), lambda bi, ti: (bi, ti, 0))

    ins, specs = [proj], [tok(D_IN_PAD)]
    if not layer0:
        ins.append(vfirst)
        specs.append(tok(BRANCH))
    ins += [cos, sin]
    specs += [pl.BlockSpec((tc, BRANCH), lambda bi, ti: (ti, 0))] * 2
    state_inputs = list(range(len(ins), len(ins) + 4))
    for s in states:
        ins.append(s)
        specs.append(per_b(s))
    small = [lp['vec'], lp['mu'], lp['bgate'], lp['wg'], lp['w2'], lp['a2']]
    if not layer0:
        small += [lp['v1'], lp['v2']]
    small += list(consts) + list(_retention_tables(tc))
    for s in small:
        ins.append(s)
        specs.append(full(s))

    out_shape = [jax.ShapeDtypeStruct((bsz, seq, D_MODEL), F32)]
    out_specs = [tok(D_MODEL)]
    if layer0:
        out_shape.append(jax.ShapeDtypeStruct((bsz, seq, BRANCH), F32))
        out_specs.append(tok(BRANCH))
    aliases = {}
    for idx, s in zip(state_inputs, states[:4]):
        aliases[idx] = len(out_shape)
        out_shape.append(jax.ShapeDtypeStruct(s.shape, F32))
        out_specs.append(per_b(s))

    scratch = [pltpu.VMEM((bb, HEAD_DIM, BRANCH), F32), pltpu.VMEM((bb, HEAD_DIM, BRANCH), F32),
               pltpu.VMEM((bb, HEAD_DIM, GLA_KEY_WIDTH), F32), pltpu.VMEM((bb, HEAD_DIM, BRANCH), F32),
               pltpu.VMEM((bb, RWKV_PW), F32)]
    for dkt in (BRANCH, GLA_KEY_WIDTH):
        scratch += [pltpu.VMEM((r, dkt), F32), pltpu.VMEM((r, dkt), F32), pltpu.VMEM((r, BRANCH), F32),
                    pltpu.VMEM((r, dkt), F32)]
    scratch += [pltpu.VMEM((r, BRANCH), F32)] * 2
    scratch += [pltpu.VMEM((r, RWKV_PW), F32), pltpu.VMEM((r, BRANCH), F32)]
    scratch += [pltpu.VMEM((r, BRANCH), F32)] * 6

    outs = pl.pallas_call(
        functools.partial(_mixers_kernel, bb=bb, tc=tc, c=c, layer0=layer0, n_t=n_t),
        grid=(bsz // bb, n_t),
        in_specs=specs,
        out_specs=out_specs,
        out_shape=out_shape,
        scratch_shapes=scratch,
        input_output_aliases=aliases,
        compiler_params=pltpu.CompilerParams(dimension_semantics=("arbitrary", "arbitrary"),
                                             vmem_limit_bytes=V7X_VMEM_LIMIT),
        name="mixers",
    )(*ins)
    if layer0:
        return outs[0], outs[1], list(outs[2:])
    return outs[0], None, list(outs[1:])


def _out_attn_kernel(x_ref, oc_ref, mk_ref, mv_ref, wout_ref, wq_ref, wo_ref, gx_ref, gf_ref, out_ref,
                     *, bbc, lc, final):
    r = bbc * lc
    x1 = x_ref[...].reshape(r, D_MODEL) + _dot(oc_ref[...].reshape(r, D_MODEL), wout_ref[...])
    q = _dot(_rms(x1, gx_ref[...]), wq_ref[...])
    seqs = range(bbc)
    scores = [_dot_nt(_head_stack(q[b * lc:(b + 1) * lc, :]), mk_ref[b]) * (HEAD_DIM ** -0.5) for b in seqs]
    probs = []
    for s in scores:
        e = jnp.exp(s - jnp.max(s, axis=-1, keepdims=True))
        probs.append(e / jnp.sum(e, axis=-1, keepdims=True))
    o4 = [_dot(probs[b], mv_ref[b]) for b in seqs]
    outs = [_head_unstack(o4[b], lc) for b in seqs]
    o = outs[0] if bbc == 1 else jnp.concatenate(outs, axis=0)
    x2 = x1 + _dot(o, wo_ref[...])
    if final:
        x2 = _rms(x2, gf_ref[...])
    out_ref[...] = x2.reshape(bbc, lc, D_MODEL)


def _out_attn(x, ocat, mem, lp, norm_f, *, bbc, lc, final):
    bsz, seq, _ = x.shape
    mk, mv, mk_spec, mv_spec = mem

    def tok(width):
        return pl.BlockSpec((bbc, lc, width), lambda bi, li: (bi, li, 0))

    def full(arr):
        nd = arr.ndim
        return pl.BlockSpec(arr.shape, lambda bi, li, _nd=nd: (0,) * _nd)

    small = [lp['w_out'], lp['wq'], lp['wo'], lp['norm_x'], norm_f]
    return pl.pallas_call(
        functools.partial(_out_attn_kernel, bbc=bbc, lc=lc, final=final),
        grid=(bsz // bbc, seq // lc),
        in_specs=[tok(D_MODEL), tok(D_MODEL), mk_spec, mv_spec] + [full(s) for s in small],
        out_specs=tok(D_MODEL),
        out_shape=jax.ShapeDtypeStruct((bsz, seq, D_MODEL), F32),
        compiler_params=pltpu.CompilerParams(dimension_semantics=("arbitrary", "arbitrary"),
                                             vmem_limit_bytes=V7X_VMEM_LIMIT),
        name="out_attn",
    )(x, ocat, mk, mv, *small)


def _rope_tables(pos0, seq):
    half = HEAD_DIM // 2
    inv = ROPE_BASE ** (-jnp.arange(half, dtype=F32) / half)
    pos = (pos0 + jnp.arange(seq, dtype=jnp.int32)).astype(F32)
    ang = pos[:, None] * inv[None, :]
    cos, sin = lax.optimization_barrier((jnp.cos(ang), jnp.sin(ang)))
    return (jnp.tile(jnp.concatenate([cos, cos], axis=-1), (1, N_HEADS)),
            jnp.tile(jnp.concatenate([-sin, sin], axis=-1), (1, N_HEADS)))


def _retention_tables(n):
    log_gamma = jnp.log1p(-jnp.exp2(-5.0 - jnp.arange(N_HEADS, dtype=F32)))
    j = jnp.arange(n, dtype=F32)
    diff = j[:, None] - j[None, :]
    dmat = jnp.where(diff[None] >= 0, jnp.exp(diff[None] * log_gamma[:, None, None]), 0.0)
    lg_lanes = jnp.repeat(log_gamma, HEAD_DIM)[None, :]
    e_q = jnp.exp((j[:, None] + 1.0) * lg_lanes)
    e_k = jnp.exp((n - 1.0 - j[:, None]) * lg_lanes)
    g_blk = jnp.exp(float(n) * lg_lanes)
    return dmat.reshape(N_HEADS * n, n), e_q, e_k, g_blk


def _rwkv_to_padded(t):
    pad = jnp.zeros(t.shape[:-1] + (RWKV_PW - RWKV_W,), t.dtype)
    return jnp.concatenate([t[..., 0:256], t[..., 288:544], t[..., 544:800], t[..., 832:1088],
                            t[..., 256:288], t[..., 800:832], pad], axis=-1)


def _rwkv_from_padded(t):
    return jnp.concatenate([t[..., 0:256], t[..., 1024:1056], t[..., 256:512], t[..., 512:768],
                            t[..., 1056:1088], t[..., 768:1024]], axis=-1)


def _pad_rows(m, rows, at=0):
    out = jnp.zeros((rows, m.shape[1]), m.dtype)
    return out.at[at:at + m.shape[0]].set(m)


def _layer_params(l, P):
    sm = jax.nn.softmax(P['hgrn_lb_logits'].astype(F32), axis=0)
    lb = (jnp.cumsum(sm, axis=0) - sm[0])[l]
    rows = [None] * N_VEC
    rows[V_LB_A] = jnp.log(jnp.maximum(lb, LB_FLOOR))
    rows[V_LB_B] = jnp.log1p(-lb)
    rows[V_LB_OM] = 1.0 - lb
    rows[V_HGRN_G] = jnp.tile(P['hgrn_norm'][l], N_HEADS)
    rows[V_GLA_G] = jnp.tile(P['gla_norm'][l], N_HEADS)
    rows[V_W0] = P['rwkv_w0'][l]
    rows[V_A0] = P['rwkv_a0'][l]
    rows[V_V0] = P['rwkv_v0'][l - 1] if l > 0 else jnp.zeros((BRANCH,), F32)
    rows[V_KK] = P['rwkv_k_k'][l]
    rows[V_KA] = P['rwkv_k_a'][l]
    rows[V_RK] = P['rwkv_r_k'][l].reshape(BRANCH)
    rows[V_GN_G] = P['rwkv_gn_g'][l]
    rows[V_GN_B] = P['rwkv_gn_b'][l]
    zero = jnp.zeros((BRANCH,), F32)
    vec = jnp.stack([zero if x is None else x.astype(F32) for x in rows], axis=0)

    lp = {
        'norm_mix': P['norm_mix'][l][None, :],
        'vec': vec,
        'mu': _rwkv_to_padded(P['rwkv_mu'][l])[None, :],
        'bgate': P['gla_b_gate'][l][None, :],
        'wg': _pad_rows(P['gla_w_gate2'][l], 128).astype(BF16),
        'w2': _pad_rows(P['rwkv_w2'][l], 128, 0).astype(BF16),
        'a2': _pad_rows(P['rwkv_a2'][l], 128, 32).astype(BF16),
        'w_out': P['w_out'][l].astype(BF16),
        'wq': P['wq_x'][l].astype(BF16),
        'wo': P['wo_x'][l].astype(BF16),
        'norm_x': P['norm_x'][l][None, :],
    }
    if l > 0:
        lp['v1'] = jnp.pad(P['rwkv_v1'][l - 1], ((0, 0), (0, 96))).astype(BF16)
        lp['v2'] = _pad_rows(P['rwkv_v2'][l - 1], 128).astype(BF16)
    return lp


def _consts():
    i256 = jnp.arange(BRANCH) // HEAD_DIM
    i128 = jnp.arange(GLA_KEY_WIDTH) // GLA_KEY_DIM
    bo = (i256[:, None] == i256[None, :])
    bog = (i128[:, None] == i256[None, :])
    lane = jnp.arange(BRANCH)
    partner = jnp.where(lane % HEAD_DIM < HEAD_DIM // 2, lane + HEAD_DIM // 2, lane - HEAD_DIM // 2)
    swap = (lane[:, None] == partner[None, :])
    return bo.astype(BF16), bog.astype(BF16), swap.astype(BF16)


def _run_trunk(x, pos0, mems, init, lps, w_in_all, norm_f, consts, *, bb, tc, c, bbc, lc, tm):
    bsz, seq, _ = x.shape
    cos, sin = _rope_tables(pos0, seq)
    new_shift = []
    states = [init['ret'], init['hgrn'], init['gla'], init['rwkv'], _rwkv_to_padded(init['shift'])]
    vfirst = None
    for l in range(DEPTH):
        lp = lps[l]
        proj = _in_proj(x.reshape(bsz * seq, D_MODEL), lp['norm_mix'], w_in_all, tm, layer=l)
        proj = proj.reshape(bsz, seq, D_IN_PAD)
        ocat, vf, updated = _mixers(proj, vfirst, cos, sin, states, l, lp, consts, bb=bb, tc=tc, c=c)
        states = updated + states[4:]
        if l == 0:
            vfirst = vf
        new_shift.append(_rwkv_from_padded(proj[:, seq - 1, RWKV_OFF:RWKV_OFF + RWKV_PW]))
        x = _out_attn(x, ocat, mems[l], lp, norm_f, bbc=bbc, lc=lc, final=(l == DEPTH - 1))
    return x, {'ret': states[0], 'hgrn': states[1], 'gla': states[2], 'rwkv': states[3],
               'shift': jnp.stack(new_shift, axis=0)}


def kernel(x_prompt, x_sample, mem_prompt, state_ret, state_hgrn, state_gla, state_rwkv, state_rwkv_shift,
           cache_mem_k, cache_mem_v, norm_mix, w_in, hgrn_lb_logits, hgrn_norm, gla_w_gate2, gla_b_gate,
           gla_norm, rwkv_mu, rwkv_w0, rwkv_w2, rwkv_a0, rwkv_a2, rwkv_v0, rwkv_v1, rwkv_v2, rwkv_k_k,
           rwkv_k_a, rwkv_r_k, rwkv_gn_g, rwkv_gn_b, w_out, norm_x, wq_x, wo_x, norm_mem, wk_x, wv_x, norm_f):
    P = {'norm_mix': norm_mix, 'hgrn_lb_logits': hgrn_lb_logits, 'hgrn_norm': hgrn_norm,
         'gla_w_gate2': gla_w_gate2, 'gla_b_gate': gla_b_gate, 'gla_norm': gla_norm,
         'rwkv_mu': rwkv_mu, 'rwkv_w0': rwkv_w0, 'rwkv_w2': rwkv_w2, 'rwkv_a0': rwkv_a0,
         'rwkv_a2': rwkv_a2, 'rwkv_v0': rwkv_v0, 'rwkv_v1': rwkv_v1, 'rwkv_v2': rwkv_v2,
         'rwkv_k_k': rwkv_k_k, 'rwkv_k_a': rwkv_k_a, 'rwkv_r_k': rwkv_r_k,
         'rwkv_gn_g': rwkv_gn_g, 'rwkv_gn_b': rwkv_gn_b, 'w_out': w_out,
         'norm_x': norm_x, 'wq_x': wq_x, 'wo_x': wo_x}
    lps = [_layer_params(l, P) for l in range(DEPTH)]
    w_in_all = _w_in_layout(w_in)
    consts = _consts()
    norm_f2 = norm_f[None, :]
    bsz, seq, _ = x_prompt.shape
    dbsz, dseq, _ = x_sample.shape
    p_bbc, s_bbc = 2, 32

    mem2d = mem_prompt.reshape(bsz * N_MEM, D_MODEL)
    kv_l, p_mems = [], []
    for l in range(DEPTH):
        wkv = jnp.concatenate([wk_x[l], wv_x[l]], axis=1).astype(BF16)
        kv = _in_proj(mem2d, norm_mem[l][None, :], wkv, 256).reshape(bsz, N_MEM, 2 * BRANCH)
        kv_l.append(kv)
        p_mems.append((kv, kv, pl.BlockSpec((p_bbc, N_MEM, BRANCH), lambda bi, li: (bi, 0, 0)),
                       pl.BlockSpec((p_bbc, N_MEM, BRANCH), lambda bi, li: (bi, 0, 1))))
    zero_init = {
        'ret': jnp.zeros((DEPTH, bsz, N_HEADS, HEAD_DIM, HEAD_DIM), F32),
        'hgrn': jnp.zeros((DEPTH, bsz, N_HEADS, HEAD_DIM, HEAD_DIM), F32),
        'gla': jnp.zeros((DEPTH, bsz, N_HEADS, GLA_KEY_DIM, HEAD_DIM), F32),
        'rwkv': jnp.zeros((DEPTH, bsz, N_HEADS, HEAD_DIM, HEAD_DIM), F32),
        'shift': jnp.zeros((DEPTH, bsz, RWKV_W), F32),
    }
    y_prompt, sp = _run_trunk(x_prompt, 0, p_mems, zero_init, lps, w_in_all, norm_f2, consts,
                              bb=8, tc=64, c=CHUNK, bbc=p_bbc, lc=512, tm=512)

    cached = {'ret': state_ret, 'hgrn': state_hgrn, 'gla': state_gla, 'rwkv': state_rwkv,
              'shift': state_rwkv_shift}
    cmk = cache_mem_k.reshape(DEPTH, dbsz, N_MEM, BRANCH)
    cmv = cache_mem_v.reshape(DEPTH, dbsz, N_MEM, BRANCH)
    s_mems = []
    for l in range(DEPTH):
        spec = pl.BlockSpec((None, s_bbc, N_MEM, BRANCH), lambda bi, li, _l=l: (_l, bi, 0, 0))
        s_mems.append((cmk, cmv, spec, spec))
    y_sample, ss = _run_trunk(x_sample, PAST_LEN, s_mems, cached, lps, w_in_all, norm_f2, consts,
                              bb=16, tc=dseq, c=dseq, bbc=s_bbc, lc=dseq, tm=512)

    kv_all = jnp.stack(kv_l, axis=0)
    mem_k_p = kv_all[..., :BRANCH].reshape(DEPTH, bsz, N_MEM, N_HEADS, HEAD_DIM)
    mem_v_p = kv_all[..., BRANCH:].reshape(DEPTH, bsz, N_MEM, N_HEADS, HEAD_DIM)
    return (y_prompt, y_sample, sp['ret'], ss['ret'], sp['hgrn'], ss['hgrn'], sp['gla'], ss['gla'],
            sp['rwkv'], ss['rwkv'], sp['shift'], ss['shift'], mem_k_p, mem_v_p)
```

```python
import functools

import jax
import jax.numpy as jnp
from jax import lax
from jax.experimental import pallas as pl
from jax.experimental.pallas import tpu as pltpu

F32 = jnp.float32
BF16 = jnp.bfloat16

D_MODEL = 1024
DEPTH = 2
PAST_LEN = 16384
N_HEADS = 4
HEAD_DIM = 64
BRANCH = N_HEADS * HEAD_DIM
GLA_KEY_DIM = 32
GLA_KEY_WIDTH = N_HEADS * GLA_KEY_DIM
GLA_TAU = 16.0
N_MEM = 256
ROPE_BASE = 10000.0
RMS_EPS = 1e-6
RET_GN_EPS = 1e-5
RWKV_GN_EPS = 64e-5
LB_FLOOR = 1e-20
RWKV_W = 1088
CHUNK = 16

RET_OFF = 0
HGRN_OFF = 1024
GLA_OFF = 2048
RWKV_OFF = 2944
RWKV_PW = 1152
D_IN_PAD = 4096

V7X_VMEM_LIMIT = 60 * 1024 * 1024
SUBLANES = 8
NEG_BIG = -1e30

(V_LB_A, V_LB_B, V_LB_OM, V_HGRN_G, V_GLA_G, V_W0, V_A0, V_V0, V_KK, V_KA, V_RK,
 V_GN_G, V_GN_B) = range(13)
N_VEC = 16


def _rms(x, g):
    return x * lax.rsqrt(jnp.mean(x * x, axis=-1, keepdims=True) + RMS_EPS) * g


def _sigmoid(x):
    return 0.5 * jnp.tanh(0.5 * x) + 0.5


def _silu(x):
    h = 0.5 * x
    return h * jnp.tanh(h) + h


def _log1p_exp(x):
    return jnp.log(1.0 + jnp.exp(x))


def _log_sigmoid(x):
    return jnp.minimum(x, 0.0) - _log1p_exp(-jnp.abs(x))


def _dot(a, b):
    return jnp.dot(a.astype(BF16), b.astype(BF16), preferred_element_type=F32)


def _dot_nt(a, b):
    return lax.dot_general(a.astype(BF16), b.astype(BF16), (((1,), (1,)), ((), ())),
                           preferred_element_type=F32)


def _dot_tn(a, b):
    return lax.dot_general(a.astype(BF16), b.astype(BF16), (((0,), (0,)), ((), ())),
                           preferred_element_type=F32)


def _row_tiles(x, n):
    return [x[i * SUBLANES:(i + 1) * SUBLANES, :] for i in range(n)]


def _w_in_layout_kernel(w_ref, o_ref):
    w = w_ref[...]
    zeros = lambda n: jnp.zeros((w.shape[0], n), w.dtype)
    o_ref[...] = jnp.concatenate([
        w[:, 0:2560],
        w[:, 2576:2832], w[:, 2560:2576], zeros(112),
        w[:, 2832:3088], w[:, 3120:3376], w[:, 3376:3632], w[:, 3664:3920],
        w[:, 3088:3120], w[:, 3632:3664], zeros(64)], axis=1).astype(BF16)


def _w_in_layout(w_in):
    depth, d, n = w_in.shape
    rows = 128
    return pl.pallas_call(
        _w_in_layout_kernel,
        grid=(depth, d // rows),
        in_specs=[pl.BlockSpec((None, rows, n), lambda l, i: (l, i, 0))],
        out_specs=pl.BlockSpec((None, rows, D_IN_PAD), lambda l, i: (l, i, 0)),
        out_shape=jax.ShapeDtypeStruct((depth, d, D_IN_PAD), BF16),
        compiler_params=pltpu.CompilerParams(dimension_semantics=("arbitrary", "arbitrary")),
        name="w_in_layout",
    )(w_in)


def _in_proj_kernel(x_ref, g_ref, w_ref, o_ref):
    o_ref[...] = _dot(_rms(x_ref[...], g_ref[...]), w_ref[...])


def _in_proj(x2d, gain, w_bf16, tm, layer=None):
    m, d = x2d.shape
    n = w_bf16.shape[-1]
    w_spec = (pl.BlockSpec((d, n), lambda i: (0, 0)) if layer is None
              else pl.BlockSpec((None, d, n), lambda i: (layer, 0, 0)))
    return pl.pallas_call(
        _in_proj_kernel,
        grid=(m // tm,),
        in_specs=[pl.BlockSpec((tm, d), lambda i: (i, 0)),
                  pl.BlockSpec((1, d), lambda i: (0, 0)),
                  w_spec],
        out_specs=pl.BlockSpec((tm, n), lambda i: (i, 0)),
        out_shape=jax.ShapeDtypeStruct((m, n), F32),
        compiler_params=pltpu.CompilerParams(dimension_semantics=("arbitrary",),
                                             vmem_limit_bytes=V7X_VMEM_LIMIT),
        name="in_proj",
    )(x2d, gain, w_bf16)


def _chunk_cumsum(x, c):
    t_local = jnp.bitwise_and(lax.broadcasted_iota(jnp.int32, x.shape, 0), c - 1)
    s = 1
    while s < c:
        x = x + jnp.where(t_local >= s, pltpu.roll(x, s, 0), 0.0)
        s *= 2
    return x


def _head_stack(x, dk=HEAD_DIM):
    lane_head = jnp.right_shift(lax.broadcasted_iota(jnp.int32, x.shape, 1), dk.bit_length() - 1)
    return jnp.concatenate([jnp.where(lane_head == h, x, 0.0) for h in range(N_HEADS)], axis=0)


def _head_unstack(x4, n):
    lane_head = jnp.right_shift(lax.broadcasted_iota(jnp.int32, (n, BRANCH), 1), 6)
    out = jnp.where(lane_head == 0, x4[0:n, :], 0.0)
    for h in range(1, N_HEADS):
        out = out + jnp.where(lane_head == h, x4[h * n:(h + 1) * n, :], 0.0)
    return out


def _heads_to_rows(x):
    return jnp.concatenate([x[:, h * HEAD_DIM:(h + 1) * HEAD_DIM] for h in range(N_HEADS)], axis=0)


def _rows_to_heads(x4, n):
    return jnp.concatenate([x4[h * n:(h + 1) * n, :] for h in range(N_HEADS)], axis=1)


def _state_read(lhs_stacked, st, n):
    return _rows_to_heads(_dot_nt(lhs_stacked, st), n)


def _state_outer(values, keys_stacked):
    return _dot_tn(_heads_to_rows(values), keys_stacked)


def _gl_tiles(q, k, b, c):
    n_rt = c // SUBLANES
    rows = lax.broadcasted_iota(jnp.int32, (SUBLANES, q.shape[1]), 0)
    q_t, b_t = _row_tiles(q, n_rt), _row_tiles(b, n_rt)
    tiles, where_ = [], {}
    for m in range(c):
        for rt in range(m // SUBLANES, n_rt):
            arg = b_t[rt] - b[m:m + 1, :]
            if rt == m // SUBLANES:
                arg = jnp.where(rows >= m % SUBLANES, arg, NEG_BIG)
            where_[m, rt] = len(tiles) * SUBLANES
            tiles.append(q_t[rt] * k[m:m + 1, :] * jnp.exp(arg))
    return tiles, where_


def _gl_output(scores, o_state, v, where_, c):
    o = _row_tiles(o_state, c // SUBLANES)
    for (m, rt), off in where_.items():
        o[rt] = o[rt] + scores[off:off + SUBLANES, :] * v[m:m + 1, :]
    return jnp.concatenate(o, axis=0)


def _gl_state(k, v, b, st, c):
    b_last = b[c - 1:c, :]
    dk = k.shape[1] // N_HEADS
    return st * jnp.exp(b_last) + _state_outer(v, _head_stack(k * jnp.exp(b_last - b), dk))


def _rwkv_tiles(kkt, ahat, c):
    n_rt = c // SUBLANES
    rows = lax.broadcasted_iota(jnp.int32, (SUBLANES, BRANCH), 0)
    kk_t = _row_tiles(kkt, n_rt)
    tiles, where_ = [], {}
    for m in range(c - 1):
        for rt_i in range(m // SUBLANES, n_rt):
            on_diag = rt_i == m // SUBLANES
            if on_diag and m % SUBLANES == SUBLANES - 1:
                continue
            kk_m = jnp.where(rows > m % SUBLANES, kk_t[rt_i], 0.0) if on_diag else kk_t[rt_i]
            where_[m, rt_i] = len(tiles) * SUBLANES
            tiles.append(kk_m * ahat[m:m + 1, :])
    return tiles, where_


def _rwkv_compact(lhs_stacked, khat, ahat):
    return _dot_nt(lhs_stacked, jnp.concatenate([khat, ahat], axis=0))


def _rwkv_masks(sc, c):
    row = lax.broadcasted_iota(jnp.int32, (2 * N_HEADS * c, c), 0)
    col = lax.broadcasted_iota(jnp.int32, (2 * N_HEADS * c, c), 1)
    keep = col < jnp.bitwise_and(row, c - 1) + jnp.where(jnp.bitwise_and(row, c) != 0, 1, 0)
    r_a = jnp.concatenate([sc[(2 * h + 1) * c:(2 * h + 2) * c, c:2 * c] for h in range(N_HEADS)], axis=0)
    row4 = lax.broadcasted_iota(jnp.int32, (N_HEADS * c, c), 0)
    col4 = lax.broadcasted_iota(jnp.int32, (N_HEADS * c, c), 1)
    return jnp.where(keep, sc[:, 0:c], 0.0), jnp.where(col4 <= jnp.bitwise_and(row4, c - 1), r_a, 0.0)


def _rwkv_solve(sc_a, rhs, where_, c):
    n_rt = c // SUBLANES
    u = _row_tiles(rhs, n_rt)
    for s in range(c - 1):
        u_s = u[s // SUBLANES][s % SUBLANES:s % SUBLANES + 1, :]
        for rt_i in range(s // SUBLANES, n_rt):
            if (s, rt_i) in where_:
                off = where_[s, rt_i]
                u[rt_i] = u[rt_i] - sc_a[off:off + SUBLANES, :] * u_s
    return jnp.concatenate(u, axis=0)


def _rwkv_state(u, v, khat, ahat, p_last, st):
    return st * p_last + _state_outer(jnp.concatenate([v, -u], axis=0),
                                      _head_stack(jnp.concatenate([khat * p_last, ahat * p_last], axis=0)))


def _mixers_kernel(*refs, bb, tc, c, layer0, n_t):
    it = iter(refs)
    proj_ref = next(it)
    vfirst_in_ref = None if layer0 else next(it)
    cos_ref, sin_ref = next(it), next(it)
    s_ret0, s_hgrn0, s_gla0, s_rw0, shift0 = next(it), next(it), next(it), next(it), next(it)
    vec_ref, mu_ref, bgate_ref = next(it), next(it), next(it)
    wg_ref, w2_ref, a2_ref = next(it), next(it), next(it)
    v1_ref, v2_ref = (None, None) if layer0 else (next(it), next(it))
    bo_ref, bog_ref, swap_ref = next(it), next(it), next(it)
    dmat_ref, eq_ref, ek_ref, gblk_ref = next(it), next(it), next(it), next(it)
    ocat_ref = next(it)
    vfirst_out_ref = next(it) if layer0 else None
    o_ret_st, o_hgrn_st, o_gla_st, o_rw_st = next(it), next(it), next(it), next(it)
    st_ret, st_hgrn, st_gla, st_rw, carry = next(it), next(it), next(it), next(it), next(it)
    qkvb = [[next(it) for _ in range(4)] for _ in range(2)]
    o_gl = [next(it) for _ in range(2)]
    ps_ref, orw_ref = next(it), next(it)
    rw = [next(it) for _ in range(6)]

    r = bb * tc
    ti = pl.program_id(1)
    key_value_io = ((st_ret, s_ret0, o_ret_st), (st_hgrn, s_hgrn0, o_hgrn_st), (st_gla, s_gla0, o_gla_st))

    def vec(i):
        return vec_ref[i:i + 1, :]

    @pl.when(ti == 0)
    def _init():
        for b in range(bb):
            for st_ref, s0_ref, _ in key_value_io:
                st_ref[b] = s0_ref[b].reshape(st_ref.shape[2], HEAD_DIM).T
            for h in range(N_HEADS):
                st_rw[b, :, h * HEAD_DIM:(h + 1) * HEAD_DIM] = s_rw0[b, h]
        carry[...] = shift0[...]

    def blk(lo, width):
        return proj_ref[:, :, lo:lo + width].reshape(r, width)

    bo = bo_ref[...]
    bo2 = jnp.concatenate([bo, bo], axis=0)

    def head_sum(x, two_pass=False):
        if not two_pass:
            return _dot(x, bo)
        hi = x.astype(BF16)
        lo = (x - hi.astype(F32)).astype(BF16)
        return jnp.dot(jnp.concatenate([hi, lo], axis=1), bo2, preferred_element_type=F32)

    def head_mean(x, two_pass=False):
        return head_sum(x, two_pass) * (1.0 / HEAD_DIM)

    cos = jnp.concatenate([cos_ref[...]] * bb, axis=0)
    sin = jnp.concatenate([sin_ref[...]] * bb, axis=0)
    def rope(x):
        return x * cos + _dot(x, swap_ref[...]) * sin

    q_r = rope(blk(RET_OFF, BRANCH))
    k_r = rope(blk(RET_OFF + 256, BRANCH)) * (HEAD_DIM ** -0.5)
    v_r = blk(RET_OFF + 512, BRANCH)
    seqs = range(bb)
    blocks = [slice(b * tc, (b + 1) * tc) for b in seqs]
    ret_sc = [_dot_nt(_head_stack(q_r[rows, :]), k_r[rows, :]) * dmat_ref[...] for rows in blocks]
    ret_o4 = [_dot(ret_sc[b], v_r[blocks[b], :]) for b in seqs]
    ret_os = [_state_read(_head_stack(q_r[blocks[b], :] * eq_ref[...]), st_ret[b], tc) for b in seqs]
    for b in seqs:
        upd = _state_outer(v_r[blocks[b], :], _head_stack(k_r[blocks[b], :] * ek_ref[...]))
        st_ret[b] = st_ret[b] * gblk_ref[...] + upd
    o_ret = jnp.concatenate([ret_os[b] + _head_unstack(ret_o4[b], tc) for b in seqs], axis=0)

    q_h, k_h, v_h, b_h = qkvb[0]
    q_h[...] = _silu(blk(HGRN_OFF, BRANCH))
    ff = blk(HGRN_OFF + 256, BRANCH)
    y = vec(V_LB_B) + _log_sigmoid(ff)
    a_lb = jnp.broadcast_to(vec(V_LB_A), y.shape)
    mx = jnp.maximum(a_lb, y)
    log_f = mx + _log1p_exp(jnp.minimum(a_lb, y) - mx)
    k_h[...] = vec(V_LB_OM) * _sigmoid(-ff)
    v_h[...] = blk(HGRN_OFF + 512, BRANCH)
    b_h[...] = _chunk_cumsum(log_f, c)

    q_g, k_g, v_g, b_g = qkvb[1]
    q_g[...] = blk(GLA_OFF, GLA_KEY_WIDTH) * (GLA_KEY_DIM ** -0.5)
    k_g[...] = blk(GLA_OFF + 128, GLA_KEY_WIDTH)
    v_g[...] = blk(GLA_OFF + 256, BRANCH)
    z = _dot(blk(GLA_OFF + 768, 128), wg_ref[...]) + bgate_ref[...]
    b_g[...] = _chunk_cumsum(_log_sigmoid(z) / GLA_TAU, c)

    p_rw = blk(RWKV_OFF, RWKV_PW)
    mu = mu_ref[...]
    ps_ref[...] = p_rw + (pltpu.roll(p_rw, 1, 0) - p_rw) * mu
    for b in range(bb):
        p0 = p_rw[b * tc:b * tc + 1, :]
        ps_ref[b * tc:b * tc + 1, :] = p0 + (carry[b:b + 1, :] - p0) * mu
        carry[b:b + 1, :] = p_rw[(b + 1) * tc - 1:(b + 1) * tc, :]

    rr = ps_ref[:, 0:256]
    k_raw = ps_ref[:, 256:512]
    vv = ps_ref[:, 512:768]
    low_rank = ps_ref[:, 1024:1152]
    w_pre = vec(V_W0) + _dot(jnp.tanh(low_rank), w2_ref[...])
    w_log = -(jnp.maximum(-w_pre, 0.0) + _log1p_exp(-jnp.abs(w_pre))) - 0.5
    log_w = -jnp.exp(w_log)
    a = _sigmoid(vec(V_A0) + _dot(low_rank, a2_ref[...]))
    if layer0:
        vfirst_out_ref[...] = vv.reshape(bb, tc, BRANCH)
    else:
        vmix = _sigmoid(vec(V_V0) + _dot(_dot(vv, v1_ref[...]), v2_ref[...]))
        vv = vv + (vfirst_in_ref[...].reshape(r, BRANCH) - vv) * vmix
    kk_raw = k_raw * vec(V_KK)
    kk = kk_raw / jnp.maximum(jnp.sqrt(head_sum(kk_raw * kk_raw, two_pass=True)), 1e-12)
    kmod = k_raw * (1.0 + (a - 1.0) * vec(V_KA))
    b_w = _chunk_cumsum(log_w, c)
    inv_p = jnp.exp(-b_w)
    p_w = jnp.exp(b_w)
    for ref, arr in zip(rw, (kk * jnp.exp(b_w - log_w), rr * p_w, kmod * inv_p, kk * a * inv_p, vv, p_w)):
        ref[...] = arr

    n_c = tc // c
    mix = ((qkvb[0], o_gl[0], st_hgrn, bo_ref, HEAD_DIM), (qkvb[1], o_gl[1], st_gla, bog_ref, GLA_KEY_DIM))

    def chunk_body(ci, _):
        rows = [pl.ds(pl.multiple_of(b * tc + ci * c, c), c) for b in seqs]

        def all_scores(tiles, bo_m):
            sc = _dot(jnp.concatenate(tiles, axis=0), bo_m)
            n = len(tiles) // bb * SUBLANES
            return [sc[b * n:(b + 1) * n, :] for b in seqs]

        gl_ins = [[(q_s[rows[b], :], k_s[rows[b], :], v_s[rows[b], :], b_s[rows[b], :]) for b in seqs]
                  for (q_s, k_s, v_s, b_s), _, _, _, _ in mix]
        rw_in = [tuple(x[rows[b], :] for x in rw) for b in seqs]
        rw_lhs = [_head_stack(jnp.concatenate([kkt, rt], axis=0)) for kkt, rt, _, _, _, _ in rw_in]
        rw_cmp = [_rwkv_compact(rw_lhs[b], rw_in[b][2], rw_in[b][3]) for b in seqs]
        gl_os = [[_state_read(_head_stack(q * jnp.exp(b_c), dk), st_ref[b], c)
                  for b, (q, _, _, b_c) in enumerate(ins)] for ins, (_, _, st_ref, _, dk) in zip(gl_ins, mix)]
        rw_fs = [_state_read(rw_lhs[b], st_rw[b], 2 * c) for b in seqs]
        tiles = []
        for kkt, _, _, ahat, _, _ in rw_in:
            t, rw_where = _rwkv_tiles(kkt, ahat, c)
            tiles += t
        rw_sc = all_scores(tiles, bo)
        gl = []
        for ins, (_, o_s, st_ref, bo_m, dk) in zip(gl_ins, mix):
            tiles = []
            for q, k, _, b_c in ins:
                t, where_ = _gl_tiles(q, k, b_c, c)
                tiles += t
            gl.append((ins, all_scores(tiles, bo_m[...]), where_, o_s, st_ref, dk))
        rw_m = [_rwkv_masks(rw_cmp[b], c) for b in seqs]
        rw_fv = [_head_unstack(_dot(rw_m[b][0], rw_in[b][4]), 2 * c) for b in seqs]
        for ins, _, _, _, st_ref, _ in gl:
            for b, (_, k, v, b_c) in enumerate(ins):
                st_ref[b] = _gl_state(k, v, b_c, st_ref[b], c)
        rhs = [rw_fs[b] + rw_fv[b] for b in seqs]
        us = [_rwkv_solve(rw_sc[b], rhs[b][0:c, :], rw_where, c) for b in seqs]
        rw_ou = [_dot(rw_m[b][1], us[b]) for b in seqs]
        for b in seqs:
            _, _, khat, ahat, v_w, p_c = rw_in[b]
            st_rw[b] = _rwkv_state(us[b], v_w, khat, ahat, p_c[c - 1:c, :], st_rw[b])
        for (ins, sc, where_, o_s, _, _), o_state in zip(gl, gl_os):
            for b in seqs:
                o_s[rows[b], :] = _gl_output(sc[b], o_state[b], ins[b][2], where_, c)
        for b in seqs:
            orw_ref[rows[b], :] = rhs[b][c:2 * c, :] - _head_unstack(rw_ou[b], c)
        return 0

    lax.fori_loop(0, n_c, chunk_body, 0)

    def group_norm(o, eps):
        cen = o - head_mean(o, two_pass=True)
        return cen * lax.rsqrt(head_mean(cen * cen) + eps)

    def head_rms(o, g):
        return o * lax.rsqrt(head_mean(o * o) + RMS_EPS) * g

    def put(lo, val):
        ocat_ref[:, :, lo:lo + BRANCH] = val.reshape(bb, tc, BRANCH)

    put(0, group_norm(o_ret, RET_GN_EPS) * _silu(blk(RET_OFF + 768, BRANCH)))
    put(256, head_rms(o_gl[0][...], vec(V_HGRN_G)) * _silu(blk(HGRN_OFF + 768, BRANCH)))
    put(512, head_rms(o_gl[1][...], vec(V_GLA_G)) * _silu(blk(GLA_OFF + 512, BRANCH)))
    o_rw = group_norm(orw_ref[...], RWKV_GN_EPS) * vec(V_GN_G) + vec(V_GN_B)
    bonus = head_sum(rr * kmod * vec(V_RK))
    put(768, (o_rw + bonus * vv) * _silu(ps_ref[:, 768:1024]))

    @pl.when(ti == n_t - 1)
    def _final():
        for b in range(bb):
            for st_ref, _, out_ref in key_value_io:
                out_ref[b] = st_ref[b].T.reshape(out_ref.shape[1:])
            for h in range(N_HEADS):
                o_rw_st[b, h] = st_rw[b][:, h * HEAD_DIM:(h + 1) * HEAD_DIM]


def _mixers(proj, vfirst, cos, sin, states, layer, lp, consts, *, bb, tc, c):
    bsz, seq, _ = proj.shape
    layer0 = vfirst is None
    n_t = seq // tc
    r = bb * tc

    def full(arr):
        nd = arr.ndim
        return pl.BlockSpec(arr.shape, lambda bi, ti, _nd=nd: (0,) * _nd)

    def per_b(arr):
        nd = arr.ndim
        return pl.BlockSpec((None, bb) + arr.shape[2:], lambda bi, ti, _nd=nd: (layer, bi) + (0,) * (_nd - 2))

    def tok(width):
        return pl.BlockSpec((bb, tc, width), lambda bi, ti: (bi, ti, 0))

    ins, specs = [proj], [tok(D_IN_PAD)]
    if not layer0:
        ins.append(vfirst)
        specs.append(tok(BRANCH))
    ins += [cos, sin]
    specs += [pl.BlockSpec((tc, BRANCH), lambda bi, ti: (ti, 0))] * 2
    state_inputs = list(range(len(ins), len(ins) + 4))
    for s in states:
        ins.append(s)
        specs.append(per_b(s))
    small = [lp['vec'], lp['mu'], lp['bgate'], lp['wg'], lp['w2'], lp['a2']]
    if not layer0:
        small += [lp['v1'], lp['v2']]
    small += list(consts) + list(_retention_tables(tc))
    for s in small:
        ins.append(s)
        specs.append(full(s))

    out_shape = [jax.ShapeDtypeStruct((bsz, seq, D_MODEL), F32)]
    out_specs = [tok(D_MODEL)]
    if layer0:
        out_shape.append(jax.ShapeDtypeStruct((bsz, seq, BRANCH), F32))
        out_specs.append(tok(BRANCH))
    aliases = {}
    for idx, s in zip(state_inputs, states[:4]):
        aliases[idx] = len(out_shape)
        out_shape.append(jax.ShapeDtypeStruct(s.shape, F32))
        out_specs.append(per_b(s))

    scratch = [pltpu.VMEM((bb, HEAD_DIM, BRANCH), F32), pltpu.VMEM((bb, HEAD_DIM, BRANCH), F32),
               pltpu.VMEM((bb, HEAD_DIM, GLA_KEY_WIDTH), F32), pltpu.VMEM((bb, HEAD_DIM, BRANCH), F32),
               pltpu.VMEM((bb, RWKV_PW), F32)]
    for dkt in (BRANCH, GLA_KEY_WIDTH):
        scratch += [pltpu.VMEM((r, dkt), F32), pltpu.VMEM((r, dkt), F32), pltpu.VMEM((r, BRANCH), F32),
                    pltpu.VMEM((r, dkt), F32)]
    scratch += [pltpu.VMEM((r, BRANCH), F32)] * 2
    scratch += [pltpu.VMEM((r, RWKV_PW), F32), pltpu.VMEM((r, BRANCH), F32)]
    scratch += [pltpu.VMEM((r, BRANCH), F32)] * 6

    outs = pl.pallas_call(
        functools.partial(_mixers_kernel, bb=bb, tc=tc, c=c, layer0=layer0, n_t=n_t),
        grid=(bsz // bb, n_t),
        in_specs=specs,
        out_specs=out_specs,
        out_shape=out_shape,
        scratch_shapes=scratch,
        input_output_aliases=aliases,
        compiler_params=pltpu.CompilerParams(dimension_semantics=("arbitrary", "arbitrary"),
                                             vmem_limit_bytes=V7X_VMEM_LIMIT),
        name="mixers",
    )(*ins)
    if layer0:
        return outs[0], outs[1], list(outs[2:])
    return outs[0], None, list(outs[1:])


def _out_attn_kernel(x_ref, oc_ref, mk_ref, mv_ref, wout_ref, wq_ref, wo_ref, gx_ref, gf_ref, out_ref,
                     *, bbc, lc, final):
    r = bbc * lc
    x1 = x_ref[...].reshape(r, D_MODEL) + _dot(oc_ref[...].reshape(r, D_MODEL), wout_ref[...])
    q = _dot(_rms(x1, gx_ref[...]), wq_ref[...])
    seqs = range(bbc)
    scores = [_dot_nt(_head_stack(q[b * lc:(b + 1) * lc, :]), mk_ref[b]) * (HEAD_DIM ** -0.5) for b in seqs]
    probs = []
    for s in scores:
        e = jnp.exp(s - jnp.max(s, axis=-1, keepdims=True))
        probs.append(e / jnp.sum(e, axis=-1, keepdims=True))
    o4 = [_dot(probs[b], mv_ref[b]) for b in seqs]
    outs = [_head_unstack(o4[b], lc) for b in seqs]
    o = outs[0] if bbc == 1 else jnp.concatenate(outs, axis=0)
    x2 = x1 + _dot(o, wo_ref[...])
    if final:
        x2 = _rms(x2, gf_ref[...])
    out_ref[...] = x2.reshape(bbc, lc, D_MODEL)


def _out_attn(x, ocat, mem, lp, norm_f, *, bbc, lc, final):
    bsz, seq, _ = x.shape
    mk, mv, mk_spec, mv_spec = mem

    def tok(width):
        return pl.BlockSpec((bbc, lc, width), lambda bi, li: (bi, li, 0))

    def full(arr):
        nd = arr.ndim
        return pl.BlockSpec(arr.shape, lambda bi, li, _nd=nd: (0,) * _nd)

    small = [lp['w_out'], lp['wq'], lp['wo'], lp['norm_x'], norm_f]
    return pl.pallas_call(
        functools.partial(_out_attn_kernel, bbc=bbc, lc=lc, final=final),
        grid=(bsz // bbc, seq // lc),
        in_specs=[tok(D_MODEL), tok(D_MODEL), mk_spec, mv_spec] + [full(s) for s in small],
        out_specs=tok(D_MODEL),
        out_shape=jax.ShapeDtypeStruct((bsz, seq, D_MODEL), F32),
        compiler_params=pltpu.CompilerParams(dimension_semantics=("arbitrary", "arbitrary"),
                                             vmem_limit_bytes=V7X_VMEM_LIMIT),
        name="out_attn",
    )(x, ocat, mk, mv, *small)


def _rope_tables(pos0, seq):
    half = HEAD_DIM // 2
    inv = ROPE_BASE ** (-jnp.arange(half, dtype=F32) / half)
    pos = (pos0 + jnp.arange(seq, dtype=jnp.int32)).astype(F32)
    ang = pos[:, None] * inv[None, :]
    cos, sin = lax.optimization_barrier((jnp.cos(ang), jnp.sin(ang)))
    return (jnp.tile(jnp.concatenate([cos, cos], axis=-1), (1, N_HEADS)),
            jnp.tile(jnp.concatenate([-sin, sin], axis=-1), (1, N_HEADS)))


def _retention_tables(n):
    log_gamma = jnp.log1p(-jnp.exp2(-5.0 - jnp.arange(N_HEADS, dtype=F32)))
    j = jnp.arange(n, dtype=F32)
    diff = j[:, None] - j[None, :]
    dmat = jnp.where(diff[None] >= 0, jnp.exp(diff[None] * log_gamma[:, None, None]), 0.0)
    lg_lanes = jnp.repeat(log_gamma, HEAD_DIM)[None, :]
    e_q = jnp.exp((j[:, None] + 1.0) * lg_lanes)
    e_k = jnp.exp((n - 1.0 - j[:, None]) * lg_lanes)
    g_blk = jnp.exp(float(n) * lg_lanes)
    return dmat.reshape(N_HEADS * n, n), e_q, e_k, g_blk


def _rwkv_to_padded(t):
    pad = jnp.zeros(t.shape[:-1] + (RWKV_PW - RWKV_W,), t.dtype)
    return jnp.concatenate([t[..., 0:256], t[..., 288:544], t[..., 544:800], t[..., 832:1088],
                            t[..., 256:288], t[..., 800:832], pad], axis=-1)


def _rwkv_from_padded(t):
    return jnp.concatenate([t[..., 0:256], t[..., 1024:1056], t[..., 256:512], t[..., 512:768],
                            t[..., 1056:1088], t[..., 768:1024]], axis=-1)


def _pad_rows(m, rows, at=0):
    out = jnp.zeros((rows, m.shape[1]), m.dtype)
    return out.at[at:at + m.shape[0]].set(m)


def _layer_params(l, P):
    sm = jax.nn.softmax(P['hgrn_lb_logits'].astype(F32), axis=0)
    lb = (jnp.cumsum(sm, axis=0) - sm[0])[l]
    rows = [None] * N_VEC
    rows[V_LB_A] = jnp.log(jnp.maximum(lb, LB_FLOOR))
    rows[V_LB_B] = jnp.log1p(-lb)
    rows[V_LB_OM] = 1.0 - lb
    rows[V_HGRN_G] = jnp.tile(P['hgrn_norm'][l], N_HEADS)
    rows[V_GLA_G] = jnp.tile(P['gla_norm'][l], N_HEADS)
    rows[V_W0] = P['rwkv_w0'][l]
    rows[V_A0] = P['rwkv_a0'][l]
    rows[V_V0] = P['rwkv_v0'][l - 1] if l > 0 else jnp.zeros((BRANCH,), F32)
    rows[V_KK] = P['rwkv_k_k'][l]
    rows[V_KA] = P['rwkv_k_a'][l]
    rows[V_RK] = P['rwkv_r_k'][l].reshape(BRANCH)
    rows[V_GN_G] = P['rwkv_gn_g'][l]
    rows[V_GN_B] = P['rwkv_gn_b'][l]
    zero = jnp.zeros((BRANCH,), F32)
    vec = jnp.stack([zero if x is None else x.astype(F32) for x in rows], axis=0)

    lp = {
        'norm_mix': P['norm_mix'][l][None, :],
        'vec': vec,
        'mu': _rwkv_to_padded(P['rwkv_mu'][l])[None, :],
        'bgate': P['gla_b_gate'][l][None, :],
        'wg': _pad_rows(P['gla_w_gate2'][l], 128).astype(BF16),
        'w2': _pad_rows(P['rwkv_w2'][l], 128, 0).astype(BF16),
        'a2': _pad_rows(P['rwkv_a2'][l], 128, 32).astype(BF16),
        'w_out': P['w_out'][l].astype(BF16),
        'wq': P['wq_x'][l].astype(BF16),
        'wo': P['wo_x'][l].astype(BF16),
        'norm_x': P['norm_x'][l][None, :],
    }
    if l > 0:
        lp['v1'] = jnp.pad(P['rwkv_v1'][l - 1], ((0, 0), (0, 96))).astype(BF16)
        lp['v2'] = _pad_rows(P['rwkv_v2'][l - 1], 128).astype(BF16)
    return lp


def _consts():
    i256 = jnp.arange(BRANCH) // HEAD_DIM
    i128 = jnp.arange(GLA_KEY_WIDTH) // GLA_KEY_DIM
    bo = (i256[:, None] == i256[None, :])
    bog = (i128[:, None] == i256[None, :])
    lane = jnp.arange(BRANCH)
    partner = jnp.where(lane % HEAD_DIM < HEAD_DIM // 2, lane + HEAD_DIM // 2, lane - HEAD_DIM // 2)
    swap = (lane[:, None] == partner[None, :])
    return bo.astype(BF16), bog.astype(BF16), swap.astype(BF16)


def _run_trunk(x, pos0, mems, init, lps, w_in_all, norm_f, consts, *, bb, tc, c, bbc, lc, tm):
    bsz, seq, _ = x.shape
    cos, sin = _rope_tables(pos0, seq)
    new_shift = []
    states = [init['ret'], init['hgrn'], init['gla'], init['rwkv'], _rwkv_to_padded(init['shift'])]
    vfirst = None
    for l in range(DEPTH):
        lp = lps[l]
        proj = _in_proj(x.reshape(bsz * seq, D_MODEL), lp['norm_mix'], w_in_all, tm, layer=l)
        proj = proj.reshape(bsz, seq, D_IN_PAD)
        ocat, vf, updated = _mixers(proj, vfirst, cos, sin, states, l, lp, consts, bb=bb, tc=tc, c=c)
        states = updated + states[4:]
        if l == 0:
            vfirst = vf
        new_shift.append(_rwkv_from_padded(proj[:, seq - 1, RWKV_OFF:RWKV_OFF + RWKV_PW]))
        x = _out_attn(x, ocat, mems[l], lp, norm_f, bbc=bbc, lc=lc, final=(l == DEPTH - 1))
    return x, {'ret': states[0], 'hgrn': states[1], 'gla': states[2], 'rwkv': states[3],
               'shift': jnp.stack(new_shift, axis=0)}


def kernel(x_prompt, x_sample, mem_prompt, state_ret, state_hgrn, state_gla, state_rwkv, state_rwkv_shift,
           cache_mem_k, cache_mem_v, norm_mix, w_in, hgrn_lb_logits, hgrn_norm, gla_w_gate2, gla_b_gate,
           gla_norm, rwkv_mu, rwkv_w0, rwkv_w2, rwkv_a0, rwkv_a2, rwkv_v0, rwkv_v1, rwkv_v2, rwkv_k_k,
           rwkv_k_a, rwkv_r_k, rwkv_gn_g, rwkv_gn_b, w_out, norm_x, wq_x, wo_x, norm_mem, wk_x, wv_x, norm_f):
    P = {'norm_mix': norm_mix, 'hgrn_lb_logits': hgrn_lb_logits, 'hgrn_norm': hgrn_norm,
         'gla_w_gate2': gla_w_gate2, 'gla_b_gate': gla_b_gate, 'gla_norm': gla_norm,
         'rwkv_mu': rwkv_mu, 'rwkv_w0': rwkv_w0, 'rwkv_w2': rwkv_w2, 'rwkv_a0': rwkv_a0,
         'rwkv_a2': rwkv_a2, 'rwkv_v0': rwkv_v0, 'rwkv_v1': rwkv_v1, 'rwkv_v2': rwkv_v2,
         'rwkv_k_k': rwkv_k_k, 'rwkv_k_a': rwkv_k_a, 'rwkv_r_k': rwkv_r_k,
         'rwkv_gn_g': rwkv_gn_g, 'rwkv_gn_b': rwkv_gn_b, 'w_out': w_out,
         'norm_x': norm_x, 'wq_x': wq_x, 'wo_x': wo_x}
    lps = [_layer_params(l, P) for l in range(DEPTH)]
    w_in_all = _w_in_layout(w_in)
    consts = _consts()
    norm_f2 = norm_f[None, :]
    bsz, seq, _ = x_prompt.shape
    dbsz, dseq, _ = x_sample.shape
    p_bbc, s_bbc = 2, 32

    mem2d = mem_prompt.reshape(bsz * N_MEM, D_MODEL)
    kv_l, p_mems = [], []
    for l in range(DEPTH):
        wkv = jnp.concatenate([wk_x[l], wv_x[l]], axis=1).astype(BF16)
        kv = _in_proj(mem2d, norm_mem[l][None, :], wkv, 256).reshape(bsz, N_MEM, 2 * BRANCH)
        kv_l.append(kv)
        p_mems.append((kv, kv, pl.BlockSpec((p_bbc, N_MEM, BRANCH), lambda bi, li: (bi, 0, 0)),
                       pl.BlockSpec((p_bbc, N_MEM, BRANCH), lambda bi, li: (bi, 0, 1))))
    zero_init = {
        'ret': jnp.zeros((DEPTH, bsz, N_HEADS, HEAD_DIM, HEAD_DIM), F32),
        'hgrn': jnp.zeros((DEPTH, bsz, N_HEADS, HEAD_DIM, HEAD_DIM), F32),
        'gla': jnp.zeros((DEPTH, bsz, N_HEADS, GLA_KEY_DIM, HEAD_DIM), F32),
        'rwkv': jnp.zeros((DEPTH, bsz, N_HEADS, HEAD_DIM, HEAD_DIM), F32),
        'shift': jnp.zeros((DEPTH, bsz, RWKV_W), F32),
    }
    y_prompt, sp = _run_trunk(x_prompt, 0, p_mems, zero_init, lps, w_in_all, norm_f2, consts,
                              bb=8, tc=64, c=CHUNK, bbc=p_bbc, lc=512, tm=512)

    cached = {'ret': state_ret, 'hgrn': state_hgrn, 'gla': state_gla, 'rwkv': state_rwkv,
              'shift': state_rwkv_shift}
    cmk = cache_mem_k.reshape(DEPTH, dbsz, N_MEM, BRANCH)
    cmv = cache_mem_v.reshape(DEPTH, dbsz, N_MEM, BRANCH)
    s_mems = []
    for l in range(DEPTH):
        spec = pl.BlockSpec((None, s_bbc, N_MEM, BRANCH), lambda bi, li, _l=l: (_l, bi, 0, 0))
        s_mems.append((cmk, cmv, spec, spec))
    y_sample, ss = _run_trunk(x_sample, PAST_LEN, s_mems, cached, lps, w_in_all, norm_f2, consts,
                              bb=16, tc=dseq, c=dseq, bbc=s_bbc, lc=dseq, tm=512)

    kv_all = jnp.stack(kv_l, axis=0)
    mem_k_p = kv_all[..., :BRANCH].reshape(DEPTH, bsz, N_MEM, N_HEADS, HEAD_DIM)
    mem_v_p = kv_all[..., BRANCH:].reshape(DEPTH, bsz, N_MEM, N_HEADS, HEAD_DIM)
    return (y_prompt, y_sample, sp['ret'], ss['ret'], sp['hgrn'], ss['hgrn'], sp['gla'], ss['gla'],
            sp['rwkv'], ss['rwkv'], sp['shift'], ss['shift'], mem_k_p, mem_v_p)
```

```python
import functools

import jax
import jax.numpy as jnp
from jax import lax
from jax.experimental import pallas as pl
from jax.experimental.pallas import tpu as pltpu

F32 = jnp.float32
BF16 = jnp.bfloat16

D_MODEL = 1024
DEPTH = 2
PAST_LEN = 16384
N_HEADS = 4
HEAD_DIM = 64
BRANCH = N_HEADS * HEAD_DIM
GLA_KEY_DIM = 32
GLA_KEY_WIDTH = N_HEADS * GLA_KEY_DIM
GLA_TAU = 16.0
N_MEM = 256
ROPE_BASE = 10000.0
RMS_EPS = 1e-6
RET_GN_EPS = 1e-5
RWKV_GN_EPS = 64e-5
LB_FLOOR = 1e-20
RWKV_W = 1088
CHUNK = 16

RET_OFF = 0
HGRN_OFF = 1024
GLA_OFF = 2048
RWKV_OFF = 2944
RWKV_PW = 1152
D_IN_PAD = 4096

V7X_VMEM_LIMIT = 60 * 1024 * 1024
SUBLANES = 8
NEG_BIG = -1e30

(V_LB_A, V_LB_B, V_LB_OM, V_HGRN_G, V_GLA_G, V_W0, V_A0, V_V0, V_KK, V_KA, V_RK,
 V_GN_G, V_GN_B) = range(13)
N_VEC = 16


def _rms(x, g):
    return x * lax.rsqrt(jnp.mean(x * x, axis=-1, keepdims=True) + RMS_EPS) * g


def _sigmoid(x):
    return 0.5 * jnp.tanh(0.5 * x) + 0.5


def _silu(x):
    h = 0.5 * x
    return h * jnp.tanh(h) + h


def _log1p_exp(x):
    return jnp.log(1.0 + jnp.exp(x))


def _log_sigmoid(x):
    return jnp.minimum(x, 0.0) - _log1p_exp(-jnp.abs(x))


def _dot(a, b):
    return jnp.dot(a.astype(BF16), b.astype(BF16), preferred_element_type=F32)


def _dot_nt(a, b):
    return lax.dot_general(a.astype(BF16), b.astype(BF16), (((1,), (1,)), ((), ())),
                           preferred_element_type=F32)


def _dot_tn(a, b):
    return lax.dot_general(a.astype(BF16), b.astype(BF16), (((0,), (0,)), ((), ())),
                           preferred_element_type=F32)


def _row_tiles(x, n):
    return [x[i * SUBLANES:(i + 1) * SUBLANES, :] for i in range(n)]


def _w_in_layout_kernel(w_ref, o_ref):
    w = w_ref[...]
    zeros = lambda n: jnp.zeros((w.shape[0], n), w.dtype)
    o_ref[...] = jnp.concatenate([
        w[:, 0:2560],
        w[:, 2576:2832], w[:, 2560:2576], zeros(112),
        w[:, 2832:3088], w[:, 3120:3376], w[:, 3376:3632], w[:, 3664:3920],
        w[:, 3088:3120], w[:, 3632:3664], zeros(64)], axis=1).astype(BF16)


def _w_in_layout(w_in):
    depth, d, n = w_in.shape
    rows = 128
    return pl.pallas_call(
        _w_in_layout_kernel,
        grid=(depth, d // rows),
        in_specs=[pl.BlockSpec((None, rows, n), lambda l, i: (l, i, 0))],
        out_specs=pl.BlockSpec((None, rows, D_IN_PAD), lambda l, i: (l, i, 0)),
        out_shape=jax.ShapeDtypeStruct((depth, d, D_IN_PAD), BF16),
        compiler_params=pltpu.CompilerParams(dimension_semantics=("arbitrary", "arbitrary")),
        name="w_in_layout",
    )(w_in)


def _in_proj_kernel(x_ref, g_ref, w_ref, o_ref):
    o_ref[...] = _dot(_rms(x_ref[...], g_ref[...]), w_ref[...])


def _in_proj(x2d, gain, w_bf16, tm, layer=None):
    m, d = x2d.shape
    n = w_bf16.shape[-1]
    w_spec = (pl.BlockSpec((d, n), lambda i: (0, 0)) if layer is None
              else pl.BlockSpec((None, d, n), lambda i: (layer, 0, 0)))
    return pl.pallas_call(
        _in_proj_kernel,
        grid=(m // tm,),
        in_specs=[pl.BlockSpec((tm, d), lambda i: (i, 0)),
                  pl.BlockSpec((1, d), lambda i: (0, 0)),
                  w_spec],
        out_specs=pl.BlockSpec((tm, n), lambda i: (i, 0)),
        out_shape=jax.ShapeDtypeStruct((m, n), F32),
        compiler_params=pltpu.CompilerParams(dimension_semantics=("arbitrary",),
                                             vmem_limit_bytes=V7X_VMEM_LIMIT),
        name="in_proj",
    )(x2d, gain, w_bf16)


def _chunk_cumsum(x, c):
    t_local = jnp.bitwise_and(lax.broadcasted_iota(jnp.int32, x.shape, 0), c - 1)
    s = 1
    while s < c:
        x = x + jnp.where(t_local >= s, pltpu.roll(x, s, 0), 0.0)
        s *= 2
    return x


def _head_stack(x, dk=HEAD_DIM):
    lane_head = jnp.right_shift(lax.broadcasted_iota(jnp.int32, x.shape, 1), dk.bit_length() - 1)
    return jnp.concatenate([jnp.where(lane_head == h, x, 0.0) for h in range(N_HEADS)], axis=0)


def _head_unstack(x4, n):
    lane_head = jnp.right_shift(lax.broadcasted_iota(jnp.int32, (n, BRANCH), 1), 6)
    out = jnp.where(lane_head == 0, x4[0:n, :], 0.0)
    for h in range(1, N_HEADS):
        out = out + jnp.where(lane_head == h, x4[h * n:(h + 1) * n, :], 0.0)
    return out


def _heads_to_rows(x):
    return jnp.concatenate([x[:, h * HEAD_DIM:(h + 1) * HEAD_DIM] for h in range(N_HEADS)], axis=0)


def _rows_to_heads(x4, n):
    return jnp.concatenate([x4[h * n:(h + 1) * n, :] for h in range(N_HEADS)], axis=1)


def _state_read(lhs_stacked, st, n):
    return _rows_to_heads(_dot_nt(lhs_stacked, st), n)


def _state_outer(values, keys_stacked):
    return _dot_tn(_heads_to_rows(values), keys_stacked)


def _gl_tiles(q, k, b, c):
    n_rt = c // SUBLANES
    rows = lax.broadcasted_iota(jnp.int32, (SUBLANES, q.shape[1]), 0)
    q_t, b_t = _row_tiles(q, n_rt), _row_tiles(b, n_rt)
    tiles, where_ = [], {}
    for m in range(c):
        for rt in range(m // SUBLANES, n_rt):
            arg = b_t[rt] - b[m:m + 1, :]
            if rt == m // SUBLANES:
                arg = jnp.where(rows >= m % SUBLANES, arg, NEG_BIG)
            where_[m, rt] = len(tiles) * SUBLANES
            tiles.append(q_t[rt] * k[m:m + 1, :] * jnp.exp(arg))
    return tiles, where_


def _gl_output(scores, o_state, v, where_, c):
    o = _row_tiles(o_state, c // SUBLANES)
    for (m, rt), off in where_.items():
        o[rt] = o[rt] + scores[off:off + SUBLANES, :] * v[m:m + 1, :]
    return jnp.concatenate(o, axis=0)


def _gl_state(k, v, b, st, c):
    b_last = b[c - 1:c, :]
    dk = k.shape[1] // N_HEADS
    return st * jnp.exp(b_last) + _state_outer(v, _head_stack(k * jnp.exp(b_last - b), dk))


def _rwkv_tiles(kkt, ahat, c):
    n_rt = c // SUBLANES
    rows = lax.broadcasted_iota(jnp.int32, (SUBLANES, BRANCH), 0)
    kk_t = _row_tiles(kkt, n_rt)
    tiles, where_ = [], {}
    for m in range(c - 1):
        for rt_i in range(m // SUBLANES, n_rt):
            on_diag = rt_i == m // SUBLANES
            if on_diag and m % SUBLANES == SUBLANES - 1:
                continue
            kk_m = jnp.where(rows > m % SUBLANES, kk_t[rt_i], 0.0) if on_diag else kk_t[rt_i]
            where_[m, rt_i] = len(tiles) * SUBLANES
            tiles.append(kk_m * ahat[m:m + 1, :])
    return tiles, where_


def _rwkv_compact(lhs_stacked, khat, ahat):
    return _dot_nt(lhs_stacked, jnp.concatenate([khat, ahat], axis=0))


def _rwkv_masks(sc, c):
    row = lax.broadcasted_iota(jnp.int32, (2 * N_HEADS * c, c), 0)
    col = lax.broadcasted_iota(jnp.int32, (2 * N_HEADS * c, c), 1)
    keep = col < jnp.bitwise_and(row, c - 1) + jnp.where(jnp.bitwise_and(row, c) != 0, 1, 0)
    r_a = jnp.concatenate([sc[(2 * h + 1) * c:(2 * h + 2) * c, c:2 * c] for h in range(N_HEADS)], axis=0)
    row4 = lax.broadcasted_iota(jnp.int32, (N_HEADS * c, c), 0)
    col4 = lax.broadcasted_iota(jnp.int32, (N_HEADS * c, c), 1)
    return jnp.where(keep, sc[:, 0:c], 0.0), jnp.where(col4 <= jnp.bitwise_and(row4, c - 1), r_a, 0.0)


def _rwkv_solve(sc_a, rhs, where_, c):
    n_rt = c // SUBLANES
    u = _row_tiles(rhs, n_rt)
    for s in range(c - 1):
        u_s = u[s // SUBLANES][s % SUBLANES:s % SUBLANES + 1, :]
        for rt_i in range(s // SUBLANES, n_rt):
            if (s, rt_i) in where_:
                off = where_[s, rt_i]
                u[rt_i] = u[rt_i] - sc_a[off:off + SUBLANES, :] * u_s
    return jnp.concatenate(u, axis=0)


def _rwkv_state(u, v, khat, ahat, p_last, st):
    return st * p_last + _state_outer(jnp.concatenate([v, -u], axis=0),
                                      _head_stack(jnp.concatenate([khat * p_last, ahat * p_last], axis=0)))


def _mixers_kernel(*refs, bb, tc, c, layer0, n_t):
    it = iter(refs)
    proj_ref = next(it)
    vfirst_in_ref = None if layer0 else next(it)
    cos_ref, sin_ref = next(it), next(it)
    s_ret0, s_hgrn0, s_gla0, s_rw0, shift0 = next(it), next(it), next(it), next(it), next(it)
    vec_ref, mu_ref, bgate_ref = next(it), next(it), next(it)
    wg_ref, w2_ref, a2_ref = next(it), next(it), next(it)
    v1_ref, v2_ref = (None, None) if layer0 else (next(it), next(it))
    bo_ref, bog_ref, swap_ref = next(it), next(it), next(it)
    dmat_ref, eq_ref, ek_ref, gblk_ref = next(it), next(it), next(it), next(it)
    ocat_ref = next(it)
    vfirst_out_ref = next(it) if layer0 else None
    o_ret_st, o_hgrn_st, o_gla_st, o_rw_st = next(it), next(it), next(it), next(it)
    st_ret, st_hgrn, st_gla, st_rw, carry = next(it), next(it), next(it), next(it), next(it)
    qkvb = [[next(it) for _ in range(4)] for _ in range(2)]
    o_gl = [next(it) for _ in range(2)]
    ps_ref, orw_ref = next(it), next(it)
    rw = [next(it) for _ in range(6)]

    r = bb * tc
    ti = pl.program_id(1)
    key_value_io = ((st_ret, s_ret0, o_ret_st), (st_hgrn, s_hgrn0, o_hgrn_st), (st_gla, s_gla0, o_gla_st))

    def vec(i):
        return vec_ref[i:i + 1, :]

    @pl.when(ti == 0)
    def _init():
        for b in range(bb):
            for st_ref, s0_ref, _ in key_value_io:
                st_ref[b] = s0_ref[b].reshape(st_ref.shape[2], HEAD_DIM).T
            for h in range(N_HEADS):
                st_rw[b, :, h * HEAD_DIM:(h + 1) * HEAD_DIM] = s_rw0[b, h]
        carry[...] = shift0[...]

    def blk(lo, width):
        return proj_ref[:, :, lo:lo + width].reshape(r, width)

    bo = bo_ref[...]
    bo2 = jnp.concatenate([bo, bo], axis=0)

    def head_sum(x, two_pass=False):
        if not two_pass:
            return _dot(x, bo)
        hi = x.astype(BF16)
        lo = (x - hi.astype(F32)).astype(BF16)
        return jnp.dot(jnp.concatenate([hi, lo], axis=1), bo2, preferred_element_type=F32)

    def head_mean(x, two_pass=False):
        return head_sum(x, two_pass) * (1.0 / HEAD_DIM)

    cos = jnp.concatenate([cos_ref[...]] * bb, axis=0)
    sin = jnp.concatenate([sin_ref[...]] * bb, axis=0)
    def rope(x):
        return x * cos + _dot(x, swap_ref[...]) * sin

    q_r = rope(blk(RET_OFF, BRANCH))
    k_r = rope(blk(RET_OFF + 256, BRANCH)) * (HEAD_DIM ** -0.5)
    v_r = blk(RET_OFF + 512, BRANCH)
    seqs = range(bb)
    blocks = [slice(b * tc, (b + 1) * tc) for b in seqs]
    ret_os = [_state_read(_head_stack(q_r[blocks[b], :] * eq_ref[...]), st_ret[b], tc) for b in seqs]
    ret_sc = [_dot_nt(_head_stack(q_r[rows, :]), k_r[rows, :]) * dmat_ref[...] for rows in blocks]
    ret_o4 = [_dot(ret_sc[b], v_r[blocks[b], :]) for b in seqs]
    for b in seqs:
        upd = _state_outer(v_r[blocks[b], :], _head_stack(k_r[blocks[b], :] * ek_ref[...]))
        st_ret[b] = st_ret[b] * gblk_ref[...] + upd
    o_ret = jnp.concatenate([ret_os[b] + _head_unstack(ret_o4[b], tc) for b in seqs], axis=0)

    q_h, k_h, v_h, b_h = qkvb[0]
    a_lb = jnp.broadcast_to(vec(V_LB_A), (tc, BRANCH))
    for b, rows in enumerate(blocks):
        q_h[rows, :] = _silu(proj_ref[b, :, HGRN_OFF:HGRN_OFF + BRANCH])
        ff = proj_ref[b, :, HGRN_OFF + 256:HGRN_OFF + 512]
        y = vec(V_LB_B) + _log_sigmoid(ff)
        mx = jnp.maximum(a_lb, y)
        log_f = mx + _log1p_exp(jnp.minimum(a_lb, y) - mx)
        k_h[rows, :] = vec(V_LB_OM) * _sigmoid(-ff)
        v_h[rows, :] = proj_ref[b, :, HGRN_OFF + 512:HGRN_OFF + 768]
        b_h[rows, :] = _chunk_cumsum(log_f, c)

    q_g, k_g, v_g, b_g = qkvb[1]
    q_g[...] = blk(GLA_OFF, GLA_KEY_WIDTH) * (GLA_KEY_DIM ** -0.5)
    k_g[...] = blk(GLA_OFF + 128, GLA_KEY_WIDTH)
    v_g[...] = blk(GLA_OFF + 256, BRANCH)
    z = _dot(blk(GLA_OFF + 768, 128), wg_ref[...]) + bgate_ref[...]
    b_g[...] = _chunk_cumsum(_log_sigmoid(z) / GLA_TAU, c)

    mu = mu_ref[...]
    for b, rows in enumerate(blocks):
        p_rw = proj_ref[b, :, RWKV_OFF:RWKV_OFF + RWKV_PW]
        ps_ref[rows, :] = p_rw + (pltpu.roll(p_rw, 1, 0) - p_rw) * mu
        p0 = p_rw[0:1, :]
        ps_ref[b * tc:b * tc + 1, :] = p0 + (carry[b:b + 1, :] - p0) * mu
        carry[b:b + 1, :] = p_rw[tc - 1:tc, :]

    rr = ps_ref[:, 0:256]
    k_raw = ps_ref[:, 256:512]
    vv = ps_ref[:, 512:768]
    low_rank = ps_ref[:, 1024:1152]
    w_pre = vec(V_W0) + _dot(jnp.tanh(low_rank), w2_ref[...])
    w_log = -(jnp.maximum(-w_pre, 0.0) + _log1p_exp(-jnp.abs(w_pre))) - 0.5
    log_w = -jnp.exp(w_log)
    a = _sigmoid(vec(V_A0) + _dot(low_rank, a2_ref[...]))
    if layer0:
        vfirst_out_ref[...] = vv.reshape(bb, tc, BRANCH)
    else:
        vmix = _sigmoid(vec(V_V0) + _dot(_dot(vv, v1_ref[...]), v2_ref[...]))
        vv = vv + (vfirst_in_ref[...].reshape(r, BRANCH) - vv) * vmix
    kk_raw = k_raw * vec(V_KK)
    kk = kk_raw / jnp.maximum(jnp.sqrt(head_sum(kk_raw * kk_raw, two_pass=True)), 1e-12)
    kmod = k_raw * (1.0 + (a - 1.0) * vec(V_KA))
    b_w = _chunk_cumsum(log_w, c)
    inv_p = jnp.exp(-b_w)
    p_w = jnp.exp(b_w)
    for ref, arr in zip(rw, (kk * jnp.exp(b_w - log_w), rr * p_w, kmod * inv_p, kk * a * inv_p, vv, p_w)):
        ref[...] = arr

    n_c = tc // c
    mix = ((qkvb[0], o_gl[0], st_hgrn, bo_ref, HEAD_DIM), (qkvb[1], o_gl[1], st_gla, bog_ref, GLA_KEY_DIM))

    def chunk_body(ci, _):
        rows = [pl.ds(pl.multiple_of(b * tc + ci * c, c), c) for b in seqs]

        def all_scores(tiles, bo_m):
            sc = _dot(jnp.concatenate(tiles, axis=0), bo_m)
            n = len(tiles) // bb * SUBLANES
            return [sc[b * n:(b + 1) * n, :] for b in seqs]

        gl_ins = [[(q_s[rows[b], :], k_s[rows[b], :], v_s[rows[b], :], b_s[rows[b], :]) for b in seqs]
                  for (q_s, k_s, v_s, b_s), _, _, _, _ in mix]
        rw_in = [tuple(x[rows[b], :] for x in rw) for b in seqs]
        rw_lhs = [_head_stack(jnp.concatenate([kkt, rt], axis=0)) for kkt, rt, _, _, _, _ in rw_in]
        rw_cmp = [_rwkv_compact(rw_lhs[b], rw_in[b][2], rw_in[b][3]) for b in seqs]
        gl_os = [[_state_read(_head_stack(q * jnp.exp(b_c), dk), st_ref[b], c)
                  for b, (q, _, _, b_c) in enumerate(ins)] for ins, (_, _, st_ref, _, dk) in zip(gl_ins, mix)]
        rw_fs = [_state_read(rw_lhs[b], st_rw[b], 2 * c) for b in seqs]
        tiles = []
        for kkt, _, _, ahat, _, _ in rw_in:
            t, rw_where = _rwkv_tiles(kkt, ahat, c)
            tiles += t
        rw_sc = all_scores(tiles, bo)
        gl = []
        for ins, (_, o_s, st_ref, bo_m, dk) in zip(gl_ins, mix):
            tiles = []
            for q, k, _, b_c in ins:
                t, where_ = _gl_tiles(q, k, b_c, c)
                tiles += t
            gl.append((ins, all_scores(tiles, bo_m[...]), where_, o_s, st_ref, dk))
        rw_m = [_rwkv_masks(rw_cmp[b], c) for b in seqs]
        rw_fv = [_head_unstack(_dot(rw_m[b][0], rw_in[b][4]), 2 * c) for b in seqs]
        for ins, _, _, _, st_ref, _ in gl:
            for b, (_, k, v, b_c) in enumerate(ins):
                st_ref[b] = _gl_state(k, v, b_c, st_ref[b], c)
        rhs = [rw_fs[b] + rw_fv[b] for b in seqs]
        us = [_rwkv_solve(rw_sc[b], rhs[b][0:c, :], rw_where, c) for b in seqs]
        rw_ou = [_dot(rw_m[b][1], us[b]) for b in seqs]
        for b in seqs:
            _, _, khat, ahat, v_w, p_c = rw_in[b]
            st_rw[b] = _rwkv_state(us[b], v_w, khat, ahat, p_c[c - 1:c, :], st_rw[b])
        for (ins, sc, where_, o_s, _, _), o_state in zip(gl, gl_os):
            for b in seqs:
                o_s[rows[b], :] = _gl_output(sc[b], o_state[b], ins[b][2], where_, c)
        for b in seqs:
            orw_ref[rows[b], :] = rhs[b][c:2 * c, :] - _head_unstack(rw_ou[b], c)
        return 0

    lax.fori_loop(0, n_c, chunk_body, 0)

    def group_norm(o, eps):
        cen = o - head_mean(o, two_pass=True)
        return cen * lax.rsqrt(head_mean(cen * cen) + eps)

    def head_rms(o, g):
        return o * lax.rsqrt(head_mean(o * o) + RMS_EPS) * g

    def put(lo, val):
        ocat_ref[:, :, lo:lo + BRANCH] = val.reshape(bb, tc, BRANCH)

    put(0, group_norm(o_ret, RET_GN_EPS) * _silu(blk(RET_OFF + 768, BRANCH)))
    put(256, head_rms(o_gl[0][...], vec(V_HGRN_G)) * _silu(blk(HGRN_OFF + 768, BRANCH)))
    put(512, head_rms(o_gl[1][...], vec(V_GLA_G)) * _silu(blk(GLA_OFF + 512, BRANCH)))
    o_rw = group_norm(orw_ref[...], RWKV_GN_EPS) * vec(V_GN_G) + vec(V_GN_B)
    bonus = head_sum(rr * kmod * vec(V_RK))
    put(768, (o_rw + bonus * vv) * _silu(ps_ref[:, 768:1024]))

    @pl.when(ti == n_t - 1)
    def _final():
        for b in range(bb):
            for st_ref, _, out_ref in key_value_io:
                out_ref[b] = st_ref[b].T.reshape(out_ref.shape[1:])
            for h in range(N_HEADS):
                o_rw_st[b, h] = st_rw[b][:, h * HEAD_DIM:(h + 1) * HEAD_DIM]


def _mixers(proj, vfirst, cos, sin, states, layer, lp, consts, *, bb, tc, c):
    bsz, seq, _ = proj.shape
    layer0 = vfirst is None
    n_t = seq // tc
    r = bb * tc

    def full(arr):
        nd = arr.ndim
        return pl.BlockSpec(arr.shape, lambda bi, ti, _nd=nd: (0,) * _nd)

    def per_b(arr):
        nd = arr.ndim
        return pl.BlockSpec((None, bb) + arr.shape[2:], lambda bi, ti, _nd=nd: (layer, bi) + (0,) * (_nd - 2))

    def tok(width):
        return pl.BlockSpec((bb, tc, width), lambda bi, ti: (bi, ti, 0))

    ins, specs = [proj], [tok(D_IN_PAD)]
    if not layer0:
        ins.append(vfirst)
        specs.append(tok(BRANCH))
    ins += [cos, sin]
    specs += [pl.BlockSpec((tc, BRANCH), lambda bi, ti: (ti, 0))] * 2
    state_inputs = list(range(len(ins), len(ins) + 4))
    for s in states:
        ins.append(s)
        specs.append(per_b(s))
    small = [lp['vec'], lp['mu'], lp['bgate'], lp['wg'], lp['w2'], lp['a2']]
    if not layer0:
        small += [lp['v1'], lp['v2']]
    small += list(consts) + list(_retention_tables(tc))
    for s in small:
        ins.append(s)
        specs.append(full(s))

    out_shape = [jax.ShapeDtypeStruct((bsz, seq, D_MODEL), F32)]
    out_specs = [tok(D_MODEL)]
    if layer0:
        out_shape.append(jax.ShapeDtypeStruct((bsz, seq, BRANCH), F32))
        out_specs.append(tok(BRANCH))
    aliases = {}
    for idx, s in zip(state_inputs, states[:4]):
        aliases[idx] = len(out_shape)
        out_shape.append(jax.ShapeDtypeStruct(s.shape, F32))
        out_specs.append(per_b(s))

    scratch = [pltpu.VMEM((bb, HEAD_DIM, BRANCH), F32), pltpu.VMEM((bb, HEAD_DIM, BRANCH), F32),
               pltpu.VMEM((bb, HEAD_DIM, GLA_KEY_WIDTH), F32), pltpu.VMEM((bb, HEAD_DIM, BRANCH), F32),
               pltpu.VMEM((bb, RWKV_PW), F32)]
    for dkt in (BRANCH, GLA_KEY_WIDTH):
        scratch += [pltpu.VMEM((r, dkt), F32), pltpu.VMEM((r, dkt), F32), pltpu.VMEM((r, BRANCH), F32),
                    pltpu.VMEM((r, dkt), F32)]
    scratch += [pltpu.VMEM((r, BRANCH), F32)] * 2
    scratch += [pltpu.VMEM((r, RWKV_PW), F32), pltpu.VMEM((r, BRANCH), F32)]
    scratch += [pltpu.VMEM((r, BRANCH), F32)] * 6

    outs = pl.pallas_call(
        functools.partial(_mixers_kernel, bb=bb, tc=tc, c=c, layer0=layer0, n_t=n_t),
        grid=(bsz // bb, n_t),
        in_specs=specs,
        out_specs=out_specs,
        out_shape=out_shape,
        scratch_shapes=scratch,
        input_output_aliases=aliases,
        compiler_params=pltpu.CompilerParams(dimension_semantics=("arbitrary", "arbitrary"),
                                             vmem_limit_bytes=V7X_VMEM_LIMIT),
        name="mixers",
    )(*ins)
    if layer0:
        return outs[0], outs[1], list(outs[2:])
    return outs[0], None, list(outs[1:])


def _out_attn_kernel(x_ref, oc_ref, mk_ref, mv_ref, wout_ref, wq_ref, wo_ref, gx_ref, gf_ref, out_ref,
                     *, bbc, lc, final):
    r = bbc * lc
    x1 = x_ref[...].reshape(r, D_MODEL) + _dot(oc_ref[...].reshape(r, D_MODEL), wout_ref[...])
    q = _dot(_rms(x1, gx_ref[...]), wq_ref[...])
    seqs = range(bbc)
    scores = [_dot_nt(_head_stack(q[b * lc:(b + 1) * lc, :]), mk_ref[b]) * (HEAD_DIM ** -0.5) for b in seqs]
    probs = []
    for s in scores:
        e = jnp.exp(s - jnp.max(s, axis=-1, keepdims=True))
        probs.append(e / jnp.sum(e, axis=-1, keepdims=True))
    o4 = [_dot(probs[b], mv_ref[b]) for b in seqs]
    outs = [_head_unstack(o4[b], lc) for b in seqs]
    o = outs[0] if bbc == 1 else jnp.concatenate(outs, axis=0)
    x2 = x1 + _dot(o, wo_ref[...])
    if final:
        x2 = _rms(x2, gf_ref[...])
    out_ref[...] = x2.reshape(bbc, lc, D_MODEL)


def _out_attn(x, ocat, mem, lp, norm_f, *, bbc, lc, final):
    bsz, seq, _ = x.shape
    mk, mv, mk_spec, mv_spec = mem

    def tok(width):
        return pl.BlockSpec((bbc, lc, width), lambda bi, li: (bi, li, 0))

    def full(arr):
        nd = arr.ndim
        return pl.BlockSpec(arr.shape, lambda bi, li, _nd=nd: (0,) * _nd)

    small = [lp['w_out'], lp['wq'], lp['wo'], lp['norm_x'], norm_f]
    return pl.pallas_call(
        functools.partial(_out_attn_kernel, bbc=bbc, lc=lc, final=final),
        grid=(bsz // bbc, seq // lc),
        in_specs=[tok(D_MODEL), tok(D_MODEL), mk_spec, mv_spec] + [full(s) for s in small],
        out_specs=tok(D_MODEL),
        out_shape=jax.ShapeDtypeStruct((bsz, seq, D_MODEL), F32),
        compiler_params=pltpu.CompilerParams(dimension_semantics=("arbitrary", "arbitrary"),
                                             vmem_limit_bytes=V7X_VMEM_LIMIT),
        name="out_attn",
    )(x, ocat, mk, mv, *small)


def _rope_tables(pos0, seq):
    half = HEAD_DIM // 2
    inv = ROPE_BASE ** (-jnp.arange(half, dtype=F32) / half)
    pos = (pos0 + jnp.arange(seq, dtype=jnp.int32)).astype(F32)
    ang = pos[:, None] * inv[None, :]
    cos, sin = lax.optimization_barrier((jnp.cos(ang), jnp.sin(ang)))
    return (jnp.tile(jnp.concatenate([cos, cos], axis=-1), (1, N_HEADS)),
            jnp.tile(jnp.concatenate([-sin, sin], axis=-1), (1, N_HEADS)))


def _retention_tables(n):
    log_gamma = jnp.log1p(-jnp.exp2(-5.0 - jnp.arange(N_HEADS, dtype=F32)))
    j = jnp.arange(n, dtype=F32)
    diff = j[:, None] - j[None, :]
    dmat = jnp.where(diff[None] >= 0, jnp.exp(diff[None] * log_gamma[:, None, None]), 0.0)
    lg_lanes = jnp.repeat(log_gamma, HEAD_DIM)[None, :]
    e_q = jnp.exp((j[:, None] + 1.0) * lg_lanes)
    e_k = jnp.exp((n - 1.0 - j[:, None]) * lg_lanes)
    g_blk = jnp.exp(float(n) * lg_lanes)
    return dmat.reshape(N_HEADS * n, n), e_q, e_k, g_blk


def _rwkv_to_padded(t):
    pad = jnp.zeros(t.shape[:-1] + (RWKV_PW - RWKV_W,), t.dtype)
    return jnp.concatenate([t[..., 0:256], t[..., 288:544], t[..., 544:800], t[..., 832:1088],
                            t[..., 256:288], t[..., 800:832], pad], axis=-1)


def _rwkv_from_padded(t):
    return jnp.concatenate([t[..., 0:256], t[..., 1024:1056], t[..., 256:512], t[..., 512:768],
                            t[..., 1056:1088], t[..., 768:1024]], axis=-1)


def _pad_rows(m, rows, at=0):
    out = jnp.zeros((rows, m.shape[1]), m.dtype)
    return out.at[at:at + m.shape[0]].set(m)


def _layer_params(l, P):
    sm = jax.nn.softmax(P['hgrn_lb_logits'].astype(F32), axis=0)
    lb = (jnp.cumsum(sm, axis=0) - sm[0])[l]
    rows = [None] * N_VEC
    rows[V_LB_A] = jnp.log(jnp.maximum(lb, LB_FLOOR))
    rows[V_LB_B] = jnp.log1p(-lb)
    rows[V_LB_OM] = 1.0 - lb
    rows[V_HGRN_G] = jnp.tile(P['hgrn_norm'][l], N_HEADS)
    rows[V_GLA_G] = jnp.tile(P['gla_norm'][l], N_HEADS)
    rows[V_W0] = P['rwkv_w0'][l]
    rows[V_A0] = P['rwkv_a0'][l]
    rows[V_V0] = P['rwkv_v0'][l - 1] if l > 0 else jnp.zeros((BRANCH,), F32)
    rows[V_KK] = P['rwkv_k_k'][l]
    rows[V_KA] = P['rwkv_k_a'][l]
    rows[V_RK] = P['rwkv_r_k'][l].reshape(BRANCH)
    rows[V_GN_G] = P['rwkv_gn_g'][l]
    rows[V_GN_B] = P['rwkv_gn_b'][l]
    zero = jnp.zeros((BRANCH,), F32)
    vec = jnp.stack([zero if x is None else x.astype(F32) for x in rows], axis=0)

    lp = {
        'norm_mix': P['norm_mix'][l][None, :],
        'vec': vec,
        'mu': _rwkv_to_padded(P['rwkv_mu'][l])[None, :],
        'bgate': P['gla_b_gate'][l][None, :],
        'wg': _pad_rows(P['gla_w_gate2'][l], 128).astype(BF16),
        'w2': _pad_rows(P['rwkv_w2'][l], 128, 0).astype(BF16),
        'a2': _pad_rows(P['rwkv_a2'][l], 128, 32).astype(BF16),
        'w_out': P['w_out'][l].astype(BF16),
        'wq': P['wq_x'][l].astype(BF16),
        'wo': P['wo_x'][l].astype(BF16),
        'norm_x': P['norm_x'][l][None, :],
    }
    if l > 0:
        lp['v1'] = jnp.pad(P['rwkv_v1'][l - 1], ((0, 0), (0, 96))).astype(BF16)
        lp['v2'] = _pad_rows(P['rwkv_v2'][l - 1], 128).astype(BF16)
    return lp


def _consts():
    i256 = jnp.arange(BRANCH) // HEAD_DIM
    i128 = jnp.arange(GLA_KEY_WIDTH) // GLA_KEY_DIM
    bo = (i256[:, None] == i256[None, :])
    bog = (i128[:, None] == i256[None, :])
    lane = jnp.arange(BRANCH)
    partner = jnp.where(lane % HEAD_DIM < HEAD_DIM // 2, lane + HEAD_DIM // 2, lane - HEAD_DIM // 2)
    swap = (lane[:, None] == partner[None, :])
    return bo.astype(BF16), bog.astype(BF16), swap.astype(BF16)


def _run_trunk(x, pos0, mems, init, lps, w_in_all, norm_f, consts, *, bb, tc, c, bbc, lc, tm):
    bsz, seq, _ = x.shape
    cos, sin = _rope_tables(pos0, seq)
    new_shift = []
    states = [init['ret'], init['hgrn'], init['gla'], init['rwkv'], _rwkv_to_padded(init['shift'])]
    vfirst = None
    for l in range(DEPTH):
        lp = lps[l]
        proj = _in_proj(x.reshape(bsz * seq, D_MODEL), lp['norm_mix'], w_in_all, tm, layer=l)
        proj = proj.reshape(bsz, seq, D_IN_PAD)
        ocat, vf, updated = _mixers(proj, vfirst, cos, sin, states, l, lp, consts, bb=bb, tc=tc, c=c)
        states = updated + states[4:]
        if l == 0:
            vfirst = vf
        new_shift.append(_rwkv_from_padded(proj[:, seq - 1, RWKV_OFF:RWKV_OFF + RWKV_PW]))
        x = _out_attn(x, ocat, mems[l], lp, norm_f, bbc=bbc, lc=lc, final=(l == DEPTH - 1))
    return x, {'ret': states[0], 'hgrn': states[1], 'gla': states[2], 'rwkv': states[3],
               'shift': jnp.stack(new_shift, axis=0)}


def kernel(x_prompt, x_sample, mem_prompt, state_ret, state_hgrn, state_gla, state_rwkv, state_rwkv_shift,
           cache_mem_k, cache_mem_v, norm_mix, w_in, hgrn_lb_logits, hgrn_norm, gla_w_gate2, gla_b_gate,
           gla_norm, rwkv_mu, rwkv_w0, rwkv_w2, rwkv_a0, rwkv_a2, rwkv_v0, rwkv_v1, rwkv_v2, rwkv_k_k,
           rwkv_k_a, rwkv_r_k, rwkv_gn_g, rwkv_gn_b, w_out, norm_x, wq_x, wo_x, norm_mem, wk_x, wv_x, norm_f):
    P = {'norm_mix': norm_mix, 'hgrn_lb_logits': hgrn_lb_logits, 'hgrn_norm': hgrn_norm,
         'gla_w_gate2': gla_w_gate2, 'gla_b_gate': gla_b_gate, 'gla_norm': gla_norm,
         'rwkv_mu': rwkv_mu, 'rwkv_w0': rwkv_w0, 'rwkv_w2': rwkv_w2, 'rwkv_a0': rwkv_a0,
         'rwkv_a2': rwkv_a2, 'rwkv_v0': rwkv_v0, 'rwkv_v1': rwkv_v1, 'rwkv_v2': rwkv_v2,
         'rwkv_k_k': rwkv_k_k, 'rwkv_k_a': rwkv_k_a, 'rwkv_r_k': rwkv_r_k,
         'rwkv_gn_g': rwkv_gn_g, 'rwkv_gn_b': rwkv_gn_b, 'w_out': w_out,
         'norm_x': norm_x, 'wq_x': wq_x, 'wo_x': wo_x}
    lps = [_layer_params(l, P) for l in range(DEPTH)]
    w_in_all = _w_in_layout(w_in)
    consts = _consts()
    norm_f2 = norm_f[None, :]
    bsz, seq, _ = x_prompt.shape
    dbsz, dseq, _ = x_sample.shape
    p_bbc, s_bbc = 2, 32

    mem2d = mem_prompt.reshape(bsz * N_MEM, D_MODEL)
    kv_l, p_mems = [], []
    for l in range(DEPTH):
        wkv = jnp.concatenate([wk_x[l], wv_x[l]], axis=1).astype(BF16)
        kv = _in_proj(mem2d, norm_mem[l][None, :], wkv, 256).reshape(bsz, N_MEM, 2 * BRANCH)
        kv_l.append(kv)
        p_mems.append((kv, kv, pl.BlockSpec((p_bbc, N_MEM, BRANCH), lambda bi, li: (bi, 0, 0)),
                       pl.BlockSpec((p_bbc, N_MEM, BRANCH), lambda bi, li: (bi, 0, 1))))
    zero_init = {
        'ret': jnp.zeros((DEPTH, bsz, N_HEADS, HEAD_DIM, HEAD_DIM), F32),
        'hgrn': jnp.zeros((DEPTH, bsz, N_HEADS, HEAD_DIM, HEAD_DIM), F32),
        'gla': jnp.zeros((DEPTH, bsz, N_HEADS, GLA_KEY_DIM, HEAD_DIM), F32),
        'rwkv': jnp.zeros((DEPTH, bsz, N_HEADS, HEAD_DIM, HEAD_DIM), F32),
        'shift': jnp.zeros((DEPTH, bsz, RWKV_W), F32),
    }
    y_prompt, sp = _run_trunk(x_prompt, 0, p_mems, zero_init, lps, w_in_all, norm_f2, consts,
                              bb=8, tc=64, c=CHUNK, bbc=p_bbc, lc=512, tm=512)

    cached = {'ret': state_ret, 'hgrn': state_hgrn, 'gla': state_gla, 'rwkv': state_rwkv,
              'shift': state_rwkv_shift}
    cmk = cache_mem_k.reshape(DEPTH, dbsz, N_MEM, BRANCH)
    cmv = cache_mem_v.reshape(DEPTH, dbsz, N_MEM, BRANCH)
    s_mems = []
    for l in range(DEPTH):
        spec = pl.BlockSpec((None, s_bbc, N_MEM, BRANCH), lambda bi, li, _l=l: (_l, bi, 0, 0))
        s_mems.append((cmk, cmv, spec, spec))
    y_sample, ss = _run_trunk(x_sample, PAST_LEN, s_mems, cached, lps, w_in_all, norm_f2, consts,
                              bb=16, tc=dseq, c=dseq, bbc=s_bbc, lc=dseq, tm=512)

    kv_all = jnp.stack(kv_l, axis=0)
    mem_k_p = kv_all[..., :BRANCH].reshape(DEPTH, bsz, N_MEM, N_HEADS, HEAD_DIM)
    mem_v_p = kv_all[..., BRANCH:].reshape(DEPTH, bsz, N_MEM, N_HEADS, HEAD_DIM)
    return (y_prompt, y_sample, sp['ret'], ss['ret'], sp['hgrn'], ss['hgrn'], sp['gla'], ss['gla'],
            sp['rwkv'], ss['rwkv'], sp['shift'], ss['shift'], mem_k_p, mem_v_p)
```

```python
import functools

import jax
import jax.numpy as jnp
from jax import lax
from jax.experimental import pallas as pl
from jax.experimental.pallas import tpu as pltpu

F32 = jnp.float32
BF16 = jnp.bfloat16

D_MODEL = 1024
DEPTH = 2
PAST_LEN = 16384
N_HEADS = 4
HEAD_DIM = 64
BRANCH = N_HEADS * HEAD_DIM
GLA_KEY_DIM = 32
GLA_KEY_WIDTH = N_HEADS * GLA_KEY_DIM
GLA_TAU = 16.0
N_MEM = 256
ROPE_BASE = 10000.0
RMS_EPS = 1e-6
RET_GN_EPS = 1e-5
RWKV_GN_EPS = 64e-5
LB_FLOOR = 1e-20
RWKV_W = 1088
CHUNK = 16

RET_OFF = 0
HGRN_OFF = 1024
GLA_OFF = 2048
RWKV_OFF = 2944
RWKV_PW = 1152
D_IN_PAD = 4096

V7X_VMEM_LIMIT = 60 * 1024 * 1024
SUBLANES = 8
NEG_BIG = -1e30

(V_LB_A, V_LB_B, V_LB_OM, V_HGRN_G, V_GLA_G, V_W0, V_A0, V_V0, V_KK, V_KA, V_RK,
 V_GN_G, V_GN_B) = range(13)
N_VEC = 16


def _rms(x, g):
    return x * lax.rsqrt(jnp.mean(x * x, axis=-1, keepdims=True) + RMS_EPS) * g


def _sigmoid(x):
    return 0.5 * jnp.tanh(0.5 * x) + 0.5


def _silu(x):
    h = 0.5 * x
    return h * jnp.tanh(h) + h


def _log1p_exp(x):
    return jnp.log(1.0 + jnp.exp(x))


def _log_sigmoid(x):
    return jnp.minimum(x, 0.0) - _log1p_exp(-jnp.abs(x))


def _dot(a, b):
    return jnp.dot(a.astype(BF16), b.astype(BF16), preferred_element_type=F32)


def _dot_nt(a, b):
    return lax.dot_general(a.astype(BF16), b.astype(BF16), (((1,), (1,)), ((), ())),
                           preferred_element_type=F32)


def _dot_tn(a, b):
    return lax.dot_general(a.astype(BF16), b.astype(BF16), (((0,), (0,)), ((), ())),
                           preferred_element_type=F32)


def _row_tiles(x, n):
    return [x[i * SUBLANES:(i + 1) * SUBLANES, :] for i in range(n)]


def _w_in_layout_kernel(w_ref, o_ref):
    w = w_ref[...]
    zeros = lambda n: jnp.zeros((w.shape[0], n), w.dtype)
    o_ref[...] = jnp.concatenate([
        w[:, 0:2560],
        w[:, 2576:2832], w[:, 2560:2576], zeros(112),
        w[:, 2832:3088], w[:, 3120:3376], w[:, 3376:3632], w[:, 3664:3920],
        w[:, 3088:3120], w[:, 3632:3664], zeros(64)], axis=1).astype(BF16)


def _w_in_layout(w_in):
    depth, d, n = w_in.shape
    rows = 128
    return pl.pallas_call(
        _w_in_layout_kernel,
        grid=(depth, d // rows),
        in_specs=[pl.BlockSpec((None, rows, n), lambda l, i: (l, i, 0))],
        out_specs=pl.BlockSpec((None, rows, D_IN_PAD), lambda l, i: (l, i, 0)),
        out_shape=jax.ShapeDtypeStruct((depth, d, D_IN_PAD), BF16),
        compiler_params=pltpu.CompilerParams(dimension_semantics=("arbitrary", "arbitrary")),
        name="w_in_layout",
    )(w_in)


def _in_proj_kernel(x_ref, g_ref, w_ref, o_ref):
    o_ref[...] = _dot(_rms(x_ref[...], g_ref[...]), w_ref[...])


def _in_proj(x2d, gain, w_bf16, tm, layer=None):
    m, d = x2d.shape
    n = w_bf16.shape[-1]
    w_spec = (pl.BlockSpec((d, n), lambda i: (0, 0)) if layer is None
              else pl.BlockSpec((None, d, n), lambda i: (layer, 0, 0)))
    return pl.pallas_call(
        _in_proj_kernel,
        grid=(m // tm,),
        in_specs=[pl.BlockSpec((tm, d), lambda i: (i, 0)),
                  pl.BlockSpec((1, d), lambda i: (0, 0)),
                  w_spec],
        out_specs=pl.BlockSpec((tm, n), lambda i: (i, 0)),
        out_shape=jax.ShapeDtypeStruct((m, n), F32),
        compiler_params=pltpu.CompilerParams(dimension_semantics=("arbitrary",),
                                             vmem_limit_bytes=V7X_VMEM_LIMIT),
        name="in_proj",
    )(x2d, gain, w_bf16)


def _chunk_cumsum(x, c):
    t_local = jnp.bitwise_and(lax.broadcasted_iota(jnp.int32, x.shape, 0), c - 1)
    s = 1
    while s < c:
        x = x + jnp.where(t_local >= s, pltpu.roll(x, s, 0), 0.0)
        s *= 2
    return x


def _head_stack(x, dk=HEAD_DIM):
    lane_head = jnp.right_shift(lax.broadcasted_iota(jnp.int32, x.shape, 1), dk.bit_length() - 1)
    return jnp.concatenate([jnp.where(lane_head == h, x, 0.0) for h in range(N_HEADS)], axis=0)


def _head_unstack(x4, n):
    lane_head = jnp.right_shift(lax.broadcasted_iota(jnp.int32, (n, BRANCH), 1), 6)
    out = jnp.where(lane_head == 0, x4[0:n, :], 0.0)
    for h in range(1, N_HEADS):
        out = out + jnp.where(lane_head == h, x4[h * n:(h + 1) * n, :], 0.0)
    return out


def _heads_to_rows(x):
    return jnp.concatenate([x[:, h * HEAD_DIM:(h + 1) * HEAD_DIM] for h in range(N_HEADS)], axis=0)


def _rows_to_heads(x4, n):
    return jnp.concatenate([x4[h * n:(h + 1) * n, :] for h in range(N_HEADS)], axis=1)


def _state_read(lhs_stacked, st, n):
    return _rows_to_heads(_dot_nt(lhs_stacked, st), n)


def _state_outer(values, keys_stacked):
    return _dot_tn(_heads_to_rows(values), keys_stacked)


def _gl_tiles(q, k, b, c):
    n_rt = c // SUBLANES
    rows = lax.broadcasted_iota(jnp.int32, (SUBLANES, q.shape[1]), 0)
    q_t, b_t = _row_tiles(q, n_rt), _row_tiles(b, n_rt)
    tiles, where_ = [], {}
    for m in range(c):
        rt = m // SUBLANES
        arg = jnp.where(rows >= m % SUBLANES, b_t[rt] - b[m:m + 1, :], NEG_BIG)
        where_[m, rt] = len(tiles) * SUBLANES
        tiles.append(q_t[rt] * k[m:m + 1, :] * jnp.exp(arg))
    return tiles, where_


def _gl_cross_scores(q, k, b, c):
    dk = q.shape[1] // N_HEADS
    blocks = []
    for rt in range(1, c // SUBLANES):
        e = rt * SUBLANES
        b_e = b[e - 1:e, :]
        q_hat = q[e:e + SUBLANES, :] * jnp.exp(b[e:e + SUBLANES, :] - b_e)
        k_hat = k[0:e, :] * jnp.exp(b_e - b[0:e, :])
        blocks.append(_dot_nt(_head_stack(q_hat, dk), k_hat))
    return blocks


def _gl_cross_output(cross_scores, v):
    return [_head_unstack(_dot(sc, v[0:sc.shape[1], :]), SUBLANES) for sc in cross_scores]


def _gl_output(scores, o_state, v, where_, cross, c):
    o = _row_tiles(o_state, c // SUBLANES)
    for (m, rt), off in where_.items():
        o[rt] = o[rt] + scores[off:off + SUBLANES, :] * v[m:m + 1, :]
    for rt, term in enumerate(cross, start=1):
        o[rt] = o[rt] + term
    return jnp.concatenate(o, axis=0)


def _gl_state(k, v, b, st, c):
    b_last = b[c - 1:c, :]
    dk = k.shape[1] // N_HEADS
    return st * jnp.exp(b_last) + _state_outer(v, _head_stack(k * jnp.exp(b_last - b), dk))


def _rwkv_tiles(kkt, ahat, c):
    n_rt = c // SUBLANES
    rows = lax.broadcasted_iota(jnp.int32, (SUBLANES, BRANCH), 0)
    kk_t = _row_tiles(kkt, n_rt)
    tiles, where_ = [], {}
    for m in range(c - 1):
        for rt_i in range(m // SUBLANES, n_rt):
            on_diag = rt_i == m // SUBLANES
            if on_diag and m % SUBLANES == SUBLANES - 1:
                continue
            kk_m = jnp.where(rows > m % SUBLANES, kk_t[rt_i], 0.0) if on_diag else kk_t[rt_i]
            where_[m, rt_i] = len(tiles) * SUBLANES
            tiles.append(kk_m * ahat[m:m + 1, :])
    return tiles, where_


def _rwkv_compact(lhs_stacked, khat, ahat):
    return _dot_nt(lhs_stacked, jnp.concatenate([khat, ahat], axis=0))


def _rwkv_masks(sc, c):
    row = lax.broadcasted_iota(jnp.int32, (2 * N_HEADS * c, c), 0)
    col = lax.broadcasted_iota(jnp.int32, (2 * N_HEADS * c, c), 1)
    keep = col < jnp.bitwise_and(row, c - 1) + jnp.where(jnp.bitwise_and(row, c) != 0, 1, 0)
    r_a = jnp.concatenate([sc[(2 * h + 1) * c:(2 * h + 2) * c, c:2 * c] for h in range(N_HEADS)], axis=0)
    row4 = lax.broadcasted_iota(jnp.int32, (N_HEADS * c, c), 0)
    col4 = lax.broadcasted_iota(jnp.int32, (N_HEADS * c, c), 1)
    return jnp.where(keep, sc[:, 0:c], 0.0), jnp.where(col4 <= jnp.bitwise_and(row4, c - 1), r_a, 0.0)


def _rwkv_solve(sc_a, rhs, where_, c):
    n_rt = c // SUBLANES
    u = _row_tiles(rhs, n_rt)
    for s in range(c - 1):
        u_s = u[s // SUBLANES][s % SUBLANES:s % SUBLANES + 1, :]
        for rt_i in range(s // SUBLANES, n_rt):
            if (s, rt_i) in where_:
                off = where_[s, rt_i]
                u[rt_i] = u[rt_i] - sc_a[off:off + SUBLANES, :] * u_s
    return jnp.concatenate(u, axis=0)


def _rwkv_state(u, v, khat, ahat, p_last, st):
    return st * p_last + _state_outer(jnp.concatenate([v, -u], axis=0),
                                      _head_stack(jnp.concatenate([khat * p_last, ahat * p_last], axis=0)))


def _mixers_kernel(*refs, bb, tc, c, layer0, n_t):
    it = iter(refs)
    proj_ref = next(it)
    vfirst_in_ref = None if layer0 else next(it)
    cos_ref, sin_ref = next(it), next(it)
    s_ret0, s_hgrn0, s_gla0, s_rw0, shift0 = next(it), next(it), next(it), next(it), next(it)
    vec_ref, mu_ref, bgate_ref = next(it), next(it), next(it)
    wg_ref, w2_ref, a2_ref = next(it), next(it), next(it)
    v1_ref, v2_ref = (None, None) if layer0 else (next(it), next(it))
    bo_ref, bog_ref, swap_ref = next(it), next(it), next(it)
    dmat_ref, eq_ref, ek_ref, gblk_ref = next(it), next(it), next(it), next(it)
    ocat_ref = next(it)
    vfirst_out_ref = next(it) if layer0 else None
    o_ret_st, o_hgrn_st, o_gla_st, o_rw_st = next(it), next(it), next(it), next(it)
    st_ret, st_hgrn, st_gla, st_rw, carry = next(it), next(it), next(it), next(it), next(it)
    qkvb = [[next(it) for _ in range(4)] for _ in range(2)]
    o_gl = [next(it) for _ in range(2)]
    ps_ref, orw_ref = next(it), next(it)
    rw = [next(it) for _ in range(6)]

    r = bb * tc
    ti = pl.program_id(1)
    key_value_io = ((st_ret, s_ret0, o_ret_st), (st_hgrn, s_hgrn0, o_hgrn_st), (st_gla, s_gla0, o_gla_st))

    def vec(i):
        return vec_ref[i:i + 1, :]

    @pl.when(ti == 0)
    def _init():
        for b in range(bb):
            for st_ref, s0_ref, _ in key_value_io:
                st_ref[b] = s0_ref[b].reshape(st_ref.shape[2], HEAD_DIM).T
            for h in range(N_HEADS):
                st_rw[b, :, h * HEAD_DIM:(h + 1) * HEAD_DIM] = s_rw0[b, h]
        carry[...] = shift0[...]

    def blk(lo, width):
        return proj_ref[:, :, lo:lo + width].reshape(r, width)

    bo = bo_ref[...]
    bo2 = jnp.concatenate([bo, bo], axis=0)

    def head_sum(x, two_pass=False):
        if not two_pass:
            return _dot(x, bo)
        hi = x.astype(BF16)
        lo = (x - hi.astype(F32)).astype(BF16)
        return jnp.dot(jnp.concatenate([hi, lo], axis=1), bo2, preferred_element_type=F32)

    def head_mean(x, two_pass=False):
        return head_sum(x, two_pass) * (1.0 / HEAD_DIM)

    cos = jnp.concatenate([cos_ref[...]] * bb, axis=0)
    sin = jnp.concatenate([sin_ref[...]] * bb, axis=0)
    def rope(x):
        return x * cos + _dot(x, swap_ref[...]) * sin

    q_r = rope(blk(RET_OFF, BRANCH))
    k_r = rope(blk(RET_OFF + 256, BRANCH)) * (HEAD_DIM ** -0.5)
    v_r = blk(RET_OFF + 512, BRANCH)
    seqs = range(bb)
    blocks = [slice(b * tc, (b + 1) * tc) for b in seqs]
    ret_os = [_state_read(_head_stack(q_r[blocks[b], :] * eq_ref[...]), st_ret[b], tc) for b in seqs]
    ret_sc = [_dot_nt(_head_stack(q_r[rows, :]), k_r[rows, :]) * dmat_ref[...] for rows in blocks]
    ret_o4 = [_dot(ret_sc[b], v_r[blocks[b], :]) for b in seqs]
    for b in seqs:
        upd = _state_outer(v_r[blocks[b], :], _head_stack(k_r[blocks[b], :] * ek_ref[...]))
        st_ret[b] = st_ret[b] * gblk_ref[...] + upd
    o_ret = jnp.concatenate([ret_os[b] + _head_unstack(ret_o4[b], tc) for b in seqs], axis=0)

    q_h, k_h, v_h, b_h = qkvb[0]
    a_lb = jnp.broadcast_to(vec(V_LB_A), (tc, BRANCH))
    for b, rows in enumerate(blocks):
        q_h[rows, :] = _silu(proj_ref[b, :, HGRN_OFF:HGRN_OFF + BRANCH])
        ff = proj_ref[b, :, HGRN_OFF + 256:HGRN_OFF + 512]
        y = vec(V_LB_B) + _log_sigmoid(ff)
        mx = jnp.maximum(a_lb, y)
        log_f = mx + _log1p_exp(jnp.minimum(a_lb, y) - mx)
        k_h[rows, :] = vec(V_LB_OM) * _sigmoid(-ff)
        v_h[rows, :] = proj_ref[b, :, HGRN_OFF + 512:HGRN_OFF + 768]
        b_h[rows, :] = _chunk_cumsum(log_f, c)

    q_g, k_g, v_g, b_g = qkvb[1]
    q_g[...] = blk(GLA_OFF, GLA_KEY_WIDTH) * (GLA_KEY_DIM ** -0.5)
    k_g[...] = blk(GLA_OFF + 128, GLA_KEY_WIDTH)
    v_g[...] = blk(GLA_OFF + 256, BRANCH)
    z = _dot(blk(GLA_OFF + 768, 128), wg_ref[...]) + bgate_ref[...]
    b_g[...] = _chunk_cumsum(_log_sigmoid(z) / GLA_TAU, c)

    mu = mu_ref[...]
    for b, rows in enumerate(blocks):
        p_rw = proj_ref[b, :, RWKV_OFF:RWKV_OFF + RWKV_PW]
        ps_ref[rows, :] = p_rw + (pltpu.roll(p_rw, 1, 0) - p_rw) * mu
        p0 = p_rw[0:1, :]
        ps_ref[b * tc:b * tc + 1, :] = p0 + (carry[b:b + 1, :] - p0) * mu
        carry[b:b + 1, :] = p_rw[tc - 1:tc, :]

    rr = ps_ref[:, 0:256]
    k_raw = ps_ref[:, 256:512]
    vv = ps_ref[:, 512:768]
    low_rank = ps_ref[:, 1024:1152]
    w_pre = vec(V_W0) + _dot(jnp.tanh(low_rank), w2_ref[...])
    w_log = -(jnp.maximum(-w_pre, 0.0) + _log1p_exp(-jnp.abs(w_pre))) - 0.5
    log_w = -jnp.exp(w_log)
    a = _sigmoid(vec(V_A0) + _dot(low_rank, a2_ref[...]))
    if layer0:
        vfirst_out_ref[...] = vv.reshape(bb, tc, BRANCH)
    else:
        vmix = _sigmoid(vec(V_V0) + _dot(_dot(vv, v1_ref[...]), v2_ref[...]))
        vv = vv + (vfirst_in_ref[...].reshape(r, BRANCH) - vv) * vmix
    kk_raw = k_raw * vec(V_KK)
    kk = kk_raw / jnp.maximum(jnp.sqrt(head_sum(kk_raw * kk_raw, two_pass=True)), 1e-12)
    kmod = k_raw * (1.0 + (a - 1.0) * vec(V_KA))
    b_w = _chunk_cumsum(log_w, c)
    inv_p = jnp.exp(-b_w)
    p_w = jnp.exp(b_w)
    for ref, arr in zip(rw, (kk * jnp.exp(b_w - log_w), rr * p_w, kmod * inv_p, kk * a * inv_p, vv, p_w)):
        ref[...] = arr

    n_c = tc // c
    mix = ((qkvb[0], o_gl[0], st_hgrn, bo_ref, HEAD_DIM), (qkvb[1], o_gl[1], st_gla, bog_ref, GLA_KEY_DIM))

    def chunk_body(ci, _):
        rows = [pl.ds(pl.multiple_of(b * tc + ci * c, c), c) for b in seqs]

        def all_scores(tiles, bo_m):
            sc = _dot(jnp.concatenate(tiles, axis=0), bo_m)
            n = len(tiles) // bb * SUBLANES
            return [sc[b * n:(b + 1) * n, :] for b in seqs]

        gl_ins = [[(q_s[rows[b], :], k_s[rows[b], :], v_s[rows[b], :], b_s[rows[b], :]) for b in seqs]
                  for (q_s, k_s, v_s, b_s), _, _, _, _ in mix]
        rw_in = [tuple(x[rows[b], :] for x in rw) for b in seqs]
        rw_lhs = [_head_stack(jnp.concatenate([kkt, rt], axis=0)) for kkt, rt, _, _, _, _ in rw_in]
        rw_cmp = [_rwkv_compact(rw_lhs[b], rw_in[b][2], rw_in[b][3]) for b in seqs]
        gl_os = [[_state_read(_head_stack(q * jnp.exp(b_c), dk), st_ref[b], c)
                  for b, (q, _, _, b_c) in enumerate(ins)] for ins, (_, _, st_ref, _, dk) in zip(gl_ins, mix)]
        rw_fs = [_state_read(rw_lhs[b], st_rw[b], 2 * c) for b in seqs]
        gl_cross_sc = [[_gl_cross_scores(q, k, b_c, c) for q, k, _, b_c in ins] for ins in gl_ins]
        tiles = []
        for kkt, _, _, ahat, _, _ in rw_in:
            t, rw_where = _rwkv_tiles(kkt, ahat, c)
            tiles += t
        rw_sc = all_scores(tiles, bo)
        gl = []
        for ins, cross_sc, (_, o_s, st_ref, bo_m, dk) in zip(gl_ins, gl_cross_sc, mix):
            tiles = []
            for q, k, _, b_c in ins:
                t, where_ = _gl_tiles(q, k, b_c, c)
                tiles += t
            gl.append((ins, all_scores(tiles, bo_m[...]), where_, o_s, st_ref, cross_sc))
        rw_m = [_rwkv_masks(rw_cmp[b], c) for b in seqs]
        rw_fv = [_head_unstack(_dot(rw_m[b][0], rw_in[b][4]), 2 * c) for b in seqs]
        for ins, _, _, _, st_ref, _ in gl:
            for b, (_, k, v, b_c) in enumerate(ins):
                st_ref[b] = _gl_state(k, v, b_c, st_ref[b], c)
        rhs = [rw_fs[b] + rw_fv[b] for b in seqs]
        us = [_rwkv_solve(rw_sc[b], rhs[b][0:c, :], rw_where, c) for b in seqs]
        rw_ou = [_dot(rw_m[b][1], us[b]) for b in seqs]
        for b in seqs:
            _, _, khat, ahat, v_w, p_c = rw_in[b]
            st_rw[b] = _rwkv_state(us[b], v_w, khat, ahat, p_c[c - 1:c, :], st_rw[b])
        gl_cross = [[_gl_cross_output(cross_sc[b], ins[b][2]) for b in seqs] for ins, _, _, _, _, cross_sc in gl]
        for (ins, sc, where_, o_s, _, _), o_state, cross in zip(gl, gl_os, gl_cross):
            for b in seqs:
                o_s[rows[b], :] = _gl_output(sc[b], o_state[b], ins[b][2], where_, cross[b], c)
        for b in seqs:
            orw_ref[rows[b], :] = rhs[b][c:2 * c, :] - _head_unstack(rw_ou[b], c)
        return 0

    lax.fori_loop(0, n_c, chunk_body, 0)

    def group_norm(o, eps):
        cen = o - head_mean(o, two_pass=True)
        return cen * lax.rsqrt(head_mean(cen * cen) + eps)

    def head_rms(o, g):
        return o * lax.rsqrt(head_mean(o * o) + RMS_EPS) * g

    def put(lo, val):
        ocat_ref[:, :, lo:lo + BRANCH] = val.reshape(bb, tc, BRANCH)

    put(0, group_norm(o_ret, RET_GN_EPS) * _silu(blk(RET_OFF + 768, BRANCH)))
    put(256, head_rms(o_gl[0][...], vec(V_HGRN_G)) * _silu(blk(HGRN_OFF + 768, BRANCH)))
    put(512, head_rms(o_gl[1][...], vec(V_GLA_G)) * _silu(blk(GLA_OFF + 512, BRANCH)))
    o_rw = group_norm(orw_ref[...], RWKV_GN_EPS) * vec(V_GN_G) + vec(V_GN_B)
    bonus = head_sum(rr * kmod * vec(V_RK))
    put(768, (o_rw + bonus * vv) * _silu(ps_ref[:, 768:1024]))

    @pl.when(ti == n_t - 1)
    def _final():
        for b in range(bb):
            for st_ref, _, out_ref in key_value_io:
                out_ref[b] = st_ref[b].T.reshape(out_ref.shape[1:])
            for h in range(N_HEADS):
                o_rw_st[b, h] = st_rw[b][:, h * HEAD_DIM:(h + 1) * HEAD_DIM]


def _mixers(proj, vfirst, cos, sin, states, layer, lp, consts, *, bb, tc, c):
    bsz, seq, _ = proj.shape
    layer0 = vfirst is None
    n_t = seq // tc
    r = bb * tc

    def full(arr):
        nd = arr.ndim
        return pl.BlockSpec(arr.shape, lambda bi, ti, _nd=nd: (0,) * _nd)

    def per_b(arr):
        nd = arr.ndim
        return pl.BlockSpec((None, bb) + arr.shape[2:], lambda bi, ti, _nd=nd: (layer, bi) + (0,) * (_nd - 2))

    def tok(width):
        return pl.BlockSpec((bb, tc, width), lambda bi, ti: (bi, ti, 0))

    ins, specs = [proj], [tok(D_IN_PAD)]
    if not layer0:
        ins.append(vfirst)
        specs.append(tok(BRANCH))
    ins += [cos, sin]
    specs += [pl.BlockSpec((tc, BRANCH), lambda bi, ti: (ti, 0))] * 2
    state_inputs = list(range(len(ins), len(ins) + 4))
    for s in states:
        ins.append(s)
        specs.append(per_b(s))
    small = [lp['vec'], lp['mu'], lp['bgate'], lp['wg'], lp['w2'], lp['a2']]
    if not layer0:
        small += [lp['v1'], lp['v2']]
    small += list(consts) + list(_retention_tables(tc))
    for s in small:
        ins.append(s)
        specs.append(full(s))

    out_shape = [jax.ShapeDtypeStruct((bsz, seq, D_MODEL), F32)]
    out_specs = [tok(D_MODEL)]
    if layer0:
        out_shape.append(jax.ShapeDtypeStruct((bsz, seq, BRANCH), F32))
        out_specs.append(tok(BRANCH))
    aliases = {}
    for idx, s in zip(state_inputs, states[:4]):
        aliases[idx] = len(out_shape)
        out_shape.append(jax.ShapeDtypeStruct(s.shape, F32))
        out_specs.append(per_b(s))

    scratch = [pltpu.VMEM((bb, HEAD_DIM, BRANCH), F32), pltpu.VMEM((bb, HEAD_DIM, BRANCH), F32),
               pltpu.VMEM((bb, HEAD_DIM, GLA_KEY_WIDTH), F32), pltpu.VMEM((bb, HEAD_DIM, BRANCH), F32),
               pltpu.VMEM((bb, RWKV_PW), F32)]
    for dkt in (BRANCH, GLA_KEY_WIDTH):
        scratch += [pltpu.VMEM((r, dkt), F32), pltpu.VMEM((r, dkt), F32), pltpu.VMEM((r, BRANCH), F32),
                    pltpu.VMEM((r, dkt), F32)]
    scratch += [pltpu.VMEM((r, BRANCH), F32)] * 2
    scratch += [pltpu.VMEM((r, RWKV_PW), F32), pltpu.VMEM((r, BRANCH), F32)]
    scratch += [pltpu.VMEM((r, BRANCH), F32)] * 6

    outs = pl.pallas_call(
        functools.partial(_mixers_kernel, bb=bb, tc=tc, c=c, layer0=layer0, n_t=n_t),
        grid=(bsz // bb, n_t),
        in_specs=specs,
        out_specs=out_specs,
        out_shape=out_shape,
        scratch_shapes=scratch,
        input_output_aliases=aliases,
        compiler_params=pltpu.CompilerParams(dimension_semantics=("arbitrary", "arbitrary"),
                                             vmem_limit_bytes=V7X_VMEM_LIMIT),
        name="mixers",
    )(*ins)
    if layer0:
        return outs[0], outs[1], list(outs[2:])
    return outs[0], None, list(outs[1:])


def _out_attn_kernel(x_ref, oc_ref, mk_ref, mv_ref, wout_ref, wq_ref, wo_ref, gx_ref, gf_ref, out_ref,
                     *, bbc, lc, final):
    r = bbc * lc
    x1 = x_ref[...].reshape(r, D_MODEL) + _dot(oc_ref[...].reshape(r, D_MODEL), wout_ref[...])
    q = _dot(_rms(x1, gx_ref[...]), wq_ref[...])
    seqs = range(bbc)
    scores = [_dot_nt(_head_stack(q[b * lc:(b + 1) * lc, :]), mk_ref[b]) * (HEAD_DIM ** -0.5) for b in seqs]
    probs = []
    for s in scores:
        e = jnp.exp(s - jnp.max(s, axis=-1, keepdims=True))
        probs.append(e / jnp.sum(e, axis=-1, keepdims=True))
    o4 = [_dot(probs[b], mv_ref[b]) for b in seqs]
    outs = [_head_unstack(o4[b], lc) for b in seqs]
    o = outs[0] if bbc == 1 else jnp.concatenate(outs, axis=0)
    x2 = x1 + _dot(o, wo_ref[...])
    if final:
        x2 = _rms(x2, gf_ref[...])
    out_ref[...] = x2.reshape(bbc, lc, D_MODEL)


def _out_attn(x, ocat, mem, lp, norm_f, *, bbc, lc, final):
    bsz, seq, _ = x.shape
    mk, mv, mk_spec, mv_spec = mem

    def tok(width):
        return pl.BlockSpec((bbc, lc, width), lambda bi, li: (bi, li, 0))

    def full(arr):
        nd = arr.ndim
        return pl.BlockSpec(arr.shape, lambda bi, li, _nd=nd: (0,) * _nd)

    small = [lp['w_out'], lp['wq'], lp['wo'], lp['norm_x'], norm_f]
    return pl.pallas_call(
        functools.partial(_out_attn_kernel, bbc=bbc, lc=lc, final=final),
        grid=(bsz // bbc, seq // lc),
        in_specs=[tok(D_MODEL), tok(D_MODEL), mk_spec, mv_spec] + [full(s) for s in small],
        out_specs=tok(D_MODEL),
        out_shape=jax.ShapeDtypeStruct((bsz, seq, D_MODEL), F32),
        compiler_params=pltpu.CompilerParams(dimension_semantics=("arbitrary", "arbitrary"),
                                             vmem_limit_bytes=V7X_VMEM_LIMIT),
        name="out_attn",
    )(x, ocat, mk, mv, *small)


def _rope_tables(pos0, seq):
    half = HEAD_DIM // 2
    inv = ROPE_BASE ** (-jnp.arange(half, dtype=F32) / half)
    pos = (pos0 + jnp.arange(seq, dtype=jnp.int32)).astype(F32)
    ang = pos[:, None] * inv[None, :]
    cos, sin = lax.optimization_barrier((jnp.cos(ang), jnp.sin(ang)))
    return (jnp.tile(jnp.concatenate([cos, cos], axis=-1), (1, N_HEADS)),
            jnp.tile(jnp.concatenate([-sin, sin], axis=-1), (1, N_HEADS)))


def _retention_tables(n):
    log_gamma = jnp.log1p(-jnp.exp2(-5.0 - jnp.arange(N_HEADS, dtype=F32)))
    j = jnp.arange(n, dtype=F32)
    diff = j[:, None] - j[None, :]
    dmat = jnp.where(diff[None] >= 0, jnp.exp(diff[None] * log_gamma[:, None, None]), 0.0)
    lg_lanes = jnp.repeat(log_gamma, HEAD_DIM)[None, :]
    e_q = jnp.exp((j[:, None] + 1.0) * lg_lanes)
    e_k = jnp.exp((n - 1.0 - j[:, None]) * lg_lanes)
    g_blk = jnp.exp(float(n) * lg_lanes)
    return dmat.reshape(N_HEADS * n, n), e_q, e_k, g_blk


def _rwkv_to_padded(t):
    pad = jnp.zeros(t.shape[:-1] + (RWKV_PW - RWKV_W,), t.dtype)
    return jnp.concatenate([t[..., 0:256], t[..., 288:544], t[..., 544:800], t[..., 832:1088],
                            t[..., 256:288], t[..., 800:832], pad], axis=-1)


def _rwkv_from_padded(t):
    return jnp.concatenate([t[..., 0:256], t[..., 1024:1056], t[..., 256:512], t[..., 512:768],
                            t[..., 1056:1088], t[..., 768:1024]], axis=-1)


def _pad_rows(m, rows, at=0):
    out = jnp.zeros((rows, m.shape[1]), m.dtype)
    return out.at[at:at + m.shape[0]].set(m)


def _layer_params(l, P):
    sm = jax.nn.softmax(P['hgrn_lb_logits'].astype(F32), axis=0)
    lb = (jnp.cumsum(sm, axis=0) - sm[0])[l]
    rows = [None] * N_VEC
    rows[V_LB_A] = jnp.log(jnp.maximum(lb, LB_FLOOR))
    rows[V_LB_B] = jnp.log1p(-lb)
    rows[V_LB_OM] = 1.0 - lb
    rows[V_HGRN_G] = jnp.tile(P['hgrn_norm'][l], N_HEADS)
    rows[V_GLA_G] = jnp.tile(P['gla_norm'][l], N_HEADS)
    rows[V_W0] = P['rwkv_w0'][l]
    rows[V_A0] = P['rwkv_a0'][l]
    rows[V_V0] = P['rwkv_v0'][l - 1] if l > 0 else jnp.zeros((BRANCH,), F32)
    rows[V_KK] = P['rwkv_k_k'][l]
    rows[V_KA] = P['rwkv_k_a'][l]
    rows[V_RK] = P['rwkv_r_k'][l].reshape(BRANCH)
    rows[V_GN_G] = P['rwkv_gn_g'][l]
    rows[V_GN_B] = P['rwkv_gn_b'][l]
    zero = jnp.zeros((BRANCH,), F32)
    vec = jnp.stack([zero if x is None else x.astype(F32) for x in rows], axis=0)

    lp = {
        'norm_mix': P['norm_mix'][l][None, :],
        'vec': vec,
        'mu': _rwkv_to_padded(P['rwkv_mu'][l])[None, :],
        'bgate': P['gla_b_gate'][l][None, :],
        'wg': _pad_rows(P['gla_w_gate2'][l], 128).astype(BF16),
        'w2': _pad_rows(P['rwkv_w2'][l], 128, 0).astype(BF16),
        'a2': _pad_rows(P['rwkv_a2'][l], 128, 32).astype(BF16),
        'w_out': P['w_out'][l].astype(BF16),
        'wq': P['wq_x'][l].astype(BF16),
        'wo': P['wo_x'][l].astype(BF16),
        'norm_x': P['norm_x'][l][None, :],
    }
    if l > 0:
        lp['v1'] = jnp.pad(P['rwkv_v1'][l - 1], ((0, 0), (0, 96))).astype(BF16)
        lp['v2'] = _pad_rows(P['rwkv_v2'][l - 1], 128).astype(BF16)
    return lp


def _consts():
    i256 = jnp.arange(BRANCH) // HEAD_DIM
    i128 = jnp.arange(GLA_KEY_WIDTH) // GLA_KEY_DIM
    bo = (i256[:, None] == i256[None, :])
    bog = (i128[:, None] == i256[None, :])
    lane = jnp.arange(BRANCH)
    partner = jnp.where(lane % HEAD_DIM < HEAD_DIM // 2, lane + HEAD_DIM // 2, lane - HEAD_DIM // 2)
    swap = (lane[:, None] == partner[None, :])
    return bo.astype(BF16), bog.astype(BF16), swap.astype(BF16)


def _run_trunk(x, pos0, mems, init, lps, w_in_all, norm_f, consts, *, bb, tc, c, bbc, lc, tm):
    bsz, seq, _ = x.shape
    cos, sin = _rope_tables(pos0, seq)
    new_shift = []
    states = [init['ret'], init['hgrn'], init['gla'], init['rwkv'], _rwkv_to_padded(init['shift'])]
    vfirst = None
    for l in range(DEPTH):
        lp = lps[l]
        proj = _in_proj(x.reshape(bsz * seq, D_MODEL), lp['norm_mix'], w_in_all, tm, layer=l)
        proj = proj.reshape(bsz, seq, D_IN_PAD)
        ocat, vf, updated = _mixers(proj, vfirst, cos, sin, states, l, lp, consts, bb=bb, tc=tc, c=c)
        states = updated + states[4:]
        if l == 0:
            vfirst = vf
        new_shift.append(_rwkv_from_padded(proj[:, seq - 1, RWKV_OFF:RWKV_OFF + RWKV_PW]))
        x = _out_attn(x, ocat, mems[l], lp, norm_f, bbc=bbc, lc=lc, final=(l == DEPTH - 1))
    return x, {'ret': states[0], 'hgrn': states[1], 'gla': states[2], 'rwkv': states[3],
               'shift': jnp.stack(new_shift, axis=0)}


def kernel(x_prompt, x_sample, mem_prompt, state_ret, state_hgrn, state_gla, state_rwkv, state_rwkv_shift,
           cache_mem_k, cache_mem_v, norm_mix, w_in, hgrn_lb_logits, hgrn_norm, gla_w_gate2, gla_b_gate,
           gla_norm, rwkv_mu, rwkv_w0, rwkv_w2, rwkv_a0, rwkv_a2, rwkv_v0, rwkv_v1, rwkv_v2, rwkv_k_k,
           rwkv_k_a, rwkv_r_k, rwkv_gn_g, rwkv_gn_b, w_out, norm_x, wq_x, wo_x, norm_mem, wk_x, wv_x, norm_f):
    P = {'norm_mix': norm_mix, 'hgrn_lb_logits': hgrn_lb_logits, 'hgrn_norm': hgrn_norm,
         'gla_w_gate2': gla_w_gate2, 'gla_b_gate': gla_b_gate, 'gla_norm': gla_norm,
         'rwkv_mu': rwkv_mu, 'rwkv_w0': rwkv_w0, 'rwkv_w2': rwkv_w2, 'rwkv_a0': rwkv_a0,
         'rwkv_a2': rwkv_a2, 'rwkv_v0': rwkv_v0, 'rwkv_v1': rwkv_v1, 'rwkv_v2': rwkv_v2,
         'rwkv_k_k': rwkv_k_k, 'rwkv_k_a': rwkv_k_a, 'rwkv_r_k': rwkv_r_k,
         'rwkv_gn_g': rwkv_gn_g, 'rwkv_gn_b': rwkv_gn_b, 'w_out': w_out,
         'norm_x': norm_x, 'wq_x': wq_x, 'wo_x': wo_x}
    lps = [_layer_params(l, P) for l in range(DEPTH)]
    w_in_all = _w_in_layout(w_in)
    consts = _consts()
    norm_f2 = norm_f[None, :]
    bsz, seq, _ = x_prompt.shape
    dbsz, dseq, _ = x_sample.shape
    p_bbc, s_bbc = 2, 32

    mem2d = mem_prompt.reshape(bsz * N_MEM, D_MODEL)
    kv_l, p_mems = [], []
    for l in range(DEPTH):
        wkv = jnp.concatenate([wk_x[l], wv_x[l]], axis=1).astype(BF16)
        kv = _in_proj(mem2d, norm_mem[l][None, :], wkv, 256).reshape(bsz, N_MEM, 2 * BRANCH)
        kv_l.append(kv)
        p_mems.append((kv, kv, pl.BlockSpec((p_bbc, N_MEM, BRANCH), lambda bi, li: (bi, 0, 0)),
                       pl.BlockSpec((p_bbc, N_MEM, BRANCH), lambda bi, li: (bi, 0, 1))))
    zero_init = {
        'ret': jnp.zeros((DEPTH, bsz, N_HEADS, HEAD_DIM, HEAD_DIM), F32),
        'hgrn': jnp.zeros((DEPTH, bsz, N_HEADS, HEAD_DIM, HEAD_DIM), F32),
        'gla': jnp.zeros((DEPTH, bsz, N_HEADS, GLA_KEY_DIM, HEAD_DIM), F32),
        'rwkv': jnp.zeros((DEPTH, bsz, N_HEADS, HEAD_DIM, HEAD_DIM), F32),
        'shift': jnp.zeros((DEPTH, bsz, RWKV_W), F32),
    }
    y_prompt, sp = _run_trunk(x_prompt, 0, p_mems, zero_init, lps, w_in_all, norm_f2, consts,
                              bb=8, tc=64, c=CHUNK, bbc=p_bbc, lc=512, tm=512)

    cached = {'ret': state_ret, 'hgrn': state_hgrn, 'gla': state_gla, 'rwkv': state_rwkv,
              'shift': state_rwkv_shift}
    cmk = cache_mem_k.reshape(DEPTH, dbsz, N_MEM, BRANCH)
    cmv = cache_mem_v.reshape(DEPTH, dbsz, N_MEM, BRANCH)
    s_mems = []
    for l in range(DEPTH):
        spec = pl.BlockSpec((None, s_bbc, N_MEM, BRANCH), lambda bi, li, _l=l: (_l, bi, 0, 0))
        s_mems.append((cmk, cmv, spec, spec))
    y_sample, ss = _run_trunk(x_sample, PAST_LEN, s_mems, cached, lps, w_in_all, norm_f2, consts,
                              bb=16, tc=dseq, c=dseq, bbc=s_bbc, lc=dseq, tm=512)

    kv_all = jnp.stack(kv_l, axis=0)
    mem_k_p = kv_all[..., :BRANCH].reshape(DEPTH, bsz, N_MEM, N_HEADS, HEAD_DIM)
    mem_v_p = kv_all[..., BRANCH:].reshape(DEPTH, bsz, N_MEM, N_HEADS, HEAD_DIM)
    return (y_prompt, y_sample, sp['ret'], ss['ret'], sp['hgrn'], ss['hgrn'], sp['gla'], ss['gla'],
            sp['rwkv'], ss['rwkv'], sp['shift'], ss['shift'], mem_k_p, mem_v_p)
```
